```python
import jax, jax.numpy as jnp
from jax import lax
import numpy as np

D_MODEL = 1024
BATCH = 8
SEQ = 4096
DEPTH = 2

CHUNK = 64
N_MIXERS = 2
N_ATTN_LAYERS = (DEPTH + 1) // 2
N_REC_LAYERS = DEPTH // 2

ATTN_HEADS = 16
HEAD_DIM = D_MODEL // ATTN_HEADS
LEFT_CHUNKS = 8
BAND = (LEFT_CHUNKS + 1) * CHUNK
REL_CLIP = 256
N_REL = 2 * REL_CLIP + 1
NEG_INF = -1e30

LRU_WIDTH = D_MODEL
LRU_HEADS = 4
LRU_BLOCK = LRU_WIDTH // LRU_HEADS
CONV_WIDTH = 4
LRU_C = 8.0

PEER_HEADS = 8
N_KEYS = 128
N_EXPERTS = N_KEYS * N_KEYS
PEER_TOPK = 16
D_KEY = 256
D_HALF = D_KEY // 2
PEER_TOKEN_BLOCK = 128

EPS = 1e-6

kernel_name = "hybrid_chunkattn_rglru_peer"


def rms_norm(x, g):
    xf = x.astype(jnp.float32)
    y = xf * lax.rsqrt(jnp.mean(xf * xf, axis=-1, keepdims=True) + EPS)
    return (y * g.astype(jnp.float32)).astype(x.dtype)


def rel_bias(table):
    i = jnp.arange(CHUNK)[:, None]
    j = jnp.arange(BAND)[None, :]
    dist = LEFT_CHUNKS * CHUNK + i - j
    idx = jnp.clip(dist, -REL_CLIP, REL_CLIP) + REL_CLIP
    return table[:, idx]


def chunk_attention(x, w_qkv, q_g, k_g, bias_table, w_out):
    b, s, _ = x.shape
    n_chunks = s // CHUNK
    q, k, v = jnp.split(x @ w_qkv, 3, axis=-1)
    q = rms_norm(q.reshape(b, s, ATTN_HEADS, HEAD_DIM), q_g)
    k = rms_norm(k.reshape(b, s, ATTN_HEADS, HEAD_DIM), k_g)
    v = v.reshape(b, s, ATTN_HEADS, HEAD_DIM)
    pad = ((0, 0), (LEFT_CHUNKS * CHUNK, 0), (0, 0), (0, 0))
    k_pad = jnp.pad(k, pad)
    v_pad = jnp.pad(v, pad)
    bias = rel_bias(bias_table).astype(jnp.float32)
    scale = HEAD_DIM ** -0.5
    key_off = jnp.arange(BAND)

    def one_chunk(c):
        start = c * CHUNK
        q_c = lax.dynamic_slice_in_dim(q, start, CHUNK, axis=1)
        k_c = lax.dynamic_slice_in_dim(k_pad, start, BAND, axis=1)
        v_c = lax.dynamic_slice_in_dim(v_pad, start, BAND, axis=1)
        scores = jnp.einsum('bqhd,bkhd->bhqk', q_c, k_c,
                            preferred_element_type=jnp.float32) * scale + bias
        valid = (key_off + start) >= LEFT_CHUNKS * CHUNK
        scores = jnp.where(valid, scores, NEG_INF)
        p = jax.nn.softmax(scores, axis=-1).astype(v.dtype)
        return jnp.einsum('bhqk,bkhd->bqhd', p, v_c)

    out = lax.map(one_chunk, jnp.arange(n_chunks))
    out = jnp.moveaxis(out, 0, 1).reshape(b, s, D_MODEL)
    return out @ w_out


def rglru_mixer(x, w_in, conv_w, conv_b, w_a, b_a, w_x, b_x, lam, w_out):
    b, s, _ = x.shape
    gate, u = jnp.split(x @ w_in, 2, axis=-1)
    u = lax.conv_general_dilated(
        u, conv_w[:, None, :], window_strides=(1,), padding=[(CONV_WIDTH - 1, 0)],
        dimension_numbers=('NWC', 'WIO', 'NWC'), feature_group_count=LRU_WIDTH) + conv_b
    uh = u.reshape(b, s, LRU_HEADS, LRU_BLOCK)
    r = jax.nn.sigmoid(jnp.einsum('bshi,hij->bshj', uh, w_a).reshape(b, s, LRU_WIDTH) + b_a)
    i = jax.nn.sigmoid(jnp.einsum('bshi,hij->bshj', uh, w_x).reshape(b, s, LRU_WIDTH) + b_x)
    log_a = -LRU_C * r.astype(jnp.float32) * jax.nn.softplus(-lam.astype(jnp.float32))
    a = jnp.exp(log_a)
    mult = jnp.sqrt(-jnp.expm1(2.0 * log_a))
    bx = mult * (i * u).astype(jnp.float32)

    def combine(left, right):
        a_l, b_l = left
        a_r, b_r = right
        return a_l * a_r, a_r * b_l + b_r

    _, h = lax.associative_scan(combine, (a, bx), axis=1)
    y = jax.nn.gelu(gate) * h.astype(x.dtype)
    return y @ w_out


def peer_ffn(h, w_q, sub_keys, u_tab, v_tab):
    b, s, d = h.shape
    t = b * s
    blocks = h.reshape(t // PEER_TOKEN_BLOCK, PEER_TOKEN_BLOCK, d)

    def one_block(xb):
        tb = xb.shape[0]
        q = (xb @ w_q).reshape(tb, PEER_HEADS, 2, D_HALF)
        s_half = jnp.einsum('thpc,pnc->thpn', q, sub_keys,
                            preferred_element_type=jnp.float32)
        top_s, top_i = lax.top_k(s_half, PEER_TOPK)
        cand = (top_s[:, :, 0, :, None] + top_s[:, :, 1, None, :]).reshape(
            tb, PEER_HEADS, PEER_TOPK * PEER_TOPK)
        best_s, best_c = lax.top_k(cand, PEER_TOPK)
        i1 = jnp.take_along_axis(top_i[:, :, 0], best_c // PEER_TOPK, axis=-1)
        i2 = jnp.take_along_axis(top_i[:, :, 1], best_c % PEER_TOPK, axis=-1)
        expert = i1 * N_KEYS + i2
        g = jax.nn.softmax(best_s, axis=-1)
        u_e = u_tab[expert]
        v_e = v_tab[expert]
        act = jax.nn.gelu(jnp.einsum('td,thkd->thk', xb, u_e,
                                     preferred_element_type=jnp.float32))
        return jnp.einsum('thk,thkd->td', (g * act).astype(xb.dtype), v_e)

    return lax.map(one_block, blocks).reshape(b, s, d)


def setup_inputs(seed: int = 0) -> dict:
    key = jax.random.key(seed)
    ks = jax.random.split(key, 24)
    f32 = jnp.float32
    NA, NR, W = N_ATTN_LAYERS, N_REC_LAYERS, LRU_WIDTH
    nrm = lambda k, shp, sc: jax.random.normal(k, shp, f32) * sc
    x = jax.random.normal(ks[0], (BATCH, SEQ, D_MODEL), f32)
    attn_norm_g = 1.0 + nrm(ks[1], (NA, D_MODEL), 0.05)
    attn_w_qkv = nrm(ks[2], (NA, D_MODEL, 3 * D_MODEL), D_MODEL ** -0.5)
    attn_q_g = 1.0 + nrm(ks[3], (NA, HEAD_DIM), 0.05)
    attn_k_g = 1.0 + nrm(ks[4], (NA, HEAD_DIM), 0.05)
    attn_rel_bias = nrm(ks[5], (NA, ATTN_HEADS, N_REL), 0.2)
    attn_w_out = nrm(ks[6], (NA, D_MODEL, D_MODEL), D_MODEL ** -0.5)
    rec_norm_g = 1.0 + nrm(ks[7], (NR, D_MODEL), 0.05)
    rec_w_in = nrm(ks[8], (NR, D_MODEL, 2 * W), D_MODEL ** -0.5)
    rec_conv_w = nrm(ks[9], (NR, CONV_WIDTH, W), CONV_WIDTH ** -0.5)
    rec_conv_b = nrm(ks[10], (NR, W), 0.02)
    rec_w_a = nrm(ks[11], (NR, LRU_HEADS, LRU_BLOCK, LRU_BLOCK), LRU_BLOCK ** -0.5)
    rec_b_a = nrm(ks[12], (NR, W), 0.02)
    rec_w_x = nrm(ks[13], (NR, LRU_HEADS, LRU_BLOCK, LRU_BLOCK), LRU_BLOCK ** -0.5)
    rec_b_x = nrm(ks[14], (NR, W), 0.02)
    a0 = jax.random.uniform(ks[15], (NR, W), f32, minval=0.9, maxval=0.999)
    base = a0 ** (1.0 / LRU_C)
    rec_lambda = jnp.log(base) - jnp.log1p(-base)
    rec_w_out = nrm(ks[16], (NR, W, D_MODEL), W ** -0.5)
    ffn_norm_g = 1.0 + nrm(ks[17], (DEPTH, D_MODEL), 0.05)
    peer_w_q = nrm(ks[18], (DEPTH, D_MODEL, PEER_HEADS * D_KEY), D_MODEL ** -0.5)
    peer_sub_keys = nrm(ks[19], (DEPTH, 2, N_KEYS, D_HALF), D_HALF ** -0.5)
    peer_u = nrm(ks[20], (DEPTH, N_EXPERTS, D_MODEL), D_MODEL ** -0.5)
    peer_v = nrm(ks[21], (DEPTH, N_EXPERTS, D_MODEL), D_MODEL ** -0.5)
    return {"x": x, "attn_norm_g": attn_norm_g, "attn_w_qkv": attn_w_qkv,
            "attn_q_g": attn_q_g, "attn_k_g": attn_k_g, "attn_rel_bias": attn_rel_bias,
            "attn_w_out": attn_w_out, "rec_norm_g": rec_norm_g, "rec_w_in": rec_w_in,
            "rec_conv_w": rec_conv_w, "rec_conv_b": rec_conv_b, "rec_w_a": rec_w_a,
            "rec_b_a": rec_b_a, "rec_w_x": rec_w_x, "rec_b_x": rec_b_x,
            "rec_lambda": rec_lambda, "rec_w_out": rec_w_out, "ffn_norm_g": ffn_norm_g,
            "peer_w_q": peer_w_q, "peer_sub_keys": peer_sub_keys,
            "peer_u": peer_u, "peer_v": peer_v}


def reference(x, attn_norm_g, attn_w_qkv, attn_q_g, attn_k_g, attn_rel_bias, attn_w_out,
              rec_norm_g, rec_w_in, rec_conv_w, rec_conv_b, rec_w_a, rec_b_a, rec_w_x,
              rec_b_x, rec_lambda, rec_w_out, ffn_norm_g, peer_w_q, peer_sub_keys,
              peer_u, peer_v):
    for layer in range(DEPTH):
        j = layer // N_MIXERS
        if layer % N_MIXERS == 0:
            x = x + chunk_attention(rms_norm(x, attn_norm_g[j]), attn_w_qkv[j], attn_q_g[j],
                                    attn_k_g[j], attn_rel_bias[j], attn_w_out[j])
        else:
            x = x + rglru_mixer(rms_norm(x, rec_norm_g[j]), rec_w_in[j], rec_conv_w[j],
                                rec_conv_b[j], rec_w_a[j], rec_b_a[j], rec_w_x[j], rec_b_x[j],
                                rec_lambda[j], rec_w_out[j])
        x = x + peer_ffn(rms_norm(x, ffn_norm_g[layer]), peer_w_q[layer], peer_sub_keys[layer],
                         peer_u[layer], peer_v[layer])
    return x
```

```python
import functools

import numpy as np
import jax
import jax.numpy as jnp
from jax import lax
from jax.experimental import pallas as pl
from jax.experimental.pallas import tpu as pltpu

F32 = jnp.float32
BF16 = jnp.bfloat16
U32 = jnp.uint32
I32 = jnp.int32

EPS = 1e-6
NEG_INF = -1e30

D_MODEL = 1024
CHUNK = 64
ATTN_HEADS = 16
HEAD_DIM = 64
LEFT_CHUNKS = 8
REL_CLIP = 256
LRU_HEADS = 4
LRU_BLOCK = 256
CONV_WIDTH = 4
LRU_C = 8.0
PEER_HEADS = 8
N_KEYS = 128
PEER_TOPK = 16
D_HALF = 128

SUBLANES = 8
LANES = 128

Q_TILE = 4 * CHUNK
N_KEY_BLOCKS = 3
ROW_TILE = 512
PEER_TOKENS = 128
LRU_TILE = 256
N_SLOTS = PEER_HEADS * PEER_TOPK

MIB = 1024 * 1024


def _rms_rows(x, g):
    ms = jnp.mean(x * x, axis=-1, keepdims=True)
    return (x * lax.rsqrt(ms + EPS)) * g


def _split_bf16(v):
    hi = v.astype(BF16)
    lo = (v - hi.astype(F32)).astype(BF16)
    return hi, lo


def _dot(a, b):
    return jnp.dot(a, b, preferred_element_type=F32)


def _dot_nt(a, b):
    return lax.dot_general(a, b, (((1,), (1,)), ((), ())), preferred_element_type=F32)


def _qkv_kernel(x_ref, g_ref, w_ref, gain_ref, bsel_ref, bexp_ref, o_ref, xn_ref):
    j = pl.program_id(1)

    @pl.when(j == 0)
    def _():
        xn_ref[...] = _rms_rows(x_ref[...], g_ref[...]).astype(BF16)

    y = _dot(xn_ref[...], w_ref[...])

    @pl.when(j < 2)
    def _():
        hi, lo = _split_bf16(y * y)
        ms = _dot(hi, bsel_ref[...]) + _dot(lo, bsel_ref[...])
        rhi, rlo = _split_bf16(lax.rsqrt(ms + EPS))
        rs = _dot(rhi, bexp_ref[...]) + _dot(rlo, bexp_ref[...])
        o_ref[...] = ((y * rs) * gain_ref[0]).astype(BF16)

    @pl.when(j == 2)
    def _():
        o_ref[...] = y.astype(BF16)


def _qkv_proj(x2d, norm_g, w_qkv, q_g, k_g):
    t, d = x2d.shape
    scale = HEAD_DIM ** -0.5
    gains = jnp.stack([jnp.tile(q_g, ATTN_HEADS) * scale, jnp.tile(k_g, ATTN_HEADS),
                       jnp.ones((d,), F32)]).reshape(3, 1, d)
    head_of_col = np.arange(d) // HEAD_DIM
    bsel = (head_of_col[:, None] == np.arange(LANES)[None, :]).astype(np.float32) / HEAD_DIM
    bexp = (np.arange(LANES)[:, None] == head_of_col[None, :]).astype(np.float32)
    return pl.pallas_call(
        _qkv_kernel,
        grid=(t // ROW_TILE, 3),
        in_specs=[
            pl.BlockSpec((ROW_TILE, d), lambda i, j: (i, 0)),
            pl.BlockSpec((1, d), lambda i, j: (0, 0)),
            pl.BlockSpec((d, d), lambda i, j: (0, j)),
            pl.BlockSpec((1, 1, d), lambda i, j: (j, 0, 0)),
            pl.BlockSpec((d, LANES), lambda i, j: (0, 0)),
            pl.BlockSpec((LANES, d), lambda i, j: (0, 0)),
        ],
        out_specs=pl.BlockSpec((ROW_TILE, d), lambda i, j: (i, j)),
        out_shape=jax.ShapeDtypeStruct((t, 3 * d), BF16),
        scratch_shapes=[pltpu.VMEM((ROW_TILE, d), BF16)],
        compiler_params=pltpu.CompilerParams(
            dimension_semantics=("parallel", "arbitrary"), vmem_limit_bytes=40 * MIB),
        name="qkv_proj",
    )(x2d, norm_g.reshape(1, d), w_qkv.astype(BF16), gains,
      jnp.asarray(bsel, BF16), jnp.asarray(bexp, BF16))


def _attn_kernel(x_ref, q_ref, k0_ref, k1_ref, k2_ref, v0_ref, v1_ref, v2_ref,
                 bias_ref, wout_ref, o_ref, att_ref):
    qt = pl.program_id(1)
    k_refs = (k0_ref, k1_ref, k2_ref)
    v_refs = (v0_ref, v1_ref, v2_ref)
    negs = [jnp.where(qt - (N_KEY_BLOCKS - 1) + kb < 0, NEG_INF, 0.0).astype(F32)
            for kb in range(N_KEY_BLOCKS)]
    lane = lax.broadcasted_iota(I32, (Q_TILE, LANES), 1)
    first_half = lane < HEAD_DIM
    for p in range(ATTN_HEADS // 2):
        cols = slice(p * LANES, (p + 1) * LANES)
        qp = q_ref[0, :, cols]
        outs = []
        for hh in range(2):
            h = 2 * p + hh
            keep = first_half if hh == 0 else jnp.logical_not(first_half)
            qm = jnp.where(keep, qp, jnp.zeros_like(qp))
            s = [_dot_nt(qm, k_refs[kb][0, :, cols]) + bias_ref[kb, h] + negs[kb]
                 for kb in range(N_KEY_BLOCKS)]
            m = jnp.max(s[0], axis=-1, keepdims=True)
            for kb in range(1, N_KEY_BLOCKS):
                m = jnp.maximum(m, jnp.max(s[kb], axis=-1, keepdims=True))
            acc = jnp.zeros((Q_TILE, LANES), F32)
            l = jnp.zeros((Q_TILE, 1), F32)
            for kb in range(N_KEY_BLOCKS):
                e = jnp.exp(s[kb] - m)
                l = l + jnp.sum(e, axis=-1, keepdims=True)
                acc = acc + _dot(e.astype(BF16), v_refs[kb][0, :, cols])
            outs.append(acc / l)
        att_ref[:, cols] = jnp.where(first_half, outs[0], outs[1]).astype(BF16)
    o_ref[0] = x_ref[0] + _dot(att_ref[...], wout_ref[...])


def _band_bias(table):
    qi = np.arange(Q_TILE)[:, None]
    kj = np.arange(Q_TILE)[None, :]
    blocks = []
    for kb in range(N_KEY_BLOCKS):
        dist = (N_KEY_BLOCKS - 1 - kb) * Q_TILE + qi - kj
        idx = np.clip(dist, -REL_CLIP, REL_CLIP) + REL_CLIP
        dchunk = (Q_TILE // CHUNK) * (kb - (N_KEY_BLOCKS - 1)) + kj // CHUNK - qi // CHUNK
        valid = (dchunk >= -LEFT_CHUNKS) & (dchunk <= 0)
        blocks.append(jnp.where(valid[None], table[:, idx], NEG_INF))
    return jnp.stack(blocks).astype(F32)


def _attention(x, qkv, rel_bias_table, w_out):
    b, s, d = x.shape
    nq = s // Q_TILE
    qkv3 = qkv.reshape(b, s, 3 * d)
    bias = _band_bias(rel_bias_table)

    def kv_spec(kb, col):
        return pl.BlockSpec(
            (1, Q_TILE, d),
            lambda bi, qt: (bi, jnp.maximum(qt - (N_KEY_BLOCKS - 1) + kb, 0), col))

    in_specs = [pl.BlockSpec((1, Q_TILE, d), lambda bi, qt: (bi, qt, 0)),
                pl.BlockSpec((1, Q_TILE, d), lambda bi, qt: (bi, qt, 0))]
    in_specs += [kv_spec(kb, 1) for kb in range(N_KEY_BLOCKS)]
    in_specs += [kv_spec(kb, 2) for kb in range(N_KEY_BLOCKS)]
    in_specs += [pl.BlockSpec(memory_space=pltpu.VMEM), pl.BlockSpec(memory_space=pltpu.VMEM)]
    return pl.pallas_call(
        _attn_kernel,
        grid=(b, nq),
        in_specs=in_specs,
        out_specs=pl.BlockSpec((1, Q_TILE, d), lambda bi, qt: (bi, qt, 0)),
        out_shape=jax.ShapeDtypeStruct((b, s, d), F32),
        scratch_shapes=[pltpu.VMEM((Q_TILE, d), BF16)],
        compiler_params=pltpu.CompilerParams(
            dimension_semantics=("parallel", "parallel"), vmem_limit_bytes=48 * MIB),
        name="band_attention",
    )(x, qkv3, qkv3, qkv3, qkv3, qkv3, qkv3, qkv3, bias, w_out.astype(BF16))


def _rglru_kernel(x_ref, g_ref, win_ref, cw_ref, cb_ref, wa_ref, ba_ref, wx_ref, bx_ref,
                  lam_ref, wout_ref, o_ref, ext_ref, h_ref, a_scr, b_scr, hs_scr):
    ts = x_ref.shape[1]
    w = LRU_HEADS * LRU_BLOCK

    @pl.when(pl.program_id(1) == 0)
    def _():
        ext_ref[...] = jnp.zeros_like(ext_ref)
        h_ref[...] = jnp.zeros_like(h_ref)

    x = x_ref[0]
    xn = _rms_rows(x, g_ref[...]).astype(BF16)
    gu = _dot(xn, win_ref[...])
    gate = gu[:, :w]
    u_raw = gu[:, w:]
    ext_ref[0:SUBLANES, :] = ext_ref[ts:ts + SUBLANES, :]
    ext_ref[SUBLANES:, :] = u_raw
    u = cb_ref[...] + jnp.zeros((ts, w), F32)
    for k in range(CONV_WIDTH):
        off = SUBLANES - (CONV_WIDTH - 1) + k
        u = u + cw_ref[k:k + 1, :] * ext_ref[off:off + ts, :]
    ub = u.astype(BF16)

    def block_diag(wref):
        return jnp.concatenate(
            [_dot(ub[:, hh * LRU_BLOCK:(hh + 1) * LRU_BLOCK], wref[hh]) for hh in range(LRU_HEADS)],
            axis=1)

    r = jax.nn.sigmoid(block_diag(wa_ref) + ba_ref[...])
    i = jax.nn.sigmoid(block_diag(wx_ref) + bx_ref[...])
    z = -lam_ref[...]
    softplus = jnp.maximum(z, 0.0) + jnp.log1p(jnp.exp(-jnp.abs(z)))
    log_a = (-LRU_C) * r * softplus
    a_scr[...] = jnp.exp(log_a)
    th = jnp.tanh(log_a)
    b_scr[...] = jnp.sqrt(2.0 * th / (th - 1.0)) * (i * u)

    def step(t, h):
        h = a_scr[pl.ds(t, 1), :] * h + b_scr[pl.ds(t, 1), :]
        hs_scr[pl.ds(t, 1), :] = h
        return h

    h_ref[0:1, :] = lax.fori_loop(0, ts, step, h_ref[0:1, :], unroll=8)
    y = jax.nn.gelu(gate) * hs_scr[...]
    o_ref[0] = x + _dot(y.astype(BF16), wout_ref[...])


def _rglru(x, norm_g, w_in, conv_w, conv_b, w_a, b_a, w_x, b_x, lam, w_out):
    b, s, d = x.shape
    w = LRU_HEADS * LRU_BLOCK
    row = lambda v: v.reshape(1, -1)
    whole = pl.BlockSpec(memory_space=pltpu.VMEM)
    return pl.pallas_call(
        _rglru_kernel,
        grid=(b, s // LRU_TILE),
        in_specs=[pl.BlockSpec((1, LRU_TILE, d), lambda bi, si: (bi, si, 0))] + [whole] * 10,
        out_specs=pl.BlockSpec((1, LRU_TILE, d), lambda bi, si: (bi, si, 0)),
        out_shape=jax.ShapeDtypeStruct((b, s, d), F32),
        scratch_shapes=[pltpu.VMEM((LRU_TILE + SUBLANES, w), F32), pltpu.VMEM((SUBLANES, w), F32),
                        pltpu.VMEM((LRU_TILE, w), F32), pltpu.VMEM((LRU_TILE, w), F32),
                        pltpu.VMEM((LRU_TILE, w), F32)],
        compiler_params=pltpu.CompilerParams(
            dimension_semantics=("arbitrary", "arbitrary"), vmem_limit_bytes=48 * MIB),
        name="rglru_mixer",
    )(x, row(norm_g), w_in.astype(BF16), conv_w, row(conv_b), w_a.astype(BF16), row(b_a),
      w_x.astype(BF16), row(b_x), row(lam), w_out.astype(BF16))


def _peer_scores_kernel(x_ref, g_ref, wq_ref, sk_ref, xn_ref, st_ref):
    xn = _rms_rows(x_ref[...], g_ref[...])
    xn_ref[...] = xn
    q = _dot(xn.astype(BF16), wq_ref[...]).astype(BF16)
    for hp in range(2 * PEER_HEADS):
        st_ref[hp] = _dot_nt(sk_ref[hp % 2], q[:, hp * D_HALF:(hp + 1) * D_HALF])


def _peer_scores(x2d, norm_g, w_q, sub_keys):
    t, d = x2d.shape
    nq = w_q.shape[1]
    whole = pl.BlockSpec(memory_space=pltpu.VMEM)
    return pl.pallas_call(
        _peer_scores_kernel,
        grid=(t // ROW_TILE,),
        in_specs=[pl.BlockSpec((ROW_TILE, d), lambda i: (i, 0)), whole, whole, whole],
        out_specs=[pl.BlockSpec((ROW_TILE, d), lambda i: (i, 0)),
                   pl.BlockSpec((2 * PEER_HEADS, N_KEYS, ROW_TILE), lambda i: (0, 0, i))],
        out_shape=[jax.ShapeDtypeStruct((t, d), F32),
                   jax.ShapeDtypeStruct((2 * PEER_HEADS, N_KEYS, t), F32)],
        compiler_params=pltpu.CompilerParams(
            dimension_semantics=("parallel",), vmem_limit_bytes=48 * MIB),
        name="peer_scores",
    )(x2d, norm_g.reshape(1, d), w_q.astype(BF16), sub_keys.astype(BF16))


def _candidate_blocks():
    blocks = []
    for i in range(PEER_TOPK // 2):
        nj = PEER_TOPK // (i + 1)
        for j0 in range(0, nj, SUBLANES):
            blocks.append(("row", i, j0, min(SUBLANES, nj - j0)))
    blocks.append(("col", PEER_TOPK // 2, PEER_TOPK // 2))
    return blocks


def _peer_topk_kernel(st_ref, r8_ref, p16_ref, g_ref):
    tk = st_ref.shape[2]
    key_iota = lax.broadcasted_iota(I32, (N_KEYS, tk), 0)
    row16 = lax.broadcasted_iota(I32, (PEER_TOPK, tk), 0)
    row8 = lax.broadcasted_iota(I32, (SUBLANES, tk), 0)
    neg_inf = jnp.float32(-jnp.inf)
    blocks = _candidate_blocks()
    big = jnp.int32(PEER_TOPK * PEER_TOPK)

    def top16(x):
        vals = jnp.zeros((PEER_TOPK, tk), F32)
        idxs = jnp.zeros((PEER_TOPK, tk), I32)
        for k in range(PEER_TOPK):
            m = jnp.max(x, axis=0, keepdims=True)
            idx = jnp.min(jnp.where(x == m, key_iota, N_KEYS), axis=0, keepdims=True)
            x = jnp.where(key_iota == idx, neg_inf, x)
            vals = jnp.where(row16 == k, m, vals)
            idxs = jnp.where(row16 == k, idx, idxs)
        return vals, idxs

    for h in range(PEER_HEADS):
        s0, i0 = top16(st_ref[2 * h])
        s1, i1 = top16(st_ref[2 * h + 1])
        cand, flat, expert = [], [], []
        for blk in blocks:
            if blk[0] == "row":
                _, i, j0, n = blk
                c = s0[i:i + 1, :] + s1[j0:j0 + SUBLANES, :]
                f = PEER_TOPK * i + j0 + row8
                e = i0[i:i + 1, :] * N_KEYS + i1[j0:j0 + SUBLANES, :]
            else:
                _, i_start, n = blk
                c = s0[i_start:i_start + SUBLANES, :] + s1[0:1, :]
                f = PEER_TOPK * (i_start + row8)
                e = i0[i_start:i_start + SUBLANES, :] * N_KEYS + i1[0:1, :]
            cand.append(jnp.where(row8 < n, c, neg_inf))
            flat.append(f)
            expert.append(e)
        cand = jnp.concatenate(cand, axis=0)
        flat = jnp.concatenate(flat, axis=0)
        expert = jnp.concatenate(expert, axis=0)
        best_s = jnp.zeros((PEER_TOPK, tk), F32)
        best_e = jnp.zeros((PEER_TOPK, tk), I32)
        for k in range(PEER_TOPK):
            m = jnp.max(cand, axis=0, keepdims=True)
            fidx = jnp.min(jnp.where(cand == m, flat, big), axis=0, keepdims=True)
            pick = flat == fidx
            e_sel = jnp.max(jnp.where(pick, expert, -1), axis=0, keepdims=True)
            cand = jnp.where(pick, neg_inf, cand)
            best_s = jnp.where(row16 == k, m, best_s)
            best_e = jnp.where(row16 == k, e_sel, best_e)
        ex = jnp.exp(best_s - best_s[0:1, :])
        gate = ex / jnp.sum(ex, axis=0, keepdims=True)
        rows = slice(h * PEER_TOPK, (h + 1) * PEER_TOPK)
        r8_ref[0, rows, :] = (best_e >> 1) * SUBLANES
        p16_ref[0, rows, :] = (best_e & 1) * (2 * SUBLANES)
        g_ref[0, rows, :] = gate


def _peer_topk(scores_t):
    t = scores_t.shape[2]
    nblk = t // PEER_TOKENS
    out_block = pl.BlockSpec((1, N_SLOTS, PEER_TOKENS), lambda i: (i, 0, 0))
    return pl.pallas_call(
        _peer_topk_kernel,
        grid=(nblk,),
        in_specs=[pl.BlockSpec((2 * PEER_HEADS, N_KEYS, PEER_TOKENS), lambda i: (0, 0, i))],
        out_specs=[out_block, out_block, out_block],
        out_shape=[jax.ShapeDtypeStruct((nblk, N_SLOTS, PEER_TOKENS), I32),
                   jax.ShapeDtypeStruct((nblk, N_SLOTS, PEER_TOKENS), I32),
                   jax.ShapeDtypeStruct((nblk, N_SLOTS, PEER_TOKENS), F32)],
        compiler_params=pltpu.CompilerParams(dimension_semantics=("parallel",)),
        name="peer_topk",
    )(scores_t)


def _pack_table(tab):
    e, d = tab.shape
    bits = lax.bitcast_convert_type(tab.astype(BF16), jnp.uint16).astype(U32)
    bits = bits.reshape(e, 2, d // (2 * LANES), LANES)
    words = bits[:, 0] | (bits[:, 1] << 16)
    return words.reshape(e * d // (2 * LANES), LANES)


def _unpack_words(w):
    lo = lax.bitcast_convert_type(w << 16, F32)
    hi = lax.bitcast_convert_type(w & jnp.uint32(0xFFFF0000), F32)
    return lo, hi


def _sublane_iota():
    return lax.broadcasted_iota(I32, (SUBLANES, LANES), 0)


def _merge(a, b, shift, mask):
    ta = a + pltpu.roll(a, shift, 0)
    tb = b + pltpu.roll(b, SUBLANES - shift, 0)
    return jnp.where(mask, ta, tb)


def _reduce8(q, sub):
    m4 = sub >= 4
    m2 = (sub & 2) != 0
    m1 = (sub & 1) != 0
    c = _merge(q[7], q[3], 4, m4)
    d = _merge(q[5], q[1], 4, m4)
    c2 = _merge(q[6], q[2], 4, m4)
    d2 = _merge(q[4], q[0], 4, m4)
    e = _merge(c, d, 2, m2)
    f = _merge(c2, d2, 2, m2)
    return _merge(e, f, 1, m1)


def _peer_u_kernel(r8_ref, p16_ref, xn_ref, g_ref, tab_ref, a_ref, xv_ref):
    tb = xn_ref.shape[0]
    sub = _sublane_iota()
    lane_t = lax.broadcasted_iota(I32, (N_SLOTS, tb), 1)
    zero4 = jnp.zeros((SUBLANES // 2, LANES), F32)

    def token(t, acc):
        xr = xn_ref[t]
        lo4 = xr[0:4, :]
        hi4 = xr[4:8, :]
        xv_ref[0:8, :] = jnp.concatenate([lo4, zero4], axis=0)
        xv_ref[8:16, :] = jnp.concatenate([hi4, zero4], axis=0)
        xv_ref[16:24, :] = jnp.concatenate([zero4, lo4], axis=0)
        xv_ref[24:32, :] = jnp.concatenate([zero4, hi4], axis=0)
        cols = []
        for grp in range(N_SLOTS // SUBLANES):
            prods = []
            for jj in range(SUBLANES):
                j = grp * SUBLANES + jj
                r8 = pl.multiple_of(r8_ref[0, j, t], SUBLANES)
                p16 = pl.multiple_of(p16_ref[0, j, t], 2 * SUBLANES)
                lo, hi = _unpack_words(tab_ref[pl.ds(r8, SUBLANES), :])
                xl = xv_ref[pl.ds(p16, SUBLANES), :]
                xh = xv_ref[pl.ds(p16 + SUBLANES, SUBLANES), :]
                prods.append(lo * xl + hi * xh)
            cols.append(jnp.sum(_reduce8(prods, sub), axis=1, keepdims=True))
        col = jnp.concatenate(cols, axis=0)
        return jnp.where(lane_t == t, col, acc)

    s = lax.fori_loop(0, tb, token, jnp.zeros((N_SLOTS, tb), F32))
    a_ref[0] = g_ref[0] * jax.nn.gelu(s)


def _peer_v_kernel(r8_ref, p16_ref, a_ref, x_ref, tab_ref, hm_ref, o_ref):
    tb = x_ref.shape[0]
    sub = _sublane_iota()
    n_acc = 4

    def token(t, carry):
        acc_lo = [jnp.zeros((SUBLANES, LANES), F32) for _ in range(n_acc)]
        acc_hi = [jnp.zeros((SUBLANES, LANES), F32) for _ in range(n_acc)]
        for j in range(N_SLOTS):
            r8 = pl.multiple_of(r8_ref[0, j, t], SUBLANES)
            p16 = pl.multiple_of(p16_ref[0, j, t], 2 * SUBLANES)
            lo, hi = _unpack_words(tab_ref[pl.ds(r8, SUBLANES), :])
            am = jnp.full((SUBLANES, LANES), a_ref[0, j, t], F32) * hm_ref[pl.ds(p16, SUBLANES), :]
            acc_lo[j % n_acc] = acc_lo[j % n_acc] + lo * am
            acc_hi[j % n_acc] = acc_hi[j % n_acc] + hi * am
        lo_t = (acc_lo[0] + acc_lo[1]) + (acc_lo[2] + acc_lo[3])
        hi_t = (acc_hi[0] + acc_hi[1]) + (acc_hi[2] + acc_hi[3])
        lo_t = lo_t + pltpu.roll(lo_t, SUBLANES // 2, 0)
        hi_t = hi_t + pltpu.roll(hi_t, SUBLANES // 2, 0)
        o_ref[t] = x_ref[t] + jnp.where(sub < SUBLANES // 2, lo_t, hi_t)
        return carry

    lax.fori_loop(0, tb, token, 0)


def _peer_experts(x2d, xn, r8, p16, gates, u_tab, v_tab):
    t, d = x2d.shape
    nblk = t // PEER_TOKENS
    rows = d // LANES
    smem_block = pl.BlockSpec((1, N_SLOTS, PEER_TOKENS), lambda i: (i, 0, 0), memory_space=pltpu.SMEM)
    vmem_block = pl.BlockSpec((1, N_SLOTS, PEER_TOKENS), lambda i: (i, 0, 0))
    tok_block = pl.BlockSpec((PEER_TOKENS, rows, LANES), lambda i: (i, 0, 0))
    whole = pl.BlockSpec(memory_space=pltpu.VMEM)
    params = pltpu.CompilerParams(dimension_semantics=("parallel",), vmem_limit_bytes=56 * MIB)

    act = pl.pallas_call(
        _peer_u_kernel,
        grid=(nblk,),
        in_specs=[smem_block, smem_block, tok_block, vmem_block, whole],
        out_specs=vmem_block,
        out_shape=jax.ShapeDtypeStruct((nblk, N_SLOTS, PEER_TOKENS), F32),
        scratch_shapes=[pltpu.VMEM((4 * SUBLANES, LANES), F32)],
        compiler_params=params,
        name="peer_expert_in",
    )(r8, p16, xn.reshape(t, rows, LANES), gates, _pack_table(u_tab))

    half_mask = np.zeros((4 * SUBLANES, LANES), np.float32)
    half_mask[0:SUBLANES // 2] = 1.0
    half_mask[2 * SUBLANES + SUBLANES // 2:3 * SUBLANES] = 1.0
    out = pl.pallas_call(
        _peer_v_kernel,
        grid=(nblk,),
        in_specs=[smem_block, smem_block, smem_block, tok_block, whole, whole],
        out_specs=tok_block,
        out_shape=jax.ShapeDtypeStruct((t, rows, LANES), F32),
        compiler_params=params,
        name="peer_expert_out",
    )(r8, p16, act, x2d.reshape(t, rows, LANES), _pack_table(v_tab), jnp.asarray(half_mask))
    return out.reshape(t, d)


def _peer_ffn(x2d, norm_g, w_q, sub_keys, u_tab, v_tab):
    xn, scores_t = _peer_scores(x2d, norm_g, w_q, sub_keys)
    r8, p16, gates = _peer_topk(scores_t)
    return _peer_experts(x2d, xn, r8, p16, gates, u_tab, v_tab)


def kernel(x, attn_norm_g, attn_w_qkv, attn_q_g, attn_k_g, attn_rel_bias, attn_w_out, rec_norm_g, rec_w_in, rec_conv_w, rec_conv_b, rec_w_a, rec_b_a, rec_w_x, rec_b_x, rec_lambda, rec_w_out, ffn_norm_g, peer_w_q, peer_sub_keys, peer_u, peer_v):
    b, s, d = x.shape
    depth = ffn_norm_g.shape[0]
    for layer in range(depth):
        j = layer // 2
        if layer % 2 == 0:
            qkv = _qkv_proj(x.reshape(b * s, d), attn_norm_g[j], attn_w_qkv[j], attn_q_g[j], attn_k_g[j])
            x = _attention(x, qkv, attn_rel_bias[j], attn_w_out[j])
        else:
            x = _rglru(x, rec_norm_g[j], rec_w_in[j], rec_conv_w[j], rec_conv_b[j], rec_w_a[j],
                       rec_b_a[j], rec_w_x[j], rec_b_x[j], rec_lambda[j], rec_w_out[j])
        x = _peer_ffn(x.reshape(b * s, d), ffn_norm_g[layer], peer_w_q[layer], peer_sub_keys[layer],
                      peer_u[layer], peer_v[layer]).reshape(b, s, d)
    return x
```

```python
import functools

import numpy as np
import jax
import jax.numpy as jnp
from jax import lax
from jax.experimental import pallas as pl
from jax.experimental.pallas import tpu as pltpu

F32 = jnp.float32
BF16 = jnp.bfloat16
U32 = jnp.uint32
I32 = jnp.int32

EPS = 1e-6
NEG_INF = -1e30

D_MODEL = 1024
CHUNK = 64
ATTN_HEADS = 16
HEAD_DIM = 64
LEFT_CHUNKS = 8
REL_CLIP = 256
LRU_HEADS = 4
LRU_BLOCK = 256
CONV_WIDTH = 4
LRU_C = 8.0
PEER_HEADS = 8
N_KEYS = 128
PEER_TOPK = 16
D_HALF = 128

SUBLANES = 8
LANES = 128

Q_TILE = 4 * CHUNK
N_KEY_BLOCKS = 3
ROW_TILE = 512
PEER_TOKENS = 128
LRU_TILE = 256
N_SLOTS = PEER_HEADS * PEER_TOPK

MIB = 1024 * 1024


def _rms_rows(x, g):
    ms = jnp.mean(x * x, axis=-1, keepdims=True)
    return (x * lax.rsqrt(ms + EPS)) * g


def _split_bf16(v):
    hi = v.astype(BF16)
    lo = (v - hi.astype(F32)).astype(BF16)
    return hi, lo


def _dot(a, b):
    return jnp.dot(a, b, preferred_element_type=F32)


def _dot_nt(a, b):
    return lax.dot_general(a, b, (((1,), (1,)), ((), ())), preferred_element_type=F32)


def _qkv_kernel(x_ref, g_ref, w_ref, gain_ref, bsel_ref, bexp_ref, o_ref, xn_ref):
    j = pl.program_id(1)

    @pl.when(j == 0)
    def _():
        xn_ref[...] = _rms_rows(x_ref[...], g_ref[...]).astype(BF16)

    y = _dot(xn_ref[...], w_ref[...])

    @pl.when(j < 2)
    def _():
        hi, lo = _split_bf16(y * y)
        ms = _dot(hi, bsel_ref[...]) + _dot(lo, bsel_ref[...])
        rhi, rlo = _split_bf16(lax.rsqrt(ms + EPS))
        rs = _dot(rhi, bexp_ref[...]) + _dot(rlo, bexp_ref[...])
        o_ref[...] = ((y * rs) * gain_ref[0]).astype(BF16)

    @pl.when(j == 2)
    def _():
        o_ref[...] = y.astype(BF16)


def _qkv_proj(x2d, norm_g, w_qkv, q_g, k_g):
    t, d = x2d.shape
    scale = HEAD_DIM ** -0.5
    gains = jnp.stack([jnp.tile(q_g, ATTN_HEADS) * scale, jnp.tile(k_g, ATTN_HEADS),
                       jnp.ones((d,), F32)]).reshape(3, 1, d)
    head_of_col = np.arange(d) // HEAD_DIM
    bsel = (head_of_col[:, None] == np.arange(LANES)[None, :]).astype(np.float32) / HEAD_DIM
    bexp = (np.arange(LANES)[:, None] == head_of_col[None, :]).astype(np.float32)
    return pl.pallas_call(
        _qkv_kernel,
        grid=(t // ROW_TILE, 3),
        in_specs=[
            pl.BlockSpec((ROW_TILE, d), lambda i, j: (i, 0)),
            pl.BlockSpec((1, d), lambda i, j: (0, 0)),
            pl.BlockSpec((d, d), lambda i, j: (0, j)),
            pl.BlockSpec((1, 1, d), lambda i, j: (j, 0, 0)),
            pl.BlockSpec((d, LANES), lambda i, j: (0, 0)),
            pl.BlockSpec((LANES, d), lambda i, j: (0, 0)),
        ],
        out_specs=pl.BlockSpec((ROW_TILE, d), lambda i, j: (i, j)),
        out_shape=jax.ShapeDtypeStruct((t, 3 * d), BF16),
        scratch_shapes=[pltpu.VMEM((ROW_TILE, d), BF16)],
        compiler_params=pltpu.CompilerParams(
            dimension_semantics=("parallel", "arbitrary"), vmem_limit_bytes=40 * MIB),
        name="qkv_proj",
    )(x2d, norm_g.reshape(1, d), w_qkv.astype(BF16), gains,
      jnp.asarray(bsel, BF16), jnp.asarray(bexp, BF16))


def _attn_kernel(x_ref, q_ref, k0_ref, k1_ref, k2_ref, v0_ref, v1_ref, v2_ref,
                 bias_ref, wout_ref, o_ref, att_ref):
    qt = pl.program_id(1)
    k_refs = (k0_ref, k1_ref, k2_ref)
    v_refs = (v0_ref, v1_ref, v2_ref)
    negs = [jnp.where(qt - (N_KEY_BLOCKS - 1) + kb < 0, NEG_INF, 0.0).astype(F32)
            for kb in range(N_KEY_BLOCKS)]
    lane = lax.broadcasted_iota(I32, (Q_TILE, LANES), 1)
    first_half = lane < HEAD_DIM
    for p in range(ATTN_HEADS // 2):
        cols = slice(p * LANES, (p + 1) * LANES)
        qp = q_ref[0, :, cols]
        outs = []
        for hh in range(2):
            h = 2 * p + hh
            keep = first_half if hh == 0 else jnp.logical_not(first_half)
            qm = jnp.where(keep, qp, jnp.zeros_like(qp))
            s = [_dot_nt(qm, k_refs[kb][0, :, cols]) + bias_ref[kb, h] + negs[kb]
                 for kb in range(N_KEY_BLOCKS)]
            m = jnp.max(s[0], axis=-1, keepdims=True)
            for kb in range(1, N_KEY_BLOCKS):
                m = jnp.maximum(m, jnp.max(s[kb], axis=-1, keepdims=True))
            acc = jnp.zeros((Q_TILE, LANES), F32)
            l = jnp.zeros((Q_TILE, 1), F32)
            for kb in range(N_KEY_BLOCKS):
                e = jnp.exp(s[kb] - m)
                l = l + jnp.sum(e, axis=-1, keepdims=True)
                acc = acc + _dot(e.astype(BF16), v_refs[kb][0, :, cols])
            outs.append(acc / l)
        att_ref[:, cols] = jnp.where(first_half, outs[0], outs[1]).astype(BF16)
    o_ref[0] = x_ref[0] + _dot(att_ref[...], wout_ref[...])


def _band_bias(table):
    qi = np.arange(Q_TILE)[:, None]
    kj = np.arange(Q_TILE)[None, :]
    period = 2 * Q_TILE
    m = np.arange(period)
    delta = np.where(m < Q_TILE, m, m - period)
    blocks = []
    for kb in range(N_KEY_BLOCKS):
        dist = (N_KEY_BLOCKS - 1 - kb) * Q_TILE - delta
        diag = table[:, np.clip(dist, -REL_CLIP, REL_CLIP) + REL_CLIP]
        toep = jnp.tile(diag, (1, Q_TILE))[:, :Q_TILE * (period - 1)]
        toep = toep.reshape(-1, Q_TILE, period - 1)[:, :, :Q_TILE]
        dchunk = (Q_TILE // CHUNK) * (kb - (N_KEY_BLOCKS - 1)) + kj // CHUNK - qi // CHUNK
        valid = (dchunk >= -LEFT_CHUNKS) & (dchunk <= 0)
        blocks.append(jnp.where(valid[None], toep, NEG_INF))
    return jnp.stack(blocks).astype(F32)


def _attention(x, qkv, rel_bias_table, w_out):
    b, s, d = x.shape
    nq = s // Q_TILE
    qkv3 = qkv.reshape(b, s, 3 * d)
    bias = _band_bias(rel_bias_table)

    def kv_spec(kb, col):
        return pl.BlockSpec(
            (1, Q_TILE, d),
            lambda bi, qt: (bi, jnp.maximum(qt - (N_KEY_BLOCKS - 1) + kb, 0), col))

    in_specs = [pl.BlockSpec((1, Q_TILE, d), lambda bi, qt: (bi, qt, 0)),
                pl.BlockSpec((1, Q_TILE, d), lambda bi, qt: (bi, qt, 0))]
    in_specs += [kv_spec(kb, 1) for kb in range(N_KEY_BLOCKS)]
    in_specs += [kv_spec(kb, 2) for kb in range(N_KEY_BLOCKS)]
    in_specs += [pl.BlockSpec(memory_space=pltpu.VMEM), pl.BlockSpec(memory_space=pltpu.VMEM)]
    return pl.pallas_call(
        _attn_kernel,
        grid=(b, nq),
        in_specs=in_specs,
        out_specs=pl.BlockSpec((1, Q_TILE, d), lambda bi, qt: (bi, qt, 0)),
        out_shape=jax.ShapeDtypeStruct((b, s, d), F32),
        scratch_shapes=[pltpu.VMEM((Q_TILE, d), BF16)],
        compiler_params=pltpu.CompilerParams(
            dimension_semantics=("parallel", "parallel"), vmem_limit_bytes=48 * MIB),
        name="band_attention",
    )(x, qkv3, qkv3, qkv3, qkv3, qkv3, qkv3, qkv3, bias, w_out.astype(BF16))


def _rglru_kernel(x_ref, g_ref, win_ref, cw_ref, cb_ref, wa_ref, ba_ref, wx_ref, bx_ref,
                  lam_ref, wout_ref, o_ref, ext_ref, h_ref, a_scr, b_scr, hs_scr):
    ts = x_ref.shape[1]
    w = LRU_HEADS * LRU_BLOCK

    @pl.when(pl.program_id(1) == 0)
    def _():
        ext_ref[...] = jnp.zeros_like(ext_ref)
        h_ref[...] = jnp.zeros_like(h_ref)

    x = x_ref[0]
    xn = _rms_rows(x, g_ref[...]).astype(BF16)
    gu = _dot(xn, win_ref[...])
    gate = gu[:, :w]
    u_raw = gu[:, w:]
    ext_ref[0:SUBLANES, :] = ext_ref[ts:ts + SUBLANES, :]
    ext_ref[SUBLANES:, :] = u_raw
    u = cb_ref[...] + jnp.zeros((ts, w), F32)
    for k in range(CONV_WIDTH):
        off = SUBLANES - (CONV_WIDTH - 1) + k
        u = u + cw_ref[k:k + 1, :] * ext_ref[off:off + ts, :]
    ub = u.astype(BF16)

    def block_diag(wref):
        return jnp.concatenate(
            [_dot(ub[:, hh * LRU_BLOCK:(hh + 1) * LRU_BLOCK], wref[hh]) for hh in range(LRU_HEADS)],
            axis=1)

    r = jax.nn.sigmoid(block_diag(wa_ref) + ba_ref[...])
    i = jax.nn.sigmoid(block_diag(wx_ref) + bx_ref[...])
    z = -lam_ref[...]
    softplus = jnp.maximum(z, 0.0) + jnp.log1p(jnp.exp(-jnp.abs(z)))
    log_a = (-LRU_C) * r * softplus
    a_scr[...] = jnp.exp(log_a)
    th = jnp.tanh(log_a)
    b_scr[...] = jnp.sqrt(2.0 * th / (th - 1.0)) * (i * u)

    def step(t, h):
        h = a_scr[pl.ds(t, 1), :] * h + b_scr[pl.ds(t, 1), :]
        hs_scr[pl.ds(t, 1), :] = h
        return h

    h_ref[0:1, :] = lax.fori_loop(0, ts, step, h_ref[0:1, :], unroll=8)
    y = jax.nn.gelu(gate) * hs_scr[...]
    o_ref[0] = x + _dot(y.astype(BF16), wout_ref[...])


def _rglru(x, norm_g, w_in, conv_w, conv_b, w_a, b_a, w_x, b_x, lam, w_out):
    b, s, d = x.shape
    w = LRU_HEADS * LRU_BLOCK
    row = lambda v: v.reshape(1, -1)
    whole = pl.BlockSpec(memory_space=pltpu.VMEM)
    return pl.pallas_call(
        _rglru_kernel,
        grid=(b, s // LRU_TILE),
        in_specs=[pl.BlockSpec((1, LRU_TILE, d), lambda bi, si: (bi, si, 0))] + [whole] * 10,
        out_specs=pl.BlockSpec((1, LRU_TILE, d), lambda bi, si: (bi, si, 0)),
        out_shape=jax.ShapeDtypeStruct((b, s, d), F32),
        scratch_shapes=[pltpu.VMEM((LRU_TILE + SUBLANES, w), F32), pltpu.VMEM((SUBLANES, w), F32),
                        pltpu.VMEM((LRU_TILE, w), F32), pltpu.VMEM((LRU_TILE, w), F32),
                        pltpu.VMEM((LRU_TILE, w), F32)],
        compiler_params=pltpu.CompilerParams(
            dimension_semantics=("arbitrary", "arbitrary"), vmem_limit_bytes=48 * MIB),
        name="rglru_mixer",
    )(x, row(norm_g), w_in.astype(BF16), conv_w, row(conv_b), w_a.astype(BF16), row(b_a),
      w_x.astype(BF16), row(b_x), row(lam), w_out.astype(BF16))


def _peer_scores_kernel(x_ref, g_ref, wq_ref, sk_ref, xn_ref, st_ref):
    xn = _rms_rows(x_ref[...], g_ref[...])
    xn_ref[...] = xn
    q = _dot(xn.astype(BF16), wq_ref[...]).astype(BF16)
    for hp in range(2 * PEER_HEADS):
        st_ref[hp] = _dot_nt(sk_ref[hp % 2], q[:, hp * D_HALF:(hp + 1) * D_HALF])


def _peer_scores(x2d, norm_g, w_q, sub_keys):
    t, d = x2d.shape
    nq = w_q.shape[1]
    whole = pl.BlockSpec(memory_space=pltpu.VMEM)
    return pl.pallas_call(
        _peer_scores_kernel,
        grid=(t // ROW_TILE,),
        in_specs=[pl.BlockSpec((ROW_TILE, d), lambda i: (i, 0)), whole, whole, whole],
        out_specs=[pl.BlockSpec((ROW_TILE, d), lambda i: (i, 0)),
                   pl.BlockSpec((2 * PEER_HEADS, N_KEYS, ROW_TILE), lambda i: (0, 0, i))],
        out_shape=[jax.ShapeDtypeStruct((t, d), F32),
                   jax.ShapeDtypeStruct((2 * PEER_HEADS, N_KEYS, t), F32)],
        compiler_params=pltpu.CompilerParams(
            dimension_semantics=("parallel",), vmem_limit_bytes=48 * MIB),
        name="peer_scores",
    )(x2d, norm_g.reshape(1, d), w_q.astype(BF16), sub_keys.astype(BF16))


def _candidate_blocks():
    blocks = []
    for i in range(PEER_TOPK // 2):
        nj = PEER_TOPK // (i + 1)
        for j0 in range(0, nj, SUBLANES):
            blocks.append(("row", i, j0, min(SUBLANES, nj - j0)))
    blocks.append(("col", PEER_TOPK // 2, PEER_TOPK // 2))
    return blocks


def _peer_topk_kernel(st_ref, r8_ref, par_ref, g_ref, r8t_ref):
    tk = st_ref.shape[2]
    key_iota = lax.broadcasted_iota(I32, (N_KEYS, tk), 0)
    row16 = lax.broadcasted_iota(I32, (PEER_TOPK, tk), 0)
    row8 = lax.broadcasted_iota(I32, (SUBLANES, tk), 0)
    neg_inf = jnp.float32(-jnp.inf)
    blocks = _candidate_blocks()
    big = jnp.int32(PEER_TOPK * PEER_TOPK)

    def top16(x):
        vals = jnp.zeros((PEER_TOPK, tk), F32)
        idxs = jnp.zeros((PEER_TOPK, tk), I32)
        for k in range(PEER_TOPK):
            m = jnp.max(x, axis=0, keepdims=True)
            idx = jnp.min(jnp.where(x == m, key_iota, N_KEYS), axis=0, keepdims=True)
            x = jnp.where(key_iota == idx, neg_inf, x)
            vals = jnp.where(row16 == k, m, vals)
            idxs = jnp.where(row16 == k, idx, idxs)
        return vals, idxs

    for h in range(PEER_HEADS):
        s0, i0 = top16(st_ref[2 * h])
        s1, i1 = top16(st_ref[2 * h + 1])
        cand, flat, expert = [], [], []
        for blk in blocks:
            if blk[0] == "row":
                _, i, j0, n = blk
                c = s0[i:i + 1, :] + s1[j0:j0 + SUBLANES, :]
                f = PEER_TOPK * i + j0 + row8
                e = i0[i:i + 1, :] * N_KEYS + i1[j0:j0 + SUBLANES, :]
            else:
                _, i_start, n = blk
                c = s0[i_start:i_start + SUBLANES, :] + s1[0:1, :]
                f = PEER_TOPK * (i_start + row8)
                e = i0[i_start:i_start + SUBLANES, :] * N_KEYS + i1[0:1, :]
            cand.append(jnp.where(row8 < n, c, neg_inf))
            flat.append(f)
            expert.append(e)
        cand = jnp.concatenate(cand, axis=0)
        flat = jnp.concatenate(flat, axis=0)
        expert = jnp.concatenate(expert, axis=0)
        best_s = jnp.zeros((PEER_TOPK, tk), F32)
        best_e = jnp.zeros((PEER_TOPK, tk), I32)
        for k in range(PEER_TOPK):
            m = jnp.max(cand, axis=0, keepdims=True)
            fidx = jnp.min(jnp.where(cand == m, flat, big), axis=0, keepdims=True)
            pick = flat == fidx
            e_sel = jnp.max(jnp.where(pick, expert, -1), axis=0, keepdims=True)
            cand = jnp.where(pick, neg_inf, cand)
            best_s = jnp.where(row16 == k, m, best_s)
            best_e = jnp.where(row16 == k, e_sel, best_e)
        ex = jnp.exp(best_s - best_s[0:1, :])
        gate = ex / jnp.sum(ex, axis=0, keepdims=True)
        rows = slice(h * PEER_TOPK, (h + 1) * PEER_TOPK)
        r8t_ref[rows, :] = (best_e >> 1) * SUBLANES
        par_ref[0, rows, :] = best_e & 1
        g_ref[0, rows, :] = gate
    r8_ref[0] = r8t_ref[...].T


def _peer_topk(scores_t):
    t = scores_t.shape[2]
    nblk = t // PEER_TOKENS
    slot_major = pl.BlockSpec((1, N_SLOTS, PEER_TOKENS), lambda i: (i, 0, 0))
    token_major = pl.BlockSpec((1, PEER_TOKENS, N_SLOTS), lambda i: (i, 0, 0))
    return pl.pallas_call(
        _peer_topk_kernel,
        grid=(nblk,),
        in_specs=[pl.BlockSpec((2 * PEER_HEADS, N_KEYS, PEER_TOKENS), lambda i: (0, 0, i))],
        out_specs=[token_major, slot_major, slot_major],
        out_shape=[jax.ShapeDtypeStruct((nblk, PEER_TOKENS, N_SLOTS), I32),
                   jax.ShapeDtypeStruct((nblk, N_SLOTS, PEER_TOKENS), I32),
                   jax.ShapeDtypeStruct((nblk, N_SLOTS, PEER_TOKENS), F32)],
        scratch_shapes=[pltpu.VMEM((N_SLOTS, PEER_TOKENS), I32)],
        compiler_params=pltpu.CompilerParams(dimension_semantics=("parallel",)),
        name="peer_topk",
    )(scores_t)


def _pack_table(tab):
    e, d = tab.shape
    bits = lax.bitcast_convert_type(tab.astype(BF16), jnp.uint16).astype(U32)
    bits = bits.reshape(e, 2, d // (2 * LANES), LANES)
    words = bits[:, 0] | (bits[:, 1] << 16)
    return words.reshape(e * d // (2 * LANES), LANES)


def _unpack_words(w):
    lo = lax.bitcast_convert_type(w << 16, F32)
    hi = lax.bitcast_convert_type(w & jnp.uint32(0xFFFF0000), F32)
    return lo, hi


def _sublane_iota():
    return lax.broadcasted_iota(I32, (SUBLANES, LANES), 0)


def _merge(a, b, shift, mask):
    ta = a + pltpu.roll(a, shift, 0)
    tb = b + pltpu.roll(b, SUBLANES - shift, 0)
    return jnp.where(mask, ta, tb)


GROUP = SUBLANES // 2
N_GROUPS = N_SLOTS // GROUP
N_ROWS = 2 * N_SLOTS


def _dup_halves(v, low):
    r = pltpu.roll(v, GROUP, 0)
    return jnp.where(low, v, r), jnp.where(low, r, v)


def _peer_u_kernel(r8_ref, xn_ref, g_ref, par_ref, tab_ref, a_ref, s_ref):
    tb = xn_ref.shape[0]
    sub = _sublane_iota()
    low = sub < GROUP
    m2 = (sub & 2) != 0
    lane_t = lax.broadcasted_iota(I32, (SUBLANES, tb), 1)

    def token(t, carry):
        xr = xn_ref[t]
        xlo, xhi = _dup_halves(xr, low)
        ids = r8_ref.at[0, t]
        for grp in range(N_GROUPS):
            prods = []
            for i in range(GROUP):
                r8 = pl.multiple_of(ids[grp * GROUP + i], SUBLANES)
                lo, hi = _unpack_words(tab_ref[pl.ds(r8, SUBLANES), :])
                prods.append(lo * xlo + hi * xhi)
            halves = (_merge(prods[3], prods[1], 2, m2), _merge(prods[2], prods[0], 2, m2))
            for k in range(2):
                col = jnp.sum(halves[k], axis=1, keepdims=True)
                rows = slice((2 * grp + k) * SUBLANES, (2 * grp + k + 1) * SUBLANES)
                s_ref[rows, :] = jnp.where(lane_t == t, col, s_ref[rows, :])
        return carry

    s_ref[...] = jnp.zeros_like(s_ref)
    lax.fori_loop(0, tb, token, 0, unroll=2)

    sub_t = lax.broadcasted_iota(I32, (SUBLANES, tb), 0)
    low_t = sub_t < GROUP
    odd_t = (sub_t & 1) != 0
    want = jnp.where(low_t, 0, 1)
    for v in range(N_SLOTS // SUBLANES):
        src = slice(v * SUBLANES, (v + 1) * SUBLANES)
        gates = _dup_halves(g_ref[0, src, :], low_t)
        pars = _dup_halves(par_ref[0, src, :], low_t)
        for half in range(2):
            grp = 2 * v + half
            dots = _merge(s_ref[2 * grp * SUBLANES:(2 * grp + 1) * SUBLANES, :],
                          s_ref[(2 * grp + 1) * SUBLANES:(2 * grp + 2) * SUBLANES, :], 1, odd_t)
            act = gates[half] * jax.nn.gelu(dots)
            rows = slice(grp * SUBLANES, (grp + 1) * SUBLANES)
            a_ref[0, rows, :] = jnp.where(pars[half] == want, act, 0.0)


def _peer_v_kernel(r8_ref, a_ref, x_ref, tab_ref, o_ref, bc_ref):
    tb = x_ref.shape[0]
    sub = _sublane_iota()
    low = sub < GROUP
    lane_t = lax.broadcasted_iota(I32, (SUBLANES, tb), 1)
    n_acc = 4

    def spread(t, buf):
        for k in range(N_ROWS // SUBLANES):
            rows = slice(k * SUBLANES, (k + 1) * SUBLANES)
            col = jnp.sum(jnp.where(lane_t == t, a_ref[0, rows, :], 0.0), axis=1, keepdims=True)
            bc_ref[buf, rows, :] = jnp.broadcast_to(col, (SUBLANES, LANES))

    def accumulate(t, buf):
        acc_lo = [jnp.zeros((SUBLANES, LANES), F32) for _ in range(n_acc)]
        acc_hi = [jnp.zeros((SUBLANES, LANES), F32) for _ in range(n_acc)]
        ids = r8_ref.at[0, t]
        for j in range(N_SLOTS):
            row = (j // GROUP) * SUBLANES + j % GROUP
            r8 = pl.multiple_of(ids[j], SUBLANES)
            lo, hi = _unpack_words(tab_ref[pl.ds(r8, SUBLANES), :])
            am = jnp.where(low, bc_ref[buf, row:row + 1, :], bc_ref[buf, row + GROUP:row + GROUP + 1, :])
            acc_lo[j % n_acc] = acc_lo[j % n_acc] + lo * am
            acc_hi[j % n_acc] = acc_hi[j % n_acc] + hi * am
        lo_t = (acc_lo[0] + acc_lo[1]) + (acc_lo[2] + acc_lo[3])
        hi_t = (acc_hi[0] + acc_hi[1]) + (acc_hi[2] + acc_hi[3])
        lo_t = lo_t + pltpu.roll(lo_t, GROUP, 0)
        hi_t = hi_t + pltpu.roll(hi_t, GROUP, 0)
        o_ref[t] = x_ref[t] + jnp.where(low, lo_t, hi_t)

    def token_pair(i, carry):
        t = 2 * i
        spread(t + 1, 1)
        accumulate(t, 0)
        spread(jnp.minimum(t + 2, tb - 1), 0)
        accumulate(t + 1, 1)
        return carry

    spread(0, 0)
    lax.fori_loop(0, tb // 2, token_pair, 0)


def _peer_experts(x2d, xn, r8, par, gates, u_tab, v_tab):
    t, d = x2d.shape
    nblk = t // PEER_TOKENS
    rows = d // LANES
    smem_block = pl.BlockSpec((1, PEER_TOKENS, N_SLOTS), lambda i: (i, 0, 0), memory_space=pltpu.SMEM)
    slot_block = pl.BlockSpec((1, N_SLOTS, PEER_TOKENS), lambda i: (i, 0, 0))
    row_block = pl.BlockSpec((1, N_ROWS, PEER_TOKENS), lambda i: (i, 0, 0))
    tok_block = pl.BlockSpec((PEER_TOKENS, rows, LANES), lambda i: (i, 0, 0))
    whole = pl.BlockSpec(memory_space=pltpu.VMEM)
    params = pltpu.CompilerParams(dimension_semantics=("parallel",), vmem_limit_bytes=56 * MIB)

    act = pl.pallas_call(
        _peer_u_kernel,
        grid=(nblk,),
        in_specs=[smem_block, tok_block, slot_block, slot_block, whole],
        out_specs=row_block,
        out_shape=jax.ShapeDtypeStruct((nblk, N_ROWS, PEER_TOKENS), F32),
        scratch_shapes=[pltpu.VMEM((2 * N_ROWS, PEER_TOKENS), F32)],
        compiler_params=params,
        name="peer_expert_in",
    )(r8, xn.reshape(t, rows, LANES), gates, par, _pack_table(u_tab))

    out = pl.pallas_call(
        _peer_v_kernel,
        grid=(nblk,),
        in_specs=[smem_block, row_block, tok_block, whole],
        out_specs=tok_block,
        out_shape=jax.ShapeDtypeStruct((t, rows, LANES), F32),
        scratch_shapes=[pltpu.VMEM((2, N_ROWS, LANES), F32)],
        compiler_params=params,
        name="peer_expert_out",
    )(r8, act, x2d.reshape(t, rows, LANES), _pack_table(v_tab))
    return out.reshape(t, d)


def _peer_ffn(x2d, norm_g, w_q, sub_keys, u_tab, v_tab):
    xn, scores_t = _peer_scores(x2d, norm_g, w_q, sub_keys)
    r8, par, gates = _peer_topk(scores_t)
    return _peer_experts(x2d, xn, r8, par, gates, u_tab, v_tab)


def kernel(x, attn_norm_g, attn_w_qkv, attn_q_g, attn_k_g, attn_rel_bias, attn_w_out, rec_norm_g, rec_w_in, rec_conv_w, rec_conv_b, rec_w_a, rec_b_a, rec_w_x, rec_b_x, rec_lambda, rec_w_out, ffn_norm_g, peer_w_q, peer_sub_keys, peer_u, peer_v):
    b, s, d = x.shape
    depth = ffn_norm_g.shape[0]
    for layer in range(depth):
        j = layer // 2
        if layer % 2 == 0:
            qkv = _qkv_proj(x.reshape(b * s, d), attn_norm_g[j], attn_w_qkv[j], attn_q_g[j], attn_k_g[j])
            x = _attention(x, qkv, attn_rel_bias[j], attn_w_out[j])
        else:
            x = _rglru(x, rec_norm_g[j], rec_w_in[j], rec_conv_w[j], rec_conv_b[j], rec_w_a[j],
                       rec_b_a[j], rec_w_x[j], rec_b_x[j], rec_lambda[j], rec_w_out[j])
        x = _peer_ffn(x.reshape(b * s, d), ffn_norm_g[layer], peer_w_q[layer], peer_sub_keys[layer],
                      peer_u[layer], peer_v[layer]).reshape(b, s, d)
    return x
```

```python
import functools

import numpy as np
import jax
import jax.numpy as jnp
from jax import lax
from jax.experimental import pallas as pl
from jax.experimental.pallas import tpu as pltpu

F32 = jnp.float32
BF16 = jnp.bfloat16
U32 = jnp.uint32
I32 = jnp.int32

EPS = 1e-6
NEG_INF = -1e30

D_MODEL = 1024
CHUNK = 64
ATTN_HEADS = 16
HEAD_DIM = 64
LEFT_CHUNKS = 8
REL_CLIP = 256
LRU_HEADS = 4
LRU_BLOCK = 256
CONV_WIDTH = 4
LRU_C = 8.0
PEER_HEADS = 8
N_KEYS = 128
PEER_TOPK = 16
D_HALF = 128

SUBLANES = 8
LANES = 128

Q_TILE = 4 * CHUNK
N_KEY_BLOCKS = 3
ROW_TILE = 512
PEER_TOKENS = 128
LRU_TILE = 256
N_SLOTS = PEER_HEADS * PEER_TOPK

MIB = 1024 * 1024


def _rms_rows(x, g):
    ms = jnp.mean(x * x, axis=-1, keepdims=True)
    return (x * lax.rsqrt(ms + EPS)) * g


def _split_bf16(v):
    hi = v.astype(BF16)
    lo = (v - hi.astype(F32)).astype(BF16)
    return hi, lo


def _dot(a, b):
    return jnp.dot(a, b, preferred_element_type=F32)


def _dot_nt(a, b):
    return lax.dot_general(a, b, (((1,), (1,)), ((), ())), preferred_element_type=F32)


def _qkv_kernel(x_ref, g_ref, w_ref, gain_ref, bsel_ref, bexp_ref, o_ref, xn_ref):
    j = pl.program_id(1)

    @pl.when(j == 0)
    def _():
        xn_ref[...] = _rms_rows(x_ref[...], g_ref[...]).astype(BF16)

    y = _dot(xn_ref[...], w_ref[...])

    @pl.when(j < 2)
    def _():
        hi, lo = _split_bf16(y * y)
        ms = _dot(hi, bsel_ref[...]) + _dot(lo, bsel_ref[...])
        rhi, rlo = _split_bf16(lax.rsqrt(ms + EPS))
        rs = _dot(rhi, bexp_ref[...]) + _dot(rlo, bexp_ref[...])
        o_ref[...] = ((y * rs) * gain_ref[0]).astype(BF16)

    @pl.when(j == 2)
    def _():
        o_ref[...] = y.astype(BF16)


def _qkv_proj(x2d, norm_g, w_qkv, q_g, k_g):
    t, d = x2d.shape
    scale = HEAD_DIM ** -0.5
    gains = jnp.stack([jnp.tile(q_g, ATTN_HEADS) * scale, jnp.tile(k_g, ATTN_HEADS),
                       jnp.ones((d,), F32)]).reshape(3, 1, d)
    head_of_col = np.arange(d) // HEAD_DIM
    bsel = (head_of_col[:, None] == np.arange(LANES)[None, :]).astype(np.float32) / HEAD_DIM
    bexp = (np.arange(LANES)[:, None] == head_of_col[None, :]).astype(np.float32)
    return pl.pallas_call(
        _qkv_kernel,
        grid=(t // ROW_TILE, 3),
        in_specs=[
            pl.BlockSpec((ROW_TILE, d), lambda i, j: (i, 0)),
            pl.BlockSpec((1, d), lambda i, j: (0, 0)),
            pl.BlockSpec((d, d), lambda i, j: (0, j)),
            pl.BlockSpec((1, 1, d), lambda i, j: (j, 0, 0)),
            pl.BlockSpec((d, LANES), lambda i, j: (0, 0)),
            pl.BlockSpec((LANES, d), lambda i, j: (0, 0)),
        ],
        out_specs=pl.BlockSpec((ROW_TILE, d), lambda i, j: (i, j)),
        out_shape=jax.ShapeDtypeStruct((t, 3 * d), BF16),
        scratch_shapes=[pltpu.VMEM((ROW_TILE, d), BF16)],
        compiler_params=pltpu.CompilerParams(
            dimension_semantics=("parallel", "arbitrary"), vmem_limit_bytes=40 * MIB),
        name="qkv_proj",
    )(x2d, norm_g.reshape(1, d), w_qkv.astype(BF16), gains,
      jnp.asarray(bsel, BF16), jnp.asarray(bexp, BF16))


def _attn_kernel(x_ref, q_ref, k0_ref, k1_ref, k2_ref, v0_ref, v1_ref, v2_ref,
                 bias_ref, wout_ref, o_ref, att_ref):
    qt = pl.program_id(1)
    k_refs = (k0_ref, k1_ref, k2_ref)
    v_refs = (v0_ref, v1_ref, v2_ref)
    negs = [jnp.where(qt - (N_KEY_BLOCKS - 1) + kb < 0, NEG_INF, 0.0).astype(F32)
            for kb in range(N_KEY_BLOCKS)]
    lane = lax.broadcasted_iota(I32, (Q_TILE, LANES), 1)
    first_half = lane < HEAD_DIM
    for p in range(ATTN_HEADS // 2):
        cols = slice(p * LANES, (p + 1) * LANES)
        qp = q_ref[0, :, cols]
        outs = []
        for hh in range(2):
            h = 2 * p + hh
            keep = first_half if hh == 0 else jnp.logical_not(first_half)
            qm = jnp.where(keep, qp, jnp.zeros_like(qp))
            s = [_dot_nt(qm, k_refs[kb][0, :, cols]) + bias_ref[kb, h] + negs[kb]
                 for kb in range(N_KEY_BLOCKS)]
            m = jnp.max(s[0], axis=-1, keepdims=True)
            for kb in range(1, N_KEY_BLOCKS):
                m = jnp.maximum(m, jnp.max(s[kb], axis=-1, keepdims=True))
            acc = jnp.zeros((Q_TILE, LANES), F32)
            l = jnp.zeros((Q_TILE, 1), F32)
            for kb in range(N_KEY_BLOCKS):
                e = jnp.exp(s[kb] - m)
                l = l + jnp.sum(e, axis=-1, keepdims=True)
                acc = acc + _dot(e.astype(BF16), v_refs[kb][0, :, cols])
            outs.append(acc / l)
        att_ref[:, cols] = jnp.where(first_half, outs[0], outs[1]).astype(BF16)
    o_ref[0] = x_ref[0] + _dot(att_ref[...], wout_ref[...])


def _band_bias(table):
    qi = np.arange(Q_TILE)[:, None]
    kj = np.arange(Q_TILE)[None, :]
    period = 2 * Q_TILE
    m = np.arange(period)
    delta = np.where(m < Q_TILE, m, m - period)
    blocks = []
    for kb in range(N_KEY_BLOCKS):
        dist = (N_KEY_BLOCKS - 1 - kb) * Q_TILE - delta
        diag = table[:, np.clip(dist, -REL_CLIP, REL_CLIP) + REL_CLIP]
        toep = jnp.tile(diag, (1, Q_TILE))[:, :Q_TILE * (period - 1)]
        toep = toep.reshape(-1, Q_TILE, period - 1)[:, :, :Q_TILE]
        dchunk = (Q_TILE // CHUNK) * (kb - (N_KEY_BLOCKS - 1)) + kj // CHUNK - qi // CHUNK
        valid = (dchunk >= -LEFT_CHUNKS) & (dchunk <= 0)
        blocks.append(jnp.where(valid[None], toep, NEG_INF))
    return jnp.stack(blocks).astype(F32)


def _attention(x, qkv, rel_bias_table, w_out):
    b, s, d = x.shape
    nq = s // Q_TILE
    qkv3 = qkv.reshape(b, s, 3 * d)
    bias = _band_bias(rel_bias_table)

    def kv_spec(kb, col):
        return pl.BlockSpec(
            (1, Q_TILE, d),
            lambda bi, qt: (bi, jnp.maximum(qt - (N_KEY_BLOCKS - 1) + kb, 0), col))

    in_specs = [pl.BlockSpec((1, Q_TILE, d), lambda bi, qt: (bi, qt, 0)),
                pl.BlockSpec((1, Q_TILE, d), lambda bi, qt: (bi, qt, 0))]
    in_specs += [kv_spec(kb, 1) for kb in range(N_KEY_BLOCKS)]
    in_specs += [kv_spec(kb, 2) for kb in range(N_KEY_BLOCKS)]
    in_specs += [pl.BlockSpec(memory_space=pltpu.VMEM), pl.BlockSpec(memory_space=pltpu.VMEM)]
    return pl.pallas_call(
        _attn_kernel,
        grid=(b, nq),
        in_specs=in_specs,
        out_specs=pl.BlockSpec((1, Q_TILE, d), lambda bi, qt: (bi, qt, 0)),
        out_shape=jax.ShapeDtypeStruct((b, s, d), F32),
        scratch_shapes=[pltpu.VMEM((Q_TILE, d), BF16)],
        compiler_params=pltpu.CompilerParams(
            dimension_semantics=("parallel", "parallel"), vmem_limit_bytes=48 * MIB),
        name="band_attention",
    )(x, qkv3, qkv3, qkv3, qkv3, qkv3, qkv3, qkv3, bias, w_out.astype(BF16))


def _rglru_kernel(x_ref, g_ref, win_ref, cw_ref, cb_ref, wa_ref, ba_ref, wx_ref, bx_ref,
                  lam_ref, wout_ref, o_ref, ext_ref, h_ref, a_scr, b_scr, hs_scr):
    ts = x_ref.shape[1]
    w = LRU_HEADS * LRU_BLOCK

    @pl.when(pl.program_id(1) == 0)
    def _():
        ext_ref[...] = jnp.zeros_like(ext_ref)
        h_ref[...] = jnp.zeros_like(h_ref)

    x = x_ref[0]
    xn = _rms_rows(x, g_ref[...]).astype(BF16)
    gu = _dot(xn, win_ref[...])
    gate = gu[:, :w]
    u_raw = gu[:, w:]
    ext_ref[0:SUBLANES, :] = ext_ref[ts:ts + SUBLANES, :]
    ext_ref[SUBLANES:, :] = u_raw
    u = cb_ref[...] + jnp.zeros((ts, w), F32)
    for k in range(CONV_WIDTH):
        off = SUBLANES - (CONV_WIDTH - 1) + k
        u = u + cw_ref[k:k + 1, :] * ext_ref[off:off + ts, :]
    ub = u.astype(BF16)

    def block_diag(wref):
        return jnp.concatenate(
            [_dot(ub[:, hh * LRU_BLOCK:(hh + 1) * LRU_BLOCK], wref[hh]) for hh in range(LRU_HEADS)],
            axis=1)

    r = jax.nn.sigmoid(block_diag(wa_ref) + ba_ref[...])
    i = jax.nn.sigmoid(block_diag(wx_ref) + bx_ref[...])
    z = -lam_ref[...]
    softplus = jnp.maximum(z, 0.0) + jnp.log1p(jnp.exp(-jnp.abs(z)))
    log_a = (-LRU_C) * r * softplus
    a_scr[...] = jnp.exp(log_a)
    th = jnp.tanh(log_a)
    b_scr[...] = jnp.sqrt(2.0 * th / (th - 1.0)) * (i * u)

    def step(t, h):
        h = a_scr[pl.ds(t, 1), :] * h + b_scr[pl.ds(t, 1), :]
        hs_scr[pl.ds(t, 1), :] = h
        return h

    h_ref[0:1, :] = lax.fori_loop(0, ts, step, h_ref[0:1, :], unroll=8)
    y = jax.nn.gelu(gate) * hs_scr[...]
    o_ref[0] = x + _dot(y.astype(BF16), wout_ref[...])


def _rglru(x, norm_g, w_in, conv_w, conv_b, w_a, b_a, w_x, b_x, lam, w_out):
    b, s, d = x.shape
    w = LRU_HEADS * LRU_BLOCK
    row = lambda v: v.reshape(1, -1)
    whole = pl.BlockSpec(memory_space=pltpu.VMEM)
    return pl.pallas_call(
        _rglru_kernel,
        grid=(b, s // LRU_TILE),
        in_specs=[pl.BlockSpec((1, LRU_TILE, d), lambda bi, si: (bi, si, 0))] + [whole] * 10,
        out_specs=pl.BlockSpec((1, LRU_TILE, d), lambda bi, si: (bi, si, 0)),
        out_shape=jax.ShapeDtypeStruct((b, s, d), F32),
        scratch_shapes=[pltpu.VMEM((LRU_TILE + SUBLANES, w), F32), pltpu.VMEM((SUBLANES, w), F32),
                        pltpu.VMEM((LRU_TILE, w), F32), pltpu.VMEM((LRU_TILE, w), F32),
                        pltpu.VMEM((LRU_TILE, w), F32)],
        compiler_params=pltpu.CompilerParams(
            dimension_semantics=("arbitrary", "arbitrary"), vmem_limit_bytes=48 * MIB),
        name="rglru_mixer",
    )(x, row(norm_g), w_in.astype(BF16), conv_w, row(conv_b), w_a.astype(BF16), row(b_a),
      w_x.astype(BF16), row(b_x), row(lam), w_out.astype(BF16))


def _peer_scores_kernel(x_ref, g_ref, wq_ref, sk_ref, xp_ref, st_ref):
    xb = _rms_rows(x_ref[...], g_ref[...]).astype(BF16)
    bits = lax.bitcast_convert_type(xb.astype(F32), U32)
    half = bits.shape[1] // 2
    words = (bits[:, :half] >> 16) | (bits[:, half:] & jnp.uint32(0xFFFF0000))
    xp_ref[:, :half] = words
    xp_ref[:, half:] = words
    q = _dot(xb, wq_ref[...]).astype(BF16)
    for hp in range(2 * PEER_HEADS):
        st_ref[hp] = _dot_nt(sk_ref[hp % 2], q[:, hp * D_HALF:(hp + 1) * D_HALF])


def _peer_scores(x2d, norm_g, w_q, sub_keys):
    t, d = x2d.shape
    nq = w_q.shape[1]
    whole = pl.BlockSpec(memory_space=pltpu.VMEM)
    return pl.pallas_call(
        _peer_scores_kernel,
        grid=(t // ROW_TILE,),
        in_specs=[pl.BlockSpec((ROW_TILE, d), lambda i: (i, 0)), whole, whole, whole],
        out_specs=[pl.BlockSpec((ROW_TILE, d), lambda i: (i, 0)),
                   pl.BlockSpec((2 * PEER_HEADS, N_KEYS, ROW_TILE), lambda i: (0, 0, i))],
        out_shape=[jax.ShapeDtypeStruct((t, d), U32),
                   jax.ShapeDtypeStruct((2 * PEER_HEADS, N_KEYS, t), F32)],
        compiler_params=pltpu.CompilerParams(
            dimension_semantics=("parallel",), vmem_limit_bytes=48 * MIB),
        name="peer_scores",
    )(x2d, norm_g.reshape(1, d), w_q.astype(BF16), sub_keys.astype(BF16))


def _candidate_blocks():
    blocks = []
    for i in range(PEER_TOPK // 2):
        nj = PEER_TOPK // (i + 1)
        for j0 in range(0, nj, SUBLANES):
            blocks.append(("row", i, j0, min(SUBLANES, nj - j0)))
    blocks.append(("col", PEER_TOPK // 2, PEER_TOPK // 2))
    return blocks


def _peer_topk_kernel(st_ref, r8_ref, par_ref, g_ref, r8t_ref):
    tk = st_ref.shape[2]
    key_iota = lax.broadcasted_iota(I32, (N_KEYS, tk), 0)
    row16 = lax.broadcasted_iota(I32, (PEER_TOPK, tk), 0)
    row8 = lax.broadcasted_iota(I32, (SUBLANES, tk), 0)
    neg_inf = jnp.float32(-jnp.inf)
    blocks = _candidate_blocks()
    big = jnp.int32(PEER_TOPK * PEER_TOPK)

    def top16(x):
        vals = jnp.zeros((PEER_TOPK, tk), F32)
        idxs = jnp.zeros((PEER_TOPK, tk), I32)
        for k in range(PEER_TOPK):
            m = jnp.max(x, axis=0, keepdims=True)
            idx = jnp.min(jnp.where(x == m, key_iota, N_KEYS), axis=0, keepdims=True)
            x = jnp.where(key_iota == idx, neg_inf, x)
            vals = jnp.where(row16 == k, m, vals)
            idxs = jnp.where(row16 == k, idx, idxs)
        return vals, idxs

    for h in range(PEER_HEADS):
        s0, i0 = top16(st_ref[2 * h])
        s1, i1 = top16(st_ref[2 * h + 1])
        cand, flat, expert = [], [], []
        for blk in blocks:
            if blk[0] == "row":
                _, i, j0, n = blk
                c = s0[i:i + 1, :] + s1[j0:j0 + SUBLANES, :]
                f = PEER_TOPK * i + j0 + row8
                e = i0[i:i + 1, :] * N_KEYS + i1[j0:j0 + SUBLANES, :]
            else:
                _, i_start, n = blk
                c = s0[i_start:i_start + SUBLANES, :] + s1[0:1, :]
                f = PEER_TOPK * (i_start + row8)
                e = i0[i_start:i_start + SUBLANES, :] * N_KEYS + i1[0:1, :]
            cand.append(jnp.where(row8 < n, c, neg_inf))
            flat.append(f)
            expert.append(e)
        cand = jnp.concatenate(cand, axis=0)
        flat = jnp.concatenate(flat, axis=0)
        expert = jnp.concatenate(expert, axis=0)
        best_s = jnp.zeros((PEER_TOPK, tk), F32)
        best_e = jnp.zeros((PEER_TOPK, tk), I32)
        for k in range(PEER_TOPK):
            m = jnp.max(cand, axis=0, keepdims=True)
            fidx = jnp.min(jnp.where(cand == m, flat, big), axis=0, keepdims=True)
            pick = flat == fidx
            e_sel = jnp.max(jnp.where(pick, expert, -1), axis=0, keepdims=True)
            cand = jnp.where(pick, neg_inf, cand)
            best_s = jnp.where(row16 == k, m, best_s)
            best_e = jnp.where(row16 == k, e_sel, best_e)
        ex = jnp.exp(best_s - best_s[0:1, :])
        gate = ex / jnp.sum(ex, axis=0, keepdims=True)
        rows = slice(h * PEER_TOPK, (h + 1) * PEER_TOPK)
        r8t_ref[rows, :] = (best_e >> 1) * SUBLANES
        par_ref[0, rows, :] = best_e & 1
        g_ref[0, rows, :] = gate
    r8_ref[0] = r8t_ref[...].T


def _peer_topk(scores_t):
    t = scores_t.shape[2]
    nblk = t // PEER_TOKENS
    slot_major = pl.BlockSpec((1, N_SLOTS, PEER_TOKENS), lambda i: (i, 0, 0))
    token_major = pl.BlockSpec((1, PEER_TOKENS, N_SLOTS), lambda i: (i, 0, 0))
    return pl.pallas_call(
        _peer_topk_kernel,
        grid=(nblk,),
        in_specs=[pl.BlockSpec((2 * PEER_HEADS, N_KEYS, PEER_TOKENS), lambda i: (0, 0, i))],
        out_specs=[token_major, slot_major, slot_major],
        out_shape=[jax.ShapeDtypeStruct((nblk, PEER_TOKENS, N_SLOTS), I32),
                   jax.ShapeDtypeStruct((nblk, N_SLOTS, PEER_TOKENS), I32),
                   jax.ShapeDtypeStruct((nblk, N_SLOTS, PEER_TOKENS), F32)],
        scratch_shapes=[pltpu.VMEM((N_SLOTS, PEER_TOKENS), I32)],
        compiler_params=pltpu.CompilerParams(dimension_semantics=("parallel",)),
        name="peer_topk",
    )(scores_t)


def _pack_table(tab):
    e, d = tab.shape
    bits = lax.bitcast_convert_type(tab.astype(BF16), jnp.uint16).astype(U32)
    bits = bits.reshape(e, 2, d // (2 * LANES), LANES)
    words = bits[:, 0] | (bits[:, 1] << 16)
    return words.reshape(e * d // (2 * LANES), LANES)


def _unpack_words(w):
    lo = lax.bitcast_convert_type(w << 16, F32)
    hi = lax.bitcast_convert_type(w & jnp.uint32(0xFFFF0000), F32)
    return lo, hi


def _as_bf16(words):
    return pltpu.bitcast(words, BF16)


def _as_words(packed):
    return pltpu.bitcast(packed, U32)


def _merge_packed(a, b, shift, mask):
    ta = a + _as_bf16(pltpu.roll(_as_words(a), shift, 0))
    tb = b + _as_bf16(pltpu.roll(_as_words(b), SUBLANES - shift, 0))
    return jnp.where(mask, _as_words(ta), _as_words(tb))


def _sublane_iota():
    return lax.broadcasted_iota(I32, (SUBLANES, LANES), 0)


def _merge(a, b, shift, mask):
    ta = a + pltpu.roll(a, shift, 0)
    tb = b + pltpu.roll(b, SUBLANES - shift, 0)
    return jnp.where(mask, ta, tb)


GROUP = SUBLANES // 2
N_GROUPS = N_SLOTS // GROUP
N_ROWS = 2 * N_SLOTS
N_SPREAD = 4
SPREAD_AHEAD = 2


def _dup_halves(v, low):
    r = pltpu.roll(v, GROUP, 0)
    return jnp.where(low, v, r), jnp.where(low, r, v)


def _peer_u_kernel(r8_ref, xp_ref, g_ref, par_ref, tab_ref, a_ref, s_ref):
    tb = xp_ref.shape[0]
    sub = _sublane_iota()
    m2 = (sub & 2) != 0
    lane_t = lax.broadcasted_iota(I32, (SUBLANES, tb), 1)

    def token(t, carry):
        xb = _as_bf16(xp_ref[t])
        ids = r8_ref.at[0, t]
        for grp in range(N_GROUPS):
            prods = []
            for i in range(GROUP):
                r8 = pl.multiple_of(ids[grp * GROUP + i], SUBLANES)
                prods.append(_as_bf16(tab_ref[pl.ds(r8, SUBLANES), :]) * xb)
            halves = (_merge_packed(prods[3], prods[1], 2, m2), _merge_packed(prods[2], prods[0], 2, m2))
            for k in range(2):
                lo, hi = _unpack_words(halves[k])
                col = jnp.sum(lo + hi, axis=1, keepdims=True)
                rows = slice((2 * grp + k) * SUBLANES, (2 * grp + k + 1) * SUBLANES)
                s_ref[rows, :] = jnp.where(lane_t == t, col, s_ref[rows, :])
        return carry

    s_ref[...] = jnp.zeros_like(s_ref)
    lax.fori_loop(0, tb, token, 0, unroll=2)

    sub_t = lax.broadcasted_iota(I32, (SUBLANES, tb), 0)
    low_t = sub_t < GROUP
    odd_t = (sub_t & 1) != 0
    want = jnp.where(low_t, 0, 1)
    for v in range(N_SLOTS // SUBLANES):
        src = slice(v * SUBLANES, (v + 1) * SUBLANES)
        gates = _dup_halves(g_ref[0, src, :], low_t)
        pars = _dup_halves(par_ref[0, src, :], low_t)
        for half in range(2):
            grp = 2 * v + half
            dots = _merge(s_ref[2 * grp * SUBLANES:(2 * grp + 1) * SUBLANES, :],
                          s_ref[(2 * grp + 1) * SUBLANES:(2 * grp + 2) * SUBLANES, :], 1, odd_t)
            act = gates[half] * jax.nn.gelu(dots)
            rows = slice(grp * SUBLANES, (grp + 1) * SUBLANES)
            a_ref[0, rows, :] = jnp.where(pars[half] == want, act, 0.0)


def _peer_v_kernel(r8_ref, a_ref, x_ref, tab_ref, o_ref, bc_ref):
    tb = x_ref.shape[0]
    sub = _sublane_iota()
    low = sub < GROUP
    lane_t = lax.broadcasted_iota(I32, (SUBLANES, tb), 1)

    def spread(t, buf):
        for k in range(N_ROWS // SUBLANES):
            rows = slice(k * SUBLANES, (k + 1) * SUBLANES)
            col = jnp.sum(jnp.where(lane_t == t, a_ref[0, rows, :], 0.0), axis=1, keepdims=True)
            bits = lax.bitcast_convert_type(col.astype(BF16).astype(F32), U32)
            bc_ref[buf, rows, :] = jnp.broadcast_to(bits | (bits >> 16), (SUBLANES, LANES))

    def accumulate(t, buf):
        acc_lo = jnp.zeros((SUBLANES, LANES), F32)
        acc_hi = jnp.zeros((SUBLANES, LANES), F32)
        ids = r8_ref.at[0, t]
        for grp in range(N_GROUPS):
            prods = []
            for i in range(GROUP):
                row = grp * SUBLANES + i
                r8 = pl.multiple_of(ids[grp * GROUP + i], SUBLANES)
                am = jnp.where(low, bc_ref[buf, row:row + 1, :], bc_ref[buf, row + GROUP:row + GROUP + 1, :])
                prods.append(_as_bf16(tab_ref[pl.ds(r8, SUBLANES), :]) * _as_bf16(am))
            lo, hi = _unpack_words(_as_words((prods[0] + prods[1]) + (prods[2] + prods[3])))
            acc_lo = acc_lo + lo
            acc_hi = acc_hi + hi
        acc_lo = acc_lo + pltpu.roll(acc_lo, GROUP, 0)
        acc_hi = acc_hi + pltpu.roll(acc_hi, GROUP, 0)
        o_ref[t] = x_ref[t] + jnp.where(low, acc_lo, acc_hi)

    def token_group(i, carry):
        t = N_SPREAD * i
        for k in range(N_SPREAD):
            spread(jnp.minimum(t + k + SPREAD_AHEAD, tb - 1), (k + SPREAD_AHEAD) % N_SPREAD)
            accumulate(t + k, k)
        return carry

    for k in range(SPREAD_AHEAD):
        spread(k, k)
    lax.fori_loop(0, tb // N_SPREAD, token_group, 0)


def _peer_experts(x2d, xn, r8, par, gates, u_tab, v_tab):
    t, d = x2d.shape
    nblk = t // PEER_TOKENS
    rows = d // LANES
    smem_block = pl.BlockSpec((1, PEER_TOKENS, N_SLOTS), lambda i: (i, 0, 0), memory_space=pltpu.SMEM)
    slot_block = pl.BlockSpec((1, N_SLOTS, PEER_TOKENS), lambda i: (i, 0, 0))
    row_block = pl.BlockSpec((1, N_ROWS, PEER_TOKENS), lambda i: (i, 0, 0))
    tok_block = pl.BlockSpec((PEER_TOKENS, rows, LANES), lambda i: (i, 0, 0))
    whole = pl.BlockSpec(memory_space=pltpu.VMEM)
    params = pltpu.CompilerParams(dimension_semantics=("parallel",), vmem_limit_bytes=56 * MIB)

    act = pl.pallas_call(
        _peer_u_kernel,
        grid=(nblk,),
        in_specs=[smem_block, tok_block, slot_block, slot_block, whole],
        out_specs=row_block,
        out_shape=jax.ShapeDtypeStruct((nblk, N_ROWS, PEER_TOKENS), F32),
        scratch_shapes=[pltpu.VMEM((2 * N_ROWS, PEER_TOKENS), F32)],
        compiler_params=params,
        name="peer_expert_in",
    )(r8, xn.reshape(t, rows, LANES), gates, par, _pack_table(u_tab))

    out = pl.pallas_call(
        _peer_v_kernel,
        grid=(nblk,),
        in_specs=[smem_block, row_block, tok_block, whole],
        out_specs=tok_block,
        out_shape=jax.ShapeDtypeStruct((t, rows, LANES), F32),
        scratch_shapes=[pltpu.VMEM((N_SPREAD, N_ROWS, LANES), U32)],
        compiler_params=params,
        name="peer_expert_out",
    )(r8, act, x2d.reshape(t, rows, LANES), _pack_table(v_tab))
    return out.reshape(t, d)


def _peer_ffn(x2d, norm_g, w_q, sub_keys, u_tab, v_tab):
    xn, scores_t = _peer_scores(x2d, norm_g, w_q, sub_keys)
    r8, par, gates = _peer_topk(scores_t)
    return _peer_experts(x2d, xn, r8, par, gates, u_tab, v_tab)


def kernel(x, attn_norm_g, attn_w_qkv, attn_q_g, attn_k_g, attn_rel_bias, attn_w_out, rec_norm_g, rec_w_in, rec_conv_w, rec_conv_b, rec_w_a, rec_b_a, rec_w_x, rec_b_x, rec_lambda, rec_w_out, ffn_norm_g, peer_w_q, peer_sub_keys, peer_u, peer_v):
    b, s, d = x.shape
    depth = ffn_norm_g.shape[0]
    for layer in range(depth):
        j = layer // 2
        if layer % 2 == 0:
            qkv = _qkv_proj(x.reshape(b * s, d), attn_norm_g[j], attn_w_qkv[j], attn_q_g[j], attn_k_g[j])
            x = _attention(x, qkv, attn_rel_bias[j], attn_w_out[j])
        else:
            x = _rglru(x, rec_norm_g[j], rec_w_in[j], rec_conv_w[j], rec_conv_b[j], rec_w_a[j],
                       rec_b_a[j], rec_w_x[j], rec_b_x[j], rec_lambda[j], rec_w_out[j])
        x = _peer_ffn(x.reshape(b * s, d), ffn_norm_g[layer], peer_w_q[layer], peer_sub_keys[layer],
                      peer_u[layer], peer_v[layer]).reshape(b, s, d)
    return x
```

```python
import functools

import numpy as np
import jax
import jax.numpy as jnp
from jax import lax
from jax.experimental import pallas as pl
from jax.experimental.pallas import tpu as pltpu

F32 = jnp.float32
BF16 = jnp.bfloat16
U32 = jnp.uint32
I32 = jnp.int32

EPS = 1e-6
NEG_INF = -1e30

D_MODEL = 1024
CHUNK = 64
ATTN_HEADS = 16
HEAD_DIM = 64
LEFT_CHUNKS = 8
REL_CLIP = 256
LRU_HEADS = 4
LRU_BLOCK = 256
CONV_WIDTH = 4
LRU_C = 8.0
PEER_HEADS = 8
N_KEYS = 128
PEER_TOPK = 16
D_HALF = 128

SUBLANES = 8
LANES = 128

Q_TILE = 4 * CHUNK
N_KEY_BLOCKS = 3
ROW_TILE = 512
PEER_TOKENS = 128
LRU_TILE = 256
N_SLOTS = PEER_HEADS * PEER_TOPK

MIB = 1024 * 1024


def _rms_rows(x, g):
    ms = jnp.mean(x * x, axis=-1, keepdims=True)
    return (x * lax.rsqrt(ms + EPS)) * g


def _split_bf16(v):
    hi = v.astype(BF16)
    lo = (v - hi.astype(F32)).astype(BF16)
    return hi, lo


def _dot(a, b):
    return jnp.dot(a, b, preferred_element_type=F32)


def _dot_nt(a, b):
    return lax.dot_general(a, b, (((1,), (1,)), ((), ())), preferred_element_type=F32)


def _qkv_kernel(x_ref, g_ref, w_ref, gain_ref, bsel_ref, bexp_ref, o_ref, xn_ref):
    j = pl.program_id(1)

    @pl.when(j == 0)
    def _():
        xn_ref[...] = _rms_rows(x_ref[...], g_ref[...]).astype(BF16)

    y = _dot(xn_ref[...], w_ref[...])

    @pl.when(j < 2)
    def _():
        hi, lo = _split_bf16(y * y)
        ms = _dot(hi, bsel_ref[...]) + _dot(lo, bsel_ref[...])
        rhi, rlo = _split_bf16(lax.rsqrt(ms + EPS))
        rs = _dot(rhi, bexp_ref[...]) + _dot(rlo, bexp_ref[...])
        o_ref[...] = ((y * rs) * gain_ref[0]).astype(BF16)

    @pl.when(j == 2)
    def _():
        o_ref[...] = y.astype(BF16)


def _qkv_proj(x2d, norm_g, w_qkv, q_g, k_g):
    t, d = x2d.shape
    scale = HEAD_DIM ** -0.5
    gains = jnp.stack([jnp.tile(q_g, ATTN_HEADS) * scale, jnp.tile(k_g, ATTN_HEADS),
                       jnp.ones((d,), F32)]).reshape(3, 1, d)
    head_of_col = np.arange(d) // HEAD_DIM
    bsel = (head_of_col[:, None] == np.arange(LANES)[None, :]).astype(np.float32) / HEAD_DIM
    bexp = (np.arange(LANES)[:, None] == head_of_col[None, :]).astype(np.float32)
    return pl.pallas_call(
        _qkv_kernel,
        grid=(t // ROW_TILE, 3),
        in_specs=[
            pl.BlockSpec((ROW_TILE, d), lambda i, j: (i, 0)),
            pl.BlockSpec((1, d), lambda i, j: (0, 0)),
            pl.BlockSpec((d, d), lambda i, j: (0, j)),
            pl.BlockSpec((1, 1, d), lambda i, j: (j, 0, 0)),
            pl.BlockSpec((d, LANES), lambda i, j: (0, 0)),
            pl.BlockSpec((LANES, d), lambda i, j: (0, 0)),
        ],
        out_specs=pl.BlockSpec((ROW_TILE, d), lambda i, j: (i, j)),
        out_shape=jax.ShapeDtypeStruct((t, 3 * d), BF16),
        scratch_shapes=[pltpu.VMEM((ROW_TILE, d), BF16)],
        compiler_params=pltpu.CompilerParams(
            dimension_semantics=("parallel", "arbitrary"), vmem_limit_bytes=40 * MIB),
        name="qkv_proj",
    )(x2d, norm_g.reshape(1, d), w_qkv.astype(BF16), gains,
      jnp.asarray(bsel, BF16), jnp.asarray(bexp, BF16))


def _attn_kernel(x_ref, q_ref, k0_ref, k1_ref, k2_ref, v0_ref, v1_ref, v2_ref,
                 bias_ref, wout_ref, o_ref, att_ref):
    qt = pl.program_id(1)
    k_refs = (k0_ref, k1_ref, k2_ref)
    v_refs = (v0_ref, v1_ref, v2_ref)
    negs = [jnp.where(qt - (N_KEY_BLOCKS - 1) + kb < 0, NEG_INF, 0.0).astype(F32)
            for kb in range(N_KEY_BLOCKS)]
    lane = lax.broadcasted_iota(I32, (Q_TILE, LANES), 1)
    first_half = lane < HEAD_DIM
    for p in range(ATTN_HEADS // 2):
        cols = slice(p * LANES, (p + 1) * LANES)
        qp = q_ref[0, :, cols]
        outs = []
        for hh in range(2):
            h = 2 * p + hh
            keep = first_half if hh == 0 else jnp.logical_not(first_half)
            qm = jnp.where(keep, qp, jnp.zeros_like(qp))
            s = [_dot_nt(qm, k_refs[kb][0, :, cols]) + bias_ref[kb, h] + negs[kb]
                 for kb in range(N_KEY_BLOCKS)]
            m = jnp.max(s[0], axis=-1, keepdims=True)
            for kb in range(1, N_KEY_BLOCKS):
                m = jnp.maximum(m, jnp.max(s[kb], axis=-1, keepdims=True))
            acc = jnp.zeros((Q_TILE, LANES), F32)
            l = jnp.zeros((Q_TILE, 1), F32)
            for kb in range(N_KEY_BLOCKS):
                e = jnp.exp(s[kb] - m)
                l = l + jnp.sum(e, axis=-1, keepdims=True)
                acc = acc + _dot(e.astype(BF16), v_refs[kb][0, :, cols])
            outs.append(acc / l)
        att_ref[:, cols] = jnp.where(first_half, outs[0], outs[1]).astype(BF16)
    o_ref[0] = x_ref[0] + _dot(att_ref[...], wout_ref[...])


def _band_bias(table):
    qi = np.arange(Q_TILE)[:, None]
    kj = np.arange(Q_TILE)[None, :]
    period = 2 * Q_TILE
    m = np.arange(period)
    delta = np.where(m < Q_TILE, m, m - period)
    blocks = []
    for kb in range(N_KEY_BLOCKS):
        dist = (N_KEY_BLOCKS - 1 - kb) * Q_TILE - delta
        diag = table[:, np.clip(dist, -REL_CLIP, REL_CLIP) + REL_CLIP]
        toep = jnp.tile(diag, (1, Q_TILE))[:, :Q_TILE * (period - 1)]
        toep = toep.reshape(-1, Q_TILE, period - 1)[:, :, :Q_TILE]
        dchunk = (Q_TILE // CHUNK) * (kb - (N_KEY_BLOCKS - 1)) + kj // CHUNK - qi // CHUNK
        valid = (dchunk >= -LEFT_CHUNKS) & (dchunk <= 0)
        blocks.append(jnp.where(valid[None], toep, NEG_INF))
    return jnp.stack(blocks).astype(F32)


def _attention(x, qkv, rel_bias_table, w_out):
    b, s, d = x.shape
    nq = s // Q_TILE
    qkv3 = qkv.reshape(b, s, 3 * d)
    bias = _band_bias(rel_bias_table)

    def kv_spec(kb, col):
        return pl.BlockSpec(
            (1, Q_TILE, d),
            lambda bi, qt: (bi, jnp.maximum(qt - (N_KEY_BLOCKS - 1) + kb, 0), col))

    in_specs = [pl.BlockSpec((1, Q_TILE, d), lambda bi, qt: (bi, qt, 0)),
                pl.BlockSpec((1, Q_TILE, d), lambda bi, qt: (bi, qt, 0))]
    in_specs += [kv_spec(kb, 1) for kb in range(N_KEY_BLOCKS)]
    in_specs += [kv_spec(kb, 2) for kb in range(N_KEY_BLOCKS)]
    in_specs += [pl.BlockSpec(memory_space=pltpu.VMEM), pl.BlockSpec(memory_space=pltpu.VMEM)]
    return pl.pallas_call(
        _attn_kernel,
        grid=(b, nq),
        in_specs=in_specs,
        out_specs=pl.BlockSpec((1, Q_TILE, d), lambda bi, qt: (bi, qt, 0)),
        out_shape=jax.ShapeDtypeStruct((b, s, d), F32),
        scratch_shapes=[pltpu.VMEM((Q_TILE, d), BF16)],
        compiler_params=pltpu.CompilerParams(
            dimension_semantics=("parallel", "parallel"), vmem_limit_bytes=48 * MIB),
        name="band_attention",
    )(x, qkv3, qkv3, qkv3, qkv3, qkv3, qkv3, qkv3, bias, w_out.astype(BF16))


def _rglru_kernel(x_ref, g_ref, win_ref, cw_ref, cb_ref, wa_ref, ba_ref, wx_ref, bx_ref,
                  lam_ref, wout_ref, o_ref, ext_ref, h_ref, a_scr, b_scr, hs_scr):
    ts = x_ref.shape[1]
    w = LRU_HEADS * LRU_BLOCK

    @pl.when(pl.program_id(1) == 0)
    def _():
        ext_ref[...] = jnp.zeros_like(ext_ref)
        h_ref[...] = jnp.zeros_like(h_ref)

    x = x_ref[0]
    xn = _rms_rows(x, g_ref[...]).astype(BF16)
    gu = _dot(xn, win_ref[...])
    gate = gu[:, :w]
    u_raw = gu[:, w:]
    ext_ref[0:SUBLANES, :] = ext_ref[ts:ts + SUBLANES, :]
    ext_ref[SUBLANES:, :] = u_raw
    u = cb_ref[...] + jnp.zeros((ts, w), F32)
    for k in range(CONV_WIDTH):
        off = SUBLANES - (CONV_WIDTH - 1) + k
        u = u + cw_ref[k:k + 1, :] * ext_ref[off:off + ts, :]
    ub = u.astype(BF16)

    def block_diag(wref):
        return jnp.concatenate(
            [_dot(ub[:, hh * LRU_BLOCK:(hh + 1) * LRU_BLOCK], wref[hh]) for hh in range(LRU_HEADS)],
            axis=1)

    r = jax.nn.sigmoid(block_diag(wa_ref) + ba_ref[...])
    i = jax.nn.sigmoid(block_diag(wx_ref) + bx_ref[...])
    z = -lam_ref[...]
    softplus = jnp.maximum(z, 0.0) + jnp.log1p(jnp.exp(-jnp.abs(z)))
    log_a = (-LRU_C) * r * softplus
    a_scr[...] = jnp.exp(log_a)
    th = jnp.tanh(log_a)
    b_scr[...] = jnp.sqrt(2.0 * th / (th - 1.0)) * (i * u)

    def step(t, h):
        h = a_scr[pl.ds(t, 1), :] * h + b_scr[pl.ds(t, 1), :]
        hs_scr[pl.ds(t, 1), :] = h
        return h

    h_ref[0:1, :] = lax.fori_loop(0, ts, step, h_ref[0:1, :], unroll=8)
    y = jax.nn.gelu(gate) * hs_scr[...]
    o_ref[0] = x + _dot(y.astype(BF16), wout_ref[...])


def _rglru(x, norm_g, w_in, conv_w, conv_b, w_a, b_a, w_x, b_x, lam, w_out):
    b, s, d = x.shape
    w = LRU_HEADS * LRU_BLOCK
    row = lambda v: v.reshape(1, -1)
    whole = pl.BlockSpec(memory_space=pltpu.VMEM)
    return pl.pallas_call(
        _rglru_kernel,
        grid=(b, s // LRU_TILE),
        in_specs=[pl.BlockSpec((1, LRU_TILE, d), lambda bi, si: (bi, si, 0))] + [whole] * 10,
        out_specs=pl.BlockSpec((1, LRU_TILE, d), lambda bi, si: (bi, si, 0)),
        out_shape=jax.ShapeDtypeStruct((b, s, d), F32),
        scratch_shapes=[pltpu.VMEM((LRU_TILE + SUBLANES, w), F32), pltpu.VMEM((SUBLANES, w), F32),
                        pltpu.VMEM((LRU_TILE, w), F32), pltpu.VMEM((LRU_TILE, w), F32),
                        pltpu.VMEM((LRU_TILE, w), F32)],
        compiler_params=pltpu.CompilerParams(
            dimension_semantics=("arbitrary", "arbitrary"), vmem_limit_bytes=48 * MIB),
        name="rglru_mixer",
    )(x, row(norm_g), w_in.astype(BF16), conv_w, row(conv_b), w_a.astype(BF16), row(b_a),
      w_x.astype(BF16), row(b_x), row(lam), w_out.astype(BF16))


def _peer_scores_kernel(x_ref, g_ref, wq_ref, sk_ref, xp_ref, st_ref):
    xb = _rms_rows(x_ref[...], g_ref[...]).astype(BF16)
    words = _bf16_pair_words(xb)
    tm = words.shape[0]
    for c in range(SUBLANES):
        piece = words[:, (c % (SUBLANES // 2)) * LANES:(c % (SUBLANES // 2) + 1) * LANES]
        xp_ref[pl.ds(c, tm, stride=SUBLANES), :] = piece
    q = _dot(xb, wq_ref[...]).astype(BF16)
    for hp in range(2 * PEER_HEADS):
        st_ref[hp] = _dot_nt(sk_ref[hp % 2], q[:, hp * D_HALF:(hp + 1) * D_HALF])


def _peer_scores(x2d, norm_g, w_q, sub_keys):
    t, d = x2d.shape
    nq = w_q.shape[1]
    whole = pl.BlockSpec(memory_space=pltpu.VMEM)
    return pl.pallas_call(
        _peer_scores_kernel,
        grid=(t // ROW_TILE,),
        in_specs=[pl.BlockSpec((ROW_TILE, d), lambda i: (i, 0)), whole, whole, whole],
        out_specs=[pl.BlockSpec((ROW_TILE * SUBLANES, LANES), lambda i: (i, 0)),
                   pl.BlockSpec((2 * PEER_HEADS, N_KEYS, ROW_TILE), lambda i: (0, 0, i))],
        out_shape=[jax.ShapeDtypeStruct((t * SUBLANES, LANES), U32),
                   jax.ShapeDtypeStruct((2 * PEER_HEADS, N_KEYS, t), F32)],
        compiler_params=pltpu.CompilerParams(
            dimension_semantics=("parallel",), vmem_limit_bytes=48 * MIB),
        name="peer_scores",
    )(x2d, norm_g.reshape(1, d), w_q.astype(BF16), sub_keys.astype(BF16))


def _candidate_blocks():
    blocks = []
    for i in range(PEER_TOPK // 2):
        nj = PEER_TOPK // (i + 1)
        for j0 in range(0, nj, SUBLANES):
            blocks.append(("row", i, j0, min(SUBLANES, nj - j0)))
    blocks.append(("col", PEER_TOPK // 2, PEER_TOPK // 2))
    return blocks


def _peer_topk_kernel(st_ref, r8_ref, par_ref, g_ref, r8t_ref):
    tk = st_ref.shape[2]
    key_iota = lax.broadcasted_iota(I32, (N_KEYS, tk), 0)
    row16 = lax.broadcasted_iota(I32, (PEER_TOPK, tk), 0)
    row8 = lax.broadcasted_iota(I32, (SUBLANES, tk), 0)
    neg_inf = jnp.float32(-jnp.inf)
    blocks = _candidate_blocks()
    big = jnp.int32(PEER_TOPK * PEER_TOPK)

    def top16(x):
        vals = jnp.zeros((PEER_TOPK, tk), F32)
        idxs = jnp.zeros((PEER_TOPK, tk), I32)
        for k in range(PEER_TOPK):
            m = jnp.max(x, axis=0, keepdims=True)
            idx = jnp.min(jnp.where(x == m, key_iota, N_KEYS), axis=0, keepdims=True)
            x = jnp.where(key_iota == idx, neg_inf, x)
            vals = jnp.where(row16 == k, m, vals)
            idxs = jnp.where(row16 == k, idx, idxs)
        return vals, idxs

    for h in range(PEER_HEADS):
        s0, i0 = top16(st_ref[2 * h])
        s1, i1 = top16(st_ref[2 * h + 1])
        cand, flat, expert = [], [], []
        for blk in blocks:
            if blk[0] == "row":
                _, i, j0, n = blk
                c = s0[i:i + 1, :] + s1[j0:j0 + SUBLANES, :]
                f = PEER_TOPK * i + j0 + row8
                e = i0[i:i + 1, :] * N_KEYS + i1[j0:j0 + SUBLANES, :]
            else:
                _, i_start, n = blk
                c = s0[i_start:i_start + SUBLANES, :] + s1[0:1, :]
                f = PEER_TOPK * (i_start + row8)
                e = i0[i_start:i_start + SUBLANES, :] * N_KEYS + i1[0:1, :]
            cand.append(jnp.where(row8 < n, c, neg_inf))
            flat.append(f)
            expert.append(e)
        cand = jnp.concatenate(cand, axis=0)
        flat = jnp.concatenate(flat, axis=0)
        expert = jnp.concatenate(expert, axis=0)
        best_s = jnp.zeros((PEER_TOPK, tk), F32)
        best_e = jnp.zeros((PEER_TOPK, tk), I32)
        for k in range(PEER_TOPK):
            m = jnp.max(cand, axis=0, keepdims=True)
            fidx = jnp.min(jnp.where(cand == m, flat, big), axis=0, keepdims=True)
            pick = flat == fidx
            e_sel = jnp.max(jnp.where(pick, expert, -1), axis=0, keepdims=True)
            cand = jnp.where(pick, neg_inf, cand)
            best_s = jnp.where(row16 == k, m, best_s)
            best_e = jnp.where(row16 == k, e_sel, best_e)
        ex = jnp.exp(best_s - best_s[0:1, :])
        gate = ex / jnp.sum(ex, axis=0, keepdims=True)
        rows = slice(h * PEER_TOPK, (h + 1) * PEER_TOPK)
        r8t_ref[rows, :] = (best_e >> 1) * SUBLANES
        par_ref[0, rows, :] = best_e & 1
        g_ref[0, rows, :] = gate
    r8_ref[0] = r8t_ref[...].T


def _peer_topk(scores_t):
    t = scores_t.shape[2]
    nblk = t // PEER_TOKENS
    slot_major = pl.BlockSpec((1, N_SLOTS, PEER_TOKENS), lambda i: (i, 0, 0))
    token_major = pl.BlockSpec((1, PEER_TOKENS, N_SLOTS), lambda i: (i, 0, 0))
    return pl.pallas_call(
        _peer_topk_kernel,
        grid=(nblk,),
        in_specs=[pl.BlockSpec((2 * PEER_HEADS, N_KEYS, PEER_TOKENS), lambda i: (0, 0, i))],
        out_specs=[token_major, slot_major, slot_major],
        out_shape=[jax.ShapeDtypeStruct((nblk, PEER_TOKENS, N_SLOTS), I32),
                   jax.ShapeDtypeStruct((nblk, N_SLOTS, PEER_TOKENS), I32),
                   jax.ShapeDtypeStruct((nblk, N_SLOTS, PEER_TOKENS), F32)],
        scratch_shapes=[pltpu.VMEM((N_SLOTS, PEER_TOKENS), I32)],
        compiler_params=pltpu.CompilerParams(dimension_semantics=("parallel",)),
        name="peer_topk",
    )(scores_t)


def _bf16_pair_words(x):
    bits = lax.bitcast_convert_type(x.astype(BF16).astype(F32), U32)
    half = bits.shape[1] // 2
    return (bits[:, :half] >> 16) | (bits[:, half:] & jnp.uint32(0xFFFF0000))


def _pack_kernel(t_ref, o_ref):
    words = _bf16_pair_words(t_ref[...])
    rows = words.shape[0]
    n = words.shape[1] // LANES
    for s in range(n):
        o_ref[pl.ds(s, rows, stride=n), :] = words[:, s * LANES:(s + 1) * LANES]


def _pack_table(tab):
    e, d = tab.shape
    n = d // (2 * LANES)
    return pl.pallas_call(
        _pack_kernel,
        grid=(e // ROW_TILE,),
        in_specs=[pl.BlockSpec((ROW_TILE, d), lambda i: (i, 0))],
        out_specs=pl.BlockSpec((ROW_TILE * n, LANES), lambda i: (i, 0)),
        out_shape=jax.ShapeDtypeStruct((e * n, LANES), U32),
        compiler_params=pltpu.CompilerParams(dimension_semantics=("parallel",)),
        name="peer_pack_table",
    )(tab)


def _unpack_words(w):
    lo = lax.bitcast_convert_type(w << 16, F32)
    hi = lax.bitcast_convert_type(w & jnp.uint32(0xFFFF0000), F32)
    return lo, hi


def _as_bf16(words):
    return pltpu.bitcast(words, BF16)


def _as_words(packed):
    return pltpu.bitcast(packed, U32)


def _merge_packed(a, b, shift, mask):
    ta = a + _as_bf16(pltpu.roll(_as_words(a), shift, 0))
    tb = b + _as_bf16(pltpu.roll(_as_words(b), SUBLANES - shift, 0))
    return jnp.where(mask, _as_words(ta), _as_words(tb))


def _sublane_iota():
    return lax.broadcasted_iota(I32, (SUBLANES, LANES), 0)


def _merge(a, b, shift, mask):
    ta = a + pltpu.roll(a, shift, 0)
    tb = b + pltpu.roll(b, SUBLANES - shift, 0)
    return jnp.where(mask, ta, tb)


GROUP = SUBLANES // 2
N_GROUPS = N_SLOTS // GROUP
N_ROWS = 2 * N_SLOTS
N_SPREAD = 4
SPREAD_AHEAD = 2


def _dup_halves(v, low):
    r = pltpu.roll(v, GROUP, 0)
    return jnp.where(low, v, r), jnp.where(low, r, v)


def _peer_u_kernel(r8_ref, xp_ref, g_ref, par_ref, tab_ref, a_ref, s_ref, part_ref):
    tb = g_ref.shape[2]
    sub = _sublane_iota()
    m2 = (sub & 2) != 0
    lane_t = lax.broadcasted_iota(I32, (SUBLANES, tb), 1)
    n_part = 2 * N_GROUPS

    def products(t, slot):
        xb = _as_bf16(xp_ref[pl.ds(pl.multiple_of(t * SUBLANES, SUBLANES), SUBLANES), :])
        ids = r8_ref.at[0, t]
        for grp in range(N_GROUPS):
            prods = []
            for i in range(GROUP):
                r8 = pl.multiple_of(ids[grp * GROUP + i], SUBLANES)
                prods.append(_as_bf16(tab_ref[pl.ds(r8, SUBLANES), :]) * xb)
            halves = (_merge_packed(prods[3], prods[1], 2, m2), _merge_packed(prods[2], prods[0], 2, m2))
            for k in range(2):
                lo, hi = _unpack_words(halves[k])
                part_ref[slot, 2 * grp + k] = lo + hi

    def reduce_lanes(t, slot):
        for n in range(n_part):
            col = jnp.sum(part_ref[slot, n], axis=1, keepdims=True)
            rows = slice(n * SUBLANES, (n + 1) * SUBLANES)
            s_ref[rows, :] = jnp.where(lane_t == t, col, s_ref[rows, :])

    def token_pair(i, carry):
        t = 2 * i
        reduce_lanes(t - 2, 0)
        reduce_lanes(t - 1, 1)
        products(t, 0)
        products(t + 1, 1)
        return carry

    s_ref[...] = jnp.zeros_like(s_ref)
    part_ref[...] = jnp.zeros_like(part_ref)
    lax.fori_loop(0, tb // 2, token_pair, 0)
    reduce_lanes(tb - 2, 0)
    reduce_lanes(tb - 1, 1)

    sub_t = lax.broadcasted_iota(I32, (SUBLANES, tb), 0)
    low_t = sub_t < GROUP
    odd_t = (sub_t & 1) != 0
    want = jnp.where(low_t, 0, 1)
    for v in range(N_SLOTS // SUBLANES):
        src = slice(v * SUBLANES, (v + 1) * SUBLANES)
        gates = _dup_halves(g_ref[0, src, :], low_t)
        pars = _dup_halves(par_ref[0, src, :], low_t)
        for half in range(2):
            grp = 2 * v + half
            dots = _merge(s_ref[2 * grp * SUBLANES:(2 * grp + 1) * SUBLANES, :],
                          s_ref[(2 * grp + 1) * SUBLANES:(2 * grp + 2) * SUBLANES, :], 1, odd_t)
            act = gates[half] * jax.nn.gelu(dots)
            rows = slice(grp * SUBLANES, (grp + 1) * SUBLANES)
            a_ref[0, rows, :] = jnp.where(pars[half] == want, act, 0.0)


def _token_tiles(ref, xt_ref, to_tiles):
    tb, d = ref.shape
    for q in range(tb // SUBLANES):
        for c in range(d // LANES):
            rows = slice(q * SUBLANES, (q + 1) * SUBLANES)
            cols = slice(c * LANES, (c + 1) * LANES)
            strided = pl.ds(q * SUBLANES * SUBLANES + c, SUBLANES, stride=SUBLANES)
            if to_tiles:
                xt_ref[strided, :] = ref[rows, cols]
            else:
                ref[rows, cols] = xt_ref[strided, :]


def _peer_v_kernel(r8_ref, a_ref, x_ref, tab_ref, o_ref, bc_ref, xt_ref):
    tb = x_ref.shape[0]
    sub = _sublane_iota()
    low = sub < GROUP
    lane_t = lax.broadcasted_iota(I32, (SUBLANES, tb), 1)
    _token_tiles(x_ref, xt_ref, True)

    def spread(t, buf):
        for k in range(N_ROWS // SUBLANES):
            rows = slice(k * SUBLANES, (k + 1) * SUBLANES)
            col = jnp.sum(jnp.where(lane_t == t, a_ref[0, rows, :], 0.0), axis=1, keepdims=True)
            bits = lax.bitcast_convert_type(col.astype(BF16).astype(F32), U32)
            bc_ref[buf, rows, :] = jnp.broadcast_to(bits | (bits >> 16), (SUBLANES, LANES))

    def accumulate(t, buf):
        acc_lo = jnp.zeros((SUBLANES, LANES), F32)
        acc_hi = jnp.zeros((SUBLANES, LANES), F32)
        ids = r8_ref.at[0, t]
        for grp in range(N_GROUPS):
            prods = []
            for i in range(GROUP):
                row = grp * SUBLANES + i
                r8 = pl.multiple_of(ids[grp * GROUP + i], SUBLANES)
                am = jnp.where(low, bc_ref[buf, row:row + 1, :], bc_ref[buf, row + GROUP:row + GROUP + 1, :])
                prods.append(_as_bf16(tab_ref[pl.ds(r8, SUBLANES), :]) * _as_bf16(am))
            lo, hi = _unpack_words(_as_words((prods[0] + prods[1]) + (prods[2] + prods[3])))
            acc_lo = acc_lo + lo
            acc_hi = acc_hi + hi
        acc_lo = acc_lo + pltpu.roll(acc_lo, GROUP, 0)
        acc_hi = acc_hi + pltpu.roll(acc_hi, GROUP, 0)
        tile = pl.ds(pl.multiple_of(t * SUBLANES, SUBLANES), SUBLANES)
        xt_ref[tile, :] = xt_ref[tile, :] + jnp.where(low, acc_lo, acc_hi)

    def token_group(i, carry):
        t = N_SPREAD * i
        for k in range(N_SPREAD):
            spread(jnp.minimum(t + k + SPREAD_AHEAD, tb - 1), (k + SPREAD_AHEAD) % N_SPREAD)
            accumulate(t + k, k)
        return carry

    for k in range(SPREAD_AHEAD):
        spread(k, k)
    lax.fori_loop(0, tb // N_SPREAD, token_group, 0)
    _token_tiles(o_ref, xt_ref, False)


def _peer_experts(x2d, xp, r8, par, gates, u_tab, v_tab):
    t, d = x2d.shape
    nblk = t // PEER_TOKENS
    smem_block = pl.BlockSpec((1, PEER_TOKENS, N_SLOTS), lambda i: (i, 0, 0), memory_space=pltpu.SMEM)
    slot_block = pl.BlockSpec((1, N_SLOTS, PEER_TOKENS), lambda i: (i, 0, 0))
    row_block = pl.BlockSpec((1, N_ROWS, PEER_TOKENS), lambda i: (i, 0, 0))
    tile_block = pl.BlockSpec((PEER_TOKENS * SUBLANES, LANES), lambda i: (i, 0))
    tok_block = pl.BlockSpec((PEER_TOKENS, d), lambda i: (i, 0))
    whole = pl.BlockSpec(memory_space=pltpu.VMEM)
    params = pltpu.CompilerParams(dimension_semantics=("parallel",), vmem_limit_bytes=56 * MIB)

    act = pl.pallas_call(
        _peer_u_kernel,
        grid=(nblk,),
        in_specs=[smem_block, tile_block, slot_block, slot_block, whole],
        out_specs=row_block,
        out_shape=jax.ShapeDtypeStruct((nblk, N_ROWS, PEER_TOKENS), F32),
        scratch_shapes=[pltpu.VMEM((2 * N_ROWS, PEER_TOKENS), F32),
                        pltpu.VMEM((2, 2 * N_GROUPS, SUBLANES, LANES), F32)],
        compiler_params=params,
        name="peer_expert_in",
    )(r8, xp, gates, par, _pack_table(u_tab))

    return pl.pallas_call(
        _peer_v_kernel,
        grid=(nblk,),
        in_specs=[smem_block, row_block, tok_block, whole],
        out_specs=tok_block,
        out_shape=jax.ShapeDtypeStruct((t, d), F32),
        scratch_shapes=[pltpu.VMEM((N_SPREAD, N_ROWS, LANES), U32),
                        pltpu.VMEM((PEER_TOKENS * SUBLANES, LANES), F32)],
        compiler_params=params,
        name="peer_expert_out",
    )(r8, act, x2d, _pack_table(v_tab))


def _peer_ffn(x2d, norm_g, w_q, sub_keys, u_tab, v_tab):
    xp, scores_t = _peer_scores(x2d, norm_g, w_q, sub_keys)
    r8, par, gates = _peer_topk(scores_t)
    return _peer_experts(x2d, xp, r8, par, gates, u_tab, v_tab)


def kernel(x, attn_norm_g, attn_w_qkv, attn_q_g, attn_k_g, attn_rel_bias, attn_w_out, rec_norm_g, rec_w_in, rec_conv_w, rec_conv_b, rec_w_a, rec_b_a, rec_w_x, rec_b_x, rec_lambda, rec_w_out, ffn_norm_g, peer_w_q, peer_sub_keys, peer_u, peer_v):
    b, s, d = x.shape
    depth = ffn_norm_g.shape[0]
    for layer in range(depth):
        j = layer // 2
        if layer % 2 == 0:
            qkv = _qkv_proj(x.reshape(b * s, d), attn_norm_g[j], attn_w_qkv[j], attn_q_g[j], attn_k_g[j])
            x = _attention(x, qkv, attn_rel_bias[j], attn_w_out[j])
        else:
            x = _rglru(x, rec_norm_g[j], rec_w_in[j], rec_conv_w[j], rec_conv_b[j], rec_w_a[j],
                       rec_b_a[j], rec_w_x[j], rec_b_x[j], rec_lambda[j], rec_w_out[j])
        x = _peer_ffn(x.reshape(b * s, d), ffn_norm_g[layer], peer_w_q[layer], peer_sub_keys[layer],
                      peer_u[layer], peer_v[layer]).reshape(b, s, d)
    return x
```

```python
import functools

import numpy as np
import jax
import jax.numpy as jnp
from jax import lax
from jax.experimental import pallas as pl
from jax.experimental.pallas import tpu as pltpu

F32 = jnp.float32
BF16 = jnp.bfloat16
U32 = jnp.uint32
I32 = jnp.int32

EPS = 1e-6
NEG_INF = -1e30

D_MODEL = 1024
CHUNK = 64
ATTN_HEADS = 16
HEAD_DIM = 64
LEFT_CHUNKS = 8
REL_CLIP = 256
LRU_HEADS = 4
LRU_BLOCK = 256
CONV_WIDTH = 4
LRU_C = 8.0
PEER_HEADS = 8
N_KEYS = 128
PEER_TOPK = 16
D_HALF = 128

SUBLANES = 8
LANES = 128

Q_TILE = 4 * CHUNK
N_KEY_BLOCKS = 3
ROW_TILE = 512
PEER_TOKENS = 128
LRU_TILE = 256
N_SLOTS = PEER_HEADS * PEER_TOPK

MIB = 1024 * 1024


def _rms_rows(x, g):
    ms = jnp.mean(x * x, axis=-1, keepdims=True)
    return (x * lax.rsqrt(ms + EPS)) * g


def _split_bf16(v):
    hi = v.astype(BF16)
    lo = (v - hi.astype(F32)).astype(BF16)
    return hi, lo


def _dot(a, b):
    return jnp.dot(a, b, preferred_element_type=F32)


def _dot_nt(a, b):
    return lax.dot_general(a, b, (((1,), (1,)), ((), ())), preferred_element_type=F32)


def _qkv_kernel(x_ref, g_ref, w_ref, gain_ref, bsel_ref, bexp_ref, o_ref, xn_ref):
    j = pl.program_id(1)

    @pl.when(j == 0)
    def _():
        xn_ref[...] = _rms_rows(x_ref[...], g_ref[...]).astype(BF16)

    y = _dot(xn_ref[...], w_ref[...])

    @pl.when(j < 2)
    def _():
        hi, lo = _split_bf16(y * y)
        ms = _dot(hi, bsel_ref[...]) + _dot(lo, bsel_ref[...])
        rhi, rlo = _split_bf16(lax.rsqrt(ms + EPS))
        rs = _dot(rhi, bexp_ref[...]) + _dot(rlo, bexp_ref[...])
        o_ref[...] = ((y * rs) * gain_ref[0]).astype(BF16)

    @pl.when(j == 2)
    def _():
        o_ref[...] = y.astype(BF16)


def _qkv_proj(x2d, norm_g, w_qkv, q_g, k_g):
    t, d = x2d.shape
    scale = HEAD_DIM ** -0.5
    gains = jnp.stack([jnp.tile(q_g, ATTN_HEADS) * scale, jnp.tile(k_g, ATTN_HEADS),
                       jnp.ones((d,), F32)]).reshape(3, 1, d)
    head_of_col = np.arange(d) // HEAD_DIM
    bsel = (head_of_col[:, None] == np.arange(LANES)[None, :]).astype(np.float32) / HEAD_DIM
    bexp = (np.arange(LANES)[:, None] == head_of_col[None, :]).astype(np.float32)
    return pl.pallas_call(
        _qkv_kernel,
        grid=(t // ROW_TILE, 3),
        in_specs=[
            pl.BlockSpec((ROW_TILE, d), lambda i, j: (i, 0)),
            pl.BlockSpec((1, d), lambda i, j: (0, 0)),
            pl.BlockSpec((d, d), lambda i, j: (0, j)),
            pl.BlockSpec((1, 1, d), lambda i, j: (j, 0, 0)),
            pl.BlockSpec((d, LANES), lambda i, j: (0, 0)),
            pl.BlockSpec((LANES, d), lambda i, j: (0, 0)),
        ],
        out_specs=pl.BlockSpec((ROW_TILE, d), lambda i, j: (i, j)),
        out_shape=jax.ShapeDtypeStruct((t, 3 * d), BF16),
        scratch_shapes=[pltpu.VMEM((ROW_TILE, d), BF16)],
        compiler_params=pltpu.CompilerParams(
            dimension_semantics=("parallel", "arbitrary"), vmem_limit_bytes=40 * MIB),
        name="qkv_proj",
    )(x2d, norm_g.reshape(1, d), w_qkv.astype(BF16), gains,
      jnp.asarray(bsel, BF16), jnp.asarray(bexp, BF16))


def _attn_kernel(x_ref, q_ref, k0_ref, k1_ref, k2_ref, v0_ref, v1_ref, v2_ref,
                 bias_ref, wout_ref, o_ref, att_ref):
    qt = pl.program_id(1)
    k_refs = (k0_ref, k1_ref, k2_ref)
    v_refs = (v0_ref, v1_ref, v2_ref)
    negs = [jnp.where(qt - (N_KEY_BLOCKS - 1) + kb < 0, NEG_INF, 0.0).astype(F32)
            for kb in range(N_KEY_BLOCKS)]
    lane = lax.broadcasted_iota(I32, (Q_TILE, LANES), 1)
    first_half = lane < HEAD_DIM
    for p in range(ATTN_HEADS // 2):
        cols = slice(p * LANES, (p + 1) * LANES)
        qp = q_ref[0, :, cols]
        outs = []
        for hh in range(2):
            h = 2 * p + hh
            keep = first_half if hh == 0 else jnp.logical_not(first_half)
            qm = jnp.where(keep, qp, jnp.zeros_like(qp))
            s = [_dot_nt(qm, k_refs[kb][0, :, cols]) + bias_ref[kb, h] + negs[kb]
                 for kb in range(N_KEY_BLOCKS)]
            m = jnp.max(s[0], axis=-1, keepdims=True)
            for kb in range(1, N_KEY_BLOCKS):
                m = jnp.maximum(m, jnp.max(s[kb], axis=-1, keepdims=True))
            acc = jnp.zeros((Q_TILE, LANES), F32)
            l = jnp.zeros((Q_TILE, 1), F32)
            for kb in range(N_KEY_BLOCKS):
                e = jnp.exp(s[kb] - m)
                l = l + jnp.sum(e, axis=-1, keepdims=True)
                acc = acc + _dot(e.astype(BF16), v_refs[kb][0, :, cols])
            outs.append(acc / l)
        att_ref[:, cols] = jnp.where(first_half, outs[0], outs[1]).astype(BF16)
    o_ref[0] = x_ref[0] + _dot(att_ref[...], wout_ref[...])


def _band_bias(table):
    qi = np.arange(Q_TILE)[:, None]
    kj = np.arange(Q_TILE)[None, :]
    period = 2 * Q_TILE
    m = np.arange(period)
    delta = np.where(m < Q_TILE, m, m - period)
    blocks = []
    for kb in range(N_KEY_BLOCKS):
        dist = (N_KEY_BLOCKS - 1 - kb) * Q_TILE - delta
        diag = table[:, np.clip(dist, -REL_CLIP, REL_CLIP) + REL_CLIP]
        toep = jnp.tile(diag, (1, Q_TILE))[:, :Q_TILE * (period - 1)]
        toep = toep.reshape(-1, Q_TILE, period - 1)[:, :, :Q_TILE]
        dchunk = (Q_TILE // CHUNK) * (kb - (N_KEY_BLOCKS - 1)) + kj // CHUNK - qi // CHUNK
        valid = (dchunk >= -LEFT_CHUNKS) & (dchunk <= 0)
        blocks.append(jnp.where(valid[None], toep, NEG_INF))
    return jnp.stack(blocks).astype(F32)


def _attention(x, qkv, rel_bias_table, w_out):
    b, s, d = x.shape
    nq = s // Q_TILE
    qkv3 = qkv.reshape(b, s, 3 * d)
    bias = _band_bias(rel_bias_table)

    def kv_spec(kb, col):
        return pl.BlockSpec(
            (1, Q_TILE, d),
            lambda bi, qt: (bi, jnp.maximum(qt - (N_KEY_BLOCKS - 1) + kb, 0), col))

    in_specs = [pl.BlockSpec((1, Q_TILE, d), lambda bi, qt: (bi, qt, 0)),
                pl.BlockSpec((1, Q_TILE, d), lambda bi, qt: (bi, qt, 0))]
    in_specs += [kv_spec(kb, 1) for kb in range(N_KEY_BLOCKS)]
    in_specs += [kv_spec(kb, 2) for kb in range(N_KEY_BLOCKS)]
    in_specs += [pl.BlockSpec(memory_space=pltpu.VMEM), pl.BlockSpec(memory_space=pltpu.VMEM)]
    return pl.pallas_call(
        _attn_kernel,
        grid=(b, nq),
        in_specs=in_specs,
        out_specs=pl.BlockSpec((1, Q_TILE, d), lambda bi, qt: (bi, qt, 0)),
        out_shape=jax.ShapeDtypeStruct((b, s, d), F32),
        scratch_shapes=[pltpu.VMEM((Q_TILE, d), BF16)],
        compiler_params=pltpu.CompilerParams(
            dimension_semantics=("parallel", "parallel"), vmem_limit_bytes=48 * MIB),
        name="band_attention",
    )(x, qkv3, qkv3, qkv3, qkv3, qkv3, qkv3, qkv3, bias, w_out.astype(BF16))


def _rglru_kernel(x_ref, g_ref, win_ref, cw_ref, cb_ref, wa_ref, ba_ref, wx_ref, bx_ref,
                  lam_ref, wout_ref, o_ref, ext_ref, h_ref, a_scr, b_scr, hs_scr):
    ts = x_ref.shape[1]
    w = LRU_HEADS * LRU_BLOCK

    @pl.when(pl.program_id(1) == 0)
    def _():
        ext_ref[...] = jnp.zeros_like(ext_ref)
        h_ref[...] = jnp.zeros_like(h_ref)

    x = x_ref[0]
    xn = _rms_rows(x, g_ref[...]).astype(BF16)
    gu = _dot(xn, win_ref[...])
    gate = gu[:, :w]
    u_raw = gu[:, w:]
    ext_ref[0:SUBLANES, :] = ext_ref[ts:ts + SUBLANES, :]
    ext_ref[SUBLANES:, :] = u_raw
    u = cb_ref[...] + jnp.zeros((ts, w), F32)
    for k in range(CONV_WIDTH):
        off = SUBLANES - (CONV_WIDTH - 1) + k
        u = u + cw_ref[k:k + 1, :] * ext_ref[off:off + ts, :]
    ub = u.astype(BF16)

    def block_diag(wref):
        return jnp.concatenate(
            [_dot(ub[:, hh * LRU_BLOCK:(hh + 1) * LRU_BLOCK], wref[hh]) for hh in range(LRU_HEADS)],
            axis=1)

    r = jax.nn.sigmoid(block_diag(wa_ref) + ba_ref[...])
    i = jax.nn.sigmoid(block_diag(wx_ref) + bx_ref[...])
    z = -lam_ref[...]
    softplus = jnp.maximum(z, 0.0) + jnp.log1p(jnp.exp(-jnp.abs(z)))
    log_a = (-LRU_C) * r * softplus
    a_scr[...] = jnp.exp(log_a)
    th = jnp.tanh(log_a)
    b_scr[...] = jnp.sqrt(2.0 * th / (th - 1.0)) * (i * u)

    def step(t, h):
        h = a_scr[pl.ds(t, 1), :] * h + b_scr[pl.ds(t, 1), :]
        hs_scr[pl.ds(t, 1), :] = h
        return h

    h_ref[0:1, :] = lax.fori_loop(0, ts, step, h_ref[0:1, :], unroll=8)
    y = jax.nn.gelu(gate) * hs_scr[...]
    o_ref[0] = x + _dot(y.astype(BF16), wout_ref[...])


def _rglru(x, norm_g, w_in, conv_w, conv_b, w_a, b_a, w_x, b_x, lam, w_out):
    b, s, d = x.shape
    w = LRU_HEADS * LRU_BLOCK
    row = lambda v: v.reshape(1, -1)
    whole = pl.BlockSpec(memory_space=pltpu.VMEM)
    return pl.pallas_call(
        _rglru_kernel,
        grid=(b, s // LRU_TILE),
        in_specs=[pl.BlockSpec((1, LRU_TILE, d), lambda bi, si: (bi, si, 0))] + [whole] * 10,
        out_specs=pl.BlockSpec((1, LRU_TILE, d), lambda bi, si: (bi, si, 0)),
        out_shape=jax.ShapeDtypeStruct((b, s, d), F32),
        scratch_shapes=[pltpu.VMEM((LRU_TILE + SUBLANES, w), F32), pltpu.VMEM((SUBLANES, w), F32),
                        pltpu.VMEM((LRU_TILE, w), F32), pltpu.VMEM((LRU_TILE, w), F32),
                        pltpu.VMEM((LRU_TILE, w), F32)],
        compiler_params=pltpu.CompilerParams(
            dimension_semantics=("arbitrary", "arbitrary"), vmem_limit_bytes=48 * MIB),
        name="rglru_mixer",
    )(x, row(norm_g), w_in.astype(BF16), conv_w, row(conv_b), w_a.astype(BF16), row(b_a),
      w_x.astype(BF16), row(b_x), row(lam), w_out.astype(BF16))


def _peer_scores_kernel(x_ref, g_ref, wq_ref, sk_ref, xp_ref, st_ref):
    xb = _rms_rows(x_ref[...], g_ref[...]).astype(BF16)
    words = _bf16_pair_words(xb)
    tm = words.shape[0]
    for c in range(SUBLANES):
        piece = words[:, (c % (SUBLANES // 2)) * LANES:(c % (SUBLANES // 2) + 1) * LANES]
        xp_ref[pl.ds(c, tm, stride=SUBLANES), :] = piece
    q = _dot(xb, wq_ref[...]).astype(BF16)
    for hp in range(2 * PEER_HEADS):
        st_ref[hp] = _dot_nt(sk_ref[hp % 2], q[:, hp * D_HALF:(hp + 1) * D_HALF])


def _peer_scores(x2d, norm_g, w_q, sub_keys):
    t, d = x2d.shape
    nq = w_q.shape[1]
    whole = pl.BlockSpec(memory_space=pltpu.VMEM)
    return pl.pallas_call(
        _peer_scores_kernel,
        grid=(t // ROW_TILE,),
        in_specs=[pl.BlockSpec((ROW_TILE, d), lambda i: (i, 0)), whole, whole, whole],
        out_specs=[pl.BlockSpec((ROW_TILE * SUBLANES, LANES), lambda i: (i, 0)),
                   pl.BlockSpec((2 * PEER_HEADS, N_KEYS, ROW_TILE), lambda i: (0, 0, i))],
        out_shape=[jax.ShapeDtypeStruct((t * SUBLANES, LANES), U32),
                   jax.ShapeDtypeStruct((2 * PEER_HEADS, N_KEYS, t), F32)],
        compiler_params=pltpu.CompilerParams(
            dimension_semantics=("parallel",), vmem_limit_bytes=48 * MIB),
        name="peer_scores",
    )(x2d, norm_g.reshape(1, d), w_q.astype(BF16), sub_keys.astype(BF16))


def _candidate_blocks():
    blocks = []
    for i in range(PEER_TOPK // 2):
        nj = PEER_TOPK // (i + 1)
        for j0 in range(0, nj, SUBLANES):
            blocks.append(("row", i, j0, min(SUBLANES, nj - j0)))
    blocks.append(("col", PEER_TOPK // 2, PEER_TOPK // 2))
    return blocks


def _peer_topk_kernel(st_ref, r8_ref, par_ref, g_ref, r8t_ref):
    tk = st_ref.shape[2]
    key_id = lax.broadcasted_iota(I32, (N_KEYS, tk), 0).astype(F32)
    row16 = lax.broadcasted_iota(I32, (PEER_TOPK, tk), 0)
    row8 = lax.broadcasted_iota(I32, (SUBLANES, tk), 0)
    row8_f = row8.astype(F32)
    neg_inf = jnp.float32(-jnp.inf)
    blocks = _candidate_blocks()
    big = jnp.float32(PEER_TOPK * PEER_TOPK)

    def top16(x):
        vals = jnp.zeros((PEER_TOPK, tk), F32)
        idxs = jnp.zeros((PEER_TOPK, tk), F32)
        for k in range(PEER_TOPK):
            m = jnp.max(x, axis=0, keepdims=True)
            idx = jnp.min(jnp.where(x == m, key_id, float(N_KEYS)), axis=0, keepdims=True)
            x = jnp.where(key_id == idx, neg_inf, x)
            vals = jnp.where(row16 == k, m, vals)
            idxs = jnp.where(row16 == k, idx, idxs)
        return vals, idxs

    for h in range(PEER_HEADS):
        s0, i0 = top16(st_ref[2 * h])
        s1, i1 = top16(st_ref[2 * h + 1])
        cand, flat, expert = [], [], []
        for blk in blocks:
            if blk[0] == "row":
                _, i, j0, n = blk
                c = s0[i:i + 1, :] + s1[j0:j0 + SUBLANES, :]
                f = float(PEER_TOPK * i + j0) + row8_f
                e = i0[i:i + 1, :] * float(N_KEYS) + i1[j0:j0 + SUBLANES, :]
            else:
                _, i_start, n = blk
                c = s0[i_start:i_start + SUBLANES, :] + s1[0:1, :]
                f = float(PEER_TOPK) * (float(i_start) + row8_f)
                e = i0[i_start:i_start + SUBLANES, :] * float(N_KEYS) + i1[0:1, :]
            cand.append(jnp.where(row8 < n, c, neg_inf))
            flat.append(f)
            expert.append(e)
        cand = jnp.concatenate(cand, axis=0)
        flat = jnp.concatenate(flat, axis=0)
        expert = jnp.concatenate(expert, axis=0)
        best_s = jnp.zeros((PEER_TOPK, tk), F32)
        best_e = jnp.zeros((PEER_TOPK, tk), F32)
        for k in range(PEER_TOPK):
            m = jnp.max(cand, axis=0, keepdims=True)
            fidx = jnp.min(jnp.where(cand == m, flat, big), axis=0, keepdims=True)
            pick = flat == fidx
            e_sel = jnp.max(jnp.where(pick, expert, -1.0), axis=0, keepdims=True)
            cand = jnp.where(pick, neg_inf, cand)
            best_s = jnp.where(row16 == k, m, best_s)
            best_e = jnp.where(row16 == k, e_sel, best_e)
        ex = jnp.exp(best_s - best_s[0:1, :])
        gate = ex / jnp.sum(ex, axis=0, keepdims=True)
        rows = slice(h * PEER_TOPK, (h + 1) * PEER_TOPK)
        best_i = best_e.astype(I32)
        r8t_ref[rows, :] = (best_i >> 1) * SUBLANES
        par_ref[0, rows, :] = best_i & 1
        g_ref[0, rows, :] = gate
    r8_ref[0] = r8t_ref[...].T


def _peer_topk(scores_t):
    t = scores_t.shape[2]
    nblk = t // PEER_TOKENS
    slot_major = pl.BlockSpec((1, N_SLOTS, PEER_TOKENS), lambda i: (i, 0, 0))
    token_major = pl.BlockSpec((1, PEER_TOKENS, N_SLOTS), lambda i: (i, 0, 0))
    return pl.pallas_call(
        _peer_topk_kernel,
        grid=(nblk,),
        in_specs=[pl.BlockSpec((2 * PEER_HEADS, N_KEYS, PEER_TOKENS), lambda i: (0, 0, i))],
        out_specs=[token_major, slot_major, slot_major],
        out_shape=[jax.ShapeDtypeStruct((nblk, PEER_TOKENS, N_SLOTS), I32),
                   jax.ShapeDtypeStruct((nblk, N_SLOTS, PEER_TOKENS), I32),
                   jax.ShapeDtypeStruct((nblk, N_SLOTS, PEER_TOKENS), F32)],
        scratch_shapes=[pltpu.VMEM((N_SLOTS, PEER_TOKENS), I32)],
        compiler_params=pltpu.CompilerParams(dimension_semantics=("parallel",)),
        name="peer_topk",
    )(scores_t)


def _bf16_pair_words(x):
    bits = lax.bitcast_convert_type(x.astype(BF16).astype(F32), U32)
    half = bits.shape[1] // 2
    return (bits[:, :half] >> 16) | (bits[:, half:] & jnp.uint32(0xFFFF0000))


def _pack_kernel(t_ref, o_ref):
    words = _bf16_pair_words(t_ref[0])
    rows = words.shape[0]
    n = words.shape[1] // LANES
    for s in range(n):
        o_ref[pl.ds(s, rows, stride=n), :] = words[:, s * LANES:(s + 1) * LANES]


def _pack_table(tabs, layer):
    _, e, d = tabs.shape
    n = d // (2 * LANES)
    return pl.pallas_call(
        _pack_kernel,
        grid=(e // ROW_TILE,),
        in_specs=[pl.BlockSpec((1, ROW_TILE, d), lambda i: (layer, i, 0))],
        out_specs=pl.BlockSpec((ROW_TILE * n, LANES), lambda i: (i, 0)),
        out_shape=jax.ShapeDtypeStruct((e * n, LANES), U32),
        compiler_params=pltpu.CompilerParams(dimension_semantics=("parallel",)),
        name="peer_pack_table",
    )(tabs)


def _unpack_words(w):
    lo = lax.bitcast_convert_type(w << 16, F32)
    hi = lax.bitcast_convert_type(w & jnp.uint32(0xFFFF0000), F32)
    return lo, hi


def _as_bf16(words):
    return pltpu.bitcast(words, BF16)


def _as_words(packed):
    return pltpu.bitcast(packed, U32)


def _merge_packed(a, b, shift, mask):
    ta = a + _as_bf16(pltpu.roll(_as_words(a), shift, 0))
    tb = b + _as_bf16(pltpu.roll(_as_words(b), SUBLANES - shift, 0))
    return jnp.where(mask, _as_words(ta), _as_words(tb))


def _sublane_iota():
    return lax.broadcasted_iota(I32, (SUBLANES, LANES), 0)


def _merge(a, b, shift, mask):
    ta = a + pltpu.roll(a, shift, 0)
    tb = b + pltpu.roll(b, SUBLANES - shift, 0)
    return jnp.where(mask, ta, tb)


GROUP = SUBLANES // 2
N_GROUPS = N_SLOTS // GROUP
N_ROWS = 2 * N_SLOTS
N_SPREAD = 4
SPREAD_AHEAD = 2


def _dup_halves(v, low):
    r = pltpu.roll(v, GROUP, 0)
    return jnp.where(low, v, r), jnp.where(low, r, v)


def _peer_u_kernel(r8_ref, xp_ref, g_ref, par_ref, tab_ref, a_ref, s_ref, part_ref):
    tb = g_ref.shape[2]
    sub = _sublane_iota()
    m2 = (sub & 2) != 0
    lane_t = lax.broadcasted_iota(I32, (SUBLANES, tb), 1)
    n_part = 2 * N_GROUPS

    def products(t, slot):
        xb = _as_bf16(xp_ref[pl.ds(pl.multiple_of(t * SUBLANES, SUBLANES), SUBLANES), :])
        ids = r8_ref.at[0, t]
        for grp in range(N_GROUPS):
            prods = []
            for i in range(GROUP):
                r8 = pl.multiple_of(ids[grp * GROUP + i], SUBLANES)
                prods.append(_as_bf16(tab_ref[pl.ds(r8, SUBLANES), :]) * xb)
            halves = (_merge_packed(prods[3], prods[1], 2, m2), _merge_packed(prods[2], prods[0], 2, m2))
            for k in range(2):
                lo, hi = _unpack_words(halves[k])
                part_ref[slot, 2 * grp + k] = lo + hi

    def reduce_lanes(t, slot):
        for n in range(n_part):
            col = jnp.sum(part_ref[slot, n], axis=1, keepdims=True)
            rows = slice(n * SUBLANES, (n + 1) * SUBLANES)
            s_ref[rows, :] = jnp.where(lane_t == t, col, s_ref[rows, :])

    def token_pair(i, carry):
        t = 2 * i
        reduce_lanes(t - 2, 0)
        reduce_lanes(t - 1, 1)
        products(t, 0)
        products(t + 1, 1)
        return carry

    s_ref[...] = jnp.zeros_like(s_ref)
    part_ref[...] = jnp.zeros_like(part_ref)
    lax.fori_loop(0, tb // 2, token_pair, 0)
    reduce_lanes(tb - 2, 0)
    reduce_lanes(tb - 1, 1)

    sub_t = lax.broadcasted_iota(I32, (SUBLANES, tb), 0)
    low_t = sub_t < GROUP
    odd_t = (sub_t & 1) != 0
    want = jnp.where(low_t, 0, 1)
    for v in range(N_SLOTS // SUBLANES):
        src = slice(v * SUBLANES, (v + 1) * SUBLANES)
        gates = _dup_halves(g_ref[0, src, :], low_t)
        pars = _dup_halves(par_ref[0, src, :], low_t)
        for half in range(2):
            grp = 2 * v + half
            dots = _merge(s_ref[2 * grp * SUBLANES:(2 * grp + 1) * SUBLANES, :],
                          s_ref[(2 * grp + 1) * SUBLANES:(2 * grp + 2) * SUBLANES, :], 1, odd_t)
            act = gates[half] * jax.nn.gelu(dots)
            rows = slice(grp * SUBLANES, (grp + 1) * SUBLANES)
            a_ref[0, rows, :] = jnp.where(pars[half] == want, act, 0.0)


def _token_tiles(ref, xt_ref, to_tiles):
    tb, d = ref.shape
    for q in range(tb // SUBLANES):
        for c in range(d // LANES):
            rows = slice(q * SUBLANES, (q + 1) * SUBLANES)
            cols = slice(c * LANES, (c + 1) * LANES)
            strided = pl.ds(q * SUBLANES * SUBLANES + c, SUBLANES, stride=SUBLANES)
            if to_tiles:
                xt_ref[strided, :] = ref[rows, cols]
            else:
                ref[rows, cols] = xt_ref[strided, :]


def _peer_v_kernel(r8_ref, a_ref, x_ref, tab_ref, o_ref, bc_ref, xt_ref):
    tb = x_ref.shape[0]
    sub = _sublane_iota()
    low = sub < GROUP
    lane_t = lax.broadcasted_iota(I32, (SUBLANES, tb), 1)
    _token_tiles(x_ref, xt_ref, True)

    def spread(t, buf):
        for k in range(N_ROWS // SUBLANES):
            rows = slice(k * SUBLANES, (k + 1) * SUBLANES)
            col = jnp.sum(jnp.where(lane_t == t, a_ref[0, rows, :], 0.0), axis=1, keepdims=True)
            bits = lax.bitcast_convert_type(col.astype(BF16).astype(F32), U32)
            bc_ref[buf, rows, :] = jnp.broadcast_to(bits | (bits >> 16), (SUBLANES, LANES))

    def accumulate(t, buf):
        acc_lo = jnp.zeros((SUBLANES, LANES), F32)
        acc_hi = jnp.zeros((SUBLANES, LANES), F32)
        ids = r8_ref.at[0, t]
        for grp in range(N_GROUPS):
            prods = []
            for i in range(GROUP):
                row = grp * SUBLANES + i
                r8 = pl.multiple_of(ids[grp * GROUP + i], SUBLANES)
                am = jnp.where(low, bc_ref[buf, row:row + 1, :], bc_ref[buf, row + GROUP:row + GROUP + 1, :])
                prods.append(_as_bf16(tab_ref[pl.ds(r8, SUBLANES), :]) * _as_bf16(am))
            lo, hi = _unpack_words(_as_words((prods[0] + prods[1]) + (prods[2] + prods[3])))
            acc_lo = acc_lo + lo
            acc_hi = acc_hi + hi
        acc_lo = acc_lo + pltpu.roll(acc_lo, GROUP, 0)
        acc_hi = acc_hi + pltpu.roll(acc_hi, GROUP, 0)
        tile = pl.ds(pl.multiple_of(t * SUBLANES, SUBLANES), SUBLANES)
        xt_ref[tile, :] = xt_ref[tile, :] + jnp.where(low, acc_lo, acc_hi)

    def token_group(i, carry):
        t = N_SPREAD * i
        for k in range(N_SPREAD):
            spread(jnp.minimum(t + k + SPREAD_AHEAD, tb - 1), (k + SPREAD_AHEAD) % N_SPREAD)
            accumulate(t + k, k)
        return carry

    for k in range(SPREAD_AHEAD):
        spread(k, k)
    lax.fori_loop(0, tb // N_SPREAD, token_group, 0)
    _token_tiles(o_ref, xt_ref, False)


def _peer_experts(x2d, xp, r8, par, gates, u_tab, v_tab):
    t, d = x2d.shape
    nblk = t // PEER_TOKENS
    smem_block = pl.BlockSpec((1, PEER_TOKENS, N_SLOTS), lambda i: (i, 0, 0), memory_space=pltpu.SMEM)
    slot_block = pl.BlockSpec((1, N_SLOTS, PEER_TOKENS), lambda i: (i, 0, 0))
    row_block = pl.BlockSpec((1, N_ROWS, PEER_TOKENS), lambda i: (i, 0, 0))
    tile_block = pl.BlockSpec((PEER_TOKENS * SUBLANES, LANES), lambda i: (i, 0))
    tok_block = pl.BlockSpec((PEER_TOKENS, d), lambda i: (i, 0))
    whole = pl.BlockSpec(memory_space=pltpu.VMEM)
    params = pltpu.CompilerParams(dimension_semantics=("parallel",), vmem_limit_bytes=56 * MIB)

    act = pl.pallas_call(
        _peer_u_kernel,
        grid=(nblk,),
        in_specs=[smem_block, tile_block, slot_block, slot_block, whole],
        out_specs=row_block,
        out_shape=jax.ShapeDtypeStruct((nblk, N_ROWS, PEER_TOKENS), F32),
        scratch_shapes=[pltpu.VMEM((2 * N_ROWS, PEER_TOKENS), F32),
                        pltpu.VMEM((2, 2 * N_GROUPS, SUBLANES, LANES), F32)],
        compiler_params=params,
        name="peer_expert_in",
    )(r8, xp, gates, par, u_tab)

    return pl.pallas_call(
        _peer_v_kernel,
        grid=(nblk,),
        in_specs=[smem_block, row_block, tok_block, whole],
        out_specs=tok_block,
        out_shape=jax.ShapeDtypeStruct((t, d), F32),
        scratch_shapes=[pltpu.VMEM((N_SPREAD, N_ROWS, LANES), U32),
                        pltpu.VMEM((PEER_TOKENS * SUBLANES, LANES), F32)],
        compiler_params=params,
        name="peer_expert_out",
    )(r8, act, x2d, v_tab)


def _peer_ffn(x2d, norm_g, w_q, sub_keys, u_tabs, v_tabs, layer):
    xp, scores_t = _peer_scores(x2d, norm_g, w_q, sub_keys)
    r8, par, gates = _peer_topk(scores_t)
    return _peer_experts(x2d, xp, r8, par, gates, _pack_table(u_tabs, layer), _pack_table(v_tabs, layer))


def kernel(x, attn_norm_g, attn_w_qkv, attn_q_g, attn_k_g, attn_rel_bias, attn_w_out, rec_norm_g, rec_w_in, rec_conv_w, rec_conv_b, rec_w_a, rec_b_a, rec_w_x, rec_b_x, rec_lambda, rec_w_out, ffn_norm_g, peer_w_q, peer_sub_keys, peer_u, peer_v):
    b, s, d = x.shape
    depth = ffn_norm_g.shape[0]
    for layer in range(depth):
        j = layer // 2
        if layer % 2 == 0:
            qkv = _qkv_proj(x.reshape(b * s, d), attn_norm_g[j], attn_w_qkv[j], attn_q_g[j], attn_k_g[j])
            x = _attention(x, qkv, attn_rel_bias[j], attn_w_out[j])
        else:
            x = _rglru(x, rec_norm_g[j], rec_w_in[j], rec_conv_w[j], rec_conv_b[j], rec_w_a[j],
                       rec_b_a[j], rec_w_x[j], rec_b_x[j], rec_lambda[j], rec_w_out[j])
        x = _peer_ffn(x.reshape(b * s, d), ffn_norm_g[layer], peer_w_q[layer], peer_sub_keys[layer],
                      peer_u, peer_v, layer).reshape(b, s, d)
    return x
```

```python
import functools

import numpy as np
import jax
import jax.numpy as jnp
from jax import lax
from jax.experimental import pallas as pl
from jax.experimental.pallas import tpu as pltpu
from jax.experimental.pallas import tpu_sc as plsc

F32 = jnp.float32
BF16 = jnp.bfloat16
U32 = jnp.uint32
I32 = jnp.int32

EPS = 1e-6
NEG_INF = -1e30

D_MODEL = 1024
CHUNK = 64
ATTN_HEADS = 16
HEAD_DIM = 64
LEFT_CHUNKS = 8
REL_CLIP = 256
LRU_HEADS = 4
LRU_BLOCK = 256
CONV_WIDTH = 4
LRU_C = 8.0
PEER_HEADS = 8
N_KEYS = 128
PEER_TOPK = 16
D_HALF = 128

SUBLANES = 8
LANES = 128

Q_TILE = 4 * CHUNK
N_KEY_BLOCKS = 3
ROW_TILE = 512
PEER_TOKENS = 128
LRU_TILE = 256
N_SLOTS = PEER_HEADS * PEER_TOPK

MIB = 1024 * 1024

SC_CORES = 2
SC_SUBCORES = 16
SC_LANES = 16
SC_GATHER_ROWS = 32
SC_TOKEN_BLOCKS = 32


def _rms_rows(x, g):
    ms = jnp.mean(x * x, axis=-1, keepdims=True)
    return (x * lax.rsqrt(ms + EPS)) * g


def _split_bf16(v):
    hi = v.astype(BF16)
    lo = (v - hi.astype(F32)).astype(BF16)
    return hi, lo


def _dot(a, b):
    return jnp.dot(a, b, preferred_element_type=F32)


def _dot_nt(a, b):
    return lax.dot_general(a, b, (((1,), (1,)), ((), ())), preferred_element_type=F32)


def _qkv_kernel(x_ref, g_ref, w_ref, gain_ref, bsel_ref, bexp_ref, o_ref, xn_ref):
    j = pl.program_id(1)

    @pl.when(j == 0)
    def _():
        xn_ref[...] = _rms_rows(x_ref[...], g_ref[...]).astype(BF16)

    y = _dot(xn_ref[...], w_ref[...])

    @pl.when(j < 2)
    def _():
        hi, lo = _split_bf16(y * y)
        ms = _dot(hi, bsel_ref[...]) + _dot(lo, bsel_ref[...])
        rhi, rlo = _split_bf16(lax.rsqrt(ms + EPS))
        rs = _dot(rhi, bexp_ref[...]) + _dot(rlo, bexp_ref[...])
        o_ref[...] = ((y * rs) * gain_ref[0]).astype(BF16)

    @pl.when(j == 2)
    def _():
        o_ref[...] = y.astype(BF16)


def _qkv_proj(x2d, norm_g, w_qkv, q_g, k_g):
    t, d = x2d.shape
    scale = HEAD_DIM ** -0.5
    gains = jnp.stack([jnp.tile(q_g, ATTN_HEADS) * scale, jnp.tile(k_g, ATTN_HEADS),
                       jnp.ones((d,), F32)]).reshape(3, 1, d)
    head_of_col = np.arange(d) // HEAD_DIM
    bsel = (head_of_col[:, None] == np.arange(LANES)[None, :]).astype(np.float32) / HEAD_DIM
    bexp = (np.arange(LANES)[:, None] == head_of_col[None, :]).astype(np.float32)
    return pl.pallas_call(
        _qkv_kernel,
        grid=(t // ROW_TILE, 3),
        in_specs=[
            pl.BlockSpec((ROW_TILE, d), lambda i, j: (i, 0)),
            pl.BlockSpec((1, d), lambda i, j: (0, 0)),
            pl.BlockSpec((d, d), lambda i, j: (0, j)),
            pl.BlockSpec((1, 1, d), lambda i, j: (j, 0, 0)),
            pl.BlockSpec((d, LANES), lambda i, j: (0, 0)),
            pl.BlockSpec((LANES, d), lambda i, j: (0, 0)),
        ],
        out_specs=pl.BlockSpec((ROW_TILE, d), lambda i, j: (i, j)),
        out_shape=jax.ShapeDtypeStruct((t, 3 * d), BF16),
        scratch_shapes=[pltpu.VMEM((ROW_TILE, d), BF16)],
        compiler_params=pltpu.CompilerParams(
            dimension_semantics=("parallel", "arbitrary"), vmem_limit_bytes=40 * MIB),
        name="qkv_proj",
    )(x2d, norm_g.reshape(1, d), w_qkv.astype(BF16), gains,
      jnp.asarray(bsel, BF16), jnp.asarray(bexp, BF16))


def _attn_kernel(x_ref, q_ref, k0_ref, k1_ref, k2_ref, v0_ref, v1_ref, v2_ref,
                 bias_ref, wout_ref, o_ref, att_ref):
    qt = pl.program_id(1)
    k_refs = (k0_ref, k1_ref, k2_ref)
    v_refs = (v0_ref, v1_ref, v2_ref)
    negs = [jnp.where(qt - (N_KEY_BLOCKS - 1) + kb < 0, NEG_INF, 0.0).astype(F32)
            for kb in range(N_KEY_BLOCKS)]
    lane = lax.broadcasted_iota(I32, (Q_TILE, LANES), 1)
    first_half = lane < HEAD_DIM
    for p in range(ATTN_HEADS // 2):
        cols = slice(p * LANES, (p + 1) * LANES)
        qp = q_ref[0, :, cols]
        outs = []
        for hh in range(2):
            h = 2 * p + hh
            keep = first_half if hh == 0 else jnp.logical_not(first_half)
            qm = jnp.where(keep, qp, jnp.zeros_like(qp))
            s = [_dot_nt(qm, k_refs[kb][0, :, cols]) + bias_ref[kb, h] + negs[kb]
                 for kb in range(N_KEY_BLOCKS)]
            m = jnp.max(s[0], axis=-1, keepdims=True)
            for kb in range(1, N_KEY_BLOCKS):
                m = jnp.maximum(m, jnp.max(s[kb], axis=-1, keepdims=True))
            acc = jnp.zeros((Q_TILE, LANES), F32)
            l = jnp.zeros((Q_TILE, 1), F32)
            for kb in range(N_KEY_BLOCKS):
                e = jnp.exp(s[kb] - m)
                l = l + jnp.sum(e, axis=-1, keepdims=True)
                acc = acc + _dot(e.astype(BF16), v_refs[kb][0, :, cols])
            outs.append(acc / l)
        att_ref[:, cols] = jnp.where(first_half, outs[0], outs[1]).astype(BF16)
    o_ref[0] = x_ref[0] + _dot(att_ref[...], wout_ref[...])


def _band_bias(table):
    qi = np.arange(Q_TILE)[:, None]
    kj = np.arange(Q_TILE)[None, :]
    period = 2 * Q_TILE
    m = np.arange(period)
    delta = np.where(m < Q_TILE, m, m - period)
    blocks = []
    for kb in range(N_KEY_BLOCKS):
        dist = (N_KEY_BLOCKS - 1 - kb) * Q_TILE - delta
        diag = table[:, np.clip(dist, -REL_CLIP, REL_CLIP) + REL_CLIP]
        toep = jnp.tile(diag, (1, Q_TILE))[:, :Q_TILE * (period - 1)]
        toep = toep.reshape(-1, Q_TILE, period - 1)[:, :, :Q_TILE]
        dchunk = (Q_TILE // CHUNK) * (kb - (N_KEY_BLOCKS - 1)) + kj // CHUNK - qi // CHUNK
        valid = (dchunk >= -LEFT_CHUNKS) & (dchunk <= 0)
        blocks.append(jnp.where(valid[None], toep, NEG_INF))
    return jnp.stack(blocks).astype(F32)


def _attention(x, qkv, rel_bias_table, w_out):
    b, s, d = x.shape
    nq = s // Q_TILE
    qkv3 = qkv.reshape(b, s, 3 * d)
    bias = _band_bias(rel_bias_table)

    def kv_spec(kb, col):
        return pl.BlockSpec(
            (1, Q_TILE, d),
            lambda bi, qt: (bi, jnp.maximum(qt - (N_KEY_BLOCKS - 1) + kb, 0), col))

    in_specs = [pl.BlockSpec((1, Q_TILE, d), lambda bi, qt: (bi, qt, 0)),
                pl.BlockSpec((1, Q_TILE, d), lambda bi, qt: (bi, qt, 0))]
    in_specs += [kv_spec(kb, 1) for kb in range(N_KEY_BLOCKS)]
    in_specs += [kv_spec(kb, 2) for kb in range(N_KEY_BLOCKS)]
    in_specs += [pl.BlockSpec(memory_space=pltpu.VMEM), pl.BlockSpec(memory_space=pltpu.VMEM)]
    return pl.pallas_call(
        _attn_kernel,
        grid=(b, nq),
        in_specs=in_specs,
        out_specs=pl.BlockSpec((1, Q_TILE, d), lambda bi, qt: (bi, qt, 0)),
        out_shape=jax.ShapeDtypeStruct((b, s, d), F32),
        scratch_shapes=[pltpu.VMEM((Q_TILE, d), BF16)],
        compiler_params=pltpu.CompilerParams(
            dimension_semantics=("parallel", "parallel"), vmem_limit_bytes=48 * MIB),
        name="band_attention",
    )(x, qkv3, qkv3, qkv3, qkv3, qkv3, qkv3, qkv3, bias, w_out.astype(BF16))


def _rglru_kernel(x_ref, g_ref, win_ref, cw_ref, cb_ref, wa_ref, ba_ref, wx_ref, bx_ref,
                  lam_ref, wout_ref, o_ref, ext_ref, h_ref, a_scr, b_scr, hs_scr):
    ts = x_ref.shape[1]
    w = LRU_HEADS * LRU_BLOCK

    @pl.when(pl.program_id(1) == 0)
    def _():
        ext_ref[...] = jnp.zeros_like(ext_ref)
        h_ref[...] = jnp.zeros_like(h_ref)

    x = x_ref[0]
    xn = _rms_rows(x, g_ref[...]).astype(BF16)
    gu = _dot(xn, win_ref[...])
    gate = gu[:, :w]
    u_raw = gu[:, w:]
    ext_ref[0:SUBLANES, :] = ext_ref[ts:ts + SUBLANES, :]
    ext_ref[SUBLANES:, :] = u_raw
    u = cb_ref[...] + jnp.zeros((ts, w), F32)
    for k in range(CONV_WIDTH):
        off = SUBLANES - (CONV_WIDTH - 1) + k
        u = u + cw_ref[k:k + 1, :] * ext_ref[off:off + ts, :]
    ub = u.astype(BF16)

    def block_diag(wref):
        return jnp.concatenate(
            [_dot(ub[:, hh * LRU_BLOCK:(hh + 1) * LRU_BLOCK], wref[hh]) for hh in range(LRU_HEADS)],
            axis=1)

    r = jax.nn.sigmoid(block_diag(wa_ref) + ba_ref[...])
    i = jax.nn.sigmoid(block_diag(wx_ref) + bx_ref[...])
    z = -lam_ref[...]
    softplus = jnp.maximum(z, 0.0) + jnp.log1p(jnp.exp(-jnp.abs(z)))
    log_a = (-LRU_C) * r * softplus
    a_scr[...] = jnp.exp(log_a)
    th = jnp.tanh(log_a)
    b_scr[...] = jnp.sqrt(2.0 * th / (th - 1.0)) * (i * u)

    def step(t, h):
        h = a_scr[pl.ds(t, 1), :] * h + b_scr[pl.ds(t, 1), :]
        hs_scr[pl.ds(t, 1), :] = h
        return h

    h_ref[0:1, :] = lax.fori_loop(0, ts, step, h_ref[0:1, :], unroll=8)
    y = jax.nn.gelu(gate) * hs_scr[...]
    o_ref[0] = x + _dot(y.astype(BF16), wout_ref[...])


def _rglru(x, norm_g, w_in, conv_w, conv_b, w_a, b_a, w_x, b_x, lam, w_out):
    b, s, d = x.shape
    w = LRU_HEADS * LRU_BLOCK
    row = lambda v: v.reshape(1, -1)
    whole = pl.BlockSpec(memory_space=pltpu.VMEM)
    return pl.pallas_call(
        _rglru_kernel,
        grid=(b, s // LRU_TILE),
        in_specs=[pl.BlockSpec((1, LRU_TILE, d), lambda bi, si: (bi, si, 0))] + [whole] * 10,
        out_specs=pl.BlockSpec((1, LRU_TILE, d), lambda bi, si: (bi, si, 0)),
        out_shape=jax.ShapeDtypeStruct((b, s, d), F32),
        scratch_shapes=[pltpu.VMEM((LRU_TILE + SUBLANES, w), F32), pltpu.VMEM((SUBLANES, w), F32),
                        pltpu.VMEM((LRU_TILE, w), F32), pltpu.VMEM((LRU_TILE, w), F32),
                        pltpu.VMEM((LRU_TILE, w), F32)],
        compiler_params=pltpu.CompilerParams(
            dimension_semantics=("arbitrary", "arbitrary"), vmem_limit_bytes=48 * MIB),
        name="rglru_mixer",
    )(x, row(norm_g), w_in.astype(BF16), conv_w, row(conv_b), w_a.astype(BF16), row(b_a),
      w_x.astype(BF16), row(b_x), row(lam), w_out.astype(BF16))


def _peer_scores_kernel(x_ref, g_ref, wq_ref, sk_ref, xp_ref, st_ref, xn_ref):
    xn = _rms_rows(x_ref[...], g_ref[...])
    xn_ref[...] = xn
    xb = xn.astype(BF16)
    words = _bf16_pair_words(xb)
    tm = words.shape[0]
    for c in range(SUBLANES):
        piece = words[:, (c % (SUBLANES // 2)) * LANES:(c % (SUBLANES // 2) + 1) * LANES]
        xp_ref[pl.ds(c, tm, stride=SUBLANES), :] = piece
    q = _dot(xb, wq_ref[...]).astype(BF16)
    for hp in range(2 * PEER_HEADS):
        st_ref[hp] = _dot_nt(sk_ref[hp % 2], q[:, hp * D_HALF:(hp + 1) * D_HALF])


def _peer_scores(x2d, norm_g, w_q, sub_keys):
    t, d = x2d.shape
    nq = w_q.shape[1]
    whole = pl.BlockSpec(memory_space=pltpu.VMEM)
    return pl.pallas_call(
        _peer_scores_kernel,
        grid=(t // ROW_TILE,),
        in_specs=[pl.BlockSpec((ROW_TILE, d), lambda i: (i, 0)), whole, whole, whole],
        out_specs=[pl.BlockSpec((ROW_TILE * SUBLANES, LANES), lambda i: (i, 0)),
                   pl.BlockSpec((2 * PEER_HEADS, N_KEYS, ROW_TILE), lambda i: (0, 0, i)),
                   pl.BlockSpec((ROW_TILE, d), lambda i: (i, 0))],
        out_shape=[jax.ShapeDtypeStruct((t * SUBLANES, LANES), U32),
                   jax.ShapeDtypeStruct((2 * PEER_HEADS, N_KEYS, t), F32),
                   jax.ShapeDtypeStruct((t, d), F32)],
        compiler_params=pltpu.CompilerParams(
            dimension_semantics=("parallel",), vmem_limit_bytes=48 * MIB),
        name="peer_scores",
    )(x2d, norm_g.reshape(1, d), w_q.astype(BF16), sub_keys.astype(BF16))


def _candidate_blocks():
    blocks = []
    for i in range(PEER_TOPK // 2):
        nj = PEER_TOPK // (i + 1)
        for j0 in range(0, nj, SUBLANES):
            blocks.append(("row", i, j0, min(SUBLANES, nj - j0)))
    blocks.append(("col", PEER_TOPK // 2, PEER_TOPK // 2))
    return blocks


def _peer_topk_kernel(st_ref, r8_ref, par_ref, g_ref, e_ref, r8t_ref, et_ref, *, expert_offset):
    tk = st_ref.shape[2]
    key_id = lax.broadcasted_iota(I32, (N_KEYS, tk), 0).astype(F32)
    row16 = lax.broadcasted_iota(I32, (PEER_TOPK, tk), 0)
    row8 = lax.broadcasted_iota(I32, (SUBLANES, tk), 0)
    row8_f = row8.astype(F32)
    neg_inf = jnp.float32(-jnp.inf)
    blocks = _candidate_blocks()
    big = jnp.float32(PEER_TOPK * PEER_TOPK)

    def top16(x):
        vals = jnp.zeros((PEER_TOPK, tk), F32)
        idxs = jnp.zeros((PEER_TOPK, tk), F32)
        for k in range(PEER_TOPK):
            m = jnp.max(x, axis=0, keepdims=True)
            idx = jnp.min(jnp.where(x == m, key_id, float(N_KEYS)), axis=0, keepdims=True)
            x = jnp.where(key_id == idx, neg_inf, x)
            vals = jnp.where(row16 == k, m, vals)
            idxs = jnp.where(row16 == k, idx, idxs)
        return vals, idxs

    for h in range(PEER_HEADS):
        s0, i0 = top16(st_ref[2 * h])
        s1, i1 = top16(st_ref[2 * h + 1])
        cand, flat, expert = [], [], []
        for blk in blocks:
            if blk[0] == "row":
                _, i, j0, n = blk
                c = s0[i:i + 1, :] + s1[j0:j0 + SUBLANES, :]
                f = float(PEER_TOPK * i + j0) + row8_f
                e = i0[i:i + 1, :] * float(N_KEYS) + i1[j0:j0 + SUBLANES, :]
            else:
                _, i_start, n = blk
                c = s0[i_start:i_start + SUBLANES, :] + s1[0:1, :]
                f = float(PEER_TOPK) * (float(i_start) + row8_f)
                e = i0[i_start:i_start + SUBLANES, :] * float(N_KEYS) + i1[0:1, :]
            cand.append(jnp.where(row8 < n, c, neg_inf))
            flat.append(f)
            expert.append(e)
        cand = jnp.concatenate(cand, axis=0)
        flat = jnp.concatenate(flat, axis=0)
        expert = jnp.concatenate(expert, axis=0)
        best_s = jnp.zeros((PEER_TOPK, tk), F32)
        best_e = jnp.zeros((PEER_TOPK, tk), F32)
        for k in range(PEER_TOPK):
            m = jnp.max(cand, axis=0, keepdims=True)
            fidx = jnp.min(jnp.where(cand == m, flat, big), axis=0, keepdims=True)
            pick = flat == fidx
            e_sel = jnp.max(jnp.where(pick, expert, -1.0), axis=0, keepdims=True)
            cand = jnp.where(pick, neg_inf, cand)
            best_s = jnp.where(row16 == k, m, best_s)
            best_e = jnp.where(row16 == k, e_sel, best_e)
        ex = jnp.exp(best_s - best_s[0:1, :])
        gate = ex / jnp.sum(ex, axis=0, keepdims=True)
        rows = slice(h * PEER_TOPK, (h + 1) * PEER_TOPK)
        best_i = best_e.astype(I32)
        r8t_ref[rows, :] = (best_i >> 1) * SUBLANES
        et_ref[rows, :] = best_i + expert_offset
        par_ref[0, rows, :] = best_i & 1
        g_ref[0, rows, :] = gate
    r8_ref[0] = r8t_ref[...].T
    e_ref[0] = et_ref[...].T


def _peer_topk(scores_t, expert_offset):
    t = scores_t.shape[2]
    nblk = t // PEER_TOKENS
    slot_major = pl.BlockSpec((1, N_SLOTS, PEER_TOKENS), lambda i: (i, 0, 0))
    token_major = pl.BlockSpec((1, PEER_TOKENS, N_SLOTS), lambda i: (i, 0, 0))
    return pl.pallas_call(
        functools.partial(_peer_topk_kernel, expert_offset=expert_offset),
        grid=(nblk,),
        in_specs=[pl.BlockSpec((2 * PEER_HEADS, N_KEYS, PEER_TOKENS), lambda i: (0, 0, i))],
        out_specs=[token_major, slot_major, slot_major, token_major],
        out_shape=[jax.ShapeDtypeStruct((nblk, PEER_TOKENS, N_SLOTS), I32),
                   jax.ShapeDtypeStruct((nblk, N_SLOTS, PEER_TOKENS), I32),
                   jax.ShapeDtypeStruct((nblk, N_SLOTS, PEER_TOKENS), F32),
                   jax.ShapeDtypeStruct((nblk, PEER_TOKENS, N_SLOTS), I32)],
        scratch_shapes=[pltpu.VMEM((N_SLOTS, PEER_TOKENS), I32), pltpu.VMEM((N_SLOTS, PEER_TOKENS), I32)],
        compiler_params=pltpu.CompilerParams(dimension_semantics=("parallel",)),
        name="peer_topk",
    )(scores_t)


def _bf16_pair_words(x):
    bits = lax.bitcast_convert_type(x.astype(BF16).astype(F32), U32)
    half = bits.shape[1] // 2
    return (bits[:, :half] >> 16) | (bits[:, half:] & jnp.uint32(0xFFFF0000))


def _pack_kernel(t_ref, o_ref):
    words = _bf16_pair_words(t_ref[0])
    rows = words.shape[0]
    n = words.shape[1] // LANES
    for s in range(n):
        o_ref[pl.ds(s, rows, stride=n), :] = words[:, s * LANES:(s + 1) * LANES]


def _pack_table(tabs, layer):
    _, e, d = tabs.shape
    n = d // (2 * LANES)
    return pl.pallas_call(
        _pack_kernel,
        grid=(e // ROW_TILE,),
        in_specs=[pl.BlockSpec((1, ROW_TILE, d), lambda i: (layer, i, 0))],
        out_specs=pl.BlockSpec((ROW_TILE * n, LANES), lambda i: (i, 0)),
        out_shape=jax.ShapeDtypeStruct((e * n, LANES), U32),
        compiler_params=pltpu.CompilerParams(dimension_semantics=("parallel",)),
        name="peer_pack_table",
    )(tabs)


def _unpack_words(w):
    lo = lax.bitcast_convert_type(w << 16, F32)
    hi = lax.bitcast_convert_type(w & jnp.uint32(0xFFFF0000), F32)
    return lo, hi


def _as_bf16(words):
    return pltpu.bitcast(words, BF16)


def _as_words(packed):
    return pltpu.bitcast(packed, U32)


def _merge_packed(a, b, shift, mask):
    ta = a + _as_bf16(pltpu.roll(_as_words(a), shift, 0))
    tb = b + _as_bf16(pltpu.roll(_as_words(b), SUBLANES - shift, 0))
    return jnp.where(mask, _as_words(ta), _as_words(tb))


def _sublane_iota():
    return lax.broadcasted_iota(I32, (SUBLANES, LANES), 0)


def _merge(a, b, shift, mask):
    ta = a + pltpu.roll(a, shift, 0)
    tb = b + pltpu.roll(b, SUBLANES - shift, 0)
    return jnp.where(mask, ta, tb)


GROUP = SUBLANES // 2
N_GROUPS = N_SLOTS // GROUP
N_ROWS = 2 * N_SLOTS
N_SPREAD = 4
SPREAD_AHEAD = 2


def _dup_halves(v, low):
    r = pltpu.roll(v, GROUP, 0)
    return jnp.where(low, v, r), jnp.where(low, r, v)


def _peer_u_kernel(r8_ref, xp_ref, g_ref, par_ref, tab_ref, a_ref, s_ref, part_ref):
    tb = g_ref.shape[2]
    sub = _sublane_iota()
    m2 = (sub & 2) != 0
    lane_t = lax.broadcasted_iota(I32, (SUBLANES, tb), 1)
    n_part = 2 * N_GROUPS

    def products(t, slot):
        xb = _as_bf16(xp_ref[pl.ds(pl.multiple_of(t * SUBLANES, SUBLANES), SUBLANES), :])
        ids = r8_ref.at[0, t]
        for grp in range(N_GROUPS):
            prods = []
            for i in range(GROUP):
                r8 = pl.multiple_of(ids[grp * GROUP + i], SUBLANES)
                prods.append(_as_bf16(tab_ref[pl.ds(r8, SUBLANES), :]) * xb)
            halves = (_merge_packed(prods[3], prods[1], 2, m2), _merge_packed(prods[2], prods[0], 2, m2))
            for k in range(2):
                lo, hi = _unpack_words(halves[k])
                part_ref[slot, 2 * grp + k] = lo + hi

    def reduce_lanes(t, slot):
        for n in range(n_part):
            col = jnp.sum(part_ref[slot, n], axis=1, keepdims=True)
            rows = slice(n * SUBLANES, (n + 1) * SUBLANES)
            s_ref[rows, :] = jnp.where(lane_t == t, col, s_ref[rows, :])

    def token_pair(i, carry):
        t = 2 * i
        reduce_lanes(t - 2, 0)
        reduce_lanes(t - 1, 1)
        products(t, 0)
        products(t + 1, 1)
        return carry

    s_ref[...] = jnp.zeros_like(s_ref)
    part_ref[...] = jnp.zeros_like(part_ref)
    lax.fori_loop(0, tb // 2, token_pair, 0)
    reduce_lanes(tb - 2, 0)
    reduce_lanes(tb - 1, 1)

    odd_t = (lax.broadcasted_iota(I32, (SUBLANES, tb), 0) & 1) != 0
    _gate_rows(lambda grp: _merge(s_ref[2 * grp * SUBLANES:(2 * grp + 1) * SUBLANES, :],
                                  s_ref[(2 * grp + 1) * SUBLANES:(2 * grp + 2) * SUBLANES, :], 1, odd_t),
               g_ref, par_ref, a_ref)


def _gate_rows(dots_of_group, g_ref, par_ref, a_ref):
    tb = g_ref.shape[2]
    low_t = lax.broadcasted_iota(I32, (SUBLANES, tb), 0) < GROUP
    want = jnp.where(low_t, 0, 1)
    for v in range(N_SLOTS // SUBLANES):
        src = slice(v * SUBLANES, (v + 1) * SUBLANES)
        gates = _dup_halves(g_ref[0, src, :], low_t)
        pars = _dup_halves(par_ref[0, src, :], low_t)
        for half in range(2):
            grp = 2 * v + half
            act = gates[half] * jax.nn.gelu(dots_of_group(grp))
            rows = slice(grp * SUBLANES, (grp + 1) * SUBLANES)
            a_ref[0, rows, :] = jnp.where(pars[half] == want, act, 0.0)


def _sc_u_dots(u_rows, e_tok, xn, first_token, n_tokens):
    d = xn.shape[1]
    n_workers = SC_CORES * SC_SUBCORES
    per_worker = n_tokens // n_workers
    mesh = plsc.VectorSubcoreMesh(core_axis_name="c", subcore_axis_name="s")

    def body(u_hbm, e_hbm, x_hbm, o_hbm, idx_v, x_v, rows_v, out_v, sem):
        wid = lax.axis_index("s") * SC_CORES + lax.axis_index("c")
        lane = lax.iota(I32, SC_LANES)

        def token(i, carry):
            local = wid * per_worker + i
            pltpu.sync_copy(e_hbm.at[first_token + local], idx_v)
            pltpu.sync_copy(x_hbm.at[first_token + local], x_v)
            for chunk in range(N_SLOTS // SC_GATHER_ROWS):
                ids = idx_v.at[pl.ds(chunk * SC_GATHER_ROWS, SC_GATHER_ROWS)]
                pltpu.async_copy(u_hbm.at[ids], rows_v, sem).wait()
                for part in range(SC_GATHER_ROWS // SC_LANES):
                    def row(r, vec):
                        rr = part * SC_LANES + r
                        acc = jnp.zeros((SC_LANES,), F32)
                        for c in range(d // SC_LANES):
                            cols = pl.ds(c * SC_LANES, SC_LANES)
                            acc = acc + rows_v[rr, cols] * x_v[cols]
                        return jnp.where(lane == r, jnp.sum(acc), vec)
                    vec = lax.fori_loop(0, SC_LANES, row, jnp.zeros((SC_LANES,), F32))
                    out_v[pl.ds(chunk * SC_GATHER_ROWS + part * SC_LANES, SC_LANES)] = vec
            pltpu.sync_copy(out_v, o_hbm.at[local])
            return carry

        lax.fori_loop(0, per_worker, token, 0)

    return pl.kernel(
        body,
        out_type=jax.ShapeDtypeStruct((n_tokens, N_SLOTS), F32),
        mesh=mesh,
        scratch_types=[pltpu.VMEM((N_SLOTS,), I32), pltpu.VMEM((d,), F32),
                       pltpu.VMEM((SC_GATHER_ROWS, d), F32), pltpu.VMEM((N_SLOTS,), F32),
                       pltpu.SemaphoreType.DMA],
        compiler_params=pltpu.CompilerParams(needs_layout_passes=False),
        name="peer_expert_in_sc",
    )(u_rows, e_tok, xn)


def _peer_gate_kernel(s_ref, g_ref, par_ref, a_ref):
    dots = s_ref[...].T
    low_t = lax.broadcasted_iota(I32, (SUBLANES, dots.shape[1]), 0) < GROUP
    _gate_rows(lambda grp: _dup_halves(dots[(grp // 2) * SUBLANES:(grp // 2 + 1) * SUBLANES, :], low_t)[grp % 2],
               g_ref, par_ref, a_ref)


def _token_tiles(ref, xt_ref, to_tiles):
    tb, d = ref.shape
    for q in range(tb // SUBLANES):
        for c in range(d // LANES):
            rows = slice(q * SUBLANES, (q + 1) * SUBLANES)
            cols = slice(c * LANES, (c + 1) * LANES)
            strided = pl.ds(q * SUBLANES * SUBLANES + c, SUBLANES, stride=SUBLANES)
            if to_tiles:
                xt_ref[strided, :] = ref[rows, cols]
            else:
                ref[rows, cols] = xt_ref[strided, :]


def _peer_v_kernel(r8_ref, a_ref, x_ref, tab_ref, o_ref, bc_ref, xt_ref):
    tb = x_ref.shape[0]
    sub = _sublane_iota()
    low = sub < GROUP
    lane_t = lax.broadcasted_iota(I32, (SUBLANES, tb), 1)
    _token_tiles(x_ref, xt_ref, True)

    def spread(t, buf):
        for k in range(N_ROWS // SUBLANES):
            rows = slice(k * SUBLANES, (k + 1) * SUBLANES)
            col = jnp.sum(jnp.where(lane_t == t, a_ref[0, rows, :], 0.0), axis=1, keepdims=True)
            bits = lax.bitcast_convert_type(col.astype(BF16).astype(F32), U32)
            bc_ref[buf, rows, :] = jnp.broadcast_to(bits | (bits >> 16), (SUBLANES, LANES))

    def accumulate(t, buf):
        acc_lo = jnp.zeros((SUBLANES, LANES), F32)
        acc_hi = jnp.zeros((SUBLANES, LANES), F32)
        ids = r8_ref.at[0, t]
        for grp in range(N_GROUPS):
            prods = []
            for i in range(GROUP):
                row = grp * SUBLANES + i
                r8 = pl.multiple_of(ids[grp * GROUP + i], SUBLANES)
                am = jnp.where(low, bc_ref[buf, row:row + 1, :], bc_ref[buf, row + GROUP:row + GROUP + 1, :])
                prods.append(_as_bf16(tab_ref[pl.ds(r8, SUBLANES), :]) * _as_bf16(am))
            lo, hi = _unpack_words(_as_words((prods[0] + prods[1]) + (prods[2] + prods[3])))
            acc_lo = acc_lo + lo
            acc_hi = acc_hi + hi
        acc_lo = acc_lo + pltpu.roll(acc_lo, GROUP, 0)
        acc_hi = acc_hi + pltpu.roll(acc_hi, GROUP, 0)
        tile = pl.ds(pl.multiple_of(t * SUBLANES, SUBLANES), SUBLANES)
        xt_ref[tile, :] = xt_ref[tile, :] + jnp.where(low, acc_lo, acc_hi)

    def token_group(i, carry):
        t = N_SPREAD * i
        for k in range(N_SPREAD):
            spread(jnp.minimum(t + k + SPREAD_AHEAD, tb - 1), (k + SPREAD_AHEAD) % N_SPREAD)
            accumulate(t + k, k)
        return carry

    for k in range(SPREAD_AHEAD):
        spread(k, k)
    lax.fori_loop(0, tb // N_SPREAD, token_group, 0)
    _token_tiles(o_ref, xt_ref, False)


def _peer_experts(x2d, xp, xn, r8, e_tok, par, gates, u_rows, u_tab, v_tab):
    t, d = x2d.shape
    nblk = t // PEER_TOKENS
    nblk_tc = nblk - SC_TOKEN_BLOCKS
    smem_block = pl.BlockSpec((1, PEER_TOKENS, N_SLOTS), lambda i: (i, 0, 0), memory_space=pltpu.SMEM)
    slot_block = pl.BlockSpec((1, N_SLOTS, PEER_TOKENS), lambda i: (i, 0, 0))
    row_block = pl.BlockSpec((1, N_ROWS, PEER_TOKENS), lambda i: (i, 0, 0))
    tile_block = pl.BlockSpec((PEER_TOKENS * SUBLANES, LANES), lambda i: (i, 0))
    tok_block = pl.BlockSpec((PEER_TOKENS, d), lambda i: (i, 0))
    whole = pl.BlockSpec(memory_space=pltpu.VMEM)
    params = pltpu.CompilerParams(dimension_semantics=("parallel",), vmem_limit_bytes=56 * MIB)

    dots_sc = _sc_u_dots(u_rows, e_tok.reshape(t, N_SLOTS), xn, nblk_tc * PEER_TOKENS,
                         SC_TOKEN_BLOCKS * PEER_TOKENS)
    act_tc = pl.pallas_call(
        _peer_u_kernel,
        grid=(nblk_tc,),
        in_specs=[smem_block, tile_block, slot_block, slot_block, whole],
        out_specs=row_block,
        out_shape=jax.ShapeDtypeStruct((nblk_tc, N_ROWS, PEER_TOKENS), F32),
        scratch_shapes=[pltpu.VMEM((2 * N_ROWS, PEER_TOKENS), F32),
                        pltpu.VMEM((2, 2 * N_GROUPS, SUBLANES, LANES), F32)],
        compiler_params=params,
        name="peer_expert_in",
    )(r8, xp, gates, par, u_tab)
    shifted = pl.BlockSpec((1, N_SLOTS, PEER_TOKENS), lambda i: (i + nblk_tc, 0, 0))
    act_sc = pl.pallas_call(
        _peer_gate_kernel,
        grid=(SC_TOKEN_BLOCKS,),
        in_specs=[pl.BlockSpec((PEER_TOKENS, N_SLOTS), lambda i: (i, 0)), shifted, shifted],
        out_specs=row_block,
        out_shape=jax.ShapeDtypeStruct((SC_TOKEN_BLOCKS, N_ROWS, PEER_TOKENS), F32),
        compiler_params=pltpu.CompilerParams(dimension_semantics=("parallel",)),
        name="peer_gate_sc",
    )(dots_sc, gates, par)
    act = jnp.concatenate([act_tc, act_sc], axis=0)

    return pl.pallas_call(
        _peer_v_kernel,
        grid=(nblk,),
        in_specs=[smem_block, row_block, tok_block, whole],
        out_specs=tok_block,
        out_shape=jax.ShapeDtypeStruct((t, d), F32),
        scratch_shapes=[pltpu.VMEM((N_SPREAD, N_ROWS, LANES), U32),
                        pltpu.VMEM((PEER_TOKENS * SUBLANES, LANES), F32)],
        compiler_params=params,
        name="peer_expert_out",
    )(r8, act, x2d, v_tab)


def _peer_ffn(x2d, norm_g, w_q, sub_keys, u_tabs, v_tabs, layer):
    n_experts, d = u_tabs.shape[1:]
    xp, scores_t, xn = _peer_scores(x2d, norm_g, w_q, sub_keys)
    r8, par, gates, e_tok = _peer_topk(scores_t, layer * n_experts)
    return _peer_experts(x2d, xp, xn, r8, e_tok, par, gates, u_tabs.reshape(-1, d),
                         _pack_table(u_tabs, layer), _pack_table(v_tabs, layer))


def kernel(x, attn_norm_g, attn_w_qkv, attn_q_g, attn_k_g, attn_rel_bias, attn_w_out, rec_norm_g, rec_w_in, rec_conv_w, rec_conv_b, rec_w_a, rec_b_a, rec_w_x, rec_b_x, rec_lambda, rec_w_out, ffn_norm_g, peer_w_q, peer_sub_keys, peer_u, peer_v):
    b, s, d = x.shape
    depth = ffn_norm_g.shape[0]
    for layer in range(depth):
        j = layer // 2
        if layer % 2 == 0:
            qkv = _qkv_proj(x.reshape(b * s, d), attn_norm_g[j], attn_w_qkv[j], attn_q_g[j], attn_k_g[j])
            x = _attention(x, qkv, attn_rel_bias[j], attn_w_out[j])
        else:
            x = _rglru(x, rec_norm_g[j], rec_w_in[j], rec_conv_w[j], rec_conv_b[j], rec_w_a[j],
                       rec_b_a[j], rec_w_x[j], rec_b_x[j], rec_lambda[j], rec_w_out[j])
        x = _peer_ffn(x.reshape(b * s, d), ffn_norm_g[layer], peer_w_q[layer], peer_sub_keys[layer],
                      peer_u, peer_v, layer).reshape(b, s, d)
    return x
```

```python
import functools

import numpy as np
import jax
import jax.numpy as jnp
from jax import lax
from jax.experimental import pallas as pl
from jax.experimental.pallas import tpu as pltpu
from jax.experimental.pallas import tpu_sc as plsc

F32 = jnp.float32
BF16 = jnp.bfloat16
U32 = jnp.uint32
I32 = jnp.int32

EPS = 1e-6
NEG_INF = -1e30

D_MODEL = 1024
CHUNK = 64
ATTN_HEADS = 16
HEAD_DIM = 64
LEFT_CHUNKS = 8
REL_CLIP = 256
LRU_HEADS = 4
LRU_BLOCK = 256
CONV_WIDTH = 4
LRU_C = 8.0
PEER_HEADS = 8
N_KEYS = 128
PEER_TOPK = 16
D_HALF = 128

SUBLANES = 8
LANES = 128

Q_TILE = 4 * CHUNK
N_KEY_BLOCKS = 3
ROW_TILE = 512
PEER_TOKENS = 128
LRU_TILE = 256
N_SLOTS = PEER_HEADS * PEER_TOPK

MIB = 1024 * 1024

SC_CORES = 2
SC_SUBCORES = 16
SC_LANES = 16
SC_GATHER_ROWS = 32
SC_TOKEN_BLOCKS = 48
SC_V_BLOCKS = 64


def _rms_rows(x, g):
    ms = jnp.mean(x * x, axis=-1, keepdims=True)
    return (x * lax.rsqrt(ms + EPS)) * g


def _split_bf16(v):
    hi = v.astype(BF16)
    lo = (v - hi.astype(F32)).astype(BF16)
    return hi, lo


def _dot(a, b):
    return jnp.dot(a, b, preferred_element_type=F32)


def _dot_nt(a, b):
    return lax.dot_general(a, b, (((1,), (1,)), ((), ())), preferred_element_type=F32)


def _qkv_kernel(x_ref, g_ref, w_ref, gain_ref, bsel_ref, bexp_ref, o_ref, xn_ref):
    j = pl.program_id(1)

    @pl.when(j == 0)
    def _():
        xn_ref[...] = _rms_rows(x_ref[...], g_ref[...]).astype(BF16)

    y = _dot(xn_ref[...], w_ref[...])

    @pl.when(j < 2)
    def _():
        hi, lo = _split_bf16(y * y)
        ms = _dot(hi, bsel_ref[...]) + _dot(lo, bsel_ref[...])
        rhi, rlo = _split_bf16(lax.rsqrt(ms + EPS))
        rs = _dot(rhi, bexp_ref[...]) + _dot(rlo, bexp_ref[...])
        o_ref[...] = ((y * rs) * gain_ref[0]).astype(BF16)

    @pl.when(j == 2)
    def _():
        o_ref[...] = y.astype(BF16)


def _qkv_proj(x2d, norm_g, w_qkv, q_g, k_g):
    t, d = x2d.shape
    scale = HEAD_DIM ** -0.5
    gains = jnp.stack([jnp.tile(q_g, ATTN_HEADS) * scale, jnp.tile(k_g, ATTN_HEADS),
                       jnp.ones((d,), F32)]).reshape(3, 1, d)
    head_of_col = np.arange(d) // HEAD_DIM
    bsel = (head_of_col[:, None] == np.arange(LANES)[None, :]).astype(np.float32) / HEAD_DIM
    bexp = (np.arange(LANES)[:, None] == head_of_col[None, :]).astype(np.float32)
    return pl.pallas_call(
        _qkv_kernel,
        grid=(t // ROW_TILE, 3),
        in_specs=[
            pl.BlockSpec((ROW_TILE, d), lambda i, j: (i, 0)),
            pl.BlockSpec((1, d), lambda i, j: (0, 0)),
            pl.BlockSpec((d, d), lambda i, j: (0, j)),
            pl.BlockSpec((1, 1, d), lambda i, j: (j, 0, 0)),
            pl.BlockSpec((d, LANES), lambda i, j: (0, 0)),
            pl.BlockSpec((LANES, d), lambda i, j: (0, 0)),
        ],
        out_specs=pl.BlockSpec((ROW_TILE, d), lambda i, j: (i, j)),
        out_shape=jax.ShapeDtypeStruct((t, 3 * d), BF16),
        scratch_shapes=[pltpu.VMEM((ROW_TILE, d), BF16)],
        compiler_params=pltpu.CompilerParams(
            dimension_semantics=("parallel", "arbitrary"), vmem_limit_bytes=40 * MIB),
        name="qkv_proj",
    )(x2d, norm_g.reshape(1, d), w_qkv.astype(BF16), gains,
      jnp.asarray(bsel, BF16), jnp.asarray(bexp, BF16))


def _attn_kernel(x_ref, q_ref, k0_ref, k1_ref, k2_ref, v0_ref, v1_ref, v2_ref,
                 bias_ref, wout_ref, o_ref, att_ref):
    qt = pl.program_id(1)
    k_refs = (k0_ref, k1_ref, k2_ref)
    v_refs = (v0_ref, v1_ref, v2_ref)
    negs = [jnp.where(qt - (N_KEY_BLOCKS - 1) + kb < 0, NEG_INF, 0.0).astype(F32)
            for kb in range(N_KEY_BLOCKS)]
    lane = lax.broadcasted_iota(I32, (Q_TILE, LANES), 1)
    first_half = lane < HEAD_DIM
    for p in range(ATTN_HEADS // 2):
        cols = slice(p * LANES, (p + 1) * LANES)
        qp = q_ref[0, :, cols]
        outs = []
        for hh in range(2):
            h = 2 * p + hh
            keep = first_half if hh == 0 else jnp.logical_not(first_half)
            qm = jnp.where(keep, qp, jnp.zeros_like(qp))
            s = [_dot_nt(qm, k_refs[kb][0, :, cols]) + bias_ref[kb, h] + negs[kb]
                 for kb in range(N_KEY_BLOCKS)]
            m = jnp.max(s[0], axis=-1, keepdims=True)
            for kb in range(1, N_KEY_BLOCKS):
                m = jnp.maximum(m, jnp.max(s[kb], axis=-1, keepdims=True))
            acc = jnp.zeros((Q_TILE, LANES), F32)
            l = jnp.zeros((Q_TILE, 1), F32)
            for kb in range(N_KEY_BLOCKS):
                e = jnp.exp(s[kb] - m)
                l = l + jnp.sum(e, axis=-1, keepdims=True)
                acc = acc + _dot(e.astype(BF16), v_refs[kb][0, :, cols])
            outs.append(acc / l)
        att_ref[:, cols] = jnp.where(first_half, outs[0], outs[1]).astype(BF16)
    o_ref[0] = x_ref[0] + _dot(att_ref[...], wout_ref[...])


def _band_bias(table):
    qi = np.arange(Q_TILE)[:, None]
    kj = np.arange(Q_TILE)[None, :]
    period = 2 * Q_TILE
    m = np.arange(period)
    delta = np.where(m < Q_TILE, m, m - period)
    blocks = []
    for kb in range(N_KEY_BLOCKS):
        dist = (N_KEY_BLOCKS - 1 - kb) * Q_TILE - delta
        diag = table[:, np.clip(dist, -REL_CLIP, REL_CLIP) + REL_CLIP]
        toep = jnp.tile(diag, (1, Q_TILE))[:, :Q_TILE * (period - 1)]
        toep = toep.reshape(-1, Q_TILE, period - 1)[:, :, :Q_TILE]
        dchunk = (Q_TILE // CHUNK) * (kb - (N_KEY_BLOCKS - 1)) + kj // CHUNK - qi // CHUNK
        valid = (dchunk >= -LEFT_CHUNKS) & (dchunk <= 0)
        blocks.append(jnp.where(valid[None], toep, NEG_INF))
    return jnp.stack(blocks).astype(F32)


def _attention(x, qkv, rel_bias_table, w_out):
    b, s, d = x.shape
    nq = s // Q_TILE
    qkv3 = qkv.reshape(b, s, 3 * d)
    bias = _band_bias(rel_bias_table)

    def kv_spec(kb, col):
        return pl.BlockSpec(
            (1, Q_TILE, d),
            lambda bi, qt: (bi, jnp.maximum(qt - (N_KEY_BLOCKS - 1) + kb, 0), col))

    in_specs = [pl.BlockSpec((1, Q_TILE, d), lambda bi, qt: (bi, qt, 0)),
                pl.BlockSpec((1, Q_TILE, d), lambda bi, qt: (bi, qt, 0))]
    in_specs += [kv_spec(kb, 1) for kb in range(N_KEY_BLOCKS)]
    in_specs += [kv_spec(kb, 2) for kb in range(N_KEY_BLOCKS)]
    in_specs += [pl.BlockSpec(memory_space=pltpu.VMEM), pl.BlockSpec(memory_space=pltpu.VMEM)]
    return pl.pallas_call(
        _attn_kernel,
        grid=(b, nq),
        in_specs=in_specs,
        out_specs=pl.BlockSpec((1, Q_TILE, d), lambda bi, qt: (bi, qt, 0)),
        out_shape=jax.ShapeDtypeStruct((b, s, d), F32),
        scratch_shapes=[pltpu.VMEM((Q_TILE, d), BF16)],
        compiler_params=pltpu.CompilerParams(
            dimension_semantics=("parallel", "parallel"), vmem_limit_bytes=48 * MIB),
        name="band_attention",
    )(x, qkv3, qkv3, qkv3, qkv3, qkv3, qkv3, qkv3, bias, w_out.astype(BF16))


def _rglru_kernel(x_ref, g_ref, win_ref, cw_ref, cb_ref, wa_ref, ba_ref, wx_ref, bx_ref,
                  lam_ref, wout_ref, o_ref, ext_ref, h_ref, a_scr, b_scr, hs_scr):
    ts = x_ref.shape[1]
    w = LRU_HEADS * LRU_BLOCK

    @pl.when(pl.program_id(1) == 0)
    def _():
        ext_ref[...] = jnp.zeros_like(ext_ref)
        h_ref[...] = jnp.zeros_like(h_ref)

    x = x_ref[0]
    xn = _rms_rows(x, g_ref[...]).astype(BF16)
    gu = _dot(xn, win_ref[...])
    gate = gu[:, :w]
    u_raw = gu[:, w:]
    ext_ref[0:SUBLANES, :] = ext_ref[ts:ts + SUBLANES, :]
    ext_ref[SUBLANES:, :] = u_raw
    u = cb_ref[...] + jnp.zeros((ts, w), F32)
    for k in range(CONV_WIDTH):
        off = SUBLANES - (CONV_WIDTH - 1) + k
        u = u + cw_ref[k:k + 1, :] * ext_ref[off:off + ts, :]
    ub = u.astype(BF16)

    def block_diag(wref):
        return jnp.concatenate(
            [_dot(ub[:, hh * LRU_BLOCK:(hh + 1) * LRU_BLOCK], wref[hh]) for hh in range(LRU_HEADS)],
            axis=1)

    r = jax.nn.sigmoid(block_diag(wa_ref) + ba_ref[...])
    i = jax.nn.sigmoid(block_diag(wx_ref) + bx_ref[...])
    z = -lam_ref[...]
    softplus = jnp.maximum(z, 0.0) + jnp.log1p(jnp.exp(-jnp.abs(z)))
    log_a = (-LRU_C) * r * softplus
    a_scr[...] = jnp.exp(log_a)
    th = jnp.tanh(log_a)
    b_scr[...] = jnp.sqrt(2.0 * th / (th - 1.0)) * (i * u)

    def step(t, h):
        h = a_scr[pl.ds(t, 1), :] * h + b_scr[pl.ds(t, 1), :]
        hs_scr[pl.ds(t, 1), :] = h
        return h

    h_ref[0:1, :] = lax.fori_loop(0, ts, step, h_ref[0:1, :], unroll=8)
    y = jax.nn.gelu(gate) * hs_scr[...]
    o_ref[0] = x + _dot(y.astype(BF16), wout_ref[...])


def _rglru(x, norm_g, w_in, conv_w, conv_b, w_a, b_a, w_x, b_x, lam, w_out):
    b, s, d = x.shape
    w = LRU_HEADS * LRU_BLOCK
    row = lambda v: v.reshape(1, -1)
    whole = pl.BlockSpec(memory_space=pltpu.VMEM)
    return pl.pallas_call(
        _rglru_kernel,
        grid=(b, s // LRU_TILE),
        in_specs=[pl.BlockSpec((1, LRU_TILE, d), lambda bi, si: (bi, si, 0))] + [whole] * 10,
        out_specs=pl.BlockSpec((1, LRU_TILE, d), lambda bi, si: (bi, si, 0)),
        out_shape=jax.ShapeDtypeStruct((b, s, d), F32),
        scratch_shapes=[pltpu.VMEM((LRU_TILE + SUBLANES, w), F32), pltpu.VMEM((SUBLANES, w), F32),
                        pltpu.VMEM((LRU_TILE, w), F32), pltpu.VMEM((LRU_TILE, w), F32),
                        pltpu.VMEM((LRU_TILE, w), F32)],
        compiler_params=pltpu.CompilerParams(
            dimension_semantics=("arbitrary", "arbitrary"), vmem_limit_bytes=48 * MIB),
        name="rglru_mixer",
    )(x, row(norm_g), w_in.astype(BF16), conv_w, row(conv_b), w_a.astype(BF16), row(b_a),
      w_x.astype(BF16), row(b_x), row(lam), w_out.astype(BF16))


def _peer_scores_kernel(x_ref, g_ref, wq_ref, sk_ref, xp_ref, st_ref, xn_ref):
    xn = _rms_rows(x_ref[...], g_ref[...])
    xn_ref[...] = xn
    xb = xn.astype(BF16)
    words = _bf16_pair_words(xb)
    tm = words.shape[0]
    for c in range(SUBLANES):
        piece = words[:, (c % (SUBLANES // 2)) * LANES:(c % (SUBLANES // 2) + 1) * LANES]
        xp_ref[pl.ds(c, tm, stride=SUBLANES), :] = piece
    q = _dot(xb, wq_ref[...]).astype(BF16)
    for hp in range(2 * PEER_HEADS):
        st_ref[hp] = _dot_nt(sk_ref[hp % 2], q[:, hp * D_HALF:(hp + 1) * D_HALF])


def _peer_scores(x2d, norm_g, w_q, sub_keys):
    t, d = x2d.shape
    nq = w_q.shape[1]
    whole = pl.BlockSpec(memory_space=pltpu.VMEM)
    return pl.pallas_call(
        _peer_scores_kernel,
        grid=(t // ROW_TILE,),
        in_specs=[pl.BlockSpec((ROW_TILE, d), lambda i: (i, 0)), whole, whole, whole],
        out_specs=[pl.BlockSpec((ROW_TILE * SUBLANES, LANES), lambda i: (i, 0)),
                   pl.BlockSpec((2 * PEER_HEADS, N_KEYS, ROW_TILE), lambda i: (0, 0, i)),
                   pl.BlockSpec((ROW_TILE, d), lambda i: (i, 0))],
        out_shape=[jax.ShapeDtypeStruct((t * SUBLANES, LANES), U32),
                   jax.ShapeDtypeStruct((2 * PEER_HEADS, N_KEYS, t), F32),
                   jax.ShapeDtypeStruct((t, d), F32)],
        compiler_params=pltpu.CompilerParams(
            dimension_semantics=("parallel",), vmem_limit_bytes=48 * MIB),
        name="peer_scores",
    )(x2d, norm_g.reshape(1, d), w_q.astype(BF16), sub_keys.astype(BF16))


def _candidate_blocks():
    blocks = []
    for i in range(PEER_TOPK // 2):
        nj = PEER_TOPK // (i + 1)
        for j0 in range(0, nj, SUBLANES):
            blocks.append(("row", i, j0, min(SUBLANES, nj - j0)))
    blocks.append(("col", PEER_TOPK // 2, PEER_TOPK // 2))
    return blocks


def _peer_topk_kernel(st_ref, r8_ref, par_ref, g_ref, e_ref, r8t_ref, et_ref, *, expert_offset):
    tk = st_ref.shape[2]
    key_id = lax.broadcasted_iota(I32, (N_KEYS, tk), 0).astype(F32)
    row16 = lax.broadcasted_iota(I32, (PEER_TOPK, tk), 0)
    row8 = lax.broadcasted_iota(I32, (SUBLANES, tk), 0)
    row8_f = row8.astype(F32)
    neg_inf = jnp.float32(-jnp.inf)
    blocks = _candidate_blocks()
    big = jnp.float32(PEER_TOPK * PEER_TOPK)

    def top16(x):
        vals = jnp.zeros((PEER_TOPK, tk), F32)
        idxs = jnp.zeros((PEER_TOPK, tk), F32)
        for k in range(PEER_TOPK):
            m = jnp.max(x, axis=0, keepdims=True)
            idx = jnp.min(jnp.where(x == m, key_id, float(N_KEYS)), axis=0, keepdims=True)
            x = jnp.where(key_id == idx, neg_inf, x)
            vals = jnp.where(row16 == k, m, vals)
            idxs = jnp.where(row16 == k, idx, idxs)
        return vals, idxs

    for h in range(PEER_HEADS):
        s0, i0 = top16(st_ref[2 * h])
        s1, i1 = top16(st_ref[2 * h + 1])
        cand, flat, expert = [], [], []
        for blk in blocks:
            if blk[0] == "row":
                _, i, j0, n = blk
                c = s0[i:i + 1, :] + s1[j0:j0 + SUBLANES, :]
                f = float(PEER_TOPK * i + j0) + row8_f
                e = i0[i:i + 1, :] * float(N_KEYS) + i1[j0:j0 + SUBLANES, :]
            else:
                _, i_start, n = blk
                c = s0[i_start:i_start + SUBLANES, :] + s1[0:1, :]
                f = float(PEER_TOPK) * (float(i_start) + row8_f)
                e = i0[i_start:i_start + SUBLANES, :] * float(N_KEYS) + i1[0:1, :]
            cand.append(jnp.where(row8 < n, c, neg_inf))
            flat.append(f)
            expert.append(e)
        cand = jnp.concatenate(cand, axis=0)
        flat = jnp.concatenate(flat, axis=0)
        expert = jnp.concatenate(expert, axis=0)
        best_s = jnp.zeros((PEER_TOPK, tk), F32)
        best_e = jnp.zeros((PEER_TOPK, tk), F32)
        for k in range(PEER_TOPK):
            m = jnp.max(cand, axis=0, keepdims=True)
            fidx = jnp.min(jnp.where(cand == m, flat, big), axis=0, keepdims=True)
            pick = flat == fidx
            e_sel = jnp.max(jnp.where(pick, expert, -1.0), axis=0, keepdims=True)
            cand = jnp.where(pick, neg_inf, cand)
            best_s = jnp.where(row16 == k, m, best_s)
            best_e = jnp.where(row16 == k, e_sel, best_e)
        ex = jnp.exp(best_s - best_s[0:1, :])
        gate = ex / jnp.sum(ex, axis=0, keepdims=True)
        rows = slice(h * PEER_TOPK, (h + 1) * PEER_TOPK)
        best_i = best_e.astype(I32)
        r8t_ref[rows, :] = (best_i >> 1) * SUBLANES
        et_ref[rows, :] = best_i + expert_offset
        par_ref[0, rows, :] = best_i & 1
        g_ref[0, rows, :] = gate
    r8_ref[0] = r8t_ref[...].T
    e_ref[0] = et_ref[...].T


def _peer_topk(scores_t, expert_offset):
    t = scores_t.shape[2]
    nblk = t // PEER_TOKENS
    slot_major = pl.BlockSpec((1, N_SLOTS, PEER_TOKENS), lambda i: (i, 0, 0))
    token_major = pl.BlockSpec((1, PEER_TOKENS, N_SLOTS), lambda i: (i, 0, 0))
    return pl.pallas_call(
        functools.partial(_peer_topk_kernel, expert_offset=expert_offset),
        grid=(nblk,),
        in_specs=[pl.BlockSpec((2 * PEER_HEADS, N_KEYS, PEER_TOKENS), lambda i: (0, 0, i))],
        out_specs=[token_major, slot_major, slot_major, token_major],
        out_shape=[jax.ShapeDtypeStruct((nblk, PEER_TOKENS, N_SLOTS), I32),
                   jax.ShapeDtypeStruct((nblk, N_SLOTS, PEER_TOKENS), I32),
                   jax.ShapeDtypeStruct((nblk, N_SLOTS, PEER_TOKENS), F32),
                   jax.ShapeDtypeStruct((nblk, PEER_TOKENS, N_SLOTS), I32)],
        scratch_shapes=[pltpu.VMEM((N_SLOTS, PEER_TOKENS), I32), pltpu.VMEM((N_SLOTS, PEER_TOKENS), I32)],
        compiler_params=pltpu.CompilerParams(dimension_semantics=("parallel",)),
        name="peer_topk",
    )(scores_t)


def _bf16_pair_words(x):
    bits = lax.bitcast_convert_type(x.astype(BF16).astype(F32), U32)
    half = bits.shape[1] // 2
    return (bits[:, :half] >> 16) | (bits[:, half:] & jnp.uint32(0xFFFF0000))


def _pack_kernel(t_ref, o_ref):
    words = _bf16_pair_words(t_ref[0])
    rows = words.shape[0]
    n = words.shape[1] // LANES
    for s in range(n):
        o_ref[pl.ds(s, rows, stride=n), :] = words[:, s * LANES:(s + 1) * LANES]


def _pack_table(tabs, layer):
    _, e, d = tabs.shape
    n = d // (2 * LANES)
    return pl.pallas_call(
        _pack_kernel,
        grid=(e // ROW_TILE,),
        in_specs=[pl.BlockSpec((1, ROW_TILE, d), lambda i: (layer, i, 0))],
        out_specs=pl.BlockSpec((ROW_TILE * n, LANES), lambda i: (i, 0)),
        out_shape=jax.ShapeDtypeStruct((e * n, LANES), U32),
        compiler_params=pltpu.CompilerParams(dimension_semantics=("parallel",)),
        name="peer_pack_table",
    )(tabs)


def _unpack_words(w):
    lo = lax.bitcast_convert_type(w << 16, F32)
    hi = lax.bitcast_convert_type(w & jnp.uint32(0xFFFF0000), F32)
    return lo, hi


def _as_bf16(words):
    return pltpu.bitcast(words, BF16)


def _as_words(packed):
    return pltpu.bitcast(packed, U32)


def _merge_packed(a, b, shift, mask):
    ta = a + _as_bf16(pltpu.roll(_as_words(a), shift, 0))
    tb = b + _as_bf16(pltpu.roll(_as_words(b), SUBLANES - shift, 0))
    return jnp.where(mask, _as_words(ta), _as_words(tb))


def _sublane_iota():
    return lax.broadcasted_iota(I32, (SUBLANES, LANES), 0)


def _merge(a, b, shift, mask):
    ta = a + pltpu.roll(a, shift, 0)
    tb = b + pltpu.roll(b, SUBLANES - shift, 0)
    return jnp.where(mask, ta, tb)


GROUP = SUBLANES // 2
N_GROUPS = N_SLOTS // GROUP
N_ROWS = 2 * N_SLOTS
N_SPREAD = 4
SPREAD_AHEAD = 2


def _dup_halves(v, low):
    r = pltpu.roll(v, GROUP, 0)
    return jnp.where(low, v, r), jnp.where(low, r, v)


def _peer_u_kernel(r8_ref, xp_ref, g_ref, par_ref, tab_ref, a_ref, s_ref, part_ref):
    tb = g_ref.shape[2]
    sub = _sublane_iota()
    m2 = (sub & 2) != 0
    lane_t = lax.broadcasted_iota(I32, (SUBLANES, tb), 1)
    n_part = 2 * N_GROUPS

    def products(t, slot):
        xb = _as_bf16(xp_ref[pl.ds(pl.multiple_of(t * SUBLANES, SUBLANES), SUBLANES), :])
        ids = r8_ref.at[0, t]
        for grp in range(N_GROUPS):
            prods = []
            for i in range(GROUP):
                r8 = pl.multiple_of(ids[grp * GROUP + i], SUBLANES)
                prods.append(_as_bf16(tab_ref[pl.ds(r8, SUBLANES), :]) * xb)
            halves = (_merge_packed(prods[3], prods[1], 2, m2), _merge_packed(prods[2], prods[0], 2, m2))
            for k in range(2):
                lo, hi = _unpack_words(halves[k])
                part_ref[slot, 2 * grp + k] = lo + hi

    def reduce_lanes(t, slot):
        for n in range(n_part):
            col = jnp.sum(part_ref[slot, n], axis=1, keepdims=True)
            rows = slice(n * SUBLANES, (n + 1) * SUBLANES)
            s_ref[rows, :] = jnp.where(lane_t == t, col, s_ref[rows, :])

    def token_pair(i, carry):
        t = 2 * i
        reduce_lanes(t - 2, 0)
        reduce_lanes(t - 1, 1)
        products(t, 0)
        products(t + 1, 1)
        return carry

    s_ref[...] = jnp.zeros_like(s_ref)
    part_ref[...] = jnp.zeros_like(part_ref)
    lax.fori_loop(0, tb // 2, token_pair, 0)
    reduce_lanes(tb - 2, 0)
    reduce_lanes(tb - 1, 1)

    odd_t = (lax.broadcasted_iota(I32, (SUBLANES, tb), 0) & 1) != 0
    _gate_rows(lambda grp: _merge(s_ref[2 * grp * SUBLANES:(2 * grp + 1) * SUBLANES, :],
                                  s_ref[(2 * grp + 1) * SUBLANES:(2 * grp + 2) * SUBLANES, :], 1, odd_t),
               g_ref, par_ref, a_ref)


def _gate_rows(dots_of_group, g_ref, par_ref, a_ref):
    tb = g_ref.shape[2]
    low_t = lax.broadcasted_iota(I32, (SUBLANES, tb), 0) < GROUP
    want = jnp.where(low_t, 0, 1)
    for v in range(N_SLOTS // SUBLANES):
        src = slice(v * SUBLANES, (v + 1) * SUBLANES)
        gates = _dup_halves(g_ref[0, src, :], low_t)
        pars = _dup_halves(par_ref[0, src, :], low_t)
        for half in range(2):
            grp = 2 * v + half
            act = gates[half] * jax.nn.gelu(dots_of_group(grp))
            rows = slice(grp * SUBLANES, (grp + 1) * SUBLANES)
            a_ref[0, rows, :] = jnp.where(pars[half] == want, act, 0.0)


def _sc_u_dots(u_rows, e_tok, xn, first_token, n_tokens):
    d = xn.shape[1]
    n_workers = SC_CORES * SC_SUBCORES
    per_worker = n_tokens // n_workers
    mesh = plsc.VectorSubcoreMesh(core_axis_name="c", subcore_axis_name="s")

    def body(u_hbm, e_hbm, x_hbm, o_hbm, idx_v, x_v, rows_v, out_v, sem):
        wid = lax.axis_index("s") * SC_CORES + lax.axis_index("c")
        lane = lax.iota(I32, SC_LANES)

        def token(i, carry):
            local = wid * per_worker + i
            pltpu.sync_copy(e_hbm.at[first_token + local], idx_v)
            pltpu.sync_copy(x_hbm.at[first_token + local], x_v)
            for chunk in range(N_SLOTS // SC_GATHER_ROWS):
                ids = idx_v.at[pl.ds(chunk * SC_GATHER_ROWS, SC_GATHER_ROWS)]
                pltpu.async_copy(u_hbm.at[ids], rows_v, sem).wait()
                for part in range(SC_GATHER_ROWS // SC_LANES):
                    def row(r, vec):
                        rr = part * SC_LANES + r
                        acc = jnp.zeros((SC_LANES,), F32)
                        for c in range(d // SC_LANES):
                            cols = pl.ds(c * SC_LANES, SC_LANES)
                            acc = acc + rows_v[rr, cols] * x_v[cols]
                        return jnp.where(lane == r, jnp.sum(acc), vec)
                    vec = lax.fori_loop(0, SC_LANES, row, jnp.zeros((SC_LANES,), F32))
                    out_v[pl.ds(chunk * SC_GATHER_ROWS + part * SC_LANES, SC_LANES)] = vec
            pltpu.sync_copy(out_v, o_hbm.at[local])
            return carry

        lax.fori_loop(0, per_worker, token, 0)

    return pl.kernel(
        body,
        out_type=jax.ShapeDtypeStruct((n_tokens, N_SLOTS), F32),
        mesh=mesh,
        scratch_types=[pltpu.VMEM((N_SLOTS,), I32), pltpu.VMEM((d,), F32),
                       pltpu.VMEM((SC_GATHER_ROWS, d), F32), pltpu.VMEM((N_SLOTS,), F32),
                       pltpu.SemaphoreType.DMA],
        compiler_params=pltpu.CompilerParams(needs_layout_passes=False),
        name="peer_expert_in_sc",
    )(u_rows, e_tok, xn)


def _peer_gate_kernel(s_ref, g_ref, par_ref, a_ref):
    dots = s_ref[...].T
    low_t = lax.broadcasted_iota(I32, (SUBLANES, dots.shape[1]), 0) < GROUP
    _gate_rows(lambda grp: _dup_halves(dots[(grp // 2) * SUBLANES:(grp // 2 + 1) * SUBLANES, :], low_t)[grp % 2],
               g_ref, par_ref, a_ref)


def _peer_weights_kernel(a_ref, w_ref):
    low_t = lax.broadcasted_iota(I32, (SUBLANES, a_ref.shape[2]), 0) < GROUP
    tiles = []
    for v in range(N_SLOTS // SUBLANES):
        first = a_ref[0, 2 * v * SUBLANES:(2 * v + 1) * SUBLANES, :]
        second = a_ref[0, (2 * v + 1) * SUBLANES:(2 * v + 2) * SUBLANES, :]
        first = first + pltpu.roll(first, GROUP, 0)
        second = second + pltpu.roll(second, GROUP, 0)
        tiles.append(jnp.where(low_t, first, second))
    w_ref[...] = jnp.concatenate(tiles, axis=0).T


def _sc_v_sums(v_rows, e_tok, w_tok, x2d, first_token, n_tokens):
    d = x2d.shape[1]
    n_workers = SC_CORES * SC_SUBCORES
    per_worker = n_tokens // n_workers
    n_acc = d // SC_LANES // 2
    mesh = plsc.VectorSubcoreMesh(core_axis_name="c", subcore_axis_name="s")

    def body(v_hbm, e_hbm, w_hbm, x_hbm, o_hbm, idx_v, w_v, rows_v, out_v, sem):
        wid = lax.axis_index("s") * SC_CORES + lax.axis_index("c")

        def token(i, carry):
            local = wid * per_worker + i
            pltpu.sync_copy(e_hbm.at[first_token + local], idx_v)
            pltpu.sync_copy(w_hbm.at[local], w_v)
            pltpu.sync_copy(x_hbm.at[first_token + local], out_v)
            for chunk in range(N_SLOTS // SC_GATHER_ROWS):
                ids = idx_v.at[pl.ds(chunk * SC_GATHER_ROWS, SC_GATHER_ROWS)]
                pltpu.async_copy(v_hbm.at[ids], rows_v, sem).wait()
                for half in range(2):
                    base = half * n_acc * SC_LANES

                    def row(r, acc):
                        slot = jnp.full((SC_LANES,), chunk * SC_GATHER_ROWS + r, I32)
                        w = plsc.load_gather(w_v, [slot])
                        return tuple(acc[c] + w * rows_v[r, pl.ds(base + c * SC_LANES, SC_LANES)]
                                     for c in range(n_acc))

                    acc0 = tuple(out_v[pl.ds(base + c * SC_LANES, SC_LANES)] for c in range(n_acc))
                    acc = lax.fori_loop(0, SC_GATHER_ROWS, row, acc0)
                    for c in range(n_acc):
                        out_v[pl.ds(base + c * SC_LANES, SC_LANES)] = acc[c]
            pltpu.sync_copy(out_v, o_hbm.at[local])
            return carry

        lax.fori_loop(0, per_worker, token, 0)

    return pl.kernel(
        body,
        out_type=jax.ShapeDtypeStruct((n_tokens, d), F32),
        mesh=mesh,
        scratch_types=[pltpu.VMEM((N_SLOTS,), I32), pltpu.VMEM((N_SLOTS,), F32),
                       pltpu.VMEM((SC_GATHER_ROWS, d), F32), pltpu.VMEM((d,), F32),
                       pltpu.SemaphoreType.DMA],
        compiler_params=pltpu.CompilerParams(needs_layout_passes=False),
        name="peer_expert_out_sc",
    )(v_rows, e_tok, w_tok, x2d)


def _token_tiles(ref, xt_ref, to_tiles):
    tb, d = ref.shape
    for q in range(tb // SUBLANES):
        for c in range(d // LANES):
            rows = slice(q * SUBLANES, (q + 1) * SUBLANES)
            cols = slice(c * LANES, (c + 1) * LANES)
            strided = pl.ds(q * SUBLANES * SUBLANES + c, SUBLANES, stride=SUBLANES)
            if to_tiles:
                xt_ref[strided, :] = ref[rows, cols]
            else:
                ref[rows, cols] = xt_ref[strided, :]


def _peer_v_kernel(r8_ref, a_ref, x_ref, tab_ref, o_ref, bc_ref, xt_ref):
    tb = x_ref.shape[0]
    sub = _sublane_iota()
    low = sub < GROUP
    lane_t = lax.broadcasted_iota(I32, (SUBLANES, tb), 1)
    _token_tiles(x_ref, xt_ref, True)

    def spread(t, buf):
        for k in range(N_ROWS // SUBLANES):
            rows = slice(k * SUBLANES, (k + 1) * SUBLANES)
            col = jnp.sum(jnp.where(lane_t == t, a_ref[0, rows, :], 0.0), axis=1, keepdims=True)
            bits = lax.bitcast_convert_type(col.astype(BF16).astype(F32), U32)
            bc_ref[buf, rows, :] = jnp.broadcast_to(bits | (bits >> 16), (SUBLANES, LANES))

    def accumulate(t, buf):
        acc_lo = jnp.zeros((SUBLANES, LANES), F32)
        acc_hi = jnp.zeros((SUBLANES, LANES), F32)
        ids = r8_ref.at[0, t]
        for grp in range(N_GROUPS):
            prods = []
            for i in range(GROUP):
                row = grp * SUBLANES + i
                r8 = pl.multiple_of(ids[grp * GROUP + i], SUBLANES)
                am = jnp.where(low, bc_ref[buf, row:row + 1, :], bc_ref[buf, row + GROUP:row + GROUP + 1, :])
                prods.append(_as_bf16(tab_ref[pl.ds(r8, SUBLANES), :]) * _as_bf16(am))
            lo, hi = _unpack_words(_as_words((prods[0] + prods[1]) + (prods[2] + prods[3])))
            acc_lo = acc_lo + lo
            acc_hi = acc_hi + hi
        acc_lo = acc_lo + pltpu.roll(acc_lo, GROUP, 0)
        acc_hi = acc_hi + pltpu.roll(acc_hi, GROUP, 0)
        tile = pl.ds(pl.multiple_of(t * SUBLANES, SUBLANES), SUBLANES)
        xt_ref[tile, :] = xt_ref[tile, :] + jnp.where(low, acc_lo, acc_hi)

    def token_group(i, carry):
        t = N_SPREAD * i
        for k in range(N_SPREAD):
            spread(jnp.minimum(t + k + SPREAD_AHEAD, tb - 1), (k + SPREAD_AHEAD) % N_SPREAD)
            accumulate(t + k, k)
        return carry

    for k in range(SPREAD_AHEAD):
        spread(k, k)
    lax.fori_loop(0, tb // N_SPREAD, token_group, 0)
    _token_tiles(o_ref, xt_ref, False)


def _peer_experts(x2d, xp, xn, r8, e_tok, par, gates, u_rows, v_rows, u_tab, v_tab):
    t, d = x2d.shape
    nblk = t // PEER_TOKENS
    nblk_tc = nblk - SC_TOKEN_BLOCKS
    smem_block = pl.BlockSpec((1, PEER_TOKENS, N_SLOTS), lambda i: (i, 0, 0), memory_space=pltpu.SMEM)
    slot_block = pl.BlockSpec((1, N_SLOTS, PEER_TOKENS), lambda i: (i, 0, 0))
    row_block = pl.BlockSpec((1, N_ROWS, PEER_TOKENS), lambda i: (i, 0, 0))
    tile_block = pl.BlockSpec((PEER_TOKENS * SUBLANES, LANES), lambda i: (i, 0))
    tok_block = pl.BlockSpec((PEER_TOKENS, d), lambda i: (i, 0))
    whole = pl.BlockSpec(memory_space=pltpu.VMEM)
    params = pltpu.CompilerParams(dimension_semantics=("parallel",), vmem_limit_bytes=56 * MIB)

    dots_sc = _sc_u_dots(u_rows, e_tok.reshape(t, N_SLOTS), xn, nblk_tc * PEER_TOKENS,
                         SC_TOKEN_BLOCKS * PEER_TOKENS)
    act_tc = pl.pallas_call(
        _peer_u_kernel,
        grid=(nblk_tc,),
        in_specs=[smem_block, tile_block, slot_block, slot_block, whole],
        out_specs=row_block,
        out_shape=jax.ShapeDtypeStruct((nblk_tc, N_ROWS, PEER_TOKENS), F32),
        scratch_shapes=[pltpu.VMEM((2 * N_ROWS, PEER_TOKENS), F32),
                        pltpu.VMEM((2, 2 * N_GROUPS, SUBLANES, LANES), F32)],
        compiler_params=params,
        name="peer_expert_in",
    )(r8, xp, gates, par, u_tab)
    shifted = pl.BlockSpec((1, N_SLOTS, PEER_TOKENS), lambda i: (i + nblk_tc, 0, 0))
    act_sc = pl.pallas_call(
        _peer_gate_kernel,
        grid=(SC_TOKEN_BLOCKS,),
        in_specs=[pl.BlockSpec((PEER_TOKENS, N_SLOTS), lambda i: (i, 0)), shifted, shifted],
        out_specs=row_block,
        out_shape=jax.ShapeDtypeStruct((SC_TOKEN_BLOCKS, N_ROWS, PEER_TOKENS), F32),
        compiler_params=pltpu.CompilerParams(dimension_semantics=("parallel",)),
        name="peer_gate_sc",
    )(dots_sc, gates, par)
    act = jnp.concatenate([act_tc, act_sc], axis=0)

    nblk_v = nblk - SC_V_BLOCKS
    w_sc = pl.pallas_call(
        _peer_weights_kernel,
        grid=(SC_V_BLOCKS,),
        in_specs=[pl.BlockSpec((1, N_ROWS, PEER_TOKENS), lambda i: (i + nblk_v, 0, 0))],
        out_specs=pl.BlockSpec((PEER_TOKENS, N_SLOTS), lambda i: (i, 0)),
        out_shape=jax.ShapeDtypeStruct((SC_V_BLOCKS * PEER_TOKENS, N_SLOTS), F32),
        compiler_params=pltpu.CompilerParams(dimension_semantics=("parallel",)),
        name="peer_weights_sc",
    )(act)
    y_sc = _sc_v_sums(v_rows, e_tok.reshape(t, N_SLOTS), w_sc, x2d, nblk_v * PEER_TOKENS,
                      SC_V_BLOCKS * PEER_TOKENS)
    y_tc = pl.pallas_call(
        _peer_v_kernel,
        grid=(nblk_v,),
        in_specs=[smem_block, row_block, tok_block, whole],
        out_specs=tok_block,
        out_shape=jax.ShapeDtypeStruct((nblk_v * PEER_TOKENS, d), F32),
        scratch_shapes=[pltpu.VMEM((N_SPREAD, N_ROWS, LANES), U32),
                        pltpu.VMEM((PEER_TOKENS * SUBLANES, LANES), F32)],
        compiler_params=params,
        name="peer_expert_out",
    )(r8, act, x2d, v_tab)
    return jnp.concatenate([y_tc, y_sc], axis=0)


def _peer_ffn(x2d, norm_g, w_q, sub_keys, u_tabs, v_tabs, layer):
    n_experts, d = u_tabs.shape[1:]
    xp, scores_t, xn = _peer_scores(x2d, norm_g, w_q, sub_keys)
    r8, par, gates, e_tok = _peer_topk(scores_t, layer * n_experts)
    return _peer_experts(x2d, xp, xn, r8, e_tok, par, gates, u_tabs.reshape(-1, d), v_tabs.reshape(-1, d),
                         _pack_table(u_tabs, layer), _pack_table(v_tabs, layer))


def kernel(x, attn_norm_g, attn_w_qkv, attn_q_g, attn_k_g, attn_rel_bias, attn_w_out, rec_norm_g, rec_w_in, rec_conv_w, rec_conv_b, rec_w_a, rec_b_a, rec_w_x, rec_b_x, rec_lambda, rec_w_out, ffn_norm_g, peer_w_q, peer_sub_keys, peer_u, peer_v):
    b, s, d = x.shape
    depth = ffn_norm_g.shape[0]
    for layer in range(depth):
        j = layer // 2
        if layer % 2 == 0:
            qkv = _qkv_proj(x.reshape(b * s, d), attn_norm_g[j], attn_w_qkv[j], attn_q_g[j], attn_k_g[j])
            x = _attention(x, qkv, attn_rel_bias[j], attn_w_out[j])
        else:
            x = _rglru(x, rec_norm_g[j], rec_w_in[j], rec_conv_w[j], rec_conv_b[j], rec_w_a[j],
                       rec_b_a[j], rec_w_x[j], rec_b_x[j], rec_lambda[j], rec_w_out[j])
        x = _peer_ffn(x.reshape(b * s, d), ffn_norm_g[layer], peer_w_q[layer], peer_sub_keys[layer],
                      peer_u, peer_v, layer).reshape(b, s, d)
    return x
```

```python
import functools

import numpy as np
import jax
import jax.numpy as jnp
from jax import lax
from jax.experimental import pallas as pl
from jax.experimental.pallas import tpu as pltpu
from jax.experimental.pallas import tpu_sc as plsc

F32 = jnp.float32
BF16 = jnp.bfloat16
U32 = jnp.uint32
I32 = jnp.int32

EPS = 1e-6
NEG_INF = -1e30

D_MODEL = 1024
CHUNK = 64
ATTN_HEADS = 16
HEAD_DIM = 64
LEFT_CHUNKS = 8
REL_CLIP = 256
LRU_HEADS = 4
LRU_BLOCK = 256
CONV_WIDTH = 4
LRU_C = 8.0
PEER_HEADS = 8
N_KEYS = 128
PEER_TOPK = 16
D_HALF = 128

SUBLANES = 8
LANES = 128

Q_TILE = 4 * CHUNK
N_KEY_BLOCKS = 3
ROW_TILE = 512
PEER_TOKENS = 128
LRU_TILE = 256
N_SLOTS = PEER_HEADS * PEER_TOPK

MIB = 1024 * 1024

SC_CORES = 2
SC_SUBCORES = 16
SC_LANES = 16
SC_GATHER_ROWS = 32
SC_TOKEN_BLOCKS = 70
SC_V_BLOCKS = 76


def _rms_rows(x, g):
    ms = jnp.mean(x * x, axis=-1, keepdims=True)
    return (x * lax.rsqrt(ms + EPS)) * g


def _split_bf16(v):
    hi = v.astype(BF16)
    lo = (v - hi.astype(F32)).astype(BF16)
    return hi, lo


def _dot(a, b):
    return jnp.dot(a, b, preferred_element_type=F32)


def _dot_nt(a, b):
    return lax.dot_general(a, b, (((1,), (1,)), ((), ())), preferred_element_type=F32)


def _qkv_kernel(x_ref, g_ref, w_ref, gain_ref, bsel_ref, bexp_ref, o_ref, xn_ref):
    j = pl.program_id(1)

    @pl.when(j == 0)
    def _():
        xn_ref[...] = _rms_rows(x_ref[...], g_ref[...]).astype(BF16)

    y = _dot(xn_ref[...], w_ref[...])

    @pl.when(j < 2)
    def _():
        hi, lo = _split_bf16(y * y)
        ms = _dot(hi, bsel_ref[...]) + _dot(lo, bsel_ref[...])
        rhi, rlo = _split_bf16(lax.rsqrt(ms + EPS))
        rs = _dot(rhi, bexp_ref[...]) + _dot(rlo, bexp_ref[...])
        o_ref[...] = ((y * rs) * gain_ref[0]).astype(BF16)

    @pl.when(j == 2)
    def _():
        o_ref[...] = y.astype(BF16)


def _qkv_proj(x2d, norm_g, w_qkv, q_g, k_g):
    t, d = x2d.shape
    scale = HEAD_DIM ** -0.5
    gains = jnp.stack([jnp.tile(q_g, ATTN_HEADS) * scale, jnp.tile(k_g, ATTN_HEADS),
                       jnp.ones((d,), F32)]).reshape(3, 1, d)
    head_of_col = np.arange(d) // HEAD_DIM
    bsel = (head_of_col[:, None] == np.arange(LANES)[None, :]).astype(np.float32) / HEAD_DIM
    bexp = (np.arange(LANES)[:, None] == head_of_col[None, :]).astype(np.float32)
    return pl.pallas_call(
        _qkv_kernel,
        grid=(t // ROW_TILE, 3),
        in_specs=[
            pl.BlockSpec((ROW_TILE, d), lambda i, j: (i, 0)),
            pl.BlockSpec((1, d), lambda i, j: (0, 0)),
            pl.BlockSpec((d, d), lambda i, j: (0, j)),
            pl.BlockSpec((1, 1, d), lambda i, j: (j, 0, 0)),
            pl.BlockSpec((d, LANES), lambda i, j: (0, 0)),
            pl.BlockSpec((LANES, d), lambda i, j: (0, 0)),
        ],
        out_specs=pl.BlockSpec((ROW_TILE, d), lambda i, j: (i, j)),
        out_shape=jax.ShapeDtypeStruct((t, 3 * d), BF16),
        scratch_shapes=[pltpu.VMEM((ROW_TILE, d), BF16)],
        compiler_params=pltpu.CompilerParams(
            dimension_semantics=("parallel", "arbitrary"), vmem_limit_bytes=40 * MIB),
        name="qkv_proj",
    )(x2d, norm_g.reshape(1, d), w_qkv.astype(BF16), gains,
      jnp.asarray(bsel, BF16), jnp.asarray(bexp, BF16))


def _attn_kernel(x_ref, q_ref, k0_ref, k1_ref, k2_ref, v0_ref, v1_ref, v2_ref,
                 bias_ref, wout_ref, o_ref, att_ref):
    qt = pl.program_id(1)
    k_refs = (k0_ref, k1_ref, k2_ref)
    v_refs = (v0_ref, v1_ref, v2_ref)
    negs = [jnp.where(qt - (N_KEY_BLOCKS - 1) + kb < 0, NEG_INF, 0.0).astype(F32)
            for kb in range(N_KEY_BLOCKS)]
    lane = lax.broadcasted_iota(I32, (Q_TILE, LANES), 1)
    first_half = lane < HEAD_DIM
    for p in range(ATTN_HEADS // 2):
        cols = slice(p * LANES, (p + 1) * LANES)
        qp = q_ref[0, :, cols]
        outs = []
        for hh in range(2):
            h = 2 * p + hh
            keep = first_half if hh == 0 else jnp.logical_not(first_half)
            qm = jnp.where(keep, qp, jnp.zeros_like(qp))
            s = [_dot_nt(qm, k_refs[kb][0, :, cols]) + bias_ref[kb, h] + negs[kb]
                 for kb in range(N_KEY_BLOCKS)]
            m = jnp.max(s[0], axis=-1, keepdims=True)
            for kb in range(1, N_KEY_BLOCKS):
                m = jnp.maximum(m, jnp.max(s[kb], axis=-1, keepdims=True))
            acc = jnp.zeros((Q_TILE, LANES), F32)
            l = jnp.zeros((Q_TILE, 1), F32)
            for kb in range(N_KEY_BLOCKS):
                e = jnp.exp(s[kb] - m)
                l = l + jnp.sum(e, axis=-1, keepdims=True)
                acc = acc + _dot(e.astype(BF16), v_refs[kb][0, :, cols])
            outs.append(acc / l)
        att_ref[:, cols] = jnp.where(first_half, outs[0], outs[1]).astype(BF16)
    o_ref[0] = x_ref[0] + _dot(att_ref[...], wout_ref[...])


def _band_bias(table):
    qi = np.arange(Q_TILE)[:, None]
    kj = np.arange(Q_TILE)[None, :]
    period = 2 * Q_TILE
    m = np.arange(period)
    delta = np.where(m < Q_TILE, m, m - period)
    blocks = []
    for kb in range(N_KEY_BLOCKS):
        dist = (N_KEY_BLOCKS - 1 - kb) * Q_TILE - delta
        diag = table[:, np.clip(dist, -REL_CLIP, REL_CLIP) + REL_CLIP]
        toep = jnp.tile(diag, (1, Q_TILE))[:, :Q_TILE * (period - 1)]
        toep = toep.reshape(-1, Q_TILE, period - 1)[:, :, :Q_TILE]
        dchunk = (Q_TILE // CHUNK) * (kb - (N_KEY_BLOCKS - 1)) + kj // CHUNK - qi // CHUNK
        valid = (dchunk >= -LEFT_CHUNKS) & (dchunk <= 0)
        blocks.append(jnp.where(valid[None], toep, NEG_INF))
    return jnp.stack(blocks).astype(F32)


def _attention(x, qkv, rel_bias_table, w_out):
    b, s, d = x.shape
    nq = s // Q_TILE
    qkv3 = qkv.reshape(b, s, 3 * d)
    bias = _band_bias(rel_bias_table)

    def kv_spec(kb, col):
        return pl.BlockSpec(
            (1, Q_TILE, d),
            lambda bi, qt: (bi, jnp.maximum(qt - (N_KEY_BLOCKS - 1) + kb, 0), col))

    in_specs = [pl.BlockSpec((1, Q_TILE, d), lambda bi, qt: (bi, qt, 0)),
                pl.BlockSpec((1, Q_TILE, d), lambda bi, qt: (bi, qt, 0))]
    in_specs += [kv_spec(kb, 1) for kb in range(N_KEY_BLOCKS)]
    in_specs += [kv_spec(kb, 2) for kb in range(N_KEY_BLOCKS)]
    in_specs += [pl.BlockSpec(memory_space=pltpu.VMEM), pl.BlockSpec(memory_space=pltpu.VMEM)]
    return pl.pallas_call(
        _attn_kernel,
        grid=(b, nq),
        in_specs=in_specs,
        out_specs=pl.BlockSpec((1, Q_TILE, d), lambda bi, qt: (bi, qt, 0)),
        out_shape=jax.ShapeDtypeStruct((b, s, d), F32),
        scratch_shapes=[pltpu.VMEM((Q_TILE, d), BF16)],
        compiler_params=pltpu.CompilerParams(
            dimension_semantics=("parallel", "parallel"), vmem_limit_bytes=48 * MIB),
        name="band_attention",
    )(x, qkv3, qkv3, qkv3, qkv3, qkv3, qkv3, qkv3, bias, w_out.astype(BF16))


def _rglru_kernel(x_ref, g_ref, win_ref, cw_ref, cb_ref, wa_ref, ba_ref, wx_ref, bx_ref,
                  lam_ref, wout_ref, o_ref, ext_ref, h_ref, a_scr, b_scr, hs_scr):
    ts = x_ref.shape[1]
    w = LRU_HEADS * LRU_BLOCK

    @pl.when(pl.program_id(1) == 0)
    def _():
        ext_ref[...] = jnp.zeros_like(ext_ref)
        h_ref[...] = jnp.zeros_like(h_ref)

    x = x_ref[0]
    xn = _rms_rows(x, g_ref[...]).astype(BF16)
    gu = _dot(xn, win_ref[...])
    gate = gu[:, :w]
    u_raw = gu[:, w:]
    ext_ref[0:SUBLANES, :] = ext_ref[ts:ts + SUBLANES, :]
    ext_ref[SUBLANES:, :] = u_raw
    u = cb_ref[...] + jnp.zeros((ts, w), F32)
    for k in range(CONV_WIDTH):
        off = SUBLANES - (CONV_WIDTH - 1) + k
        u = u + cw_ref[k:k + 1, :] * ext_ref[off:off + ts, :]
    ub = u.astype(BF16)

    def block_diag(wref):
        return jnp.concatenate(
            [_dot(ub[:, hh * LRU_BLOCK:(hh + 1) * LRU_BLOCK], wref[hh]) for hh in range(LRU_HEADS)],
            axis=1)

    r = jax.nn.sigmoid(block_diag(wa_ref) + ba_ref[...])
    i = jax.nn.sigmoid(block_diag(wx_ref) + bx_ref[...])
    z = -lam_ref[...]
    softplus = jnp.maximum(z, 0.0) + jnp.log1p(jnp.exp(-jnp.abs(z)))
    log_a = (-LRU_C) * r * softplus
    a_scr[...] = jnp.exp(log_a)
    th = jnp.tanh(log_a)
    b_scr[...] = jnp.sqrt(2.0 * th / (th - 1.0)) * (i * u)

    def step(t, h):
        h = a_scr[pl.ds(t, 1), :] * h + b_scr[pl.ds(t, 1), :]
        hs_scr[pl.ds(t, 1), :] = h
        return h

    h_ref[0:1, :] = lax.fori_loop(0, ts, step, h_ref[0:1, :], unroll=8)
    y = jax.nn.gelu(gate) * hs_scr[...]
    o_ref[0] = x + _dot(y.astype(BF16), wout_ref[...])


def _rglru(x, norm_g, w_in, conv_w, conv_b, w_a, b_a, w_x, b_x, lam, w_out):
    b, s, d = x.shape
    w = LRU_HEADS * LRU_BLOCK
    row = lambda v: v.reshape(1, -1)
    whole = pl.BlockSpec(memory_space=pltpu.VMEM)
    return pl.pallas_call(
        _rglru_kernel,
        grid=(b, s // LRU_TILE),
        in_specs=[pl.BlockSpec((1, LRU_TILE, d), lambda bi, si: (bi, si, 0))] + [whole] * 10,
        out_specs=pl.BlockSpec((1, LRU_TILE, d), lambda bi, si: (bi, si, 0)),
        out_shape=jax.ShapeDtypeStruct((b, s, d), F32),
        scratch_shapes=[pltpu.VMEM((LRU_TILE + SUBLANES, w), F32), pltpu.VMEM((SUBLANES, w), F32),
                        pltpu.VMEM((LRU_TILE, w), F32), pltpu.VMEM((LRU_TILE, w), F32),
                        pltpu.VMEM((LRU_TILE, w), F32)],
        compiler_params=pltpu.CompilerParams(
            dimension_semantics=("arbitrary", "arbitrary"), vmem_limit_bytes=48 * MIB),
        name="rglru_mixer",
    )(x, row(norm_g), w_in.astype(BF16), conv_w, row(conv_b), w_a.astype(BF16), row(b_a),
      w_x.astype(BF16), row(b_x), row(lam), w_out.astype(BF16))


def _peer_scores_kernel(x_ref, g_ref, wq_ref, sk_ref, xp_ref, st_ref, xn_ref):
    xn = _rms_rows(x_ref[...], g_ref[...])
    xn_ref[...] = xn
    xb = xn.astype(BF16)
    words = _bf16_pair_words(xb)
    tm = words.shape[0]
    for c in range(SUBLANES):
        piece = words[:, (c % (SUBLANES // 2)) * LANES:(c % (SUBLANES // 2) + 1) * LANES]
        xp_ref[pl.ds(c, tm, stride=SUBLANES), :] = piece
    q = _dot(xb, wq_ref[...]).astype(BF16)
    for hp in range(2 * PEER_HEADS):
        st_ref[hp] = _dot_nt(sk_ref[hp % 2], q[:, hp * D_HALF:(hp + 1) * D_HALF])


def _peer_scores(x2d, norm_g, w_q, sub_keys):
    t, d = x2d.shape
    nq = w_q.shape[1]
    whole = pl.BlockSpec(memory_space=pltpu.VMEM)
    return pl.pallas_call(
        _peer_scores_kernel,
        grid=(t // ROW_TILE,),
        in_specs=[pl.BlockSpec((ROW_TILE, d), lambda i: (i, 0)), whole, whole, whole],
        out_specs=[pl.BlockSpec((ROW_TILE * SUBLANES, LANES), lambda i: (i, 0)),
                   pl.BlockSpec((2 * PEER_HEADS, N_KEYS, ROW_TILE), lambda i: (0, 0, i)),
                   pl.BlockSpec((ROW_TILE, d), lambda i: (i, 0))],
        out_shape=[jax.ShapeDtypeStruct((t * SUBLANES, LANES), U32),
                   jax.ShapeDtypeStruct((2 * PEER_HEADS, N_KEYS, t), F32),
                   jax.ShapeDtypeStruct((t, d), F32)],
        compiler_params=pltpu.CompilerParams(
            dimension_semantics=("parallel",), vmem_limit_bytes=48 * MIB),
        name="peer_scores",
    )(x2d, norm_g.reshape(1, d), w_q.astype(BF16), sub_keys.astype(BF16))


def _candidate_blocks():
    blocks = []
    for i in range(PEER_TOPK // 2):
        nj = PEER_TOPK // (i + 1)
        for j0 in range(0, nj, SUBLANES):
            blocks.append(("row", i, j0, min(SUBLANES, nj - j0)))
    blocks.append(("col", PEER_TOPK // 2, PEER_TOPK // 2))
    return blocks


def _peer_topk_kernel(st_ref, r8_ref, par_ref, g_ref, e_ref, r8t_ref, et_ref, *, expert_offset):
    tk = st_ref.shape[2]
    key_id = lax.broadcasted_iota(I32, (N_KEYS, tk), 0).astype(F32)
    row16 = lax.broadcasted_iota(I32, (PEER_TOPK, tk), 0)
    row8 = lax.broadcasted_iota(I32, (SUBLANES, tk), 0)
    row8_f = row8.astype(F32)
    neg_inf = jnp.float32(-jnp.inf)
    blocks = _candidate_blocks()
    big = jnp.float32(PEER_TOPK * PEER_TOPK)

    def top16(x):
        vals = jnp.zeros((PEER_TOPK, tk), F32)
        idxs = jnp.zeros((PEER_TOPK, tk), F32)
        for k in range(PEER_TOPK):
            m = jnp.max(x, axis=0, keepdims=True)
            idx = jnp.min(jnp.where(x == m, key_id, float(N_KEYS)), axis=0, keepdims=True)
            x = jnp.where(key_id == idx, neg_inf, x)
            vals = jnp.where(row16 == k, m, vals)
            idxs = jnp.where(row16 == k, idx, idxs)
        return vals, idxs

    for h in range(PEER_HEADS):
        s0, i0 = top16(st_ref[2 * h])
        s1, i1 = top16(st_ref[2 * h + 1])
        cand, flat, expert = [], [], []
        for blk in blocks:
            if blk[0] == "row":
                _, i, j0, n = blk
                c = s0[i:i + 1, :] + s1[j0:j0 + SUBLANES, :]
                f = float(PEER_TOPK * i + j0) + row8_f
                e = i0[i:i + 1, :] * float(N_KEYS) + i1[j0:j0 + SUBLANES, :]
            else:
                _, i_start, n = blk
                c = s0[i_start:i_start + SUBLANES, :] + s1[0:1, :]
                f = float(PEER_TOPK) * (float(i_start) + row8_f)
                e = i0[i_start:i_start + SUBLANES, :] * float(N_KEYS) + i1[0:1, :]
            cand.append(jnp.where(row8 < n, c, neg_inf))
            flat.append(f)
            expert.append(e)
        cand = jnp.concatenate(cand, axis=0)
        flat = jnp.concatenate(flat, axis=0)
        expert = jnp.concatenate(expert, axis=0)
        best_s = jnp.zeros((PEER_TOPK, tk), F32)
        best_e = jnp.zeros((PEER_TOPK, tk), F32)
        for k in range(PEER_TOPK):
            m = jnp.max(cand, axis=0, keepdims=True)
            fidx = jnp.min(jnp.where(cand == m, flat, big), axis=0, keepdims=True)
            pick = flat == fidx
            e_sel = jnp.max(jnp.where(pick, expert, -1.0), axis=0, keepdims=True)
            cand = jnp.where(pick, neg_inf, cand)
            best_s = jnp.where(row16 == k, m, best_s)
            best_e = jnp.where(row16 == k, e_sel, best_e)
        ex = jnp.exp(best_s - best_s[0:1, :])
        gate = ex / jnp.sum(ex, axis=0, keepdims=True)
        rows = slice(h * PEER_TOPK, (h + 1) * PEER_TOPK)
        best_i = best_e.astype(I32)
        r8t_ref[rows, :] = (best_i >> 1) * SUBLANES
        et_ref[rows, :] = best_i + expert_offset
        par_ref[0, rows, :] = best_i & 1
        g_ref[0, rows, :] = gate
    r8_ref[0] = r8t_ref[...].T
    e_ref[0] = et_ref[...].T


def _peer_topk(scores_t, expert_offset):
    t = scores_t.shape[2]
    nblk = t // PEER_TOKENS
    slot_major = pl.BlockSpec((1, N_SLOTS, PEER_TOKENS), lambda i: (i, 0, 0))
    token_major = pl.BlockSpec((1, PEER_TOKENS, N_SLOTS), lambda i: (i, 0, 0))
    return pl.pallas_call(
        functools.partial(_peer_topk_kernel, expert_offset=expert_offset),
        grid=(nblk,),
        in_specs=[pl.BlockSpec((2 * PEER_HEADS, N_KEYS, PEER_TOKENS), lambda i: (0, 0, i))],
        out_specs=[token_major, slot_major, slot_major, token_major],
        out_shape=[jax.ShapeDtypeStruct((nblk, PEER_TOKENS, N_SLOTS), I32),
                   jax.ShapeDtypeStruct((nblk, N_SLOTS, PEER_TOKENS), I32),
                   jax.ShapeDtypeStruct((nblk, N_SLOTS, PEER_TOKENS), F32),
                   jax.ShapeDtypeStruct((nblk, PEER_TOKENS, N_SLOTS), I32)],
        scratch_shapes=[pltpu.VMEM((N_SLOTS, PEER_TOKENS), I32), pltpu.VMEM((N_SLOTS, PEER_TOKENS), I32)],
        compiler_params=pltpu.CompilerParams(dimension_semantics=("parallel",)),
        name="peer_topk",
    )(scores_t)


def _bf16_pair_words(x):
    bits = lax.bitcast_convert_type(x.astype(BF16).astype(F32), U32)
    half = bits.shape[1] // 2
    return (bits[:, :half] >> 16) | (bits[:, half:] & jnp.uint32(0xFFFF0000))


def _pack_kernel(t_ref, o_ref):
    words = _bf16_pair_words(t_ref[0])
    rows = words.shape[0]
    n = words.shape[1] // LANES
    for s in range(n):
        o_ref[pl.ds(s, rows, stride=n), :] = words[:, s * LANES:(s + 1) * LANES]


def _pack_table(tabs, layer):
    _, e, d = tabs.shape
    n = d // (2 * LANES)
    return pl.pallas_call(
        _pack_kernel,
        grid=(e // ROW_TILE,),
        in_specs=[pl.BlockSpec((1, ROW_TILE, d), lambda i: (layer, i, 0))],
        out_specs=pl.BlockSpec((ROW_TILE * n, LANES), lambda i: (i, 0)),
        out_shape=jax.ShapeDtypeStruct((e * n, LANES), U32),
        compiler_params=pltpu.CompilerParams(dimension_semantics=("parallel",)),
        name="peer_pack_table",
    )(tabs)


def _unpack_words(w):
    lo = lax.bitcast_convert_type(w << 16, F32)
    hi = lax.bitcast_convert_type(w & jnp.uint32(0xFFFF0000), F32)
    return lo, hi


def _as_bf16(words):
    return pltpu.bitcast(words, BF16)


def _as_words(packed):
    return pltpu.bitcast(packed, U32)


def _merge_packed(a, b, shift, mask):
    ta = a + _as_bf16(pltpu.roll(_as_words(a), shift, 0))
    tb = b + _as_bf16(pltpu.roll(_as_words(b), SUBLANES - shift, 0))
    return jnp.where(mask, _as_words(ta), _as_words(tb))


def _sublane_iota():
    return lax.broadcasted_iota(I32, (SUBLANES, LANES), 0)


def _merge(a, b, shift, mask):
    ta = a + pltpu.roll(a, shift, 0)
    tb = b + pltpu.roll(b, SUBLANES - shift, 0)
    return jnp.where(mask, ta, tb)


GROUP = SUBLANES // 2
N_GROUPS = N_SLOTS // GROUP
N_ROWS = 2 * N_SLOTS
N_SPREAD = 4
SPREAD_AHEAD = 2


def _dup_halves(v, low):
    r = pltpu.roll(v, GROUP, 0)
    return jnp.where(low, v, r), jnp.where(low, r, v)


def _peer_u_kernel(r8_ref, xp_ref, g_ref, par_ref, tab_ref, a_ref, s_ref, part_ref):
    tb = g_ref.shape[2]
    sub = _sublane_iota()
    m2 = (sub & 2) != 0
    lane_t = lax.broadcasted_iota(I32, (SUBLANES, tb), 1)
    n_part = 2 * N_GROUPS

    def products(t, slot):
        xb = _as_bf16(xp_ref[pl.ds(pl.multiple_of(t * SUBLANES, SUBLANES), SUBLANES), :])
        ids = r8_ref.at[0, t]
        for grp in range(N_GROUPS):
            prods = []
            for i in range(GROUP):
                r8 = pl.multiple_of(ids[grp * GROUP + i], SUBLANES)
                prods.append(_as_bf16(tab_ref[pl.ds(r8, SUBLANES), :]) * xb)
            halves = (_merge_packed(prods[3], prods[1], 2, m2), _merge_packed(prods[2], prods[0], 2, m2))
            for k in range(2):
                lo, hi = _unpack_words(halves[k])
                part_ref[slot, 2 * grp + k] = lo + hi

    def reduce_lanes(t, slot):
        for n in range(n_part):
            col = jnp.sum(part_ref[slot, n], axis=1, keepdims=True)
            rows = slice(n * SUBLANES, (n + 1) * SUBLANES)
            s_ref[rows, :] = jnp.where(lane_t == t, col, s_ref[rows, :])

    def token_pair(i, carry):
        t = 2 * i
        reduce_lanes(t - 2, 0)
        reduce_lanes(t - 1, 1)
        products(t, 0)
        products(t + 1, 1)
        return carry

    s_ref[...] = jnp.zeros_like(s_ref)
    part_ref[...] = jnp.zeros_like(part_ref)
    lax.fori_loop(0, tb // 2, token_pair, 0)
    reduce_lanes(tb - 2, 0)
    reduce_lanes(tb - 1, 1)

    odd_t = (lax.broadcasted_iota(I32, (SUBLANES, tb), 0) & 1) != 0
    _gate_rows(lambda grp: _merge(s_ref[2 * grp * SUBLANES:(2 * grp + 1) * SUBLANES, :],
                                  s_ref[(2 * grp + 1) * SUBLANES:(2 * grp + 2) * SUBLANES, :], 1, odd_t),
               g_ref, par_ref, a_ref)


def _gate_rows(dots_of_group, g_ref, par_ref, a_ref):
    tb = g_ref.shape[2]
    low_t = lax.broadcasted_iota(I32, (SUBLANES, tb), 0) < GROUP
    want = jnp.where(low_t, 0, 1)
    for v in range(N_SLOTS // SUBLANES):
        src = slice(v * SUBLANES, (v + 1) * SUBLANES)
        gates = _dup_halves(g_ref[0, src, :], low_t)
        pars = _dup_halves(par_ref[0, src, :], low_t)
        for half in range(2):
            grp = 2 * v + half
            act = gates[half] * jax.nn.gelu(dots_of_group(grp))
            rows = slice(grp * SUBLANES, (grp + 1) * SUBLANES)
            a_ref[0, rows, :] = jnp.where(pars[half] == want, act, 0.0)


def _sc_token_loop(tab_hbm, e_hbm, idx_v, rows_v, sems, first_token, per_worker, begin_token, chunk_rows,
                   end_token):
    wid = lax.axis_index("s") * SC_CORES + lax.axis_index("c")
    n_chunks = N_SLOTS // SC_GATHER_ROWS
    first = first_token + wid * per_worker
    last = first + per_worker - 1

    def gather(slot, chunk):
        ids = idx_v.at[slot, pl.ds(chunk * SC_GATHER_ROWS, SC_GATHER_ROWS)]
        return pltpu.make_async_copy(tab_hbm.at[ids], rows_v.at[chunk % 2], sems.at[chunk % 2])

    def one_token(tok, slot, next_tok):
        pltpu.sync_copy(e_hbm.at[next_tok], idx_v.at[1 - slot])
        begin_token(tok)
        for chunk in range(n_chunks):
            if chunk + 1 < n_chunks:
                gather(slot, chunk + 1).start()
            else:
                gather(1 - slot, 0).start()
            gather(slot, chunk).wait()
            chunk_rows(chunk, rows_v.at[chunk % 2])
        end_token(tok)

    def token_pair(k, carry):
        tok = first + 2 * k
        one_token(tok, 0, tok + 1)
        one_token(tok + 1, 1, jnp.minimum(tok + 2, last))
        return carry

    pltpu.sync_copy(e_hbm.at[first], idx_v.at[0])
    gather(0, 0).start()
    lax.fori_loop(0, per_worker // 2, token_pair, 0)
    gather(0, 0).wait()


def _sc_scratch(d):
    return [pltpu.VMEM((2, N_SLOTS), I32), pltpu.VMEM((2, SC_GATHER_ROWS, d), F32),
            pltpu.SemaphoreType.DMA((2,))]


def _sc_u_dots(u_rows, e_tok, xn, first_token, n_tokens):
    d = xn.shape[1]
    per_worker = n_tokens // (SC_CORES * SC_SUBCORES)
    n_half = d // SC_LANES // 2
    mesh = plsc.VectorSubcoreMesh(core_axis_name="c", subcore_axis_name="s")

    def body(u_hbm, e_hbm, x_hbm, o_hbm, idx_v, rows_v, sems, x_v, out_v):
        lane = lax.iota(I32, SC_LANES)

        def begin_token(tok):
            pltpu.sync_copy(x_hbm.at[tok], x_v)

        def chunk_rows(chunk, rows):
            for part in range(SC_GATHER_ROWS // SC_LANES):
                total = jnp.zeros((SC_LANES,), F32)
                for half in range(2):
                    base = half * n_half * SC_LANES
                    xs = [x_v[pl.ds(base + c * SC_LANES, SC_LANES)] for c in range(n_half)]

                    def row(r, vec):
                        rr = part * SC_LANES + r
                        acc = rows[rr, pl.ds(base, SC_LANES)] * xs[0]
                        for c in range(1, n_half):
                            acc = acc + rows[rr, pl.ds(base + c * SC_LANES, SC_LANES)] * xs[c]
                        return jnp.where(lane == r, jnp.sum(acc), vec)

                    total = total + lax.fori_loop(0, SC_LANES, row, jnp.zeros((SC_LANES,), F32))
                out_v[pl.ds(chunk * SC_GATHER_ROWS + part * SC_LANES, SC_LANES)] = total

        def end_token(tok):
            pltpu.sync_copy(out_v, o_hbm.at[tok - first_token])

        _sc_token_loop(u_hbm, e_hbm, idx_v, rows_v, sems, first_token, per_worker, begin_token, chunk_rows,
                       end_token)

    return pl.kernel(
        body,
        out_type=jax.ShapeDtypeStruct((n_tokens, N_SLOTS), F32),
        mesh=mesh,
        scratch_types=_sc_scratch(d) + [pltpu.VMEM((d,), F32), pltpu.VMEM((N_SLOTS,), F32)],
        compiler_params=pltpu.CompilerParams(needs_layout_passes=False),
        name="peer_expert_in_sc",
    )(u_rows, e_tok, xn)


def _peer_gate_kernel(s_ref, g_ref, par_ref, a_ref):
    dots = s_ref[...].T
    low_t = lax.broadcasted_iota(I32, (SUBLANES, dots.shape[1]), 0) < GROUP
    _gate_rows(lambda grp: _dup_halves(dots[(grp // 2) * SUBLANES:(grp // 2 + 1) * SUBLANES, :], low_t)[grp % 2],
               g_ref, par_ref, a_ref)


def _peer_weights_kernel(a_ref, w_ref):
    low_t = lax.broadcasted_iota(I32, (SUBLANES, a_ref.shape[2]), 0) < GROUP
    tiles = []
    for v in range(N_SLOTS // SUBLANES):
        first = a_ref[0, 2 * v * SUBLANES:(2 * v + 1) * SUBLANES, :]
        second = a_ref[0, (2 * v + 1) * SUBLANES:(2 * v + 2) * SUBLANES, :]
        first = first + pltpu.roll(first, GROUP, 0)
        second = second + pltpu.roll(second, GROUP, 0)
        tiles.append(jnp.where(low_t, first, second))
    w_ref[...] = jnp.concatenate(tiles, axis=0).T


def _sc_v_sums(v_rows, e_tok, w_tok, x2d, first_token, n_tokens):
    d = x2d.shape[1]
    per_worker = n_tokens // (SC_CORES * SC_SUBCORES)
    n_acc = d // SC_LANES // 2
    mesh = plsc.VectorSubcoreMesh(core_axis_name="c", subcore_axis_name="s")

    def body(v_hbm, e_hbm, w_hbm, x_hbm, o_hbm, idx_v, rows_v, sems, w_v, out_v):
        def begin_token(tok):
            pltpu.sync_copy(w_hbm.at[tok - first_token], w_v)
            pltpu.sync_copy(x_hbm.at[tok], out_v)

        def chunk_rows(chunk, rows):
            for half in range(2):
                base = half * n_acc * SC_LANES

                def row(r, acc):
                    slot = jnp.full((SC_LANES,), chunk * SC_GATHER_ROWS + r, I32)
                    w = plsc.load_gather(w_v, [slot])
                    return tuple(acc[c] + w * rows[r, pl.ds(base + c * SC_LANES, SC_LANES)]
                                 for c in range(n_acc))

                acc0 = tuple(out_v[pl.ds(base + c * SC_LANES, SC_LANES)] for c in range(n_acc))
                acc = lax.fori_loop(0, SC_GATHER_ROWS, row, acc0)
                for c in range(n_acc):
                    out_v[pl.ds(base + c * SC_LANES, SC_LANES)] = acc[c]

        def end_token(tok):
            pltpu.sync_copy(out_v, o_hbm.at[tok - first_token])

        _sc_token_loop(v_hbm, e_hbm, idx_v, rows_v, sems, first_token, per_worker, begin_token, chunk_rows,
                       end_token)

    return pl.kernel(
        body,
        out_type=jax.ShapeDtypeStruct((n_tokens, d), F32),
        mesh=mesh,
        scratch_types=_sc_scratch(d) + [pltpu.VMEM((N_SLOTS,), F32), pltpu.VMEM((d,), F32)],
        compiler_params=pltpu.CompilerParams(needs_layout_passes=False),
        name="peer_expert_out_sc",
    )(v_rows, e_tok, w_tok, x2d)


def _token_tiles(ref, xt_ref, to_tiles):
    tb, d = ref.shape
    for q in range(tb // SUBLANES):
        for c in range(d // LANES):
            rows = slice(q * SUBLANES, (q + 1) * SUBLANES)
            cols = slice(c * LANES, (c + 1) * LANES)
            strided = pl.ds(q * SUBLANES * SUBLANES + c, SUBLANES, stride=SUBLANES)
            if to_tiles:
                xt_ref[strided, :] = ref[rows, cols]
            else:
                ref[rows, cols] = xt_ref[strided, :]


def _peer_v_kernel(r8_ref, a_ref, x_ref, tab_ref, o_ref, bc_ref, xt_ref):
    tb = x_ref.shape[0]
    sub = _sublane_iota()
    low = sub < GROUP
    lane_t = lax.broadcasted_iota(I32, (SUBLANES, tb), 1)
    _token_tiles(x_ref, xt_ref, True)

    def spread(t, buf):
        for k in range(N_ROWS // SUBLANES):
            rows = slice(k * SUBLANES, (k + 1) * SUBLANES)
            col = jnp.sum(jnp.where(lane_t == t, a_ref[0, rows, :], 0.0), axis=1, keepdims=True)
            bits = lax.bitcast_convert_type(col.astype(BF16).astype(F32), U32)
            bc_ref[buf, rows, :] = jnp.broadcast_to(bits | (bits >> 16), (SUBLANES, LANES))

    def accumulate(t, buf):
        acc_lo = jnp.zeros((SUBLANES, LANES), F32)
        acc_hi = jnp.zeros((SUBLANES, LANES), F32)
        ids = r8_ref.at[0, t]
        for grp in range(N_GROUPS):
            prods = []
            for i in range(GROUP):
                row = grp * SUBLANES + i
                r8 = pl.multiple_of(ids[grp * GROUP + i], SUBLANES)
                am = jnp.where(low, bc_ref[buf, row:row + 1, :], bc_ref[buf, row + GROUP:row + GROUP + 1, :])
                prods.append(_as_bf16(tab_ref[pl.ds(r8, SUBLANES), :]) * _as_bf16(am))
            lo, hi = _unpack_words(_as_words((prods[0] + prods[1]) + (prods[2] + prods[3])))
            acc_lo = acc_lo + lo
            acc_hi = acc_hi + hi
        acc_lo = acc_lo + pltpu.roll(acc_lo, GROUP, 0)
        acc_hi = acc_hi + pltpu.roll(acc_hi, GROUP, 0)
        tile = pl.ds(pl.multiple_of(t * SUBLANES, SUBLANES), SUBLANES)
        xt_ref[tile, :] = xt_ref[tile, :] + jnp.where(low, acc_lo, acc_hi)

    def token_group(i, carry):
        t = N_SPREAD * i
        for k in range(N_SPREAD):
            spread(jnp.minimum(t + k + SPREAD_AHEAD, tb - 1), (k + SPREAD_AHEAD) % N_SPREAD)
            accumulate(t + k, k)
        return carry

    for k in range(SPREAD_AHEAD):
        spread(k, k)
    lax.fori_loop(0, tb // N_SPREAD, token_group, 0)
    _token_tiles(o_ref, xt_ref, False)


def _peer_experts(x2d, xp, xn, r8, e_tok, par, gates, u_rows, v_rows, u_tab, v_tab):
    t, d = x2d.shape
    nblk = t // PEER_TOKENS
    nblk_tc = nblk - SC_TOKEN_BLOCKS
    smem_block = pl.BlockSpec((1, PEER_TOKENS, N_SLOTS), lambda i: (i, 0, 0), memory_space=pltpu.SMEM)
    slot_block = pl.BlockSpec((1, N_SLOTS, PEER_TOKENS), lambda i: (i, 0, 0))
    row_block = pl.BlockSpec((1, N_ROWS, PEER_TOKENS), lambda i: (i, 0, 0))
    tile_block = pl.BlockSpec((PEER_TOKENS * SUBLANES, LANES), lambda i: (i, 0))
    tok_block = pl.BlockSpec((PEER_TOKENS, d), lambda i: (i, 0))
    whole = pl.BlockSpec(memory_space=pltpu.VMEM)
    params = pltpu.CompilerParams(dimension_semantics=("parallel",), vmem_limit_bytes=56 * MIB)

    dots_sc = _sc_u_dots(u_rows, e_tok.reshape(t, N_SLOTS), xn, nblk_tc * PEER_TOKENS,
                         SC_TOKEN_BLOCKS * PEER_TOKENS)
    act_tc = pl.pallas_call(
        _peer_u_kernel,
        grid=(nblk_tc,),
        in_specs=[smem_block, tile_block, slot_block, slot_block, whole],
        out_specs=row_block,
        out_shape=jax.ShapeDtypeStruct((nblk_tc, N_ROWS, PEER_TOKENS), F32),
        scratch_shapes=[pltpu.VMEM((2 * N_ROWS, PEER_TOKENS), F32),
                        pltpu.VMEM((2, 2 * N_GROUPS, SUBLANES, LANES), F32)],
        compiler_params=params,
        name="peer_expert_in",
    )(r8, xp, gates, par, u_tab)
    shifted = pl.BlockSpec((1, N_SLOTS, PEER_TOKENS), lambda i: (i + nblk_tc, 0, 0))
    act_sc = pl.pallas_call(
        _peer_gate_kernel,
        grid=(SC_TOKEN_BLOCKS,),
        in_specs=[pl.BlockSpec((PEER_TOKENS, N_SLOTS), lambda i: (i, 0)), shifted, shifted],
        out_specs=row_block,
        out_shape=jax.ShapeDtypeStruct((SC_TOKEN_BLOCKS, N_ROWS, PEER_TOKENS), F32),
        compiler_params=pltpu.CompilerParams(dimension_semantics=("parallel",)),
        name="peer_gate_sc",
    )(dots_sc, gates, par)
    act = jnp.concatenate([act_tc, act_sc], axis=0)

    nblk_v = nblk - SC_V_BLOCKS
    w_sc = pl.pallas_call(
        _peer_weights_kernel,
        grid=(SC_V_BLOCKS,),
        in_specs=[pl.BlockSpec((1, N_ROWS, PEER_TOKENS), lambda i: (i + nblk_v, 0, 0))],
        out_specs=pl.BlockSpec((PEER_TOKENS, N_SLOTS), lambda i: (i, 0)),
        out_shape=jax.ShapeDtypeStruct((SC_V_BLOCKS * PEER_TOKENS, N_SLOTS), F32),
        compiler_params=pltpu.CompilerParams(dimension_semantics=("parallel",)),
        name="peer_weights_sc",
    )(act)
    y_sc = _sc_v_sums(v_rows, e_tok.reshape(t, N_SLOTS), w_sc, x2d, nblk_v * PEER_TOKENS,
                      SC_V_BLOCKS * PEER_TOKENS)
    y_tc = pl.pallas_call(
        _peer_v_kernel,
        grid=(nblk_v,),
        in_specs=[smem_block, row_block, tok_block, whole],
        out_specs=tok_block,
        out_shape=jax.ShapeDtypeStruct((nblk_v * PEER_TOKENS, d), F32),
        scratch_shapes=[pltpu.VMEM((N_SPREAD, N_ROWS, LANES), U32),
                        pltpu.VMEM((PEER_TOKENS * SUBLANES, LANES), F32)],
        compiler_params=params,
        name="peer_expert_out",
    )(r8, act, x2d, v_tab)
    return jnp.concatenate([y_tc, y_sc], axis=0)


def _peer_ffn(x2d, norm_g, w_q, sub_keys, u_tabs, v_tabs, layer):
    n_experts, d = u_tabs.shape[1:]
    xp, scores_t, xn = _peer_scores(x2d, norm_g, w_q, sub_keys)
    r8, par, gates, e_tok = _peer_topk(scores_t, layer * n_experts)
    return _peer_experts(x2d, xp, xn, r8, e_tok, par, gates, u_tabs.reshape(-1, d), v_tabs.reshape(-1, d),
                         _pack_table(u_tabs, layer), _pack_table(v_tabs, layer))


def kernel(x, attn_norm_g, attn_w_qkv, attn_q_g, attn_k_g, attn_rel_bias, attn_w_out, rec_norm_g, rec_w_in, rec_conv_w, rec_conv_b, rec_w_a, rec_b_a, rec_w_x, rec_b_x, rec_lambda, rec_w_out, ffn_norm_g, peer_w_q, peer_sub_keys, peer_u, peer_v):
    b, s, d = x.shape
    depth = ffn_norm_g.shape[0]
    for layer in range(depth):
        j = layer // 2
        if layer % 2 == 0:
            qkv = _qkv_proj(x.reshape(b * s, d), attn_norm_g[j], attn_w_qkv[j], attn_q_g[j], attn_k_g[j])
            x = _attention(x, qkv, attn_rel_bias[j], attn_w_out[j])
        else:
            x = _rglru(x, rec_norm_g[j], rec_w_in[j], rec_conv_w[j], rec_conv_b[j], rec_w_a[j],
                       rec_b_a[j], rec_w_x[j], rec_b_x[j], rec_lambda[j], rec_w_out[j])
        x = _peer_ffn(x.reshape(b * s, d), ffn_norm_g[layer], peer_w_q[layer], peer_sub_keys[layer],
                      peer_u, peer_v, layer).reshape(b, s, d)
    return x
```

```python
import functools

import numpy as np
import jax
import jax.numpy as jnp
from jax import lax
from jax.experimental import pallas as pl
from jax.experimental.pallas import tpu as pltpu
from jax.experimental.pallas import tpu_sc as plsc

F32 = jnp.float32
BF16 = jnp.bfloat16
U32 = jnp.uint32
I32 = jnp.int32

EPS = 1e-6
NEG_INF = -1e30

D_MODEL = 1024
CHUNK = 64
ATTN_HEADS = 16
HEAD_DIM = 64
LEFT_CHUNKS = 8
REL_CLIP = 256
LRU_HEADS = 4
LRU_BLOCK = 256
CONV_WIDTH = 4
LRU_C = 8.0
PEER_HEADS = 8
N_KEYS = 128
PEER_TOPK = 16
D_HALF = 128

SUBLANES = 8
LANES = 128

Q_TILE = 4 * CHUNK
N_KEY_BLOCKS = 3
ROW_TILE = 512
PEER_TOKENS = 128
LRU_TILE = 256
N_SLOTS = PEER_HEADS * PEER_TOPK

MIB = 1024 * 1024

SC_CORES = 2
SC_SUBCORES = 16
SC_LANES = 16
SC_GATHER_ROWS = 32
SC_TOKEN_BLOCKS = 96
SC_V_BLOCKS = 86


def _rms_rows(x, g):
    ms = jnp.mean(x * x, axis=-1, keepdims=True)
    return (x * lax.rsqrt(ms + EPS)) * g


def _split_bf16(v):
    hi = v.astype(BF16)
    lo = (v - hi.astype(F32)).astype(BF16)
    return hi, lo


def _dot(a, b):
    return jnp.dot(a, b, preferred_element_type=F32)


def _dot_nt(a, b):
    return lax.dot_general(a, b, (((1,), (1,)), ((), ())), preferred_element_type=F32)


def _qkv_kernel(x_ref, g_ref, w_ref, gain_ref, bsel_ref, bexp_ref, o_ref, xn_ref):
    j = pl.program_id(1)

    @pl.when(j == 0)
    def _():
        xn_ref[...] = _rms_rows(x_ref[...], g_ref[...]).astype(BF16)

    y = _dot(xn_ref[...], w_ref[...])

    @pl.when(j < 2)
    def _():
        hi, lo = _split_bf16(y * y)
        ms = _dot(hi, bsel_ref[...]) + _dot(lo, bsel_ref[...])
        rhi, rlo = _split_bf16(lax.rsqrt(ms + EPS))
        rs = _dot(rhi, bexp_ref[...]) + _dot(rlo, bexp_ref[...])
        o_ref[...] = ((y * rs) * gain_ref[0]).astype(BF16)

    @pl.when(j == 2)
    def _():
        o_ref[...] = y.astype(BF16)


def _qkv_proj(x2d, norm_g, w_qkv, q_g, k_g):
    t, d = x2d.shape
    scale = HEAD_DIM ** -0.5
    gains = jnp.stack([jnp.tile(q_g, ATTN_HEADS) * scale, jnp.tile(k_g, ATTN_HEADS),
                       jnp.ones((d,), F32)]).reshape(3, 1, d)
    head_of_col = np.arange(d) // HEAD_DIM
    bsel = (head_of_col[:, None] == np.arange(LANES)[None, :]).astype(np.float32) / HEAD_DIM
    bexp = (np.arange(LANES)[:, None] == head_of_col[None, :]).astype(np.float32)
    return pl.pallas_call(
        _qkv_kernel,
        grid=(t // ROW_TILE, 3),
        in_specs=[
            pl.BlockSpec((ROW_TILE, d), lambda i, j: (i, 0)),
            pl.BlockSpec((1, d), lambda i, j: (0, 0)),
            pl.BlockSpec((d, d), lambda i, j: (0, j)),
            pl.BlockSpec((1, 1, d), lambda i, j: (j, 0, 0)),
            pl.BlockSpec((d, LANES), lambda i, j: (0, 0)),
            pl.BlockSpec((LANES, d), lambda i, j: (0, 0)),
        ],
        out_specs=pl.BlockSpec((ROW_TILE, d), lambda i, j: (i, j)),
        out_shape=jax.ShapeDtypeStruct((t, 3 * d), BF16),
        scratch_shapes=[pltpu.VMEM((ROW_TILE, d), BF16)],
        compiler_params=pltpu.CompilerParams(
            dimension_semantics=("parallel", "arbitrary"), vmem_limit_bytes=40 * MIB),
        name="qkv_proj",
    )(x2d, norm_g.reshape(1, d), w_qkv.astype(BF16), gains,
      jnp.asarray(bsel, BF16), jnp.asarray(bexp, BF16))


def _attn_kernel(x_ref, q_ref, k0_ref, k1_ref, k2_ref, v0_ref, v1_ref, v2_ref,
                 bias_ref, wout_ref, o_ref, att_ref):
    qt = pl.program_id(1)
    k_refs = (k0_ref, k1_ref, k2_ref)
    v_refs = (v0_ref, v1_ref, v2_ref)
    negs = [jnp.where(qt - (N_KEY_BLOCKS - 1) + kb < 0, NEG_INF, 0.0).astype(F32)
            for kb in range(N_KEY_BLOCKS)]
    lane = lax.broadcasted_iota(I32, (Q_TILE, LANES), 1)
    first_half = lane < HEAD_DIM
    for p in range(ATTN_HEADS // 2):
        cols = slice(p * LANES, (p + 1) * LANES)
        qp = q_ref[0, :, cols]
        outs = []
        for hh in range(2):
            h = 2 * p + hh
            keep = first_half if hh == 0 else jnp.logical_not(first_half)
            qm = jnp.where(keep, qp, jnp.zeros_like(qp))
            s = [_dot_nt(qm, k_refs[kb][0, :, cols]) + bias_ref[kb, h] + negs[kb]
                 for kb in range(N_KEY_BLOCKS)]
            m = jnp.max(s[0], axis=-1, keepdims=True)
            for kb in range(1, N_KEY_BLOCKS):
                m = jnp.maximum(m, jnp.max(s[kb], axis=-1, keepdims=True))
            acc = jnp.zeros((Q_TILE, LANES), F32)
            l = jnp.zeros((Q_TILE, 1), F32)
            for kb in range(N_KEY_BLOCKS):
                e = jnp.exp(s[kb] - m)
                l = l + jnp.sum(e, axis=-1, keepdims=True)
                acc = acc + _dot(e.astype(BF16), v_refs[kb][0, :, cols])
            outs.append(acc / l)
        att_ref[:, cols] = jnp.where(first_half, outs[0], outs[1]).astype(BF16)
    o_ref[0] = x_ref[0] + _dot(att_ref[...], wout_ref[...])


def _band_bias(table):
    qi = np.arange(Q_TILE)[:, None]
    kj = np.arange(Q_TILE)[None, :]
    period = 2 * Q_TILE
    m = np.arange(period)
    delta = np.where(m < Q_TILE, m, m - period)
    blocks = []
    for kb in range(N_KEY_BLOCKS):
        dist = (N_KEY_BLOCKS - 1 - kb) * Q_TILE - delta
        diag = table[:, np.clip(dist, -REL_CLIP, REL_CLIP) + REL_CLIP]
        toep = jnp.tile(diag, (1, Q_TILE))[:, :Q_TILE * (period - 1)]
        toep = toep.reshape(-1, Q_TILE, period - 1)[:, :, :Q_TILE]
        dchunk = (Q_TILE // CHUNK) * (kb - (N_KEY_BLOCKS - 1)) + kj // CHUNK - qi // CHUNK
        valid = (dchunk >= -LEFT_CHUNKS) & (dchunk <= 0)
        blocks.append(jnp.where(valid[None], toep, NEG_INF))
    return jnp.stack(blocks).astype(F32)


def _attention(x, qkv, rel_bias_table, w_out):
    b, s, d = x.shape
    nq = s // Q_TILE
    qkv3 = qkv.reshape(b, s, 3 * d)
    bias = _band_bias(rel_bias_table)

    def kv_spec(kb, col):
        return pl.BlockSpec(
            (1, Q_TILE, d),
            lambda bi, qt: (bi, jnp.maximum(qt - (N_KEY_BLOCKS - 1) + kb, 0), col))

    in_specs = [pl.BlockSpec((1, Q_TILE, d), lambda bi, qt: (bi, qt, 0)),
                pl.BlockSpec((1, Q_TILE, d), lambda bi, qt: (bi, qt, 0))]
    in_specs += [kv_spec(kb, 1) for kb in range(N_KEY_BLOCKS)]
    in_specs += [kv_spec(kb, 2) for kb in range(N_KEY_BLOCKS)]
    in_specs += [pl.BlockSpec(memory_space=pltpu.VMEM), pl.BlockSpec(memory_space=pltpu.VMEM)]
    return pl.pallas_call(
        _attn_kernel,
        grid=(b, nq),
        in_specs=in_specs,
        out_specs=pl.BlockSpec((1, Q_TILE, d), lambda bi, qt: (bi, qt, 0)),
        out_shape=jax.ShapeDtypeStruct((b, s, d), F32),
        scratch_shapes=[pltpu.VMEM((Q_TILE, d), BF16)],
        compiler_params=pltpu.CompilerParams(
            dimension_semantics=("parallel", "parallel"), vmem_limit_bytes=48 * MIB),
        name="band_attention",
    )(x, qkv3, qkv3, qkv3, qkv3, qkv3, qkv3, qkv3, bias, w_out.astype(BF16))


def _rglru_kernel(x_ref, g_ref, win_ref, cw_ref, cb_ref, wa_ref, ba_ref, wx_ref, bx_ref,
                  lam_ref, wout_ref, o_ref, ext_ref, h_ref, a_scr, b_scr, hs_scr):
    ts = x_ref.shape[1]
    w = LRU_HEADS * LRU_BLOCK

    @pl.when(pl.program_id(1) == 0)
    def _():
        ext_ref[...] = jnp.zeros_like(ext_ref)
        h_ref[...] = jnp.zeros_like(h_ref)

    x = x_ref[0]
    xn = _rms_rows(x, g_ref[...]).astype(BF16)
    gu = _dot(xn, win_ref[...])
    gate = gu[:, :w]
    u_raw = gu[:, w:]
    ext_ref[0:SUBLANES, :] = ext_ref[ts:ts + SUBLANES, :]
    ext_ref[SUBLANES:, :] = u_raw
    u = cb_ref[...] + jnp.zeros((ts, w), F32)
    for k in range(CONV_WIDTH):
        off = SUBLANES - (CONV_WIDTH - 1) + k
        u = u + cw_ref[k:k + 1, :] * ext_ref[off:off + ts, :]
    ub = u.astype(BF16)

    def block_diag(wref):
        return jnp.concatenate(
            [_dot(ub[:, hh * LRU_BLOCK:(hh + 1) * LRU_BLOCK], wref[hh]) for hh in range(LRU_HEADS)],
            axis=1)

    r = jax.nn.sigmoid(block_diag(wa_ref) + ba_ref[...])
    i = jax.nn.sigmoid(block_diag(wx_ref) + bx_ref[...])
    z = -lam_ref[...]
    softplus = jnp.maximum(z, 0.0) + jnp.log1p(jnp.exp(-jnp.abs(z)))
    log_a = (-LRU_C) * r * softplus
    a_scr[...] = jnp.exp(log_a)
    th = jnp.tanh(log_a)
    b_scr[...] = jnp.sqrt(2.0 * th / (th - 1.0)) * (i * u)

    def step(t, h):
        h = a_scr[pl.ds(t, 1), :] * h + b_scr[pl.ds(t, 1), :]
        hs_scr[pl.ds(t, 1), :] = h
        return h

    h_ref[0:1, :] = lax.fori_loop(0, ts, step, h_ref[0:1, :], unroll=8)
    y = jax.nn.gelu(gate) * hs_scr[...]
    o_ref[0] = x + _dot(y.astype(BF16), wout_ref[...])


def _rglru(x, norm_g, w_in, conv_w, conv_b, w_a, b_a, w_x, b_x, lam, w_out):
    b, s, d = x.shape
    w = LRU_HEADS * LRU_BLOCK
    row = lambda v: v.reshape(1, -1)
    whole = pl.BlockSpec(memory_space=pltpu.VMEM)
    return pl.pallas_call(
        _rglru_kernel,
        grid=(b, s // LRU_TILE),
        in_specs=[pl.BlockSpec((1, LRU_TILE, d), lambda bi, si: (bi, si, 0))] + [whole] * 10,
        out_specs=pl.BlockSpec((1, LRU_TILE, d), lambda bi, si: (bi, si, 0)),
        out_shape=jax.ShapeDtypeStruct((b, s, d), F32),
        scratch_shapes=[pltpu.VMEM((LRU_TILE + SUBLANES, w), F32), pltpu.VMEM((SUBLANES, w), F32),
                        pltpu.VMEM((LRU_TILE, w), F32), pltpu.VMEM((LRU_TILE, w), F32),
                        pltpu.VMEM((LRU_TILE, w), F32)],
        compiler_params=pltpu.CompilerParams(
            dimension_semantics=("arbitrary", "arbitrary"), vmem_limit_bytes=48 * MIB),
        name="rglru_mixer",
    )(x, row(norm_g), w_in.astype(BF16), conv_w, row(conv_b), w_a.astype(BF16), row(b_a),
      w_x.astype(BF16), row(b_x), row(lam), w_out.astype(BF16))


def _peer_scores_kernel(x_ref, g_ref, wq_ref, sk_ref, xp_ref, st_ref, xn_ref):
    xn = _rms_rows(x_ref[...], g_ref[...])
    xn_ref[...] = xn
    xb = xn.astype(BF16)
    words = _bf16_pair_words(xb)
    tm = words.shape[0]
    for c in range(SUBLANES):
        piece = words[:, (c % (SUBLANES // 2)) * LANES:(c % (SUBLANES // 2) + 1) * LANES]
        xp_ref[pl.ds(c, tm, stride=SUBLANES), :] = piece
    q = _dot(xb, wq_ref[...]).astype(BF16)
    for hp in range(2 * PEER_HEADS):
        st_ref[hp] = _dot_nt(sk_ref[hp % 2], q[:, hp * D_HALF:(hp + 1) * D_HALF])


def _peer_scores(x2d, norm_g, w_q, sub_keys):
    t, d = x2d.shape
    nq = w_q.shape[1]
    whole = pl.BlockSpec(memory_space=pltpu.VMEM)
    return pl.pallas_call(
        _peer_scores_kernel,
        grid=(t // ROW_TILE,),
        in_specs=[pl.BlockSpec((ROW_TILE, d), lambda i: (i, 0)), whole, whole, whole],
        out_specs=[pl.BlockSpec((ROW_TILE * SUBLANES, LANES), lambda i: (i, 0)),
                   pl.BlockSpec((2 * PEER_HEADS, N_KEYS, ROW_TILE), lambda i: (0, 0, i)),
                   pl.BlockSpec((ROW_TILE, d), lambda i: (i, 0))],
        out_shape=[jax.ShapeDtypeStruct((t * SUBLANES, LANES), U32),
                   jax.ShapeDtypeStruct((2 * PEER_HEADS, N_KEYS, t), F32),
                   jax.ShapeDtypeStruct((t, d), F32)],
        compiler_params=pltpu.CompilerParams(
            dimension_semantics=("parallel",), vmem_limit_bytes=48 * MIB),
        name="peer_scores",
    )(x2d, norm_g.reshape(1, d), w_q.astype(BF16), sub_keys.astype(BF16))


def _candidate_blocks():
    blocks = []
    for i in range(PEER_TOPK // 2):
        nj = PEER_TOPK // (i + 1)
        for j0 in range(0, nj, SUBLANES):
            blocks.append(("row", i, j0, min(SUBLANES, nj - j0)))
    blocks.append(("col", PEER_TOPK // 2, PEER_TOPK // 2))
    return blocks


def _peer_topk_kernel(st_ref, r8_ref, par_ref, g_ref, e_ref, r8t_ref, et_ref, *, expert_offset):
    tk = st_ref.shape[2]
    key_id = lax.broadcasted_iota(I32, (N_KEYS, tk), 0).astype(F32)
    row16 = lax.broadcasted_iota(I32, (PEER_TOPK, tk), 0)
    row8 = lax.broadcasted_iota(I32, (SUBLANES, tk), 0)
    row8_f = row8.astype(F32)
    neg_inf = jnp.float32(-jnp.inf)
    blocks = _candidate_blocks()
    big = jnp.float32(PEER_TOPK * PEER_TOPK)

    def top16(x):
        vals = jnp.zeros((PEER_TOPK, tk), F32)
        idxs = jnp.zeros((PEER_TOPK, tk), F32)
        for k in range(PEER_TOPK):
            m = jnp.max(x, axis=0, keepdims=True)
            idx = jnp.min(jnp.where(x == m, key_id, float(N_KEYS)), axis=0, keepdims=True)
            x = jnp.where(key_id == idx, neg_inf, x)
            vals = jnp.where(row16 == k, m, vals)
            idxs = jnp.where(row16 == k, idx, idxs)
        return vals, idxs

    for h in range(PEER_HEADS):
        s0, i0 = top16(st_ref[2 * h])
        s1, i1 = top16(st_ref[2 * h + 1])
        cand, flat, expert = [], [], []
        for blk in blocks:
            if blk[0] == "row":
                _, i, j0, n = blk
                c = s0[i:i + 1, :] + s1[j0:j0 + SUBLANES, :]
                f = float(PEER_TOPK * i + j0) + row8_f
                e = i0[i:i + 1, :] * float(N_KEYS) + i1[j0:j0 + SUBLANES, :]
            else:
                _, i_start, n = blk
                c = s0[i_start:i_start + SUBLANES, :] + s1[0:1, :]
                f = float(PEER_TOPK) * (float(i_start) + row8_f)
                e = i0[i_start:i_start + SUBLANES, :] * float(N_KEYS) + i1[0:1, :]
            cand.append(jnp.where(row8 < n, c, neg_inf))
            flat.append(f)
            expert.append(e)
        cand = jnp.concatenate(cand, axis=0)
        flat = jnp.concatenate(flat, axis=0)
        expert = jnp.concatenate(expert, axis=0)
        best_s = jnp.zeros((PEER_TOPK, tk), F32)
        best_e = jnp.zeros((PEER_TOPK, tk), F32)
        for k in range(PEER_TOPK):
            m = jnp.max(cand, axis=0, keepdims=True)
            fidx = jnp.min(jnp.where(cand == m, flat, big), axis=0, keepdims=True)
            pick = flat == fidx
            e_sel = jnp.max(jnp.where(pick, expert, -1.0), axis=0, keepdims=True)
            cand = jnp.where(pick, neg_inf, cand)
            best_s = jnp.where(row16 == k, m, best_s)
            best_e = jnp.where(row16 == k, e_sel, best_e)
        ex = jnp.exp(best_s - best_s[0:1, :])
        gate = ex / jnp.sum(ex, axis=0, keepdims=True)
        rows = slice(h * PEER_TOPK, (h + 1) * PEER_TOPK)
        best_i = best_e.astype(I32)
        r8t_ref[rows, :] = (best_i >> 1) * SUBLANES
        et_ref[rows, :] = best_i + expert_offset
        par_ref[0, rows, :] = best_i & 1
        g_ref[0, rows, :] = gate
    r8_ref[0] = r8t_ref[...].T
    e_ref[0] = et_ref[...].T


def _peer_topk(scores_t, expert_offset, first_block, nblk):
    slot_major = pl.BlockSpec((1, N_SLOTS, PEER_TOKENS), lambda i: (i, 0, 0))
    token_major = pl.BlockSpec((1, PEER_TOKENS, N_SLOTS), lambda i: (i, 0, 0))
    return pl.pallas_call(
        functools.partial(_peer_topk_kernel, expert_offset=expert_offset),
        grid=(nblk,),
        in_specs=[pl.BlockSpec((2 * PEER_HEADS, N_KEYS, PEER_TOKENS), lambda i: (0, 0, i + first_block))],
        out_specs=[token_major, slot_major, slot_major, token_major],
        out_shape=[jax.ShapeDtypeStruct((nblk, PEER_TOKENS, N_SLOTS), I32),
                   jax.ShapeDtypeStruct((nblk, N_SLOTS, PEER_TOKENS), I32),
                   jax.ShapeDtypeStruct((nblk, N_SLOTS, PEER_TOKENS), F32),
                   jax.ShapeDtypeStruct((nblk, PEER_TOKENS, N_SLOTS), I32)],
        scratch_shapes=[pltpu.VMEM((N_SLOTS, PEER_TOKENS), I32), pltpu.VMEM((N_SLOTS, PEER_TOKENS), I32)],
        compiler_params=pltpu.CompilerParams(dimension_semantics=("parallel",)),
        name="peer_topk",
    )(scores_t)


def _bf16_pair_words(x):
    bits = lax.bitcast_convert_type(x.astype(BF16).astype(F32), U32)
    half = bits.shape[1] // 2
    return (bits[:, :half] >> 16) | (bits[:, half:] & jnp.uint32(0xFFFF0000))


def _pack_kernel(t_ref, o_ref):
    words = _bf16_pair_words(t_ref[0])
    rows = words.shape[0]
    n = words.shape[1] // LANES
    for s in range(n):
        o_ref[pl.ds(s, rows, stride=n), :] = words[:, s * LANES:(s + 1) * LANES]


def _pack_table(tabs, layer):
    _, e, d = tabs.shape
    n = d // (2 * LANES)
    return pl.pallas_call(
        _pack_kernel,
        grid=(e // ROW_TILE,),
        in_specs=[pl.BlockSpec((1, ROW_TILE, d), lambda i: (layer, i, 0))],
        out_specs=pl.BlockSpec((ROW_TILE * n, LANES), lambda i: (i, 0)),
        out_shape=jax.ShapeDtypeStruct((e * n, LANES), U32),
        compiler_params=pltpu.CompilerParams(dimension_semantics=("parallel",)),
        name="peer_pack_table",
    )(tabs)


def _unpack_words(w):
    lo = lax.bitcast_convert_type(w << 16, F32)
    hi = lax.bitcast_convert_type(w & jnp.uint32(0xFFFF0000), F32)
    return lo, hi


def _as_bf16(words):
    return pltpu.bitcast(words, BF16)


def _as_words(packed):
    return pltpu.bitcast(packed, U32)


def _merge_packed(a, b, shift, mask):
    ta = a + _as_bf16(pltpu.roll(_as_words(a), shift, 0))
    tb = b + _as_bf16(pltpu.roll(_as_words(b), SUBLANES - shift, 0))
    return jnp.where(mask, _as_words(ta), _as_words(tb))


def _sublane_iota():
    return lax.broadcasted_iota(I32, (SUBLANES, LANES), 0)


def _merge(a, b, shift, mask):
    ta = a + pltpu.roll(a, shift, 0)
    tb = b + pltpu.roll(b, SUBLANES - shift, 0)
    return jnp.where(mask, ta, tb)


GROUP = SUBLANES // 2
N_GROUPS = N_SLOTS // GROUP
N_ROWS = 2 * N_SLOTS
N_SPREAD = 4
SPREAD_AHEAD = 2


def _dup_halves(v, low):
    r = pltpu.roll(v, GROUP, 0)
    return jnp.where(low, v, r), jnp.where(low, r, v)


def _peer_u_kernel(r8_ref, xp_ref, g_ref, par_ref, tab_ref, a_ref, s_ref, part_ref):
    tb = g_ref.shape[2]
    sub = _sublane_iota()
    m2 = (sub & 2) != 0
    lane_t = lax.broadcasted_iota(I32, (SUBLANES, tb), 1)
    n_part = 2 * N_GROUPS

    def products(t, slot):
        xb = _as_bf16(xp_ref[pl.ds(pl.multiple_of(t * SUBLANES, SUBLANES), SUBLANES), :])
        ids = r8_ref.at[0, t]
        for grp in range(N_GROUPS):
            prods = []
            for i in range(GROUP):
                r8 = pl.multiple_of(ids[grp * GROUP + i], SUBLANES)
                prods.append(_as_bf16(tab_ref[pl.ds(r8, SUBLANES), :]) * xb)
            halves = (_merge_packed(prods[3], prods[1], 2, m2), _merge_packed(prods[2], prods[0], 2, m2))
            for k in range(2):
                lo, hi = _unpack_words(halves[k])
                part_ref[slot, 2 * grp + k] = lo + hi

    def reduce_lanes(t, slot):
        for n in range(n_part):
            col = jnp.sum(part_ref[slot, n], axis=1, keepdims=True)
            rows = slice(n * SUBLANES, (n + 1) * SUBLANES)
            s_ref[rows, :] = jnp.where(lane_t == t, col, s_ref[rows, :])

    def token_pair(i, carry):
        t = 2 * i
        reduce_lanes(t - 2, 0)
        reduce_lanes(t - 1, 1)
        products(t, 0)
        products(t + 1, 1)
        return carry

    s_ref[...] = jnp.zeros_like(s_ref)
    part_ref[...] = jnp.zeros_like(part_ref)
    lax.fori_loop(0, tb // 2, token_pair, 0)
    reduce_lanes(tb - 2, 0)
    reduce_lanes(tb - 1, 1)

    odd_t = (lax.broadcasted_iota(I32, (SUBLANES, tb), 0) & 1) != 0
    _gate_rows(lambda grp: _merge(s_ref[2 * grp * SUBLANES:(2 * grp + 1) * SUBLANES, :],
                                  s_ref[(2 * grp + 1) * SUBLANES:(2 * grp + 2) * SUBLANES, :], 1, odd_t),
               g_ref, par_ref, a_ref)


def _gate_rows(dots_of_group, g_ref, par_ref, a_ref):
    tb = g_ref.shape[2]
    low_t = lax.broadcasted_iota(I32, (SUBLANES, tb), 0) < GROUP
    want = jnp.where(low_t, 0, 1)
    for v in range(N_SLOTS // SUBLANES):
        src = slice(v * SUBLANES, (v + 1) * SUBLANES)
        gates = _dup_halves(g_ref[0, src, :], low_t)
        pars = _dup_halves(par_ref[0, src, :], low_t)
        for half in range(2):
            grp = 2 * v + half
            act = gates[half] * jax.nn.gelu(dots_of_group(grp))
            rows = slice(grp * SUBLANES, (grp + 1) * SUBLANES)
            a_ref[0, rows, :] = jnp.where(pars[half] == want, act, 0.0)


def _sc_token_loop(tab_hbm, e_hbm, idx_v, rows_v, sems, per_worker, begin_token, chunk_rows, end_token):
    wid = lax.axis_index("s") * SC_CORES + lax.axis_index("c")
    n_chunks = N_SLOTS // SC_GATHER_ROWS
    first = wid * per_worker
    last = first + per_worker - 1

    def gather(slot, chunk):
        ids = idx_v.at[slot, pl.ds(chunk * SC_GATHER_ROWS, SC_GATHER_ROWS)]
        return pltpu.make_async_copy(tab_hbm.at[ids], rows_v.at[chunk % 2], sems.at[chunk % 2])

    def one_token(tok, slot, next_tok):
        pltpu.sync_copy(e_hbm.at[next_tok], idx_v.at[1 - slot])
        begin_token(tok)
        for chunk in range(n_chunks):
            if chunk + 1 < n_chunks:
                gather(slot, chunk + 1).start()
            else:
                gather(1 - slot, 0).start()
            gather(slot, chunk).wait()
            chunk_rows(chunk, rows_v.at[chunk % 2])
        end_token(tok)

    def token_pair(k, carry):
        tok = first + 2 * k
        one_token(tok, 0, tok + 1)
        one_token(tok + 1, 1, jnp.minimum(tok + 2, last))
        return carry

    pltpu.sync_copy(e_hbm.at[first], idx_v.at[0])
    gather(0, 0).start()
    lax.fori_loop(0, per_worker // 2, token_pair, 0)
    gather(0, 0).wait()


def _sc_scratch(d):
    return [pltpu.VMEM((2, N_SLOTS), I32), pltpu.VMEM((2, SC_GATHER_ROWS, d), F32),
            pltpu.SemaphoreType.DMA((2,))]


def _sc_u_dots(u_rows, e_tok, xn, first_token):
    d = xn.shape[1]
    n_tokens = e_tok.shape[0]
    per_worker = n_tokens // (SC_CORES * SC_SUBCORES)
    n_half = d // SC_LANES // 2
    mesh = plsc.VectorSubcoreMesh(core_axis_name="c", subcore_axis_name="s")

    def body(u_hbm, e_hbm, x_hbm, o_hbm, idx_v, rows_v, sems, x_v, out_v):
        lane = lax.iota(I32, SC_LANES)

        def begin_token(tok):
            pltpu.sync_copy(x_hbm.at[first_token + tok], x_v)

        def chunk_rows(chunk, rows):
            for part in range(SC_GATHER_ROWS // SC_LANES):
                total = jnp.zeros((SC_LANES,), F32)
                for half in range(2):
                    base = half * n_half * SC_LANES
                    xs = [x_v[pl.ds(base + c * SC_LANES, SC_LANES)] for c in range(n_half)]

                    def row(r, vec):
                        rr = part * SC_LANES + r
                        acc = rows[rr, pl.ds(base, SC_LANES)] * xs[0]
                        for c in range(1, n_half):
                            acc = acc + rows[rr, pl.ds(base + c * SC_LANES, SC_LANES)] * xs[c]
                        return jnp.where(lane == r, jnp.sum(acc), vec)

                    total = total + lax.fori_loop(0, SC_LANES, row, jnp.zeros((SC_LANES,), F32))
                out_v[pl.ds(chunk * SC_GATHER_ROWS + part * SC_LANES, SC_LANES)] = total

        def end_token(tok):
            pltpu.sync_copy(out_v, o_hbm.at[tok])

        _sc_token_loop(u_hbm, e_hbm, idx_v, rows_v, sems, per_worker, begin_token, chunk_rows, end_token)

    return pl.kernel(
        body,
        out_type=jax.ShapeDtypeStruct((n_tokens, N_SLOTS), F32),
        mesh=mesh,
        scratch_types=_sc_scratch(d) + [pltpu.VMEM((d,), F32), pltpu.VMEM((N_SLOTS,), F32)],
        compiler_params=pltpu.CompilerParams(needs_layout_passes=False),
        name="peer_expert_in_sc",
    )(u_rows, e_tok, xn)


def _peer_gate_kernel(s_ref, g_ref, par_ref, a_ref):
    dots = s_ref[...].T
    low_t = lax.broadcasted_iota(I32, (SUBLANES, dots.shape[1]), 0) < GROUP
    _gate_rows(lambda grp: _dup_halves(dots[(grp // 2) * SUBLANES:(grp // 2 + 1) * SUBLANES, :], low_t)[grp % 2],
               g_ref, par_ref, a_ref)


def _peer_weights_kernel(a_ref, w_ref):
    low_t = lax.broadcasted_iota(I32, (SUBLANES, a_ref.shape[2]), 0) < GROUP
    tiles = []
    for v in range(N_SLOTS // SUBLANES):
        first = a_ref[0, 2 * v * SUBLANES:(2 * v + 1) * SUBLANES, :]
        second = a_ref[0, (2 * v + 1) * SUBLANES:(2 * v + 2) * SUBLANES, :]
        first = first + pltpu.roll(first, GROUP, 0)
        second = second + pltpu.roll(second, GROUP, 0)
        tiles.append(jnp.where(low_t, first, second))
    w_ref[...] = jnp.concatenate(tiles, axis=0).T


def _sc_v_sums(v_rows, e_tok, w_tok, x2d, first_token):
    d = x2d.shape[1]
    n_tokens = e_tok.shape[0]
    per_worker = n_tokens // (SC_CORES * SC_SUBCORES)
    n_acc = d // SC_LANES // 2
    mesh = plsc.VectorSubcoreMesh(core_axis_name="c", subcore_axis_name="s")

    def body(v_hbm, e_hbm, w_hbm, x_hbm, o_hbm, idx_v, rows_v, sems, w_v, out_v):
        def begin_token(tok):
            pltpu.sync_copy(w_hbm.at[tok], w_v)
            pltpu.sync_copy(x_hbm.at[first_token + tok], out_v)

        def chunk_rows(chunk, rows):
            for half in range(2):
                base = half * n_acc * SC_LANES

                def row(r, acc):
                    slot = jnp.full((SC_LANES,), chunk * SC_GATHER_ROWS + r, I32)
                    w = plsc.load_gather(w_v, [slot])
                    return tuple(acc[c] + w * rows[r, pl.ds(base + c * SC_LANES, SC_LANES)]
                                 for c in range(n_acc))

                acc0 = tuple(out_v[pl.ds(base + c * SC_LANES, SC_LANES)] for c in range(n_acc))
                acc = lax.fori_loop(0, SC_GATHER_ROWS, row, acc0)
                for c in range(n_acc):
                    out_v[pl.ds(base + c * SC_LANES, SC_LANES)] = acc[c]

        def end_token(tok):
            pltpu.sync_copy(out_v, o_hbm.at[tok])

        _sc_token_loop(v_hbm, e_hbm, idx_v, rows_v, sems, per_worker, begin_token, chunk_rows, end_token)

    return pl.kernel(
        body,
        out_type=jax.ShapeDtypeStruct((n_tokens, d), F32),
        mesh=mesh,
        scratch_types=_sc_scratch(d) + [pltpu.VMEM((N_SLOTS,), F32), pltpu.VMEM((d,), F32)],
        compiler_params=pltpu.CompilerParams(needs_layout_passes=False),
        name="peer_expert_out_sc",
    )(v_rows, e_tok, w_tok, x2d)


def _token_tiles(ref, xt_ref, to_tiles):
    tb, d = ref.shape
    for q in range(tb // SUBLANES):
        for c in range(d // LANES):
            rows = slice(q * SUBLANES, (q + 1) * SUBLANES)
            cols = slice(c * LANES, (c + 1) * LANES)
            strided = pl.ds(q * SUBLANES * SUBLANES + c, SUBLANES, stride=SUBLANES)
            if to_tiles:
                xt_ref[strided, :] = ref[rows, cols]
            else:
                ref[rows, cols] = xt_ref[strided, :]


def _peer_v_kernel(r8_ref, a_ref, x_ref, tab_ref, o_ref, bc_ref, xt_ref):
    tb = x_ref.shape[0]
    sub = _sublane_iota()
    low = sub < GROUP
    lane_t = lax.broadcasted_iota(I32, (SUBLANES, tb), 1)
    _token_tiles(x_ref, xt_ref, True)

    def spread(t, buf):
        for k in range(N_ROWS // SUBLANES):
            rows = slice(k * SUBLANES, (k + 1) * SUBLANES)
            col = jnp.sum(jnp.where(lane_t == t, a_ref[0, rows, :], 0.0), axis=1, keepdims=True)
            bits = lax.bitcast_convert_type(col.astype(BF16).astype(F32), U32)
            bc_ref[buf, rows, :] = jnp.broadcast_to(bits | (bits >> 16), (SUBLANES, LANES))

    def accumulate(t, buf):
        acc_lo = jnp.zeros((SUBLANES, LANES), F32)
        acc_hi = jnp.zeros((SUBLANES, LANES), F32)
        ids = r8_ref.at[0, t]
        for grp in range(N_GROUPS):
            prods = []
            for i in range(GROUP):
                row = grp * SUBLANES + i
                r8 = pl.multiple_of(ids[grp * GROUP + i], SUBLANES)
                am = jnp.where(low, bc_ref[buf, row:row + 1, :], bc_ref[buf, row + GROUP:row + GROUP + 1, :])
                prods.append(_as_bf16(tab_ref[pl.ds(r8, SUBLANES), :]) * _as_bf16(am))
            lo, hi = _unpack_words(_as_words((prods[0] + prods[1]) + (prods[2] + prods[3])))
            acc_lo = acc_lo + lo
            acc_hi = acc_hi + hi
        acc_lo = acc_lo + pltpu.roll(acc_lo, GROUP, 0)
        acc_hi = acc_hi + pltpu.roll(acc_hi, GROUP, 0)
        tile = pl.ds(pl.multiple_of(t * SUBLANES, SUBLANES), SUBLANES)
        xt_ref[tile, :] = xt_ref[tile, :] + jnp.where(low, acc_lo, acc_hi)

    def token_group(i, carry):
        t = N_SPREAD * i
        for k in range(N_SPREAD):
            spread(jnp.minimum(t + k + SPREAD_AHEAD, tb - 1), (k + SPREAD_AHEAD) % N_SPREAD)
            accumulate(t + k, k)
        return carry

    for k in range(SPREAD_AHEAD):
        spread(k, k)
    lax.fori_loop(0, tb // N_SPREAD, token_group, 0)
    _token_tiles(o_ref, xt_ref, False)


def _peer_experts(x2d, xp, r8, e_tok, par, gates, dots_sc, v_rows, u_tab, v_tab):
    t, d = x2d.shape
    nblk = t // PEER_TOKENS
    nblk_tc = nblk - SC_TOKEN_BLOCKS
    smem_block = pl.BlockSpec((1, PEER_TOKENS, N_SLOTS), lambda i: (i, 0, 0), memory_space=pltpu.SMEM)
    slot_block = pl.BlockSpec((1, N_SLOTS, PEER_TOKENS), lambda i: (i, 0, 0))
    row_block = pl.BlockSpec((1, N_ROWS, PEER_TOKENS), lambda i: (i, 0, 0))
    tile_block = pl.BlockSpec((PEER_TOKENS * SUBLANES, LANES), lambda i: (i, 0))
    tok_block = pl.BlockSpec((PEER_TOKENS, d), lambda i: (i, 0))
    whole = pl.BlockSpec(memory_space=pltpu.VMEM)
    params = pltpu.CompilerParams(dimension_semantics=("parallel",), vmem_limit_bytes=56 * MIB)

    act_tc = pl.pallas_call(
        _peer_u_kernel,
        grid=(nblk_tc,),
        in_specs=[smem_block, tile_block, slot_block, slot_block, whole],
        out_specs=row_block,
        out_shape=jax.ShapeDtypeStruct((nblk_tc, N_ROWS, PEER_TOKENS), F32),
        scratch_shapes=[pltpu.VMEM((2 * N_ROWS, PEER_TOKENS), F32),
                        pltpu.VMEM((2, 2 * N_GROUPS, SUBLANES, LANES), F32)],
        compiler_params=params,
        name="peer_expert_in",
    )(r8, xp, gates, par, u_tab)
    shifted = pl.BlockSpec((1, N_SLOTS, PEER_TOKENS), lambda i: (i + nblk_tc, 0, 0))
    act_sc = pl.pallas_call(
        _peer_gate_kernel,
        grid=(SC_TOKEN_BLOCKS,),
        in_specs=[pl.BlockSpec((PEER_TOKENS, N_SLOTS), lambda i: (i, 0)), shifted, shifted],
        out_specs=row_block,
        out_shape=jax.ShapeDtypeStruct((SC_TOKEN_BLOCKS, N_ROWS, PEER_TOKENS), F32),
        compiler_params=pltpu.CompilerParams(dimension_semantics=("parallel",)),
        name="peer_gate_sc",
    )(dots_sc, gates, par)
    act = jnp.concatenate([act_tc, act_sc], axis=0)

    nblk_v = nblk - SC_V_BLOCKS
    w_sc = pl.pallas_call(
        _peer_weights_kernel,
        grid=(SC_V_BLOCKS,),
        in_specs=[pl.BlockSpec((1, N_ROWS, PEER_TOKENS), lambda i: (i + nblk_v, 0, 0))],
        out_specs=pl.BlockSpec((PEER_TOKENS, N_SLOTS), lambda i: (i, 0)),
        out_shape=jax.ShapeDtypeStruct((SC_V_BLOCKS * PEER_TOKENS, N_SLOTS), F32),
        compiler_params=pltpu.CompilerParams(dimension_semantics=("parallel",)),
        name="peer_weights_sc",
    )(act)
    y_sc = _sc_v_sums(v_rows, e_tok[nblk_v:].reshape(-1, N_SLOTS), w_sc, x2d, nblk_v * PEER_TOKENS)
    y_tc = pl.pallas_call(
        _peer_v_kernel,
        grid=(nblk_v,),
        in_specs=[smem_block, row_block, tok_block, whole],
        out_specs=tok_block,
        out_shape=jax.ShapeDtypeStruct((nblk_v * PEER_TOKENS, d), F32),
        scratch_shapes=[pltpu.VMEM((N_SPREAD, N_ROWS, LANES), U32),
                        pltpu.VMEM((PEER_TOKENS * SUBLANES, LANES), F32)],
        compiler_params=params,
        name="peer_expert_out",
    )(r8, act, x2d, v_tab)
    return jnp.concatenate([y_tc, y_sc], axis=0)


def _peer_ffn(x2d, norm_g, w_q, sub_keys, u_tabs, v_tabs, layer):
    n_experts, d = u_tabs.shape[1:]
    xp, scores_t, xn = _peer_scores(x2d, norm_g, w_q, sub_keys)
    nblk = x2d.shape[0] // PEER_TOKENS
    nblk_tc = nblk - SC_TOKEN_BLOCKS
    top_sc = _peer_topk(scores_t, layer * n_experts, nblk_tc, SC_TOKEN_BLOCKS)
    dots_sc = _sc_u_dots(u_tabs.reshape(-1, d), top_sc[3].reshape(-1, N_SLOTS), xn, nblk_tc * PEER_TOKENS)
    top_tc = _peer_topk(scores_t, layer * n_experts, 0, nblk_tc)
    r8, par, gates, e_tok = [jnp.concatenate([a, b], axis=0) for a, b in zip(top_tc, top_sc)]
    return _peer_experts(x2d, xp, r8, e_tok, par, gates, dots_sc, v_tabs.reshape(-1, d),
                         _pack_table(u_tabs, layer), _pack_table(v_tabs, layer))


def kernel(x, attn_norm_g, attn_w_qkv, attn_q_g, attn_k_g, attn_rel_bias, attn_w_out, rec_norm_g, rec_w_in, rec_conv_w, rec_conv_b, rec_w_a, rec_b_a, rec_w_x, rec_b_x, rec_lambda, rec_w_out, ffn_norm_g, peer_w_q, peer_sub_keys, peer_u, peer_v):
    b, s, d = x.shape
    depth = ffn_norm_g.shape[0]
    for layer in range(depth):
        j = layer // 2
        if layer % 2 == 0:
            qkv = _qkv_proj(x.reshape(b * s, d), attn_norm_g[j], attn_w_qkv[j], attn_q_g[j], attn_k_g[j])
            x = _attention(x, qkv, attn_rel_bias[j], attn_w_out[j])
        else:
            x = _rglru(x, rec_norm_g[j], rec_w_in[j], rec_conv_w[j], rec_conv_b[j], rec_w_a[j],
                       rec_b_a[j], rec_w_x[j], rec_b_x[j], rec_lambda[j], rec_w_out[j])
        x = _peer_ffn(x.reshape(b * s, d), ffn_norm_g[layer], peer_w_q[layer], peer_sub_keys[layer],
                      peer_u, peer_v, layer).reshape(b, s, d)
    return x
```

```python
import numpy as np
import jax
import jax.numpy as jnp
from jax import lax
from jax.experimental import pallas as pl
from jax.experimental.pallas import tpu as pltpu
from jax.experimental.pallas import tpu_sc as plsc

F32 = jnp.float32
BF16 = jnp.bfloat16
U32 = jnp.uint32
I32 = jnp.int32

EPS = 1e-6
NEG_INF = -1e30

D_MODEL = 1024
CHUNK = 64
ATTN_HEADS = 16
HEAD_DIM = 64
LEFT_CHUNKS = 8
REL_CLIP = 256
LRU_HEADS = 4
LRU_BLOCK = 256
CONV_WIDTH = 4
LRU_C = 8.0
PEER_HEADS = 8
N_KEYS = 128
PEER_TOPK = 16
D_HALF = 128

SUBLANES = 8
LANES = 128

Q_TILE = 4 * CHUNK
N_KEY_BLOCKS = 3
ROW_TILE = 512
PEER_TOKENS = 128
LRU_TILE = 256
N_SLOTS = PEER_HEADS * PEER_TOPK

MIB = 1024 * 1024

SC_CORES = 2
SC_SUBCORES = 16
SC_LANES = 16
SC_GATHER_ROWS = 32
SC_TOKEN_BLOCKS = 124
SC_V_BLOCKS = 88


def _rms_rows(x, g):
    ms = jnp.mean(x * x, axis=-1, keepdims=True)
    return (x * lax.rsqrt(ms + EPS)) * g


def _split_bf16(v):
    hi = v.astype(BF16)
    lo = (v - hi.astype(F32)).astype(BF16)
    return hi, lo


def _dot(a, b):
    return jnp.dot(a, b, preferred_element_type=F32)


def _dot_nt(a, b):
    return lax.dot_general(a, b, (((1,), (1,)), ((), ())), preferred_element_type=F32)


def _qkv_kernel(x_ref, g_ref, w_ref, gain_ref, bsel_ref, bexp_ref, o_ref, xn_ref):
    j = pl.program_id(1)

    @pl.when(j == 0)
    def _():
        xn_ref[...] = _rms_rows(x_ref[...], g_ref[...]).astype(BF16)

    y = _dot(xn_ref[...], w_ref[...])

    @pl.when(j < 2)
    def _():
        hi, lo = _split_bf16(y * y)
        ms = _dot(hi, bsel_ref[...]) + _dot(lo, bsel_ref[...])
        rhi, rlo = _split_bf16(lax.rsqrt(ms + EPS))
        rs = _dot(rhi, bexp_ref[...]) + _dot(rlo, bexp_ref[...])
        o_ref[...] = ((y * rs) * gain_ref[0]).astype(BF16)

    @pl.when(j == 2)
    def _():
        o_ref[...] = y.astype(BF16)


def _qkv_proj(x2d, norm_g, w_qkv, q_g, k_g):
    t, d = x2d.shape
    scale = HEAD_DIM ** -0.5
    gains = jnp.stack([jnp.tile(q_g, ATTN_HEADS) * scale, jnp.tile(k_g, ATTN_HEADS),
                       jnp.ones((d,), F32)]).reshape(3, 1, d)
    head_of_col = np.arange(d) // HEAD_DIM
    bsel = (head_of_col[:, None] == np.arange(LANES)[None, :]).astype(np.float32) / HEAD_DIM
    bexp = (np.arange(LANES)[:, None] == head_of_col[None, :]).astype(np.float32)
    return pl.pallas_call(
        _qkv_kernel,
        grid=(t // ROW_TILE, 3),
        in_specs=[
            pl.BlockSpec((ROW_TILE, d), lambda i, j: (i, 0)),
            pl.BlockSpec((1, d), lambda i, j: (0, 0)),
            pl.BlockSpec((d, d), lambda i, j: (0, j)),
            pl.BlockSpec((1, 1, d), lambda i, j: (j, 0, 0)),
            pl.BlockSpec((d, LANES), lambda i, j: (0, 0)),
            pl.BlockSpec((LANES, d), lambda i, j: (0, 0)),
        ],
        out_specs=pl.BlockSpec((ROW_TILE, d), lambda i, j: (i, j)),
        out_shape=jax.ShapeDtypeStruct((t, 3 * d), BF16),
        scratch_shapes=[pltpu.VMEM((ROW_TILE, d), BF16)],
        compiler_params=pltpu.CompilerParams(
            dimension_semantics=("parallel", "arbitrary"), vmem_limit_bytes=40 * MIB),
        name="qkv_proj",
    )(x2d, norm_g.reshape(1, d), w_qkv.astype(BF16), gains,
      jnp.asarray(bsel, BF16), jnp.asarray(bexp, BF16))


def _attn_kernel(x_ref, q_ref, k0_ref, k1_ref, k2_ref, v0_ref, v1_ref, v2_ref,
                 bias_ref, wout_ref, o_ref, att_ref):
    qt = pl.program_id(1)
    k_refs = (k0_ref, k1_ref, k2_ref)
    v_refs = (v0_ref, v1_ref, v2_ref)
    negs = [jnp.where(qt - (N_KEY_BLOCKS - 1) + kb < 0, NEG_INF, 0.0).astype(F32)
            for kb in range(N_KEY_BLOCKS)]
    lane = lax.broadcasted_iota(I32, (Q_TILE, LANES), 1)
    first_half = lane < HEAD_DIM
    for p in range(ATTN_HEADS // 2):
        cols = slice(p * LANES, (p + 1) * LANES)
        qp = q_ref[0, :, cols]
        outs = []
        for hh in range(2):
            h = 2 * p + hh
            keep = first_half if hh == 0 else jnp.logical_not(first_half)
            qm = jnp.where(keep, qp, jnp.zeros_like(qp))
            s = [_dot_nt(qm, k_refs[kb][0, :, cols]) + bias_ref[kb, h] + negs[kb]
                 for kb in range(N_KEY_BLOCKS)]
            m = jnp.max(s[0], axis=-1, keepdims=True)
            for kb in range(1, N_KEY_BLOCKS):
                m = jnp.maximum(m, jnp.max(s[kb], axis=-1, keepdims=True))
            acc = jnp.zeros((Q_TILE, LANES), F32)
            l = jnp.zeros((Q_TILE, 1), F32)
            for kb in range(N_KEY_BLOCKS):
                e = jnp.exp(s[kb] - m)
                l = l + jnp.sum(e, axis=-1, keepdims=True)
                acc = acc + _dot(e.astype(BF16), v_refs[kb][0, :, cols])
            outs.append(acc / l)
        att_ref[:, cols] = jnp.where(first_half, outs[0], outs[1]).astype(BF16)
    o_ref[0] = x_ref[0] + _dot(att_ref[...], wout_ref[...])


def _band_bias(table):
    qi = np.arange(Q_TILE)[:, None]
    kj = np.arange(Q_TILE)[None, :]
    period = 2 * Q_TILE
    m = np.arange(period)
    delta = np.where(m < Q_TILE, m, m - period)
    blocks = []
    for kb in range(N_KEY_BLOCKS):
        dist = (N_KEY_BLOCKS - 1 - kb) * Q_TILE - delta
        diag = table[:, np.clip(dist, -REL_CLIP, REL_CLIP) + REL_CLIP]
        toep = jnp.tile(diag, (1, Q_TILE))[:, :Q_TILE * (period - 1)]
        toep = toep.reshape(-1, Q_TILE, period - 1)[:, :, :Q_TILE]
        dchunk = (Q_TILE // CHUNK) * (kb - (N_KEY_BLOCKS - 1)) + kj // CHUNK - qi // CHUNK
        valid = (dchunk >= -LEFT_CHUNKS) & (dchunk <= 0)
        blocks.append(jnp.where(valid[None], toep, NEG_INF))
    return jnp.stack(blocks).astype(F32)


def _attention(x, qkv, rel_bias_table, w_out):
    b, s, d = x.shape
    nq = s // Q_TILE
    qkv3 = qkv.reshape(b, s, 3 * d)
    bias = _band_bias(rel_bias_table)

    def kv_spec(kb, col):
        return pl.BlockSpec(
            (1, Q_TILE, d),
            lambda bi, qt: (bi, jnp.maximum(qt - (N_KEY_BLOCKS - 1) + kb, 0), col))

    in_specs = [pl.BlockSpec((1, Q_TILE, d), lambda bi, qt: (bi, qt, 0)),
                pl.BlockSpec((1, Q_TILE, d), lambda bi, qt: (bi, qt, 0))]
    in_specs += [kv_spec(kb, 1) for kb in range(N_KEY_BLOCKS)]
    in_specs += [kv_spec(kb, 2) for kb in range(N_KEY_BLOCKS)]
    in_specs += [pl.BlockSpec(memory_space=pltpu.VMEM), pl.BlockSpec(memory_space=pltpu.VMEM)]
    return pl.pallas_call(
        _attn_kernel,
        grid=(b, nq),
        in_specs=in_specs,
        out_specs=pl.BlockSpec((1, Q_TILE, d), lambda bi, qt: (bi, qt, 0)),
        out_shape=jax.ShapeDtypeStruct((b, s, d), F32),
        scratch_shapes=[pltpu.VMEM((Q_TILE, d), BF16)],
        compiler_params=pltpu.CompilerParams(
            dimension_semantics=("parallel", "parallel"), vmem_limit_bytes=48 * MIB),
        name="band_attention",
    )(x, qkv3, qkv3, qkv3, qkv3, qkv3, qkv3, qkv3, bias, w_out.astype(BF16))


def _rglru_kernel(x_ref, g_ref, win_ref, cw_ref, cb_ref, wa_ref, ba_ref, wx_ref, bx_ref,
                  lam_ref, wout_ref, o_ref, ext_ref, h_ref, a_scr, b_scr, hs_scr):
    ts = x_ref.shape[1]
    w = LRU_HEADS * LRU_BLOCK

    @pl.when(pl.program_id(1) == 0)
    def _():
        ext_ref[...] = jnp.zeros_like(ext_ref)
        h_ref[...] = jnp.zeros_like(h_ref)

    x = x_ref[0]
    xn = _rms_rows(x, g_ref[...]).astype(BF16)
    gu = _dot(xn, win_ref[...])
    gate = gu[:, :w]
    u_raw = gu[:, w:]
    ext_ref[0:SUBLANES, :] = ext_ref[ts:ts + SUBLANES, :]
    ext_ref[SUBLANES:, :] = u_raw
    u = cb_ref[...] + jnp.zeros((ts, w), F32)
    for k in range(CONV_WIDTH):
        off = SUBLANES - (CONV_WIDTH - 1) + k
        u = u + cw_ref[k:k + 1, :] * ext_ref[off:off + ts, :]
    ub = u.astype(BF16)

    def block_diag(wref):
        return jnp.concatenate(
            [_dot(ub[:, hh * LRU_BLOCK:(hh + 1) * LRU_BLOCK], wref[hh]) for hh in range(LRU_HEADS)],
            axis=1)

    r = jax.nn.sigmoid(block_diag(wa_ref) + ba_ref[...])
    i = jax.nn.sigmoid(block_diag(wx_ref) + bx_ref[...])
    z = -lam_ref[...]
    softplus = jnp.maximum(z, 0.0) + jnp.log1p(jnp.exp(-jnp.abs(z)))
    log_a = (-LRU_C) * r * softplus
    a_scr[...] = jnp.exp(log_a)
    th = jnp.tanh(log_a)
    b_scr[...] = jnp.sqrt(2.0 * th / (th - 1.0)) * (i * u)

    def step(t, h):
        h = a_scr[pl.ds(t, 1), :] * h + b_scr[pl.ds(t, 1), :]
        hs_scr[pl.ds(t, 1), :] = h
        return h

    h_ref[0:1, :] = lax.fori_loop(0, ts, step, h_ref[0:1, :], unroll=8)
    y = jax.nn.gelu(gate) * hs_scr[...]
    o_ref[0] = x + _dot(y.astype(BF16), wout_ref[...])


def _rglru(x, norm_g, w_in, conv_w, conv_b, w_a, b_a, w_x, b_x, lam, w_out):
    b, s, d = x.shape
    w = LRU_HEADS * LRU_BLOCK
    row = lambda v: v.reshape(1, -1)
    whole = pl.BlockSpec(memory_space=pltpu.VMEM)
    return pl.pallas_call(
        _rglru_kernel,
        grid=(b, s // LRU_TILE),
        in_specs=[pl.BlockSpec((1, LRU_TILE, d), lambda bi, si: (bi, si, 0))] + [whole] * 10,
        out_specs=pl.BlockSpec((1, LRU_TILE, d), lambda bi, si: (bi, si, 0)),
        out_shape=jax.ShapeDtypeStruct((b, s, d), F32),
        scratch_shapes=[pltpu.VMEM((LRU_TILE + SUBLANES, w), F32), pltpu.VMEM((SUBLANES, w), F32),
                        pltpu.VMEM((LRU_TILE, w), F32), pltpu.VMEM((LRU_TILE, w), F32),
                        pltpu.VMEM((LRU_TILE, w), F32)],
        compiler_params=pltpu.CompilerParams(
            dimension_semantics=("arbitrary", "arbitrary"), vmem_limit_bytes=48 * MIB),
        name="rglru_mixer",
    )(x, row(norm_g), w_in.astype(BF16), conv_w, row(conv_b), w_a.astype(BF16), row(b_a),
      w_x.astype(BF16), row(b_x), row(lam), w_out.astype(BF16))


def _peer_scores_kernel(x_ref, g_ref, wq_ref, sk_ref, xp_ref, st_ref, xw_ref):
    xb = _rms_rows(x_ref[...], g_ref[...]).astype(BF16)
    words = _bf16_pair_words(xb)
    xw_ref[...] = words
    tm = words.shape[0]
    for c in range(SUBLANES):
        piece = words[:, (c % (SUBLANES // 2)) * LANES:(c % (SUBLANES // 2) + 1) * LANES]
        xp_ref[pl.ds(c, tm, stride=SUBLANES), :] = piece
    q = _dot(xb, wq_ref[...]).astype(BF16)
    for hp in range(2 * PEER_HEADS):
        st_ref[hp] = _dot_nt(sk_ref[hp % 2], q[:, hp * D_HALF:(hp + 1) * D_HALF])


def _peer_scores(x2d, norm_g, w_q, sub_keys):
    t, d = x2d.shape
    nq = w_q.shape[1]
    whole = pl.BlockSpec(memory_space=pltpu.VMEM)
    return pl.pallas_call(
        _peer_scores_kernel,
        grid=(t // ROW_TILE,),
        in_specs=[pl.BlockSpec((ROW_TILE, d), lambda i: (i, 0)), whole, whole, whole],
        out_specs=[pl.BlockSpec((ROW_TILE * SUBLANES, LANES), lambda i: (i, 0)),
                   pl.BlockSpec((2 * PEER_HEADS, N_KEYS, ROW_TILE), lambda i: (0, 0, i)),
                   pl.BlockSpec((ROW_TILE, d // 2), lambda i: (i, 0))],
        out_shape=[jax.ShapeDtypeStruct((t * SUBLANES, LANES), U32),
                   jax.ShapeDtypeStruct((2 * PEER_HEADS, N_KEYS, t), F32),
                   jax.ShapeDtypeStruct((t, d // 2), U32)],
        compiler_params=pltpu.CompilerParams(
            dimension_semantics=("parallel",), vmem_limit_bytes=48 * MIB),
        name="peer_scores",
    )(x2d, norm_g.reshape(1, d), w_q.astype(BF16), sub_keys.astype(BF16))


def _candidate_blocks():
    blocks = []
    for i in range(PEER_TOPK // 2):
        nj = PEER_TOPK // (i + 1)
        for j0 in range(0, nj, SUBLANES):
            blocks.append(("row", i, j0, min(SUBLANES, nj - j0)))
    blocks.append(("col", PEER_TOPK // 2, PEER_TOPK // 2))
    return blocks


def _peer_topk_kernel(st_ref, r8_ref, par_ref, g_ref, e_ref, r8t_ref, et_ref):
    tk = st_ref.shape[2]
    key_id = lax.broadcasted_iota(I32, (N_KEYS, tk), 0).astype(F32)
    row16 = lax.broadcasted_iota(I32, (PEER_TOPK, tk), 0)
    row8 = lax.broadcasted_iota(I32, (SUBLANES, tk), 0)
    row8_f = row8.astype(F32)
    neg_inf = jnp.float32(-jnp.inf)
    blocks = _candidate_blocks()
    big = jnp.float32(PEER_TOPK * PEER_TOPK)

    def top16(x):
        vals = jnp.zeros((PEER_TOPK, tk), F32)
        idxs = jnp.zeros((PEER_TOPK, tk), F32)
        for k in range(PEER_TOPK):
            m = jnp.max(x, axis=0, keepdims=True)
            idx = jnp.min(jnp.where(x == m, key_id, float(N_KEYS)), axis=0, keepdims=True)
            x = jnp.where(key_id == idx, neg_inf, x)
            vals = jnp.where(row16 == k, m, vals)
            idxs = jnp.where(row16 == k, idx, idxs)
        return vals, idxs

    for h in range(PEER_HEADS):
        s0, i0 = top16(st_ref[2 * h])
        s1, i1 = top16(st_ref[2 * h + 1])
        cand, flat, expert = [], [], []
        for blk in blocks:
            if blk[0] == "row":
                _, i, j0, n = blk
                c = s0[i:i + 1, :] + s1[j0:j0 + SUBLANES, :]
                f = float(PEER_TOPK * i + j0) + row8_f
                e = i0[i:i + 1, :] * float(N_KEYS) + i1[j0:j0 + SUBLANES, :]
            else:
                _, i_start, n = blk
                c = s0[i_start:i_start + SUBLANES, :] + s1[0:1, :]
                f = float(PEER_TOPK) * (float(i_start) + row8_f)
                e = i0[i_start:i_start + SUBLANES, :] * float(N_KEYS) + i1[0:1, :]
            cand.append(jnp.where(row8 < n, c, neg_inf))
            flat.append(f)
            expert.append(e)
        cand = jnp.concatenate(cand, axis=0)
        flat = jnp.concatenate(flat, axis=0)
        expert = jnp.concatenate(expert, axis=0)
        best_s = jnp.zeros((PEER_TOPK, tk), F32)
        best_e = jnp.zeros((PEER_TOPK, tk), F32)
        for k in range(PEER_TOPK):
            m = jnp.max(cand, axis=0, keepdims=True)
            fidx = jnp.min(jnp.where(cand == m, flat, big), axis=0, keepdims=True)
            pick = flat == fidx
            e_sel = jnp.max(jnp.where(pick, expert, -1.0), axis=0, keepdims=True)
            cand = jnp.where(pick, neg_inf, cand)
            best_s = jnp.where(row16 == k, m, best_s)
            best_e = jnp.where(row16 == k, e_sel, best_e)
        ex = jnp.exp(best_s - best_s[0:1, :])
        gate = ex / jnp.sum(ex, axis=0, keepdims=True)
        rows = slice(h * PEER_TOPK, (h + 1) * PEER_TOPK)
        best_i = best_e.astype(I32)
        r8t_ref[rows, :] = (best_i >> 1) * SUBLANES
        et_ref[rows, :] = best_i
        par_ref[0, rows, :] = best_i & 1
        g_ref[0, rows, :] = gate
    r8_ref[0] = r8t_ref[...].T
    e_ref[0] = et_ref[...].T


def _peer_topk(scores_t, first_block, nblk):
    slot_major = pl.BlockSpec((1, N_SLOTS, PEER_TOKENS), lambda i: (i, 0, 0))
    token_major = pl.BlockSpec((1, PEER_TOKENS, N_SLOTS), lambda i: (i, 0, 0))
    return pl.pallas_call(
        _peer_topk_kernel,
        grid=(nblk,),
        in_specs=[pl.BlockSpec((2 * PEER_HEADS, N_KEYS, PEER_TOKENS), lambda i: (0, 0, i + first_block))],
        out_specs=[token_major, slot_major, slot_major, token_major],
        out_shape=[jax.ShapeDtypeStruct((nblk, PEER_TOKENS, N_SLOTS), I32),
                   jax.ShapeDtypeStruct((nblk, N_SLOTS, PEER_TOKENS), I32),
                   jax.ShapeDtypeStruct((nblk, N_SLOTS, PEER_TOKENS), F32),
                   jax.ShapeDtypeStruct((nblk, PEER_TOKENS, N_SLOTS), I32)],
        scratch_shapes=[pltpu.VMEM((N_SLOTS, PEER_TOKENS), I32), pltpu.VMEM((N_SLOTS, PEER_TOKENS), I32)],
        compiler_params=pltpu.CompilerParams(dimension_semantics=("parallel",)),
        name="peer_topk",
    )(scores_t)


def _bf16_pair_words(x):
    bits = lax.bitcast_convert_type(x.astype(BF16).astype(F32), U32)
    half = bits.shape[1] // 2
    return (bits[:, :half] >> 16) | (bits[:, half:] & jnp.uint32(0xFFFF0000))


def _pack_kernel(t_ref, o_ref, w_ref):
    words = _bf16_pair_words(t_ref[0])
    w_ref[...] = words
    rows = words.shape[0]
    n = words.shape[1] // LANES
    for s in range(n):
        o_ref[pl.ds(s, rows, stride=n), :] = words[:, s * LANES:(s + 1) * LANES]


def _pack_table(tabs, layer):
    _, e, d = tabs.shape
    n = d // (2 * LANES)
    return pl.pallas_call(
        _pack_kernel,
        grid=(e // ROW_TILE,),
        in_specs=[pl.BlockSpec((1, ROW_TILE, d), lambda i: (layer, i, 0))],
        out_specs=[pl.BlockSpec((ROW_TILE * n, LANES), lambda i: (i, 0)),
                   pl.BlockSpec((ROW_TILE, d // 2), lambda i: (i, 0))],
        out_shape=[jax.ShapeDtypeStruct((e * n, LANES), U32), jax.ShapeDtypeStruct((e, d // 2), U32)],
        compiler_params=pltpu.CompilerParams(dimension_semantics=("parallel",)),
        name="peer_pack_table",
    )(tabs)


def _unpack_words(w):
    lo = lax.bitcast_convert_type(w << 16, F32)
    hi = lax.bitcast_convert_type(w & jnp.uint32(0xFFFF0000), F32)
    return lo, hi


def _as_bf16(words):
    return pltpu.bitcast(words, BF16)


def _as_words(packed):
    return pltpu.bitcast(packed, U32)


def _merge_packed(a, b, shift, mask):
    ta = a + _as_bf16(pltpu.roll(_as_words(a), shift, 0))
    tb = b + _as_bf16(pltpu.roll(_as_words(b), SUBLANES - shift, 0))
    return jnp.where(mask, _as_words(ta), _as_words(tb))


def _sublane_iota():
    return lax.broadcasted_iota(I32, (SUBLANES, LANES), 0)


def _merge(a, b, shift, mask):
    ta = a + pltpu.roll(a, shift, 0)
    tb = b + pltpu.roll(b, SUBLANES - shift, 0)
    return jnp.where(mask, ta, tb)


GROUP = SUBLANES // 2
N_GROUPS = N_SLOTS // GROUP
N_ROWS = 2 * N_SLOTS
N_SPREAD = 4
SPREAD_AHEAD = 2


def _dup_halves(v, low):
    r = pltpu.roll(v, GROUP, 0)
    return jnp.where(low, v, r), jnp.where(low, r, v)


def _peer_u_kernel(r8_ref, xp_ref, g_ref, par_ref, tab_ref, a_ref, s_ref, part_ref):
    tb = g_ref.shape[2]
    sub = _sublane_iota()
    m2 = (sub & 2) != 0
    lane_t = lax.broadcasted_iota(I32, (SUBLANES, tb), 1)
    n_part = 2 * N_GROUPS

    def products(t, slot):
        xb = _as_bf16(xp_ref[pl.ds(pl.multiple_of(t * SUBLANES, SUBLANES), SUBLANES), :])
        ids = r8_ref.at[0, t]
        for grp in range(N_GROUPS):
            prods = []
            for i in range(GROUP):
                r8 = pl.multiple_of(ids[grp * GROUP + i], SUBLANES)
                prods.append(_as_bf16(tab_ref[pl.ds(r8, SUBLANES), :]) * xb)
            halves = (_merge_packed(prods[3], prods[1], 2, m2), _merge_packed(prods[2], prods[0], 2, m2))
            for k in range(2):
                lo, hi = _unpack_words(halves[k])
                part_ref[slot, 2 * grp + k] = lo + hi

    def reduce_lanes(t, slot):
        for n in range(n_part):
            col = jnp.sum(part_ref[slot, n], axis=1, keepdims=True)
            rows = slice(n * SUBLANES, (n + 1) * SUBLANES)
            s_ref[rows, :] = jnp.where(lane_t == t, col, s_ref[rows, :])

    def token_pair(i, carry):
        t = 2 * i
        reduce_lanes(t - 2, 0)
        reduce_lanes(t - 1, 1)
        products(t, 0)
        products(t + 1, 1)
        return carry

    s_ref[...] = jnp.zeros_like(s_ref)
    part_ref[...] = jnp.zeros_like(part_ref)
    lax.fori_loop(0, tb // 2, token_pair, 0)
    reduce_lanes(tb - 2, 0)
    reduce_lanes(tb - 1, 1)

    odd_t = (lax.broadcasted_iota(I32, (SUBLANES, tb), 0) & 1) != 0
    _gate_rows(lambda grp: _merge(s_ref[2 * grp * SUBLANES:(2 * grp + 1) * SUBLANES, :],
                                  s_ref[(2 * grp + 1) * SUBLANES:(2 * grp + 2) * SUBLANES, :], 1, odd_t),
               g_ref, par_ref, a_ref)


def _gate_rows(dots_of_group, g_ref, par_ref, a_ref):
    tb = g_ref.shape[2]
    low_t = lax.broadcasted_iota(I32, (SUBLANES, tb), 0) < GROUP
    want = jnp.where(low_t, 0, 1)
    for v in range(N_SLOTS // SUBLANES):
        src = slice(v * SUBLANES, (v + 1) * SUBLANES)
        gates = _dup_halves(g_ref[0, src, :], low_t)
        pars = _dup_halves(par_ref[0, src, :], low_t)
        for half in range(2):
            grp = 2 * v + half
            act = gates[half] * jax.nn.gelu(dots_of_group(grp))
            rows = slice(grp * SUBLANES, (grp + 1) * SUBLANES)
            a_ref[0, rows, :] = jnp.where(pars[half] == want, act, 0.0)


def _sc_token_loop(tab_hbm, e_hbm, idx_v, rows_v, sems, per_worker, begin_token, chunk_rows, end_token):
    wid = lax.axis_index("s") * SC_CORES + lax.axis_index("c")
    n_chunks = N_SLOTS // SC_GATHER_ROWS
    first = wid * per_worker
    last = first + per_worker - 1

    def gather(slot, chunk):
        ids = idx_v.at[slot, pl.ds(chunk * SC_GATHER_ROWS, SC_GATHER_ROWS)]
        return pltpu.make_async_copy(tab_hbm.at[ids], rows_v.at[chunk % 2], sems.at[chunk % 2])

    def one_token(tok, slot, next_tok):
        pltpu.sync_copy(e_hbm.at[next_tok], idx_v.at[1 - slot])
        begin_token(tok)
        for chunk in range(n_chunks):
            if chunk + 1 < n_chunks:
                gather(slot, chunk + 1).start()
            else:
                gather(1 - slot, 0).start()
            gather(slot, chunk).wait()
            chunk_rows(chunk, rows_v.at[chunk % 2])
        end_token(tok)

    def token_pair(k, carry):
        tok = first + 2 * k
        one_token(tok, 0, tok + 1)
        one_token(tok + 1, 1, jnp.minimum(tok + 2, last))
        return carry

    pltpu.sync_copy(e_hbm.at[first], idx_v.at[0])
    gather(0, 0).start()
    lax.fori_loop(0, per_worker // 2, token_pair, 0)
    gather(0, 0).wait()


def _sc_scratch(row_len, dtype):
    return [pltpu.VMEM((2, N_SLOTS), I32), pltpu.VMEM((2, SC_GATHER_ROWS, row_len), dtype),
            pltpu.SemaphoreType.DMA((2,))]


def _sc_u_dots(u_words, e_tok, x_words, first_token):
    nw = x_words.shape[1]
    n_tokens = e_tok.shape[0]
    per_worker = n_tokens // (SC_CORES * SC_SUBCORES)
    n_half = nw // SC_LANES // 2
    fold = 4
    mesh = plsc.VectorSubcoreMesh(core_axis_name="c", subcore_axis_name="s")

    def body(u_hbm, e_hbm, x_hbm, o_hbm, idx_v, rows_v, sems, x_v, out_v):
        lane = lax.iota(I32, SC_LANES)

        def begin_token(tok):
            pltpu.sync_copy(x_hbm.at[first_token + tok], x_v)

        def chunk_rows(chunk, rows):
            for part in range(SC_GATHER_ROWS // SC_LANES):
                total = jnp.zeros((SC_LANES,), F32)
                for half in range(2):
                    base = half * n_half * SC_LANES
                    xs = [plsc.bitcast(x_v[pl.ds(base + c * SC_LANES, SC_LANES)], BF16) for c in range(n_half)]

                    def row(r, vec):
                        rr = part * SC_LANES + r
                        acc = jnp.zeros((SC_LANES,), F32)
                        for g in range(n_half // fold):
                            p = [plsc.bitcast(rows[rr, pl.ds(base + (fold * g + k) * SC_LANES, SC_LANES)], BF16)
                                 * xs[fold * g + k] for k in range(fold)]
                            a, b = plsc.unpack((p[0] + p[1]) + (p[2] + p[3]), format=plsc.PackFormat.INTERLEAVED,
                                               preferred_element_type=F32)
                            acc = acc + a + b
                        return jnp.where(lane == r, jnp.sum(acc), vec)

                    total = total + lax.fori_loop(0, SC_LANES, row, jnp.zeros((SC_LANES,), F32))
                out_v[pl.ds(chunk * SC_GATHER_ROWS + part * SC_LANES, SC_LANES)] = total

        def end_token(tok):
            pltpu.sync_copy(out_v, o_hbm.at[tok])

        _sc_token_loop(u_hbm, e_hbm, idx_v, rows_v, sems, per_worker, begin_token, chunk_rows, end_token)

    return pl.kernel(
        body,
        out_type=jax.ShapeDtypeStruct((n_tokens, N_SLOTS), F32),
        mesh=mesh,
        scratch_types=_sc_scratch(nw, U32) + [pltpu.VMEM((nw,), U32), pltpu.VMEM((N_SLOTS,), F32)],
        compiler_params=pltpu.CompilerParams(needs_layout_passes=False),
        name="peer_expert_in_sc",
    )(u_words, e_tok, x_words)


def _peer_gate_kernel(s_ref, g_ref, par_ref, a_ref):
    dots = s_ref[...].T
    low_t = lax.broadcasted_iota(I32, (SUBLANES, dots.shape[1]), 0) < GROUP
    _gate_rows(lambda grp: _dup_halves(dots[(grp // 2) * SUBLANES:(grp // 2 + 1) * SUBLANES, :], low_t)[grp % 2],
               g_ref, par_ref, a_ref)


def _peer_weights_kernel(a_ref, w_ref):
    low_t = lax.broadcasted_iota(I32, (SUBLANES, a_ref.shape[2]), 0) < GROUP
    tiles = []
    for v in range(N_SLOTS // SUBLANES):
        first = a_ref[0, 2 * v * SUBLANES:(2 * v + 1) * SUBLANES, :]
        second = a_ref[0, (2 * v + 1) * SUBLANES:(2 * v + 2) * SUBLANES, :]
        first = first + pltpu.roll(first, GROUP, 0)
        second = second + pltpu.roll(second, GROUP, 0)
        tiles.append(jnp.where(low_t, first, second))
    w_ref[...] = jnp.concatenate(tiles, axis=0).T


def _sc_v_sums(v_rows, e_tok, w_tok, x2d, first_token):
    d = x2d.shape[1]
    n_tokens = e_tok.shape[0]
    per_worker = n_tokens // (SC_CORES * SC_SUBCORES)
    n_acc = d // SC_LANES // 2
    mesh = plsc.VectorSubcoreMesh(core_axis_name="c", subcore_axis_name="s")

    def body(v_hbm, e_hbm, w_hbm, x_hbm, o_hbm, idx_v, rows_v, sems, w_v, out_v):
        def begin_token(tok):
            pltpu.sync_copy(w_hbm.at[tok], w_v)
            pltpu.sync_copy(x_hbm.at[first_token + tok], out_v)

        def chunk_rows(chunk, rows):
            for half in range(2):
                base = half * n_acc * SC_LANES

                def row(r, acc):
                    slot = jnp.full((SC_LANES,), chunk * SC_GATHER_ROWS + r, I32)
                    w = plsc.load_gather(w_v, [slot])
                    return tuple(acc[c] + w * rows[r, pl.ds(base + c * SC_LANES, SC_LANES)]
                                 for c in range(n_acc))

                acc0 = tuple(out_v[pl.ds(base + c * SC_LANES, SC_LANES)] for c in range(n_acc))
                acc = lax.fori_loop(0, SC_GATHER_ROWS, row, acc0)
                for c in range(n_acc):
                    out_v[pl.ds(base + c * SC_LANES, SC_LANES)] = acc[c]

        def end_token(tok):
            pltpu.sync_copy(out_v, o_hbm.at[tok])

        _sc_token_loop(v_hbm, e_hbm, idx_v, rows_v, sems, per_worker, begin_token, chunk_rows, end_token)

    return pl.kernel(
        body,
        out_type=jax.ShapeDtypeStruct((n_tokens, d), F32),
        mesh=mesh,
        scratch_types=_sc_scratch(d, F32) + [pltpu.VMEM((N_SLOTS,), F32), pltpu.VMEM((d,), F32)],
        compiler_params=pltpu.CompilerParams(needs_layout_passes=False),
        name="peer_expert_out_sc",
    )(v_rows, e_tok, w_tok, x2d)


def _token_tiles(ref, xt_ref, to_tiles):
    tb, d = ref.shape
    for q in range(tb // SUBLANES):
        for c in range(d // LANES):
            rows = slice(q * SUBLANES, (q + 1) * SUBLANES)
            cols = slice(c * LANES, (c + 1) * LANES)
            strided = pl.ds(q * SUBLANES * SUBLANES + c, SUBLANES, stride=SUBLANES)
            if to_tiles:
                xt_ref[strided, :] = ref[rows, cols]
            else:
                ref[rows, cols] = xt_ref[strided, :]


def _peer_v_kernel(r8_ref, a_ref, x_ref, tab_ref, o_ref, bc_ref, xt_ref):
    tb = x_ref.shape[0]
    sub = _sublane_iota()
    low = sub < GROUP
    lane_t = lax.broadcasted_iota(I32, (SUBLANES, tb), 1)
    _token_tiles(x_ref, xt_ref, True)

    def spread(t, buf):
        for k in range(N_ROWS // SUBLANES):
            rows = slice(k * SUBLANES, (k + 1) * SUBLANES)
            col = jnp.sum(jnp.where(lane_t == t, a_ref[0, rows, :], 0.0), axis=1, keepdims=True)
            bits = lax.bitcast_convert_type(col.astype(BF16).astype(F32), U32)
            bc_ref[buf, rows, :] = jnp.broadcast_to(bits | (bits >> 16), (SUBLANES, LANES))

    def accumulate(t, buf):
        acc_lo = jnp.zeros((SUBLANES, LANES), F32)
        acc_hi = jnp.zeros((SUBLANES, LANES), F32)
        ids = r8_ref.at[0, t]
        for grp in range(N_GROUPS):
            prods = []
            for i in range(GROUP):
                row = grp * SUBLANES + i
                r8 = pl.multiple_of(ids[grp * GROUP + i], SUBLANES)
                am = jnp.where(low, bc_ref[buf, row:row + 1, :], bc_ref[buf, row + GROUP:row + GROUP + 1, :])
                prods.append(_as_bf16(tab_ref[pl.ds(r8, SUBLANES), :]) * _as_bf16(am))
            lo, hi = _unpack_words(_as_words((prods[0] + prods[1]) + (prods[2] + prods[3])))
            acc_lo = acc_lo + lo
            acc_hi = acc_hi + hi
        acc_lo = acc_lo + pltpu.roll(acc_lo, GROUP, 0)
        acc_hi = acc_hi + pltpu.roll(acc_hi, GROUP, 0)
        tile = pl.ds(pl.multiple_of(t * SUBLANES, SUBLANES), SUBLANES)
        xt_ref[tile, :] = xt_ref[tile, :] + jnp.where(low, acc_lo, acc_hi)

    def token_group(i, carry):
        t = N_SPREAD * i
        for k in range(N_SPREAD):
            spread(jnp.minimum(t + k + SPREAD_AHEAD, tb - 1), (k + SPREAD_AHEAD) % N_SPREAD)
            accumulate(t + k, k)
        return carry

    for k in range(SPREAD_AHEAD):
        spread(k, k)
    lax.fori_loop(0, tb // N_SPREAD, token_group, 0)
    _token_tiles(o_ref, xt_ref, False)


def _peer_experts(x2d, xp, r8, e_tok, par, gates, dots_sc, v_rows, u_tab, v_tab):
    t, d = x2d.shape
    nblk = t // PEER_TOKENS
    nblk_tc = nblk - SC_TOKEN_BLOCKS
    smem_block = pl.BlockSpec((1, PEER_TOKENS, N_SLOTS), lambda i: (i, 0, 0), memory_space=pltpu.SMEM)
    slot_block = pl.BlockSpec((1, N_SLOTS, PEER_TOKENS), lambda i: (i, 0, 0))
    row_block = pl.BlockSpec((1, N_ROWS, PEER_TOKENS), lambda i: (i, 0, 0))
    tile_block = pl.BlockSpec((PEER_TOKENS * SUBLANES, LANES), lambda i: (i, 0))
    tok_block = pl.BlockSpec((PEER_TOKENS, d), lambda i: (i, 0))
    whole = pl.BlockSpec(memory_space=pltpu.VMEM)
    params = pltpu.CompilerParams(dimension_semantics=("parallel",), vmem_limit_bytes=56 * MIB)

    act_tc = pl.pallas_call(
        _peer_u_kernel,
        grid=(nblk_tc,),
        in_specs=[smem_block, tile_block, slot_block, slot_block, whole],
        out_specs=row_block,
        out_shape=jax.ShapeDtypeStruct((nblk_tc, N_ROWS, PEER_TOKENS), F32),
        scratch_shapes=[pltpu.VMEM((2 * N_ROWS, PEER_TOKENS), F32),
                        pltpu.VMEM((2, 2 * N_GROUPS, SUBLANES, LANES), F32)],
        compiler_params=params,
        name="peer_expert_in",
    )(r8, xp, gates, par, u_tab)
    shifted = pl.BlockSpec((1, N_SLOTS, PEER_TOKENS), lambda i: (i + nblk_tc, 0, 0))
    act_sc = pl.pallas_call(
        _peer_gate_kernel,
        grid=(SC_TOKEN_BLOCKS,),
        in_specs=[pl.BlockSpec((PEER_TOKENS, N_SLOTS), lambda i: (i, 0)), shifted, shifted],
        out_specs=row_block,
        out_shape=jax.ShapeDtypeStruct((SC_TOKEN_BLOCKS, N_ROWS, PEER_TOKENS), F32),
        compiler_params=pltpu.CompilerParams(dimension_semantics=("parallel",)),
        name="peer_gate_sc",
    )(dots_sc, gates, par)
    act = jnp.concatenate([act_tc, act_sc], axis=0)

    nblk_v = nblk - SC_V_BLOCKS
    w_sc = pl.pallas_call(
        _peer_weights_kernel,
        grid=(SC_V_BLOCKS,),
        in_specs=[pl.BlockSpec((1, N_ROWS, PEER_TOKENS), lambda i: (i + nblk_v, 0, 0))],
        out_specs=pl.BlockSpec((PEER_TOKENS, N_SLOTS), lambda i: (i, 0)),
        out_shape=jax.ShapeDtypeStruct((SC_V_BLOCKS * PEER_TOKENS, N_SLOTS), F32),
        compiler_params=pltpu.CompilerParams(dimension_semantics=("parallel",)),
        name="peer_weights_sc",
    )(act)
    y_sc = _sc_v_sums(v_rows, e_tok[nblk_v:].reshape(-1, N_SLOTS), w_sc, x2d, nblk_v * PEER_TOKENS)
    y_tc = pl.pallas_call(
        _peer_v_kernel,
        grid=(nblk_v,),
        in_specs=[smem_block, row_block, tok_block, whole],
        out_specs=tok_block,
        out_shape=jax.ShapeDtypeStruct((nblk_v * PEER_TOKENS, d), F32),
        scratch_shapes=[pltpu.VMEM((N_SPREAD, N_ROWS, LANES), U32),
                        pltpu.VMEM((PEER_TOKENS * SUBLANES, LANES), F32)],
        compiler_params=params,
        name="peer_expert_out",
    )(r8, act, x2d, v_tab)
    return jnp.concatenate([y_tc, y_sc], axis=0)


def _peer_ffn(x2d, norm_g, w_q, sub_keys, u_tabs, v_tabs, layer):
    n_experts, d = u_tabs.shape[1:]
    xp, scores_t, x_words = _peer_scores(x2d, norm_g, w_q, sub_keys)
    u_tiles, u_words = _pack_table(u_tabs, layer)
    v_tiles, _ = _pack_table(v_tabs, layer)
    nblk = x2d.shape[0] // PEER_TOKENS
    nblk_tc = nblk - SC_TOKEN_BLOCKS
    top_sc = _peer_topk(scores_t, nblk_tc, SC_TOKEN_BLOCKS)
    dots_sc = _sc_u_dots(u_words, top_sc[3].reshape(-1, N_SLOTS), x_words, nblk_tc * PEER_TOKENS)
    top_tc = _peer_topk(scores_t, 0, nblk_tc)
    r8, par, gates, e_tok = [jnp.concatenate([a, b], axis=0) for a, b in zip(top_tc, top_sc)]
    return _peer_experts(x2d, xp, r8, e_tok + layer * n_experts, par, gates, dots_sc, v_tabs.reshape(-1, d),
                         u_tiles, v_tiles)


def kernel(x, attn_norm_g, attn_w_qkv, attn_q_g, attn_k_g, attn_rel_bias, attn_w_out, rec_norm_g, rec_w_in, rec_conv_w, rec_conv_b, rec_w_a, rec_b_a, rec_w_x, rec_b_x, rec_lambda, rec_w_out, ffn_norm_g, peer_w_q, peer_sub_keys, peer_u, peer_v):
    b, s, d = x.shape
    depth = ffn_norm_g.shape[0]
    for layer in range(depth):
        j = layer // 2
        if layer % 2 == 0:
            qkv = _qkv_proj(x.reshape(b * s, d), attn_norm_g[j], attn_w_qkv[j], attn_q_g[j], attn_k_g[j])
            x = _attention(x, qkv, attn_rel_bias[j], attn_w_out[j])
        else:
            x = _rglru(x, rec_norm_g[j], rec_w_in[j], rec_conv_w[j], rec_conv_b[j], rec_w_a[j],
                       rec_b_a[j], rec_w_x[j], rec_b_x[j], rec_lambda[j], rec_w_out[j])
        x = _peer_ffn(x.reshape(b * s, d), ffn_norm_g[layer], peer_w_q[layer], peer_sub_keys[layer],
                      peer_u, peer_v, layer).reshape(b, s, d)
    return x
```

```python
import numpy as np
import jax
import jax.numpy as jnp
from jax import lax
from jax.experimental import pallas as pl
from jax.experimental.pallas import tpu as pltpu
from jax.experimental.pallas import tpu_sc as plsc

F32 = jnp.float32
BF16 = jnp.bfloat16
U32 = jnp.uint32
I32 = jnp.int32

EPS = 1e-6
NEG_INF = -1e30

D_MODEL = 1024
CHUNK = 64
ATTN_HEADS = 16
HEAD_DIM = 64
LEFT_CHUNKS = 8
REL_CLIP = 256
LRU_HEADS = 4
LRU_BLOCK = 256
CONV_WIDTH = 4
LRU_C = 8.0
PEER_HEADS = 8
N_KEYS = 128
PEER_TOPK = 16
D_HALF = 128

SUBLANES = 8
LANES = 128

Q_TILE = 4 * CHUNK
N_KEY_BLOCKS = 3
ROW_TILE = 512
PEER_TOKENS = 128
LRU_TILE = 256
N_SLOTS = PEER_HEADS * PEER_TOPK

MIB = 1024 * 1024

SC_CORES = 2
SC_SUBCORES = 16
SC_LANES = 16
SC_GATHER_ROWS = 32
SC_TOKEN_BLOCKS = 138
SC_V_BLOCKS = 90


def _rms_rows(x, g):
    ms = jnp.mean(x * x, axis=-1, keepdims=True)
    return (x * lax.rsqrt(ms + EPS)) * g


def _split_bf16(v):
    hi = v.astype(BF16)
    lo = (v - hi.astype(F32)).astype(BF16)
    return hi, lo


def _dot(a, b):
    return jnp.dot(a, b, preferred_element_type=F32)


def _dot_nt(a, b):
    return lax.dot_general(a, b, (((1,), (1,)), ((), ())), preferred_element_type=F32)


def _qkv_kernel(x_ref, g_ref, w_ref, gain_ref, bsel_ref, bexp_ref, o_ref, xn_ref):
    j = pl.program_id(1)

    @pl.when(j == 0)
    def _():
        xn_ref[...] = _rms_rows(x_ref[...], g_ref[...]).astype(BF16)

    y = _dot(xn_ref[...], w_ref[...])

    @pl.when(j < 2)
    def _():
        hi, lo = _split_bf16(y * y)
        ms = _dot(hi, bsel_ref[...]) + _dot(lo, bsel_ref[...])
        rhi, rlo = _split_bf16(lax.rsqrt(ms + EPS))
        rs = _dot(rhi, bexp_ref[...]) + _dot(rlo, bexp_ref[...])
        o_ref[...] = ((y * rs) * gain_ref[0]).astype(BF16)

    @pl.when(j == 2)
    def _():
        o_ref[...] = y.astype(BF16)


def _qkv_proj(x2d, norm_g, w_qkv, q_g, k_g):
    t, d = x2d.shape
    scale = HEAD_DIM ** -0.5
    gains = jnp.stack([jnp.tile(q_g, ATTN_HEADS) * scale, jnp.tile(k_g, ATTN_HEADS),
                       jnp.ones((d,), F32)]).reshape(3, 1, d)
    head_of_col = np.arange(d) // HEAD_DIM
    bsel = (head_of_col[:, None] == np.arange(LANES)[None, :]).astype(np.float32) / HEAD_DIM
    bexp = (np.arange(LANES)[:, None] == head_of_col[None, :]).astype(np.float32)
    return pl.pallas_call(
        _qkv_kernel,
        grid=(t // ROW_TILE, 3),
        in_specs=[
            pl.BlockSpec((ROW_TILE, d), lambda i, j: (i, 0)),
            pl.BlockSpec((1, d), lambda i, j: (0, 0)),
            pl.BlockSpec((d, d), lambda i, j: (0, j)),
            pl.BlockSpec((1, 1, d), lambda i, j: (j, 0, 0)),
            pl.BlockSpec((d, LANES), lambda i, j: (0, 0)),
            pl.BlockSpec((LANES, d), lambda i, j: (0, 0)),
        ],
        out_specs=pl.BlockSpec((ROW_TILE, d), lambda i, j: (i, j)),
        out_shape=jax.ShapeDtypeStruct((t, 3 * d), BF16),
        scratch_shapes=[pltpu.VMEM((ROW_TILE, d), BF16)],
        compiler_params=pltpu.CompilerParams(
            dimension_semantics=("parallel", "arbitrary"), vmem_limit_bytes=40 * MIB),
        name="qkv_proj",
    )(x2d, norm_g.reshape(1, d), w_qkv.astype(BF16), gains,
      jnp.asarray(bsel, BF16), jnp.asarray(bexp, BF16))


def _attn_kernel(x_ref, q_ref, k0_ref, k1_ref, k2_ref, v0_ref, v1_ref, v2_ref,
                 bias_ref, wout_ref, o_ref, att_ref):
    qt = pl.program_id(1)
    k_refs = (k0_ref, k1_ref, k2_ref)
    v_refs = (v0_ref, v1_ref, v2_ref)
    negs = [jnp.where(qt - (N_KEY_BLOCKS - 1) + kb < 0, NEG_INF, 0.0).astype(F32)
            for kb in range(N_KEY_BLOCKS)]
    lane = lax.broadcasted_iota(I32, (Q_TILE, LANES), 1)
    first_half = lane < HEAD_DIM
    for p in range(ATTN_HEADS // 2):
        cols = slice(p * LANES, (p + 1) * LANES)
        qp = q_ref[0, :, cols]
        outs = []
        for hh in range(2):
            h = 2 * p + hh
            keep = first_half if hh == 0 else jnp.logical_not(first_half)
            qm = jnp.where(keep, qp, jnp.zeros_like(qp))
            s = [_dot_nt(qm, k_refs[kb][0, :, cols]) + bias_ref[kb, h] + negs[kb]
                 for kb in range(N_KEY_BLOCKS)]
            m = jnp.max(s[0], axis=-1, keepdims=True)
            for kb in range(1, N_KEY_BLOCKS):
                m = jnp.maximum(m, jnp.max(s[kb], axis=-1, keepdims=True))
            acc = jnp.zeros((Q_TILE, LANES), F32)
            l = jnp.zeros((Q_TILE, 1), F32)
            for kb in range(N_KEY_BLOCKS):
                e = jnp.exp(s[kb] - m)
                l = l + jnp.sum(e, axis=-1, keepdims=True)
                acc = acc + _dot(e.astype(BF16), v_refs[kb][0, :, cols])
            outs.append(acc / l)
        att_ref[:, cols] = jnp.where(first_half, outs[0], outs[1]).astype(BF16)
    o_ref[0] = x_ref[0] + _dot(att_ref[...], wout_ref[...])


def _band_bias(table):
    qi = np.arange(Q_TILE)[:, None]
    kj = np.arange(Q_TILE)[None, :]
    period = 2 * Q_TILE
    m = np.arange(period)
    delta = np.where(m < Q_TILE, m, m - period)
    blocks = []
    for kb in range(N_KEY_BLOCKS):
        dist = (N_KEY_BLOCKS - 1 - kb) * Q_TILE - delta
        diag = table[:, np.clip(dist, -REL_CLIP, REL_CLIP) + REL_CLIP]
        toep = jnp.tile(diag, (1, Q_TILE))[:, :Q_TILE * (period - 1)]
        toep = toep.reshape(-1, Q_TILE, period - 1)[:, :, :Q_TILE]
        dchunk = (Q_TILE // CHUNK) * (kb - (N_KEY_BLOCKS - 1)) + kj // CHUNK - qi // CHUNK
        valid = (dchunk >= -LEFT_CHUNKS) & (dchunk <= 0)
        blocks.append(jnp.where(valid[None], toep, NEG_INF))
    return jnp.stack(blocks).astype(F32)


def _attention(x, qkv, rel_bias_table, w_out):
    b, s, d = x.shape
    nq = s // Q_TILE
    qkv3 = qkv.reshape(b, s, 3 * d)
    bias = _band_bias(rel_bias_table)

    def kv_spec(kb, col):
        return pl.BlockSpec(
            (1, Q_TILE, d),
            lambda bi, qt: (bi, jnp.maximum(qt - (N_KEY_BLOCKS - 1) + kb, 0), col))

    in_specs = [pl.BlockSpec((1, Q_TILE, d), lambda bi, qt: (bi, qt, 0)),
                pl.BlockSpec((1, Q_TILE, d), lambda bi, qt: (bi, qt, 0))]
    in_specs += [kv_spec(kb, 1) for kb in range(N_KEY_BLOCKS)]
    in_specs += [kv_spec(kb, 2) for kb in range(N_KEY_BLOCKS)]
    in_specs += [pl.BlockSpec(memory_space=pltpu.VMEM), pl.BlockSpec(memory_space=pltpu.VMEM)]
    return pl.pallas_call(
        _attn_kernel,
        grid=(b, nq),
        in_specs=in_specs,
        out_specs=pl.BlockSpec((1, Q_TILE, d), lambda bi, qt: (bi, qt, 0)),
        out_shape=jax.ShapeDtypeStruct((b, s, d), F32),
        scratch_shapes=[pltpu.VMEM((Q_TILE, d), BF16)],
        compiler_params=pltpu.CompilerParams(
            dimension_semantics=("parallel", "parallel"), vmem_limit_bytes=48 * MIB),
        name="band_attention",
    )(x, qkv3, qkv3, qkv3, qkv3, qkv3, qkv3, qkv3, bias, w_out.astype(BF16))


def _rglru_kernel(x_ref, g_ref, win_ref, cw_ref, cb_ref, wa_ref, ba_ref, wx_ref, bx_ref,
                  lam_ref, wout_ref, o_ref, ext_ref, h_ref, a_scr, b_scr, hs_scr):
    ts = x_ref.shape[1]
    w = LRU_HEADS * LRU_BLOCK

    @pl.when(pl.program_id(1) == 0)
    def _():
        ext_ref[...] = jnp.zeros_like(ext_ref)
        h_ref[...] = jnp.zeros_like(h_ref)

    x = x_ref[0]
    xn = _rms_rows(x, g_ref[...]).astype(BF16)
    gu = _dot(xn, win_ref[...])
    gate = gu[:, :w]
    u_raw = gu[:, w:]
    ext_ref[0:SUBLANES, :] = ext_ref[ts:ts + SUBLANES, :]
    ext_ref[SUBLANES:, :] = u_raw
    u = cb_ref[...] + jnp.zeros((ts, w), F32)
    for k in range(CONV_WIDTH):
        off = SUBLANES - (CONV_WIDTH - 1) + k
        u = u + cw_ref[k:k + 1, :] * ext_ref[off:off + ts, :]
    ub = u.astype(BF16)

    def block_diag(wref):
        return jnp.concatenate(
            [_dot(ub[:, hh * LRU_BLOCK:(hh + 1) * LRU_BLOCK], wref[hh]) for hh in range(LRU_HEADS)],
            axis=1)

    r = jax.nn.sigmoid(block_diag(wa_ref) + ba_ref[...])
    i = jax.nn.sigmoid(block_diag(wx_ref) + bx_ref[...])
    z = -lam_ref[...]
    softplus = jnp.maximum(z, 0.0) + jnp.log1p(jnp.exp(-jnp.abs(z)))
    log_a = (-LRU_C) * r * softplus
    a_scr[...] = jnp.exp(log_a)
    th = jnp.tanh(log_a)
    b_scr[...] = jnp.sqrt(2.0 * th / (th - 1.0)) * (i * u)

    def step(t, h):
        h = a_scr[pl.ds(t, 1), :] * h + b_scr[pl.ds(t, 1), :]
        hs_scr[pl.ds(t, 1), :] = h
        return h

    h_ref[0:1, :] = lax.fori_loop(0, ts, step, h_ref[0:1, :], unroll=8)
    y = jax.nn.gelu(gate) * hs_scr[...]
    o_ref[0] = x + _dot(y.astype(BF16), wout_ref[...])


def _rglru(x, norm_g, w_in, conv_w, conv_b, w_a, b_a, w_x, b_x, lam, w_out):
    b, s, d = x.shape
    w = LRU_HEADS * LRU_BLOCK
    row = lambda v: v.reshape(1, -1)
    whole = pl.BlockSpec(memory_space=pltpu.VMEM)
    return pl.pallas_call(
        _rglru_kernel,
        grid=(b, s // LRU_TILE),
        in_specs=[pl.BlockSpec((1, LRU_TILE, d), lambda bi, si: (bi, si, 0))] + [whole] * 10,
        out_specs=pl.BlockSpec((1, LRU_TILE, d), lambda bi, si: (bi, si, 0)),
        out_shape=jax.ShapeDtypeStruct((b, s, d), F32),
        scratch_shapes=[pltpu.VMEM((LRU_TILE + SUBLANES, w), F32), pltpu.VMEM((SUBLANES, w), F32),
                        pltpu.VMEM((LRU_TILE, w), F32), pltpu.VMEM((LRU_TILE, w), F32),
                        pltpu.VMEM((LRU_TILE, w), F32)],
        compiler_params=pltpu.CompilerParams(
            dimension_semantics=("arbitrary", "arbitrary"), vmem_limit_bytes=48 * MIB),
        name="rglru_mixer",
    )(x, row(norm_g), w_in.astype(BF16), conv_w, row(conv_b), w_a.astype(BF16), row(b_a),
      w_x.astype(BF16), row(b_x), row(lam), w_out.astype(BF16))


def _peer_scores_kernel(x_ref, g_ref, wq_ref, sk_ref, xp_ref, st_ref, xw_ref):
    xb = _rms_rows(x_ref[...], g_ref[...]).astype(BF16)
    words = _bf16_pair_words(xb)
    xw_ref[...] = words
    tm = words.shape[0]
    for c in range(SUBLANES):
        piece = words[:, (c % (SUBLANES // 2)) * LANES:(c % (SUBLANES // 2) + 1) * LANES]
        xp_ref[pl.ds(c, tm, stride=SUBLANES), :] = piece
    q = _dot(xb, wq_ref[...]).astype(BF16)
    for hp in range(2 * PEER_HEADS):
        st_ref[hp] = _dot_nt(sk_ref[hp % 2], q[:, hp * D_HALF:(hp + 1) * D_HALF])


def _peer_scores(x2d, norm_g, w_q, sub_keys):
    t, d = x2d.shape
    nq = w_q.shape[1]
    whole = pl.BlockSpec(memory_space=pltpu.VMEM)
    return pl.pallas_call(
        _peer_scores_kernel,
        grid=(t // ROW_TILE,),
        in_specs=[pl.BlockSpec((ROW_TILE, d), lambda i: (i, 0)), whole, whole, whole],
        out_specs=[pl.BlockSpec((ROW_TILE * SUBLANES, LANES), lambda i: (i, 0)),
                   pl.BlockSpec((2 * PEER_HEADS, N_KEYS, ROW_TILE), lambda i: (0, 0, i)),
                   pl.BlockSpec((ROW_TILE, d // 2), lambda i: (i, 0))],
        out_shape=[jax.ShapeDtypeStruct((t * SUBLANES, LANES), U32),
                   jax.ShapeDtypeStruct((2 * PEER_HEADS, N_KEYS, t), F32),
                   jax.ShapeDtypeStruct((t, d // 2), U32)],
        compiler_params=pltpu.CompilerParams(
            dimension_semantics=("parallel",), vmem_limit_bytes=48 * MIB),
        name="peer_scores",
    )(x2d, norm_g.reshape(1, d), w_q.astype(BF16), sub_keys.astype(BF16))


def _candidate_blocks():
    blocks = []
    for i in range(PEER_TOPK // 2):
        nj = PEER_TOPK // (i + 1)
        for j0 in range(0, nj, SUBLANES):
            blocks.append(("row", i, j0, min(SUBLANES, nj - j0)))
    blocks.append(("col", PEER_TOPK // 2, PEER_TOPK // 2))
    return blocks


def _peer_topk_kernel(st_ref, r8_ref, par_ref, g_ref, e_ref, r8t_ref, et_ref):
    tk = st_ref.shape[2]
    key_id = lax.broadcasted_iota(I32, (N_KEYS, tk), 0).astype(F32)
    row16 = lax.broadcasted_iota(I32, (PEER_TOPK, tk), 0)
    row8 = lax.broadcasted_iota(I32, (SUBLANES, tk), 0)
    row8_f = row8.astype(F32)
    neg_inf = jnp.float32(-jnp.inf)
    blocks = _candidate_blocks()
    big = jnp.float32(PEER_TOPK * PEER_TOPK)

    def top16(x):
        vals = jnp.zeros((PEER_TOPK, tk), F32)
        idxs = jnp.zeros((PEER_TOPK, tk), F32)
        for k in range(PEER_TOPK):
            m = jnp.max(x, axis=0, keepdims=True)
            idx = jnp.min(jnp.where(x == m, key_id, float(N_KEYS)), axis=0, keepdims=True)
            x = jnp.where(key_id == idx, neg_inf, x)
            vals = jnp.where(row16 == k, m, vals)
            idxs = jnp.where(row16 == k, idx, idxs)
        return vals, idxs

    for h in range(PEER_HEADS):
        s0, i0 = top16(st_ref[2 * h])
        s1, i1 = top16(st_ref[2 * h + 1])
        cand, flat, expert = [], [], []
        for blk in blocks:
            if blk[0] == "row":
                _, i, j0, n = blk
                c = s0[i:i + 1, :] + s1[j0:j0 + SUBLANES, :]
                f = float(PEER_TOPK * i + j0) + row8_f
                e = i0[i:i + 1, :] * float(N_KEYS) + i1[j0:j0 + SUBLANES, :]
            else:
                _, i_start, n = blk
                c = s0[i_start:i_start + SUBLANES, :] + s1[0:1, :]
                f = float(PEER_TOPK) * (float(i_start) + row8_f)
                e = i0[i_start:i_start + SUBLANES, :] * float(N_KEYS) + i1[0:1, :]
            cand.append(jnp.where(row8 < n, c, neg_inf))
            flat.append(f)
            expert.append(e)
        cand = jnp.concatenate(cand, axis=0)
        flat = jnp.concatenate(flat, axis=0)
        expert = jnp.concatenate(expert, axis=0)
        best_s = jnp.zeros((PEER_TOPK, tk), F32)
        best_e = jnp.zeros((PEER_TOPK, tk), F32)
        for k in range(PEER_TOPK):
            m = jnp.max(cand, axis=0, keepdims=True)
            fidx = jnp.min(jnp.where(cand == m, flat, big), axis=0, keepdims=True)
            pick = flat == fidx
            e_sel = jnp.max(jnp.where(pick, expert, -1.0), axis=0, keepdims=True)
            cand = jnp.where(pick, neg_inf, cand)
            best_s = jnp.where(row16 == k, m, best_s)
            best_e = jnp.where(row16 == k, e_sel, best_e)
        ex = jnp.exp(best_s - best_s[0:1, :])
        gate = ex / jnp.sum(ex, axis=0, keepdims=True)
        rows = slice(h * PEER_TOPK, (h + 1) * PEER_TOPK)
        best_i = best_e.astype(I32)
        r8t_ref[rows, :] = (best_i >> 1) * SUBLANES
        et_ref[rows, :] = best_i
        par_ref[0, rows, :] = best_i & 1
        g_ref[0, rows, :] = gate
    r8_ref[0] = r8t_ref[...].T
    e_ref[0] = et_ref[...].T


def _peer_topk(scores_t, first_block, nblk):
    slot_major = pl.BlockSpec((1, N_SLOTS, PEER_TOKENS), lambda i: (i, 0, 0))
    token_major = pl.BlockSpec((1, PEER_TOKENS, N_SLOTS), lambda i: (i, 0, 0))
    return pl.pallas_call(
        _peer_topk_kernel,
        grid=(nblk,),
        in_specs=[pl.BlockSpec((2 * PEER_HEADS, N_KEYS, PEER_TOKENS), lambda i: (0, 0, i + first_block))],
        out_specs=[token_major, slot_major, slot_major, token_major],
        out_shape=[jax.ShapeDtypeStruct((nblk, PEER_TOKENS, N_SLOTS), I32),
                   jax.ShapeDtypeStruct((nblk, N_SLOTS, PEER_TOKENS), I32),
                   jax.ShapeDtypeStruct((nblk, N_SLOTS, PEER_TOKENS), F32),
                   jax.ShapeDtypeStruct((nblk, PEER_TOKENS, N_SLOTS), I32)],
        scratch_shapes=[pltpu.VMEM((N_SLOTS, PEER_TOKENS), I32), pltpu.VMEM((N_SLOTS, PEER_TOKENS), I32)],
        compiler_params=pltpu.CompilerParams(dimension_semantics=("parallel",)),
        name="peer_topk",
    )(scores_t)


def _bf16_pair_words(x):
    bits = lax.bitcast_convert_type(x.astype(BF16).astype(F32), U32)
    half = bits.shape[1] // 2
    return (bits[:, :half] >> 16) | (bits[:, half:] & jnp.uint32(0xFFFF0000))


def _pack_kernel(t_ref, o_ref, w_ref):
    words = _bf16_pair_words(t_ref[0])
    w_ref[...] = words
    rows = words.shape[0]
    n = words.shape[1] // LANES
    for s in range(n):
        o_ref[pl.ds(s, rows, stride=n), :] = words[:, s * LANES:(s + 1) * LANES]


def _pack_table(tabs, layer):
    _, e, d = tabs.shape
    n = d // (2 * LANES)
    return pl.pallas_call(
        _pack_kernel,
        grid=(e // ROW_TILE,),
        in_specs=[pl.BlockSpec((1, ROW_TILE, d), lambda i: (layer, i, 0))],
        out_specs=[pl.BlockSpec((ROW_TILE * n, LANES), lambda i: (i, 0)),
                   pl.BlockSpec((ROW_TILE, d // 2), lambda i: (i, 0))],
        out_shape=[jax.ShapeDtypeStruct((e * n, LANES), U32), jax.ShapeDtypeStruct((e, d // 2), U32)],
        compiler_params=pltpu.CompilerParams(dimension_semantics=("parallel",)),
        name="peer_pack_table",
    )(tabs)


def _unpack_words(w):
    lo = lax.bitcast_convert_type(w << 16, F32)
    hi = lax.bitcast_convert_type(w & jnp.uint32(0xFFFF0000), F32)
    return lo, hi


def _as_bf16(words):
    return pltpu.bitcast(words, BF16)


def _as_words(packed):
    return pltpu.bitcast(packed, U32)


def _merge_packed(a, b, shift, mask):
    ta = a + _as_bf16(pltpu.roll(_as_words(a), shift, 0))
    tb = b + _as_bf16(pltpu.roll(_as_words(b), SUBLANES - shift, 0))
    return jnp.where(mask, _as_words(ta), _as_words(tb))


def _sublane_iota():
    return lax.broadcasted_iota(I32, (SUBLANES, LANES), 0)


def _merge(a, b, shift, mask):
    ta = a + pltpu.roll(a, shift, 0)
    tb = b + pltpu.roll(b, SUBLANES - shift, 0)
    return jnp.where(mask, ta, tb)


GROUP = SUBLANES // 2
N_GROUPS = N_SLOTS // GROUP
N_ROWS = 2 * N_SLOTS
N_SPREAD = 4
SPREAD_AHEAD = 2


def _dup_halves(v, low):
    r = pltpu.roll(v, GROUP, 0)
    return jnp.where(low, v, r), jnp.where(low, r, v)


def _peer_u_kernel(r8_ref, xp_ref, g_ref, par_ref, tab_ref, a_ref, s_ref, part_ref):
    tb = g_ref.shape[2]
    sub = _sublane_iota()
    m2 = (sub & 2) != 0
    lane_t = lax.broadcasted_iota(I32, (SUBLANES, tb), 1)
    n_part = 2 * N_GROUPS

    def products(t, slot):
        xb = _as_bf16(xp_ref[pl.ds(pl.multiple_of(t * SUBLANES, SUBLANES), SUBLANES), :])
        ids = r8_ref.at[0, t]
        for grp in range(N_GROUPS):
            prods = []
            for i in range(GROUP):
                r8 = pl.multiple_of(ids[grp * GROUP + i], SUBLANES)
                prods.append(_as_bf16(tab_ref[pl.ds(r8, SUBLANES), :]) * xb)
            halves = (_merge_packed(prods[3], prods[1], 2, m2), _merge_packed(prods[2], prods[0], 2, m2))
            for k in range(2):
                lo, hi = _unpack_words(halves[k])
                part_ref[slot, 2 * grp + k] = lo + hi

    def reduce_lanes(t, slot):
        for n in range(n_part):
            col = jnp.sum(part_ref[slot, n], axis=1, keepdims=True)
            rows = slice(n * SUBLANES, (n + 1) * SUBLANES)
            s_ref[rows, :] = jnp.where(lane_t == t, col, s_ref[rows, :])

    def token_pair(i, carry):
        t = 2 * i
        reduce_lanes(t - 2, 0)
        reduce_lanes(t - 1, 1)
        products(t, 0)
        products(t + 1, 1)
        return carry

    s_ref[...] = jnp.zeros_like(s_ref)
    part_ref[...] = jnp.zeros_like(part_ref)
    lax.fori_loop(0, tb // 2, token_pair, 0)
    reduce_lanes(tb - 2, 0)
    reduce_lanes(tb - 1, 1)

    odd_t = (lax.broadcasted_iota(I32, (SUBLANES, tb), 0) & 1) != 0
    _gate_rows(lambda grp: _merge(s_ref[2 * grp * SUBLANES:(2 * grp + 1) * SUBLANES, :],
                                  s_ref[(2 * grp + 1) * SUBLANES:(2 * grp + 2) * SUBLANES, :], 1, odd_t),
               g_ref, par_ref, a_ref)


def _gate_rows(dots_of_group, g_ref, par_ref, a_ref):
    tb = g_ref.shape[2]
    low_t = lax.broadcasted_iota(I32, (SUBLANES, tb), 0) < GROUP
    want = jnp.where(low_t, 0, 1)
    for v in range(N_SLOTS // SUBLANES):
        src = slice(v * SUBLANES, (v + 1) * SUBLANES)
        gates = _dup_halves(g_ref[0, src, :], low_t)
        pars = _dup_halves(par_ref[0, src, :], low_t)
        for half in range(2):
            grp = 2 * v + half
            act = gates[half] * jax.nn.gelu(dots_of_group(grp))
            rows = slice(grp * SUBLANES, (grp + 1) * SUBLANES)
            a_ref[0, rows, :] = jnp.where(pars[half] == want, act, 0.0)


SEM_IDS, SEM_IN, SEM_OUT, N_SEMS = 2, 3, 5, 7


def _sc_token_loop(tab_hbm, e_hbm, idx_v, rows_v, sems, per_worker, token_in, out_buf, out_row, chunk_rows):
    wid = lax.axis_index("s") * SC_CORES + lax.axis_index("c")
    n_chunks = N_SLOTS // SC_GATHER_ROWS
    first = wid * per_worker
    last = first + per_worker - 1

    def gather(slot, chunk):
        ids = idx_v.at[slot, pl.ds(chunk * SC_GATHER_ROWS, SC_GATHER_ROWS)]
        return pltpu.make_async_copy(tab_hbm.at[ids], rows_v.at[chunk % 2], sems.at[chunk % 2])

    def ids_copy(tok, slot):
        return pltpu.make_async_copy(e_hbm.at[tok], idx_v.at[slot], sems.at[SEM_IDS])

    def in_copies(tok, slot):
        return [pltpu.make_async_copy(row_of(tok), buf.at[slot], sems.at[SEM_IN + slot]) for row_of, buf in token_in]

    def out_copy(tok, slot):
        return pltpu.make_async_copy(out_buf.at[slot], out_row(tok), sems.at[SEM_OUT + slot])

    def one_token(k, tok, slot, next_tok):
        ids_copy(next_tok, 1 - slot).start()
        for c in in_copies(next_tok, 1 - slot):
            c.start()
        for c in in_copies(tok, slot):
            c.wait()

        @pl.when(k > 0)
        def _():
            out_copy(tok, slot).wait()

        for chunk in range(n_chunks):
            if chunk + 1 < n_chunks:
                gather(slot, chunk + 1).start()
            else:
                ids_copy(next_tok, 1 - slot).wait()
                gather(1 - slot, 0).start()
            gather(slot, chunk).wait()
            chunk_rows(chunk, rows_v.at[chunk % 2], slot)
        out_copy(tok, slot).start()

    def token_pair(k, carry):
        tok = first + 2 * k
        one_token(k, tok, 0, tok + 1)
        one_token(k, tok + 1, 1, jnp.minimum(tok + 2, last))
        return carry

    pltpu.sync_copy(e_hbm.at[first], idx_v.at[0])
    gather(0, 0).start()
    for c in in_copies(first, 0):
        c.start()
    lax.fori_loop(0, per_worker // 2, token_pair, 0)
    gather(0, 0).wait()
    for c in in_copies(last, 0):
        c.wait()
    out_copy(last, 0).wait()
    out_copy(last, 1).wait()


def _sc_scratch(row_len, dtype):
    return [pltpu.VMEM((2, N_SLOTS), I32), pltpu.VMEM((2, SC_GATHER_ROWS, row_len), dtype),
            pltpu.SemaphoreType.DMA((N_SEMS,))]


def _sc_u_dots(u_words, e_tok, x_words, first_token):
    nw = x_words.shape[1]
    n_tokens = e_tok.shape[0]
    per_worker = n_tokens // (SC_CORES * SC_SUBCORES)
    n_half = nw // SC_LANES // 2
    fold = 4
    mesh = plsc.VectorSubcoreMesh(core_axis_name="c", subcore_axis_name="s")

    def body(u_hbm, e_hbm, x_hbm, o_hbm, idx_v, rows_v, sems, x_v, out_v):
        lane = lax.iota(I32, SC_LANES)

        def chunk_rows(chunk, rows, slot):
            for part in range(SC_GATHER_ROWS // SC_LANES):
                total = jnp.zeros((SC_LANES,), F32)
                for half in range(2):
                    base = half * n_half * SC_LANES
                    xs = [plsc.bitcast(x_v[slot, pl.ds(base + c * SC_LANES, SC_LANES)], BF16) for c in range(n_half)]

                    def row(r, vec):
                        rr = part * SC_LANES + r
                        acc = jnp.zeros((SC_LANES,), F32)
                        for g in range(n_half // fold):
                            p = [plsc.bitcast(rows[rr, pl.ds(base + (fold * g + k) * SC_LANES, SC_LANES)], BF16)
                                 * xs[fold * g + k] for k in range(fold)]
                            w32 = plsc.bitcast((p[0] + p[1]) + (p[2] + p[3]), U32)
                            acc = acc + plsc.bitcast(w32 << 16, F32) + plsc.bitcast(w32 & jnp.uint32(0xFFFF0000), F32)
                        return jnp.where(lane == r, jnp.sum(acc), vec)

                    total = total + lax.fori_loop(0, SC_LANES, row, jnp.zeros((SC_LANES,), F32))
                out_v[slot, pl.ds(chunk * SC_GATHER_ROWS + part * SC_LANES, SC_LANES)] = total

        _sc_token_loop(u_hbm, e_hbm, idx_v, rows_v, sems, per_worker,
                       [(lambda tok: x_hbm.at[first_token + tok], x_v)], out_v, lambda tok: o_hbm.at[tok], chunk_rows)

    return pl.kernel(
        body,
        out_type=jax.ShapeDtypeStruct((n_tokens, N_SLOTS), F32),
        mesh=mesh,
        scratch_types=_sc_scratch(nw, U32) + [pltpu.VMEM((2, nw), U32), pltpu.VMEM((2, N_SLOTS), F32)],
        compiler_params=pltpu.CompilerParams(needs_layout_passes=False),
        name="peer_expert_in_sc",
    )(u_words, e_tok, x_words)


def _peer_gate_kernel(s_ref, g_ref, par_ref, a_ref):
    dots = s_ref[...].T
    low_t = lax.broadcasted_iota(I32, (SUBLANES, dots.shape[1]), 0) < GROUP
    _gate_rows(lambda grp: _dup_halves(dots[(grp // 2) * SUBLANES:(grp // 2 + 1) * SUBLANES, :], low_t)[grp % 2],
               g_ref, par_ref, a_ref)


def _peer_weights_kernel(a_ref, w_ref):
    low_t = lax.broadcasted_iota(I32, (SUBLANES, a_ref.shape[2]), 0) < GROUP
    tiles = []
    for v in range(N_SLOTS // SUBLANES):
        first = a_ref[0, 2 * v * SUBLANES:(2 * v + 1) * SUBLANES, :]
        second = a_ref[0, (2 * v + 1) * SUBLANES:(2 * v + 2) * SUBLANES, :]
        first = first + pltpu.roll(first, GROUP, 0)
        second = second + pltpu.roll(second, GROUP, 0)
        tiles.append(jnp.where(low_t, first, second))
    w_ref[...] = jnp.concatenate(tiles, axis=0).T


def _sc_v_sums(v_rows, e_tok, w_tok, x2d, first_token):
    d = x2d.shape[1]
    n_tokens = e_tok.shape[0]
    per_worker = n_tokens // (SC_CORES * SC_SUBCORES)
    n_acc = d // SC_LANES // 2
    mesh = plsc.VectorSubcoreMesh(core_axis_name="c", subcore_axis_name="s")

    def body(v_hbm, e_hbm, w_hbm, x_hbm, o_hbm, idx_v, rows_v, sems, w_v, x_v, out_v):
        def chunk_rows(chunk, rows, slot):
            src = x_v if chunk == 0 else out_v
            for half in range(2):
                base = half * n_acc * SC_LANES

                def row(r, acc):
                    pos = jnp.full((SC_LANES,), chunk * SC_GATHER_ROWS + r, I32)
                    w = plsc.load_gather(w_v.at[slot], [pos])
                    return tuple(acc[c] + w * rows[r, pl.ds(base + c * SC_LANES, SC_LANES)] for c in range(n_acc))

                acc0 = tuple(src[slot, pl.ds(base + c * SC_LANES, SC_LANES)] for c in range(n_acc))
                acc = lax.fori_loop(0, SC_GATHER_ROWS, row, acc0)
                for c in range(n_acc):
                    out_v[slot, pl.ds(base + c * SC_LANES, SC_LANES)] = acc[c]

        _sc_token_loop(v_hbm, e_hbm, idx_v, rows_v, sems, per_worker,
                       [(lambda tok: w_hbm.at[tok], w_v), (lambda tok: x_hbm.at[first_token + tok], x_v)],
                       out_v, lambda tok: o_hbm.at[tok], chunk_rows)

    return pl.kernel(
        body,
        out_type=jax.ShapeDtypeStruct((n_tokens, d), F32),
        mesh=mesh,
        scratch_types=_sc_scratch(d, F32) + [pltpu.VMEM((2, N_SLOTS), F32), pltpu.VMEM((2, d), F32),
                                            pltpu.VMEM((2, d), F32)],
        compiler_params=pltpu.CompilerParams(needs_layout_passes=False),
        name="peer_expert_out_sc",
    )(v_rows, e_tok, w_tok, x2d)


def _token_tiles(ref, xt_ref, to_tiles):
    tb, d = ref.shape
    for q in range(tb // SUBLANES):
        for c in range(d // LANES):
            rows = slice(q * SUBLANES, (q + 1) * SUBLANES)
            cols = slice(c * LANES, (c + 1) * LANES)
            strided = pl.ds(q * SUBLANES * SUBLANES + c, SUBLANES, stride=SUBLANES)
            if to_tiles:
                xt_ref[strided, :] = ref[rows, cols]
            else:
                ref[rows, cols] = xt_ref[strided, :]


def _peer_v_kernel(r8_ref, a_ref, x_ref, tab_ref, o_ref, bc_ref, xt_ref):
    tb = x_ref.shape[0]
    sub = _sublane_iota()
    low = sub < GROUP
    lane_t = lax.broadcasted_iota(I32, (SUBLANES, tb), 1)
    _token_tiles(x_ref, xt_ref, True)

    def spread(t, buf):
        for k in range(N_ROWS // SUBLANES):
            rows = slice(k * SUBLANES, (k + 1) * SUBLANES)
            col = jnp.sum(jnp.where(lane_t == t, a_ref[0, rows, :], 0.0), axis=1, keepdims=True)
            bits = lax.bitcast_convert_type(col.astype(BF16).astype(F32), U32)
            bc_ref[buf, rows, :] = jnp.broadcast_to(bits | (bits >> 16), (SUBLANES, LANES))

    def accumulate(t, buf):
        acc_lo = jnp.zeros((SUBLANES, LANES), F32)
        acc_hi = jnp.zeros((SUBLANES, LANES), F32)
        ids = r8_ref.at[0, t]
        for grp in range(N_GROUPS):
            prods = []
            for i in range(GROUP):
                row = grp * SUBLANES + i
                r8 = pl.multiple_of(ids[grp * GROUP + i], SUBLANES)
                am = jnp.where(low, bc_ref[buf, row:row + 1, :], bc_ref[buf, row + GROUP:row + GROUP + 1, :])
                prods.append(_as_bf16(tab_ref[pl.ds(r8, SUBLANES), :]) * _as_bf16(am))
            lo, hi = _unpack_words(_as_words((prods[0] + prods[1]) + (prods[2] + prods[3])))
            acc_lo = acc_lo + lo
            acc_hi = acc_hi + hi
        acc_lo = acc_lo + pltpu.roll(acc_lo, GROUP, 0)
        acc_hi = acc_hi + pltpu.roll(acc_hi, GROUP, 0)
        tile = pl.ds(pl.multiple_of(t * SUBLANES, SUBLANES), SUBLANES)
        xt_ref[tile, :] = xt_ref[tile, :] + jnp.where(low, acc_lo, acc_hi)

    def token_group(i, carry):
        t = N_SPREAD * i
        for k in range(N_SPREAD):
            spread(jnp.minimum(t + k + SPREAD_AHEAD, tb - 1), (k + SPREAD_AHEAD) % N_SPREAD)
            accumulate(t + k, k)
        return carry

    for k in range(SPREAD_AHEAD):
        spread(k, k)
    lax.fori_loop(0, tb // N_SPREAD, token_group, 0)
    _token_tiles(o_ref, xt_ref, False)


def _peer_experts(x2d, xp, r8, e_tok, par, gates, dots_sc, v_rows, u_tab, v_tab):
    t, d = x2d.shape
    nblk = t // PEER_TOKENS
    nblk_tc = nblk - SC_TOKEN_BLOCKS
    smem_block = pl.BlockSpec((1, PEER_TOKENS, N_SLOTS), lambda i: (i, 0, 0), memory_space=pltpu.SMEM)
    slot_block = pl.BlockSpec((1, N_SLOTS, PEER_TOKENS), lambda i: (i, 0, 0))
    row_block = pl.BlockSpec((1, N_ROWS, PEER_TOKENS), lambda i: (i, 0, 0))
    tile_block = pl.BlockSpec((PEER_TOKENS * SUBLANES, LANES), lambda i: (i, 0))
    tok_block = pl.BlockSpec((PEER_TOKENS, d), lambda i: (i, 0))
    whole = pl.BlockSpec(memory_space=pltpu.VMEM)
    params = pltpu.CompilerParams(dimension_semantics=("parallel",), vmem_limit_bytes=56 * MIB)

    act_tc = pl.pallas_call(
        _peer_u_kernel,
        grid=(nblk_tc,),
        in_specs=[smem_block, tile_block, slot_block, slot_block, whole],
        out_specs=row_block,
        out_shape=jax.ShapeDtypeStruct((nblk_tc, N_ROWS, PEER_TOKENS), F32),
        scratch_shapes=[pltpu.VMEM((2 * N_ROWS, PEER_TOKENS), F32),
                        pltpu.VMEM((2, 2 * N_GROUPS, SUBLANES, LANES), F32)],
        compiler_params=params,
        name="peer_expert_in",
    )(r8, xp, gates, par, u_tab)
    shifted = pl.BlockSpec((1, N_SLOTS, PEER_TOKENS), lambda i: (i + nblk_tc, 0, 0))
    act_sc = pl.pallas_call(
        _peer_gate_kernel,
        grid=(SC_TOKEN_BLOCKS,),
        in_specs=[pl.BlockSpec((PEER_TOKENS, N_SLOTS), lambda i: (i, 0)), shifted, shifted],
        out_specs=row_block,
        out_shape=jax.ShapeDtypeStruct((SC_TOKEN_BLOCKS, N_ROWS, PEER_TOKENS), F32),
        compiler_params=pltpu.CompilerParams(dimension_semantics=("parallel",)),
        name="peer_gate_sc",
    )(dots_sc, gates, par)
    act = jnp.concatenate([act_tc, act_sc], axis=0)

    nblk_v = nblk - SC_V_BLOCKS
    w_sc = pl.pallas_call(
        _peer_weights_kernel,
        grid=(SC_V_BLOCKS,),
        in_specs=[pl.BlockSpec((1, N_ROWS, PEER_TOKENS), lambda i: (i + nblk_v, 0, 0))],
        out_specs=pl.BlockSpec((PEER_TOKENS, N_SLOTS), lambda i: (i, 0)),
        out_shape=jax.ShapeDtypeStruct((SC_V_BLOCKS * PEER_TOKENS, N_SLOTS), F32),
        compiler_params=pltpu.CompilerParams(dimension_semantics=("parallel",)),
        name="peer_weights_sc",
    )(act)
    y_sc = _sc_v_sums(v_rows, e_tok[nblk_v:].reshape(-1, N_SLOTS), w_sc, x2d, nblk_v * PEER_TOKENS)
    y_tc = pl.pallas_call(
        _peer_v_kernel,
        grid=(nblk_v,),
        in_specs=[smem_block, row_block, tok_block, whole],
        out_specs=tok_block,
        out_shape=jax.ShapeDtypeStruct((nblk_v * PEER_TOKENS, d), F32),
        scratch_shapes=[pltpu.VMEM((N_SPREAD, N_ROWS, LANES), U32),
                        pltpu.VMEM((PEER_TOKENS * SUBLANES, LANES), F32)],
        compiler_params=params,
        name="peer_expert_out",
    )(r8, act, x2d, v_tab)
    return jnp.concatenate([y_tc, y_sc], axis=0)


def _peer_ffn(x2d, norm_g, w_q, sub_keys, u_tabs, v_tabs, layer):
    n_experts, d = u_tabs.shape[1:]
    xp, scores_t, x_words = _peer_scores(x2d, norm_g, w_q, sub_keys)
    u_tiles, u_words = _pack_table(u_tabs, layer)
    v_tiles, _ = _pack_table(v_tabs, layer)
    nblk = x2d.shape[0] // PEER_TOKENS
    nblk_tc = nblk - SC_TOKEN_BLOCKS
    top_sc = _peer_topk(scores_t, nblk_tc, SC_TOKEN_BLOCKS)
    dots_sc = _sc_u_dots(u_words, top_sc[3].reshape(-1, N_SLOTS), x_words, nblk_tc * PEER_TOKENS)
    top_tc = _peer_topk(scores_t, 0, nblk_tc)
    r8, par, gates, e_tok = [jnp.concatenate([a, b], axis=0) for a, b in zip(top_tc, top_sc)]
    return _peer_experts(x2d, xp, r8, e_tok + layer * n_experts, par, gates, dots_sc, v_tabs.reshape(-1, d),
                         u_tiles, v_tiles)


def kernel(x, attn_norm_g, attn_w_qkv, attn_q_g, attn_k_g, attn_rel_bias, attn_w_out, rec_norm_g, rec_w_in, rec_conv_w, rec_conv_b, rec_w_a, rec_b_a, rec_w_x, rec_b_x, rec_lambda, rec_w_out, ffn_norm_g, peer_w_q, peer_sub_keys, peer_u, peer_v):
    b, s, d = x.shape
    depth = ffn_norm_g.shape[0]
    for layer in range(depth):
        j = layer // 2
        if layer % 2 == 0:
            qkv = _qkv_proj(x.reshape(b * s, d), attn_norm_g[j], attn_w_qkv[j], attn_q_g[j], attn_k_g[j])
            x = _attention(x, qkv, attn_rel_bias[j], attn_w_out[j])
        else:
            x = _rglru(x, rec_norm_g[j], rec_w_in[j], rec_conv_w[j], rec_conv_b[j], rec_w_a[j],
                       rec_b_a[j], rec_w_x[j], rec_b_x[j], rec_lambda[j], rec_w_out[j])
        x = _peer_ffn(x.reshape(b * s, d), ffn_norm_g[layer], peer_w_q[layer], peer_sub_keys[layer],
                      peer_u, peer_v, layer).reshape(b, s, d)
    return x
```

```python
import numpy as np
import jax
import jax.numpy as jnp
from jax import lax
from jax.experimental import pallas as pl
from jax.experimental.pallas import tpu as pltpu
from jax.experimental.pallas import tpu_sc as plsc

F32 = jnp.float32
BF16 = jnp.bfloat16
U32 = jnp.uint32
I32 = jnp.int32

EPS = 1e-6
NEG_INF = -1e30

D_MODEL = 1024
CHUNK = 64
ATTN_HEADS = 16
HEAD_DIM = 64
LEFT_CHUNKS = 8
REL_CLIP = 256
LRU_HEADS = 4
LRU_BLOCK = 256
CONV_WIDTH = 4
LRU_C = 8.0
PEER_HEADS = 8
N_KEYS = 128
PEER_TOPK = 16
D_HALF = 128

SUBLANES = 8
LANES = 128

Q_TILE = 4 * CHUNK
N_KEY_BLOCKS = 3
ROW_TILE = 512
PEER_TOKENS = 128
LRU_TILE = 256
N_SLOTS = PEER_HEADS * PEER_TOPK

MIB = 1024 * 1024

SC_CORES = 2
SC_SUBCORES = 16
SC_LANES = 16
SC_GATHER_ROWS = 32
SC_TOKEN_BLOCKS = 138
SC_V_BLOCKS = 98


def _rms_rows(x, g):
    ms = jnp.mean(x * x, axis=-1, keepdims=True)
    return (x * lax.rsqrt(ms + EPS)) * g


def _split_bf16(v):
    hi = v.astype(BF16)
    lo = (v - hi.astype(F32)).astype(BF16)
    return hi, lo


def _dot(a, b):
    return jnp.dot(a, b, preferred_element_type=F32)


def _dot_nt(a, b):
    return lax.dot_general(a, b, (((1,), (1,)), ((), ())), preferred_element_type=F32)


def _qkv_kernel(x_ref, g_ref, w_ref, gain_ref, bsel_ref, bexp_ref, o_ref, xn_ref):
    j = pl.program_id(1)

    @pl.when(j == 0)
    def _():
        xn_ref[...] = _rms_rows(x_ref[...], g_ref[...]).astype(BF16)

    y = _dot(xn_ref[...], w_ref[...])

    @pl.when(j < 2)
    def _():
        hi, lo = _split_bf16(y * y)
        ms = _dot(hi, bsel_ref[...]) + _dot(lo, bsel_ref[...])
        rhi, rlo = _split_bf16(lax.rsqrt(ms + EPS))
        rs = _dot(rhi, bexp_ref[...]) + _dot(rlo, bexp_ref[...])
        o_ref[...] = ((y * rs) * gain_ref[0]).astype(BF16)

    @pl.when(j == 2)
    def _():
        o_ref[...] = y.astype(BF16)


def _qkv_proj(x2d, norm_g, w_qkv, q_g, k_g):
    t, d = x2d.shape
    scale = HEAD_DIM ** -0.5
    gains = jnp.stack([jnp.tile(q_g, ATTN_HEADS) * scale, jnp.tile(k_g, ATTN_HEADS),
                       jnp.ones((d,), F32)]).reshape(3, 1, d)
    head_of_col = np.arange(d) // HEAD_DIM
    bsel = (head_of_col[:, None] == np.arange(LANES)[None, :]).astype(np.float32) / HEAD_DIM
    bexp = (np.arange(LANES)[:, None] == head_of_col[None, :]).astype(np.float32)
    return pl.pallas_call(
        _qkv_kernel,
        grid=(t // ROW_TILE, 3),
        in_specs=[
            pl.BlockSpec((ROW_TILE, d), lambda i, j: (i, 0)),
            pl.BlockSpec((1, d), lambda i, j: (0, 0)),
            pl.BlockSpec((d, d), lambda i, j: (0, j)),
            pl.BlockSpec((1, 1, d), lambda i, j: (j, 0, 0)),
            pl.BlockSpec((d, LANES), lambda i, j: (0, 0)),
            pl.BlockSpec((LANES, d), lambda i, j: (0, 0)),
        ],
        out_specs=pl.BlockSpec((ROW_TILE, d), lambda i, j: (i, j)),
        out_shape=jax.ShapeDtypeStruct((t, 3 * d), BF16),
        scratch_shapes=[pltpu.VMEM((ROW_TILE, d), BF16)],
        compiler_params=pltpu.CompilerParams(
            dimension_semantics=("parallel", "arbitrary"), vmem_limit_bytes=40 * MIB),
        name="qkv_proj",
    )(x2d, norm_g.reshape(1, d), w_qkv.astype(BF16), gains,
      jnp.asarray(bsel, BF16), jnp.asarray(bexp, BF16))


def _attn_kernel(x_ref, q_ref, k0_ref, k1_ref, k2_ref, v0_ref, v1_ref, v2_ref,
                 bias_ref, wout_ref, o_ref, att_ref):
    qt = pl.program_id(1)
    k_refs = (k0_ref, k1_ref, k2_ref)
    v_refs = (v0_ref, v1_ref, v2_ref)
    negs = [jnp.where(qt - (N_KEY_BLOCKS - 1) + kb < 0, NEG_INF, 0.0).astype(F32)
            for kb in range(N_KEY_BLOCKS)]
    lane = lax.broadcasted_iota(I32, (Q_TILE, LANES), 1)
    first_half = lane < HEAD_DIM
    for p in range(ATTN_HEADS // 2):
        cols = slice(p * LANES, (p + 1) * LANES)
        qp = q_ref[0, :, cols]
        outs = []
        for hh in range(2):
            h = 2 * p + hh
            keep = first_half if hh == 0 else jnp.logical_not(first_half)
            qm = jnp.where(keep, qp, jnp.zeros_like(qp))
            s = [_dot_nt(qm, k_refs[kb][0, :, cols]) + bias_ref[kb, h] + negs[kb]
                 for kb in range(N_KEY_BLOCKS)]
            m = jnp.max(s[0], axis=-1, keepdims=True)
            for kb in range(1, N_KEY_BLOCKS):
                m = jnp.maximum(m, jnp.max(s[kb], axis=-1, keepdims=True))
            acc = jnp.zeros((Q_TILE, LANES), F32)
            l = jnp.zeros((Q_TILE, 1), F32)
            for kb in range(N_KEY_BLOCKS):
                e = jnp.exp(s[kb] - m)
                l = l + jnp.sum(e, axis=-1, keepdims=True)
                acc = acc + _dot(e.astype(BF16), v_refs[kb][0, :, cols])
            outs.append(acc / l)
        att_ref[:, cols] = jnp.where(first_half, outs[0], outs[1]).astype(BF16)
    o_ref[0] = x_ref[0] + _dot(att_ref[...], wout_ref[...])


def _band_bias(table):
    qi = np.arange(Q_TILE)[:, None]
    kj = np.arange(Q_TILE)[None, :]
    period = 2 * Q_TILE
    m = np.arange(period)
    delta = np.where(m < Q_TILE, m, m - period)
    blocks = []
    for kb in range(N_KEY_BLOCKS):
        dist = (N_KEY_BLOCKS - 1 - kb) * Q_TILE - delta
        diag = table[:, np.clip(dist, -REL_CLIP, REL_CLIP) + REL_CLIP]
        toep = jnp.tile(diag, (1, Q_TILE))[:, :Q_TILE * (period - 1)]
        toep = toep.reshape(-1, Q_TILE, period - 1)[:, :, :Q_TILE]
        dchunk = (Q_TILE // CHUNK) * (kb - (N_KEY_BLOCKS - 1)) + kj // CHUNK - qi // CHUNK
        valid = (dchunk >= -LEFT_CHUNKS) & (dchunk <= 0)
        blocks.append(jnp.where(valid[None], toep, NEG_INF))
    return jnp.stack(blocks).astype(F32)


def _attention(x, qkv, rel_bias_table, w_out):
    b, s, d = x.shape
    nq = s // Q_TILE
    qkv3 = qkv.reshape(b, s, 3 * d)
    bias = _band_bias(rel_bias_table)

    def kv_spec(kb, col):
        return pl.BlockSpec(
            (1, Q_TILE, d),
            lambda bi, qt: (bi, jnp.maximum(qt - (N_KEY_BLOCKS - 1) + kb, 0), col))

    in_specs = [pl.BlockSpec((1, Q_TILE, d), lambda bi, qt: (bi, qt, 0)),
                pl.BlockSpec((1, Q_TILE, d), lambda bi, qt: (bi, qt, 0))]
    in_specs += [kv_spec(kb, 1) for kb in range(N_KEY_BLOCKS)]
    in_specs += [kv_spec(kb, 2) for kb in range(N_KEY_BLOCKS)]
    in_specs += [pl.BlockSpec(memory_space=pltpu.VMEM), pl.BlockSpec(memory_space=pltpu.VMEM)]
    return pl.pallas_call(
        _attn_kernel,
        grid=(b, nq),
        in_specs=in_specs,
        out_specs=pl.BlockSpec((1, Q_TILE, d), lambda bi, qt: (bi, qt, 0)),
        out_shape=jax.ShapeDtypeStruct((b, s, d), F32),
        scratch_shapes=[pltpu.VMEM((Q_TILE, d), BF16)],
        compiler_params=pltpu.CompilerParams(
            dimension_semantics=("parallel", "parallel"), vmem_limit_bytes=48 * MIB),
        name="band_attention",
    )(x, qkv3, qkv3, qkv3, qkv3, qkv3, qkv3, qkv3, bias, w_out.astype(BF16))


def _rglru_kernel(x_ref, g_ref, win_ref, cw_ref, cb_ref, wa_ref, ba_ref, wx_ref, bx_ref,
                  lam_ref, wout_ref, o_ref, ext_ref, h_ref, a_scr, b_scr, hs_scr):
    ts = x_ref.shape[1]
    w = LRU_HEADS * LRU_BLOCK

    @pl.when(pl.program_id(1) == 0)
    def _():
        ext_ref[...] = jnp.zeros_like(ext_ref)
        h_ref[...] = jnp.zeros_like(h_ref)

    x = x_ref[0]
    xn = _rms_rows(x, g_ref[...]).astype(BF16)
    gu = _dot(xn, win_ref[...])
    gate = gu[:, :w]
    u_raw = gu[:, w:]
    ext_ref[0:SUBLANES, :] = ext_ref[ts:ts + SUBLANES, :]
    ext_ref[SUBLANES:, :] = u_raw
    u = cb_ref[...] + jnp.zeros((ts, w), F32)
    for k in range(CONV_WIDTH):
        off = SUBLANES - (CONV_WIDTH - 1) + k
        u = u + cw_ref[k:k + 1, :] * ext_ref[off:off + ts, :]
    ub = u.astype(BF16)

    def block_diag(wref):
        return jnp.concatenate(
            [_dot(ub[:, hh * LRU_BLOCK:(hh + 1) * LRU_BLOCK], wref[hh]) for hh in range(LRU_HEADS)],
            axis=1)

    r = jax.nn.sigmoid(block_diag(wa_ref) + ba_ref[...])
    i = jax.nn.sigmoid(block_diag(wx_ref) + bx_ref[...])
    z = -lam_ref[...]
    softplus = jnp.maximum(z, 0.0) + jnp.log1p(jnp.exp(-jnp.abs(z)))
    log_a = (-LRU_C) * r * softplus
    a_scr[...] = jnp.exp(log_a)
    th = jnp.tanh(log_a)
    b_scr[...] = jnp.sqrt(2.0 * th / (th - 1.0)) * (i * u)

    def step(t, h):
        h = a_scr[pl.ds(t, 1), :] * h + b_scr[pl.ds(t, 1), :]
        hs_scr[pl.ds(t, 1), :] = h
        return h

    h_ref[0:1, :] = lax.fori_loop(0, ts, step, h_ref[0:1, :], unroll=8)
    y = jax.nn.gelu(gate) * hs_scr[...]
    o_ref[0] = x + _dot(y.astype(BF16), wout_ref[...])


def _rglru(x, norm_g, w_in, conv_w, conv_b, w_a, b_a, w_x, b_x, lam, w_out):
    b, s, d = x.shape
    w = LRU_HEADS * LRU_BLOCK
    row = lambda v: v.reshape(1, -1)
    whole = pl.BlockSpec(memory_space=pltpu.VMEM)
    return pl.pallas_call(
        _rglru_kernel,
        grid=(b, s // LRU_TILE),
        in_specs=[pl.BlockSpec((1, LRU_TILE, d), lambda bi, si: (bi, si, 0))] + [whole] * 10,
        out_specs=pl.BlockSpec((1, LRU_TILE, d), lambda bi, si: (bi, si, 0)),
        out_shape=jax.ShapeDtypeStruct((b, s, d), F32),
        scratch_shapes=[pltpu.VMEM((LRU_TILE + SUBLANES, w), F32), pltpu.VMEM((SUBLANES, w), F32),
                        pltpu.VMEM((LRU_TILE, w), F32), pltpu.VMEM((LRU_TILE, w), F32),
                        pltpu.VMEM((LRU_TILE, w), F32)],
        compiler_params=pltpu.CompilerParams(
            dimension_semantics=("arbitrary", "arbitrary"), vmem_limit_bytes=48 * MIB),
        name="rglru_mixer",
    )(x, row(norm_g), w_in.astype(BF16), conv_w, row(conv_b), w_a.astype(BF16), row(b_a),
      w_x.astype(BF16), row(b_x), row(lam), w_out.astype(BF16))


def _peer_scores_kernel(x_ref, g_ref, wq_ref, sk_ref, xp_ref, st_ref, xw_ref):
    xb = _rms_rows(x_ref[...], g_ref[...]).astype(BF16)
    words = _bf16_pair_words(xb)
    xw_ref[...] = words
    tm = words.shape[0]
    for c in range(SUBLANES):
        piece = words[:, (c % (SUBLANES // 2)) * LANES:(c % (SUBLANES // 2) + 1) * LANES]
        xp_ref[pl.ds(c, tm, stride=SUBLANES), :] = piece
    q = _dot(xb, wq_ref[...]).astype(BF16)
    for hp in range(2 * PEER_HEADS):
        st_ref[hp] = _dot_nt(sk_ref[hp % 2], q[:, hp * D_HALF:(hp + 1) * D_HALF])


def _peer_scores(x2d, norm_g, w_q, sub_keys):
    t, d = x2d.shape
    nq = w_q.shape[1]
    whole = pl.BlockSpec(memory_space=pltpu.VMEM)
    return pl.pallas_call(
        _peer_scores_kernel,
        grid=(t // ROW_TILE,),
        in_specs=[pl.BlockSpec((ROW_TILE, d), lambda i: (i, 0)), whole, whole, whole],
        out_specs=[pl.BlockSpec((ROW_TILE * SUBLANES, LANES), lambda i: (i, 0)),
                   pl.BlockSpec((2 * PEER_HEADS, N_KEYS, ROW_TILE), lambda i: (0, 0, i)),
                   pl.BlockSpec((ROW_TILE, d // 2), lambda i: (i, 0))],
        out_shape=[jax.ShapeDtypeStruct((t * SUBLANES, LANES), U32),
                   jax.ShapeDtypeStruct((2 * PEER_HEADS, N_KEYS, t), F32),
                   jax.ShapeDtypeStruct((t, d // 2), U32)],
        compiler_params=pltpu.CompilerParams(
            dimension_semantics=("parallel",), vmem_limit_bytes=48 * MIB),
        name="peer_scores",
    )(x2d, norm_g.reshape(1, d), w_q.astype(BF16), sub_keys.astype(BF16))


def _candidate_blocks():
    blocks = []
    for i in range(PEER_TOPK // 2):
        nj = PEER_TOPK // (i + 1)
        for j0 in range(0, nj, SUBLANES):
            blocks.append(("row", i, j0, min(SUBLANES, nj - j0)))
    blocks.append(("col", PEER_TOPK // 2, PEER_TOPK // 2))
    return blocks


def _peer_topk_kernel(st_ref, r8_ref, par_ref, g_ref, e_ref, r8t_ref, et_ref):
    tk = st_ref.shape[2]
    key_id = lax.broadcasted_iota(I32, (N_KEYS, tk), 0).astype(F32)
    row16 = lax.broadcasted_iota(I32, (PEER_TOPK, tk), 0)
    row8 = lax.broadcasted_iota(I32, (SUBLANES, tk), 0)
    row8_f = row8.astype(F32)
    neg_inf = jnp.float32(-jnp.inf)
    blocks = _candidate_blocks()
    big = jnp.float32(PEER_TOPK * PEER_TOPK)

    def top16(x):
        vals = jnp.zeros((PEER_TOPK, tk), F32)
        idxs = jnp.zeros((PEER_TOPK, tk), F32)
        for k in range(PEER_TOPK):
            m = jnp.max(x, axis=0, keepdims=True)
            idx = jnp.min(jnp.where(x == m, key_id, float(N_KEYS)), axis=0, keepdims=True)
            x = jnp.where(key_id == idx, neg_inf, x)
            vals = jnp.where(row16 == k, m, vals)
            idxs = jnp.where(row16 == k, idx, idxs)
        return vals, idxs

    for h in range(PEER_HEADS):
        s0, i0 = top16(st_ref[2 * h])
        s1, i1 = top16(st_ref[2 * h + 1])
        cand, flat, expert = [], [], []
        for blk in blocks:
            if blk[0] == "row":
                _, i, j0, n = blk
                c = s0[i:i + 1, :] + s1[j0:j0 + SUBLANES, :]
                f = float(PEER_TOPK * i + j0) + row8_f
                e = i0[i:i + 1, :] * float(N_KEYS) + i1[j0:j0 + SUBLANES, :]
            else:
                _, i_start, n = blk
                c = s0[i_start:i_start + SUBLANES, :] + s1[0:1, :]
                f = float(PEER_TOPK) * (float(i_start) + row8_f)
                e = i0[i_start:i_start + SUBLANES, :] * float(N_KEYS) + i1[0:1, :]
            cand.append(jnp.where(row8 < n, c, neg_inf))
            flat.append(f)
            expert.append(e)
        cand = jnp.concatenate(cand, axis=0)
        flat = jnp.concatenate(flat, axis=0)
        expert = jnp.concatenate(expert, axis=0)
        best_s = jnp.zeros((PEER_TOPK, tk), F32)
        best_e = jnp.zeros((PEER_TOPK, tk), F32)
        for k in range(PEER_TOPK):
            m = jnp.max(cand, axis=0, keepdims=True)
            fidx = jnp.min(jnp.where(cand == m, flat, big), axis=0, keepdims=True)
            pick = flat == fidx
            e_sel = jnp.max(jnp.where(pick, expert, -1.0), axis=0, keepdims=True)
            cand = jnp.where(pick, neg_inf, cand)
            best_s = jnp.where(row16 == k, m, best_s)
            best_e = jnp.where(row16 == k, e_sel, best_e)
        ex = jnp.exp(best_s - best_s[0:1, :])
        gate = ex / jnp.sum(ex, axis=0, keepdims=True)
        rows = slice(h * PEER_TOPK, (h + 1) * PEER_TOPK)
        best_i = best_e.astype(I32)
        r8t_ref[rows, :] = (best_i >> 1) * SUBLANES
        et_ref[rows, :] = best_i
        par_ref[0, rows, :] = best_i & 1
        g_ref[0, rows, :] = gate
    r8_ref[0] = r8t_ref[...].T
    e_ref[0] = et_ref[...].T


def _peer_topk(scores_t, first_block, nblk):
    slot_major = pl.BlockSpec((1, N_SLOTS, PEER_TOKENS), lambda i: (i, 0, 0))
    token_major = pl.BlockSpec((1, PEER_TOKENS, N_SLOTS), lambda i: (i, 0, 0))
    return pl.pallas_call(
        _peer_topk_kernel,
        grid=(nblk,),
        in_specs=[pl.BlockSpec((2 * PEER_HEADS, N_KEYS, PEER_TOKENS), lambda i: (0, 0, i + first_block))],
        out_specs=[token_major, slot_major, slot_major, token_major],
        out_shape=[jax.ShapeDtypeStruct((nblk, PEER_TOKENS, N_SLOTS), I32),
                   jax.ShapeDtypeStruct((nblk, N_SLOTS, PEER_TOKENS), I32),
                   jax.ShapeDtypeStruct((nblk, N_SLOTS, PEER_TOKENS), F32),
                   jax.ShapeDtypeStruct((nblk, PEER_TOKENS, N_SLOTS), I32)],
        scratch_shapes=[pltpu.VMEM((N_SLOTS, PEER_TOKENS), I32), pltpu.VMEM((N_SLOTS, PEER_TOKENS), I32)],
        compiler_params=pltpu.CompilerParams(dimension_semantics=("parallel",)),
        name="peer_topk",
    )(scores_t)


def _bf16_pair_words(x):
    bits = lax.bitcast_convert_type(x.astype(BF16).astype(F32), U32)
    half = bits.shape[1] // 2
    return (bits[:, :half] >> 16) | (bits[:, half:] & jnp.uint32(0xFFFF0000))


def _pack_kernel(t_ref, o_ref, w_ref):
    words = _bf16_pair_words(t_ref[0])
    w_ref[...] = words
    rows = words.shape[0]
    n = words.shape[1] // LANES
    for s in range(n):
        o_ref[pl.ds(s, rows, stride=n), :] = words[:, s * LANES:(s + 1) * LANES]


def _pack_table(tabs, layer):
    _, e, d = tabs.shape
    n = d // (2 * LANES)
    return pl.pallas_call(
        _pack_kernel,
        grid=(e // ROW_TILE,),
        in_specs=[pl.BlockSpec((1, ROW_TILE, d), lambda i: (layer, i, 0))],
        out_specs=[pl.BlockSpec((ROW_TILE * n, LANES), lambda i: (i, 0)),
                   pl.BlockSpec((ROW_TILE, d // 2), lambda i: (i, 0))],
        out_shape=[jax.ShapeDtypeStruct((e * n, LANES), U32), jax.ShapeDtypeStruct((e, d // 2), U32)],
        compiler_params=pltpu.CompilerParams(dimension_semantics=("parallel",)),
        name="peer_pack_table",
    )(tabs)


def _unpack_words(w):
    lo = lax.bitcast_convert_type(w << 16, F32)
    hi = lax.bitcast_convert_type(w & jnp.uint32(0xFFFF0000), F32)
    return lo, hi


def _as_bf16(words):
    return pltpu.bitcast(words, BF16)


def _as_words(packed):
    return pltpu.bitcast(packed, U32)


def _merge_packed(a, b, shift, mask):
    ta = a + _as_bf16(pltpu.roll(_as_words(a), shift, 0))
    tb = b + _as_bf16(pltpu.roll(_as_words(b), SUBLANES - shift, 0))
    return jnp.where(mask, _as_words(ta), _as_words(tb))


def _sublane_iota():
    return lax.broadcasted_iota(I32, (SUBLANES, LANES), 0)


def _merge(a, b, shift, mask):
    ta = a + pltpu.roll(a, shift, 0)
    tb = b + pltpu.roll(b, SUBLANES - shift, 0)
    return jnp.where(mask, ta, tb)


GROUP = SUBLANES // 2
N_GROUPS = N_SLOTS // GROUP
N_ROWS = 2 * N_SLOTS
N_SPREAD = 4
SPREAD_AHEAD = 2


def _dup_halves(v, low):
    r = pltpu.roll(v, GROUP, 0)
    return jnp.where(low, v, r), jnp.where(low, r, v)


def _peer_u_kernel(r8_ref, xp_ref, g_ref, par_ref, tab_ref, a_ref, s_ref, part_ref):
    tb = g_ref.shape[2]
    sub = _sublane_iota()
    m2 = (sub & 2) != 0
    lane_t = lax.broadcasted_iota(I32, (SUBLANES, tb), 1)
    n_part = 2 * N_GROUPS

    def products(t, slot):
        xb = _as_bf16(xp_ref[pl.ds(pl.multiple_of(t * SUBLANES, SUBLANES), SUBLANES), :])
        ids = r8_ref.at[0, t]
        for grp in range(N_GROUPS):
            prods = []
            for i in range(GROUP):
                r8 = pl.multiple_of(ids[grp * GROUP + i], SUBLANES)
                prods.append(_as_bf16(tab_ref[pl.ds(r8, SUBLANES), :]) * xb)
            halves = (_merge_packed(prods[3], prods[1], 2, m2), _merge_packed(prods[2], prods[0], 2, m2))
            for k in range(2):
                lo, hi = _unpack_words(halves[k])
                part_ref[slot, 2 * grp + k] = lo + hi

    def reduce_lanes(t, slot):
        for n in range(n_part):
            col = jnp.sum(part_ref[slot, n], axis=1, keepdims=True)
            rows = slice(n * SUBLANES, (n + 1) * SUBLANES)
            s_ref[rows, :] = jnp.where(lane_t == t, col, s_ref[rows, :])

    def token_pair(i, carry):
        t = 2 * i
        reduce_lanes(t - 2, 0)
        reduce_lanes(t - 1, 1)
        products(t, 0)
        products(t + 1, 1)
        return carry

    s_ref[...] = jnp.zeros_like(s_ref)
    part_ref[...] = jnp.zeros_like(part_ref)
    lax.fori_loop(0, tb // 2, token_pair, 0)
    reduce_lanes(tb - 2, 0)
    reduce_lanes(tb - 1, 1)

    odd_t = (lax.broadcasted_iota(I32, (SUBLANES, tb), 0) & 1) != 0
    _gate_rows(lambda grp: _merge(s_ref[2 * grp * SUBLANES:(2 * grp + 1) * SUBLANES, :],
                                  s_ref[(2 * grp + 1) * SUBLANES:(2 * grp + 2) * SUBLANES, :], 1, odd_t),
               g_ref, par_ref, a_ref)


def _gate_rows(dots_of_group, g_ref, par_ref, a_ref):
    tb = g_ref.shape[2]
    low_t = lax.broadcasted_iota(I32, (SUBLANES, tb), 0) < GROUP
    want = jnp.where(low_t, 0, 1)
    for v in range(N_SLOTS // SUBLANES):
        src = slice(v * SUBLANES, (v + 1) * SUBLANES)
        gates = _dup_halves(g_ref[0, src, :], low_t)
        pars = _dup_halves(par_ref[0, src, :], low_t)
        for half in range(2):
            grp = 2 * v + half
            act = gates[half] * jax.nn.gelu(dots_of_group(grp))
            rows = slice(grp * SUBLANES, (grp + 1) * SUBLANES)
            a_ref[0, rows, :] = jnp.where(pars[half] == want, act, 0.0)


SEM_IDS, SEM_IN, SEM_OUT, N_SEMS = 2, 3, 5, 7


def _sc_token_loop(tab_hbm, e_hbm, idx_v, rows_v, sems, per_worker, token_in, out_buf, out_row, chunk_rows):
    wid = lax.axis_index("s") * SC_CORES + lax.axis_index("c")
    n_chunks = N_SLOTS // SC_GATHER_ROWS
    first = wid * per_worker
    last = first + per_worker - 1

    def gather(slot, chunk):
        ids = idx_v.at[slot, pl.ds(chunk * SC_GATHER_ROWS, SC_GATHER_ROWS)]
        return pltpu.make_async_copy(tab_hbm.at[ids], rows_v.at[chunk % 2], sems.at[chunk % 2])

    def ids_copy(tok, slot):
        return pltpu.make_async_copy(e_hbm.at[tok], idx_v.at[slot], sems.at[SEM_IDS])

    def in_copies(tok, slot):
        return [pltpu.make_async_copy(row_of(tok), buf.at[slot], sems.at[SEM_IN + slot]) for row_of, buf in token_in]

    def out_copy(tok, slot):
        return pltpu.make_async_copy(out_buf.at[slot], out_row(tok), sems.at[SEM_OUT + slot])

    def one_token(k, tok, slot, next_tok):
        ids_copy(next_tok, 1 - slot).start()
        for c in in_copies(next_tok, 1 - slot):
            c.start()
        for c in in_copies(tok, slot):
            c.wait()

        @pl.when(k > 0)
        def _():
            out_copy(tok, slot).wait()

        for chunk in range(n_chunks):
            if chunk + 1 < n_chunks:
                gather(slot, chunk + 1).start()
            else:
                ids_copy(next_tok, 1 - slot).wait()
                gather(1 - slot, 0).start()
            gather(slot, chunk).wait()
            chunk_rows(chunk, rows_v.at[chunk % 2], slot)
        out_copy(tok, slot).start()

    def token_pair(k, carry):
        tok = first + 2 * k
        one_token(k, tok, 0, tok + 1)
        one_token(k, tok + 1, 1, jnp.minimum(tok + 2, last))
        return carry

    pltpu.sync_copy(e_hbm.at[first], idx_v.at[0])
    gather(0, 0).start()
    for c in in_copies(first, 0):
        c.start()
    lax.fori_loop(0, per_worker // 2, token_pair, 0)
    gather(0, 0).wait()
    for c in in_copies(last, 0):
        c.wait()
    out_copy(last, 0).wait()
    out_copy(last, 1).wait()


def _sc_scratch(row_len, dtype):
    return [pltpu.VMEM((2, N_SLOTS), I32), pltpu.VMEM((2, SC_GATHER_ROWS, row_len), dtype),
            pltpu.SemaphoreType.DMA((N_SEMS,))]


def _sc_u_dots(u_words, e_tok, x_words, first_token):
    nw = x_words.shape[1]
    n_tokens = e_tok.shape[0]
    per_worker = n_tokens // (SC_CORES * SC_SUBCORES)
    n_half = nw // SC_LANES // 2
    fold = 4
    mesh = plsc.VectorSubcoreMesh(core_axis_name="c", subcore_axis_name="s")

    def body(u_hbm, e_hbm, x_hbm, o_hbm, idx_v, rows_v, sems, x_v, out_v):
        lane = lax.iota(I32, SC_LANES)

        def chunk_rows(chunk, rows, slot):
            for part in range(SC_GATHER_ROWS // SC_LANES):
                total = jnp.zeros((SC_LANES,), F32)
                for half in range(2):
                    base = half * n_half * SC_LANES
                    xs = [plsc.bitcast(x_v[slot, pl.ds(base + c * SC_LANES, SC_LANES)], BF16) for c in range(n_half)]

                    def row(r, vec):
                        rr = part * SC_LANES + r
                        acc = jnp.zeros((SC_LANES,), F32)
                        for g in range(n_half // fold):
                            p = [plsc.bitcast(rows[rr, pl.ds(base + (fold * g + k) * SC_LANES, SC_LANES)], BF16)
                                 * xs[fold * g + k] for k in range(fold)]
                            w32 = plsc.bitcast((p[0] + p[1]) + (p[2] + p[3]), U32)
                            acc = acc + plsc.bitcast(w32 << 16, F32) + plsc.bitcast(w32 & jnp.uint32(0xFFFF0000), F32)
                        return jnp.where(lane == r, jnp.sum(acc), vec)

                    total = total + lax.fori_loop(0, SC_LANES, row, jnp.zeros((SC_LANES,), F32))
                out_v[slot, pl.ds(chunk * SC_GATHER_ROWS + part * SC_LANES, SC_LANES)] = total

        _sc_token_loop(u_hbm, e_hbm, idx_v, rows_v, sems, per_worker,
                       [(lambda tok: x_hbm.at[first_token + tok], x_v)], out_v, lambda tok: o_hbm.at[tok], chunk_rows)

    return pl.kernel(
        body,
        out_type=jax.ShapeDtypeStruct((n_tokens, N_SLOTS), F32),
        mesh=mesh,
        scratch_types=_sc_scratch(nw, U32) + [pltpu.VMEM((2, nw), U32), pltpu.VMEM((2, N_SLOTS), F32)],
        compiler_params=pltpu.CompilerParams(needs_layout_passes=False),
        name="peer_expert_in_sc",
    )(u_words, e_tok, x_words)


def _peer_gate_kernel(s_ref, g_ref, par_ref, a_ref):
    dots = s_ref[...].T
    low_t = lax.broadcasted_iota(I32, (SUBLANES, dots.shape[1]), 0) < GROUP
    _gate_rows(lambda grp: _dup_halves(dots[(grp // 2) * SUBLANES:(grp // 2 + 1) * SUBLANES, :], low_t)[grp % 2],
               g_ref, par_ref, a_ref)


def _peer_weights_kernel(a_ref, w_ref):
    low_t = lax.broadcasted_iota(I32, (SUBLANES, a_ref.shape[2]), 0) < GROUP
    tiles = []
    for v in range(N_SLOTS // SUBLANES):
        first = a_ref[0, 2 * v * SUBLANES:(2 * v + 1) * SUBLANES, :]
        second = a_ref[0, (2 * v + 1) * SUBLANES:(2 * v + 2) * SUBLANES, :]
        first = first + pltpu.roll(first, GROUP, 0)
        second = second + pltpu.roll(second, GROUP, 0)
        tiles.append(jnp.where(low_t, first, second))
    w_ref[...] = jnp.concatenate(tiles, axis=0).T


def _sc_v_sums(v_rows, e_tok, w_tok, x2d, first_token):
    d = x2d.shape[1]
    n_tokens = e_tok.shape[0]
    per_worker = n_tokens // (SC_CORES * SC_SUBCORES)
    n_acc = d // SC_LANES // 2
    mesh = plsc.VectorSubcoreMesh(core_axis_name="c", subcore_axis_name="s")

    def body(v_hbm, e_hbm, w_hbm, x_hbm, o_hbm, idx_v, rows_v, sems, w_v, x_v, out_v):
        def chunk_rows(chunk, rows, slot):
            src = x_v if chunk == 0 else out_v
            for half in range(2):
                base = half * n_acc * SC_LANES

                def row(r, acc):
                    pos = jnp.full((SC_LANES,), chunk * SC_GATHER_ROWS + r, I32)
                    w = plsc.load_gather(w_v.at[slot], [pos])
                    return tuple(acc[c] + w * rows[r, pl.ds(base + c * SC_LANES, SC_LANES)] for c in range(n_acc))

                acc0 = tuple(src[slot, pl.ds(base + c * SC_LANES, SC_LANES)] for c in range(n_acc))
                acc = lax.fori_loop(0, SC_GATHER_ROWS, row, acc0)
                for c in range(n_acc):
                    out_v[slot, pl.ds(base + c * SC_LANES, SC_LANES)] = acc[c]

        _sc_token_loop(v_hbm, e_hbm, idx_v, rows_v, sems, per_worker,
                       [(lambda tok: w_hbm.at[tok], w_v), (lambda tok: x_hbm.at[first_token + tok], x_v)],
                       out_v, lambda tok: o_hbm.at[tok], chunk_rows)

    return pl.kernel(
        body,
        out_type=jax.ShapeDtypeStruct((n_tokens, d), F32),
        mesh=mesh,
        scratch_types=_sc_scratch(d, F32) + [pltpu.VMEM((2, N_SLOTS), F32), pltpu.VMEM((2, d), F32),
                                            pltpu.VMEM((2, d), F32)],
        compiler_params=pltpu.CompilerParams(needs_layout_passes=False),
        name="peer_expert_out_sc",
    )(v_rows, e_tok, w_tok, x2d)


def _token_tiles(ref, xt_ref, to_tiles):
    tb, d = ref.shape
    for q in range(tb // SUBLANES):
        for c in range(d // LANES):
            rows = slice(q * SUBLANES, (q + 1) * SUBLANES)
            cols = slice(c * LANES, (c + 1) * LANES)
            strided = pl.ds(q * SUBLANES * SUBLANES + c, SUBLANES, stride=SUBLANES)
            if to_tiles:
                xt_ref[strided, :] = ref[rows, cols]
            else:
                ref[rows, cols] = xt_ref[strided, :]


def _peer_v_kernel(r8_ref, a_ref, x_ref, tab_ref, o_ref, bc_ref, xt_ref):
    tb = x_ref.shape[0]
    sub = _sublane_iota()
    low = sub < GROUP
    lane_t = lax.broadcasted_iota(I32, (SUBLANES, tb), 1)
    _token_tiles(x_ref, xt_ref, True)

    def spread(t, buf):
        for k in range(N_ROWS // SUBLANES):
            rows = slice(k * SUBLANES, (k + 1) * SUBLANES)
            col = jnp.sum(jnp.where(lane_t == t, a_ref[0, rows, :], 0.0), axis=1, keepdims=True)
            bits = lax.bitcast_convert_type(col.astype(BF16).astype(F32), U32)
            bc_ref[buf, rows, :] = jnp.broadcast_to(bits | (bits >> 16), (SUBLANES, LANES))

    def accumulate(t, buf):
        acc_lo = jnp.zeros((SUBLANES, LANES), F32)
        acc_hi = jnp.zeros((SUBLANES, LANES), F32)
        ids = r8_ref.at[0, t]
        for grp in range(N_GROUPS):
            prods = []
            for i in range(GROUP):
                row = grp * SUBLANES + i
                r8 = pl.multiple_of(ids[grp * GROUP + i], SUBLANES)
                am = jnp.where(low, bc_ref[buf, row:row + 1, :], bc_ref[buf, row + GROUP:row + GROUP + 1, :])
                prods.append(_as_bf16(tab_ref[pl.ds(r8, SUBLANES), :]) * _as_bf16(am))
            lo, hi = _unpack_words(_as_words((prods[0] + prods[1]) + (prods[2] + prods[3])))
            acc_lo = acc_lo + lo
            acc_hi = acc_hi + hi
        acc_lo = acc_lo + pltpu.roll(acc_lo, GROUP, 0)
        acc_hi = acc_hi + pltpu.roll(acc_hi, GROUP, 0)
        tile = pl.ds(pl.multiple_of(t * SUBLANES, SUBLANES), SUBLANES)
        xt_ref[tile, :] = xt_ref[tile, :] + jnp.where(low, acc_lo, acc_hi)

    def token_group(i, carry):
        t = N_SPREAD * i
        for k in range(N_SPREAD):
            spread(jnp.minimum(t + k + SPREAD_AHEAD, tb - 1), (k + SPREAD_AHEAD) % N_SPREAD)
            accumulate(t + k, k)
        return carry

    for k in range(SPREAD_AHEAD):
        spread(k, k)
    lax.fori_loop(0, tb // N_SPREAD, token_group, 0)
    _token_tiles(o_ref, xt_ref, False)


def _peer_experts(x2d, xp, r8, e_tok, par, gates, dots_sc, v_rows, u_tab, v_tab):
    t, d = x2d.shape
    nblk = t // PEER_TOKENS
    nblk_tc = nblk - SC_TOKEN_BLOCKS
    smem_block = pl.BlockSpec((1, PEER_TOKENS, N_SLOTS), lambda i: (i, 0, 0), memory_space=pltpu.SMEM)
    slot_block = pl.BlockSpec((1, N_SLOTS, PEER_TOKENS), lambda i: (i, 0, 0))
    row_block = pl.BlockSpec((1, N_ROWS, PEER_TOKENS), lambda i: (i, 0, 0))
    tile_block = pl.BlockSpec((PEER_TOKENS * SUBLANES, LANES), lambda i: (i, 0))
    tok_block = pl.BlockSpec((PEER_TOKENS, d), lambda i: (i, 0))
    whole = pl.BlockSpec(memory_space=pltpu.VMEM)
    params = pltpu.CompilerParams(dimension_semantics=("parallel",), vmem_limit_bytes=56 * MIB)

    act_tc = pl.pallas_call(
        _peer_u_kernel,
        grid=(nblk_tc,),
        in_specs=[smem_block, tile_block, slot_block, slot_block, whole],
        out_specs=row_block,
        out_shape=jax.ShapeDtypeStruct((nblk_tc, N_ROWS, PEER_TOKENS), F32),
        scratch_shapes=[pltpu.VMEM((2 * N_ROWS, PEER_TOKENS), F32),
                        pltpu.VMEM((2, 2 * N_GROUPS, SUBLANES, LANES), F32)],
        compiler_params=params,
        name="peer_expert_in",
    )(r8, xp, gates, par, u_tab)
    shifted = pl.BlockSpec((1, N_SLOTS, PEER_TOKENS), lambda i: (i + nblk_tc, 0, 0))
    act_sc = pl.pallas_call(
        _peer_gate_kernel,
        grid=(SC_TOKEN_BLOCKS,),
        in_specs=[pl.BlockSpec((PEER_TOKENS, N_SLOTS), lambda i: (i, 0)), shifted, shifted],
        out_specs=row_block,
        out_shape=jax.ShapeDtypeStruct((SC_TOKEN_BLOCKS, N_ROWS, PEER_TOKENS), F32),
        compiler_params=pltpu.CompilerParams(dimension_semantics=("parallel",)),
        name="peer_gate_sc",
    )(dots_sc, gates, par)
    act = jnp.concatenate([act_tc, act_sc], axis=0)

    nblk_v = nblk - SC_V_BLOCKS
    w_sc = pl.pallas_call(
        _peer_weights_kernel,
        grid=(SC_V_BLOCKS,),
        in_specs=[pl.BlockSpec((1, N_ROWS, PEER_TOKENS), lambda i: (i + nblk_v, 0, 0))],
        out_specs=pl.BlockSpec((PEER_TOKENS, N_SLOTS), lambda i: (i, 0)),
        out_shape=jax.ShapeDtypeStruct((SC_V_BLOCKS * PEER_TOKENS, N_SLOTS), F32),
        compiler_params=pltpu.CompilerParams(dimension_semantics=("parallel",)),
        name="peer_weights_sc",
    )(act)
    y_sc = _sc_v_sums(v_rows, e_tok[nblk_v:].reshape(-1, N_SLOTS), w_sc, x2d, nblk_v * PEER_TOKENS)
    y_tc = pl.pallas_call(
        _peer_v_kernel,
        grid=(nblk_v,),
        in_specs=[smem_block, row_block, tok_block, whole],
        out_specs=tok_block,
        out_shape=jax.ShapeDtypeStruct((nblk_v * PEER_TOKENS, d), F32),
        scratch_shapes=[pltpu.VMEM((N_SPREAD, N_ROWS, LANES), U32),
                        pltpu.VMEM((PEER_TOKENS * SUBLANES, LANES), F32)],
        compiler_params=params,
        name="peer_expert_out",
    )(r8, act, x2d, v_tab)
    return jnp.concatenate([y_tc, y_sc], axis=0)


def _peer_ffn(x2d, norm_g, w_q, sub_keys, u_tabs, v_tabs, layer):
    n_experts, d = u_tabs.shape[1:]
    xp, scores_t, x_words = _peer_scores(x2d, norm_g, w_q, sub_keys)
    u_tiles, u_words = _pack_table(u_tabs, layer)
    v_tiles, _ = _pack_table(v_tabs, layer)
    nblk = x2d.shape[0] // PEER_TOKENS
    nblk_tc = nblk - SC_TOKEN_BLOCKS
    top_sc = _peer_topk(scores_t, nblk_tc, SC_TOKEN_BLOCKS)
    dots_sc = _sc_u_dots(u_words, top_sc[3].reshape(-1, N_SLOTS), x_words, nblk_tc * PEER_TOKENS)
    top_tc = _peer_topk(scores_t, 0, nblk_tc)
    r8, par, gates, e_tok = [jnp.concatenate([a, b], axis=0) for a, b in zip(top_tc, top_sc)]
    return _peer_experts(x2d, xp, r8, e_tok + layer * n_experts, par, gates, dots_sc, v_tabs.reshape(-1, d),
                         u_tiles, v_tiles)


def kernel(x, attn_norm_g, attn_w_qkv, attn_q_g, attn_k_g, attn_rel_bias, attn_w_out, rec_norm_g, rec_w_in, rec_conv_w, rec_conv_b, rec_w_a, rec_b_a, rec_w_x, rec_b_x, rec_lambda, rec_w_out, ffn_norm_g, peer_w_q, peer_sub_keys, peer_u, peer_v):
    b, s, d = x.shape
    depth = ffn_norm_g.shape[0]
    for layer in range(depth):
        j = layer // 2
        if layer % 2 == 0:
            qkv = _qkv_proj(x.reshape(b * s, d), attn_norm_g[j], attn_w_qkv[j], attn_q_g[j], attn_k_g[j])
            x = _attention(x, qkv, attn_rel_bias[j], attn_w_out[j])
        else:
            x = _rglru(x, rec_norm_g[j], rec_w_in[j], rec_conv_w[j], rec_conv_b[j], rec_w_a[j],
                       rec_b_a[j], rec_w_x[j], rec_b_x[j], rec_lambda[j], rec_w_out[j])
        x = _peer_ffn(x.reshape(b * s, d), ffn_norm_g[layer], peer_w_q[layer], peer_sub_keys[layer],
                      peer_u, peer_v, layer).reshape(b, s, d)
    return x
```

```python
import numpy as np
import jax
import jax.numpy as jnp
from jax import lax
from jax.experimental import pallas as pl
from jax.experimental.pallas import tpu as pltpu
from jax.experimental.pallas import tpu_sc as plsc

F32 = jnp.float32
BF16 = jnp.bfloat16
U32 = jnp.uint32
I32 = jnp.int32

EPS = 1e-6
NEG_INF = -1e30

D_MODEL = 1024
CHUNK = 64
ATTN_HEADS = 16
HEAD_DIM = 64
LEFT_CHUNKS = 8
REL_CLIP = 256
LRU_HEADS = 4
LRU_BLOCK = 256
CONV_WIDTH = 4
LRU_C = 8.0
PEER_HEADS = 8
N_KEYS = 128
PEER_TOPK = 16
D_HALF = 128

SUBLANES = 8
LANES = 128

Q_TILE = 4 * CHUNK
N_KEY_BLOCKS = 3
ROW_TILE = 512
PEER_TOKENS = 128
LRU_TILE = 256
N_SLOTS = PEER_HEADS * PEER_TOPK

MIB = 1024 * 1024

SC_CORES = 2
SC_SUBCORES = 16
SC_LANES = 16
SC_GATHER_ROWS = 32
SC_TOKEN_BLOCKS = 138
SC_V_BLOCKS = 110


def _rms_rows(x, g):
    ms = jnp.mean(x * x, axis=-1, keepdims=True)
    return (x * lax.rsqrt(ms + EPS)) * g


def _split_bf16(v):
    hi = v.astype(BF16)
    lo = (v - hi.astype(F32)).astype(BF16)
    return hi, lo


def _dot(a, b):
    return jnp.dot(a, b, preferred_element_type=F32)


def _dot_nt(a, b):
    return lax.dot_general(a, b, (((1,), (1,)), ((), ())), preferred_element_type=F32)


def _qkv_kernel(x_ref, g_ref, w_ref, gain_ref, bsel_ref, bexp_ref, o_ref, xn_ref):
    j = pl.program_id(1)

    @pl.when(j == 0)
    def _():
        xn_ref[...] = _rms_rows(x_ref[...], g_ref[...]).astype(BF16)

    y = _dot(xn_ref[...], w_ref[...])

    @pl.when(j < 2)
    def _():
        hi, lo = _split_bf16(y * y)
        ms = _dot(hi, bsel_ref[...]) + _dot(lo, bsel_ref[...])
        rhi, rlo = _split_bf16(lax.rsqrt(ms + EPS))
        rs = _dot(rhi, bexp_ref[...]) + _dot(rlo, bexp_ref[...])
        o_ref[...] = ((y * rs) * gain_ref[0]).astype(BF16)

    @pl.when(j == 2)
    def _():
        o_ref[...] = y.astype(BF16)


def _qkv_proj(x2d, norm_g, w_qkv, q_g, k_g):
    t, d = x2d.shape
    scale = HEAD_DIM ** -0.5
    gains = jnp.stack([jnp.tile(q_g, ATTN_HEADS) * scale, jnp.tile(k_g, ATTN_HEADS),
                       jnp.ones((d,), F32)]).reshape(3, 1, d)
    head_of_col = np.arange(d) // HEAD_DIM
    bsel = (head_of_col[:, None] == np.arange(LANES)[None, :]).astype(np.float32) / HEAD_DIM
    bexp = (np.arange(LANES)[:, None] == head_of_col[None, :]).astype(np.float32)
    return pl.pallas_call(
        _qkv_kernel,
        grid=(t // ROW_TILE, 3),
        in_specs=[
            pl.BlockSpec((ROW_TILE, d), lambda i, j: (i, 0)),
            pl.BlockSpec((1, d), lambda i, j: (0, 0)),
            pl.BlockSpec((d, d), lambda i, j: (0, j)),
            pl.BlockSpec((1, 1, d), lambda i, j: (j, 0, 0)),
            pl.BlockSpec((d, LANES), lambda i, j: (0, 0)),
            pl.BlockSpec((LANES, d), lambda i, j: (0, 0)),
        ],
        out_specs=pl.BlockSpec((ROW_TILE, d), lambda i, j: (i, j)),
        out_shape=jax.ShapeDtypeStruct((t, 3 * d), BF16),
        scratch_shapes=[pltpu.VMEM((ROW_TILE, d), BF16)],
        compiler_params=pltpu.CompilerParams(
            dimension_semantics=("parallel", "arbitrary"), vmem_limit_bytes=40 * MIB),
        name="qkv_proj",
    )(x2d, norm_g.reshape(1, d), w_qkv.astype(BF16), gains,
      jnp.asarray(bsel, BF16), jnp.asarray(bexp, BF16))


def _attn_kernel(x_ref, q_ref, k0_ref, k1_ref, k2_ref, v0_ref, v1_ref, v2_ref,
                 bias_ref, wout_ref, o_ref, att_ref):
    qt = pl.program_id(1)
    k_refs = (k0_ref, k1_ref, k2_ref)
    v_refs = (v0_ref, v1_ref, v2_ref)
    negs = [jnp.where(qt - (N_KEY_BLOCKS - 1) + kb < 0, NEG_INF, 0.0).astype(F32)
            for kb in range(N_KEY_BLOCKS)]
    lane = lax.broadcasted_iota(I32, (Q_TILE, LANES), 1)
    first_half = lane < HEAD_DIM
    for p in range(ATTN_HEADS // 2):
        cols = slice(p * LANES, (p + 1) * LANES)
        qp = q_ref[0, :, cols]
        outs = []
        for hh in range(2):
            h = 2 * p + hh
            keep = first_half if hh == 0 else jnp.logical_not(first_half)
            qm = jnp.where(keep, qp, jnp.zeros_like(qp))
            s = [_dot_nt(qm, k_refs[kb][0, :, cols]) + bias_ref[kb, h] + negs[kb]
                 for kb in range(N_KEY_BLOCKS)]
            m = jnp.max(s[0], axis=-1, keepdims=True)
            for kb in range(1, N_KEY_BLOCKS):
                m = jnp.maximum(m, jnp.max(s[kb], axis=-1, keepdims=True))
            acc = jnp.zeros((Q_TILE, LANES), F32)
            l = jnp.zeros((Q_TILE, 1), F32)
            for kb in range(N_KEY_BLOCKS):
                e = jnp.exp(s[kb] - m)
                l = l + jnp.sum(e, axis=-1, keepdims=True)
                acc = acc + _dot(e.astype(BF16), v_refs[kb][0, :, cols])
            outs.append(acc / l)
        att_ref[:, cols] = jnp.where(first_half, outs[0], outs[1]).astype(BF16)
    o_ref[0] = x_ref[0] + _dot(att_ref[...], wout_ref[...])


def _band_bias(table):
    qi = np.arange(Q_TILE)[:, None]
    kj = np.arange(Q_TILE)[None, :]
    period = 2 * Q_TILE
    m = np.arange(period)
    delta = np.where(m < Q_TILE, m, m - period)
    blocks = []
    for kb in range(N_KEY_BLOCKS):
        dist = (N_KEY_BLOCKS - 1 - kb) * Q_TILE - delta
        diag = table[:, np.clip(dist, -REL_CLIP, REL_CLIP) + REL_CLIP]
        toep = jnp.tile(diag, (1, Q_TILE))[:, :Q_TILE * (period - 1)]
        toep = toep.reshape(-1, Q_TILE, period - 1)[:, :, :Q_TILE]
        dchunk = (Q_TILE // CHUNK) * (kb - (N_KEY_BLOCKS - 1)) + kj // CHUNK - qi // CHUNK
        valid = (dchunk >= -LEFT_CHUNKS) & (dchunk <= 0)
        blocks.append(jnp.where(valid[None], toep, NEG_INF))
    return jnp.stack(blocks).astype(F32)


def _attention(x, qkv, rel_bias_table, w_out):
    b, s, d = x.shape
    nq = s // Q_TILE
    qkv3 = qkv.reshape(b, s, 3 * d)
    bias = _band_bias(rel_bias_table)

    def kv_spec(kb, col):
        return pl.BlockSpec(
            (1, Q_TILE, d),
            lambda bi, qt: (bi, jnp.maximum(qt - (N_KEY_BLOCKS - 1) + kb, 0), col))

    in_specs = [pl.BlockSpec((1, Q_TILE, d), lambda bi, qt: (bi, qt, 0)),
                pl.BlockSpec((1, Q_TILE, d), lambda bi, qt: (bi, qt, 0))]
    in_specs += [kv_spec(kb, 1) for kb in range(N_KEY_BLOCKS)]
    in_specs += [kv_spec(kb, 2) for kb in range(N_KEY_BLOCKS)]
    in_specs += [pl.BlockSpec(memory_space=pltpu.VMEM), pl.BlockSpec(memory_space=pltpu.VMEM)]
    return pl.pallas_call(
        _attn_kernel,
        grid=(b, nq),
        in_specs=in_specs,
        out_specs=pl.BlockSpec((1, Q_TILE, d), lambda bi, qt: (bi, qt, 0)),
        out_shape=jax.ShapeDtypeStruct((b, s, d), F32),
        scratch_shapes=[pltpu.VMEM((Q_TILE, d), BF16)],
        compiler_params=pltpu.CompilerParams(
            dimension_semantics=("parallel", "parallel"), vmem_limit_bytes=48 * MIB),
        name="band_attention",
    )(x, qkv3, qkv3, qkv3, qkv3, qkv3, qkv3, qkv3, bias, w_out.astype(BF16))


def _rglru_kernel(x_ref, g_ref, win_ref, cw_ref, cb_ref, wa_ref, ba_ref, wx_ref, bx_ref,
                  lam_ref, wout_ref, o_ref, ext_ref, h_ref, a_scr, b_scr, hs_scr):
    ts = x_ref.shape[1]
    w = LRU_HEADS * LRU_BLOCK

    @pl.when(pl.program_id(1) == 0)
    def _():
        ext_ref[...] = jnp.zeros_like(ext_ref)
        h_ref[...] = jnp.zeros_like(h_ref)

    x = x_ref[0]
    xn = _rms_rows(x, g_ref[...]).astype(BF16)
    gu = _dot(xn, win_ref[...])
    gate = gu[:, :w]
    u_raw = gu[:, w:]
    ext_ref[0:SUBLANES, :] = ext_ref[ts:ts + SUBLANES, :]
    ext_ref[SUBLANES:, :] = u_raw
    u = cb_ref[...] + jnp.zeros((ts, w), F32)
    for k in range(CONV_WIDTH):
        off = SUBLANES - (CONV_WIDTH - 1) + k
        u = u + cw_ref[k:k + 1, :] * ext_ref[off:off + ts, :]
    ub = u.astype(BF16)

    def block_diag(wref):
        return jnp.concatenate(
            [_dot(ub[:, hh * LRU_BLOCK:(hh + 1) * LRU_BLOCK], wref[hh]) for hh in range(LRU_HEADS)],
            axis=1)

    r = jax.nn.sigmoid(block_diag(wa_ref) + ba_ref[...])
    i = jax.nn.sigmoid(block_diag(wx_ref) + bx_ref[...])
    z = -lam_ref[...]
    softplus = jnp.maximum(z, 0.0) + jnp.log1p(jnp.exp(-jnp.abs(z)))
    log_a = (-LRU_C) * r * softplus
    a_scr[...] = jnp.exp(log_a)
    th = jnp.tanh(log_a)
    b_scr[...] = jnp.sqrt(2.0 * th / (th - 1.0)) * (i * u)

    def step(t, h):
        h = a_scr[pl.ds(t, 1), :] * h + b_scr[pl.ds(t, 1), :]
        hs_scr[pl.ds(t, 1), :] = h
        return h

    h_ref[0:1, :] = lax.fori_loop(0, ts, step, h_ref[0:1, :], unroll=8)
    y = jax.nn.gelu(gate) * hs_scr[...]
    o_ref[0] = x + _dot(y.astype(BF16), wout_ref[...])


def _rglru(x, norm_g, w_in, conv_w, conv_b, w_a, b_a, w_x, b_x, lam, w_out):
    b, s, d = x.shape
    w = LRU_HEADS * LRU_BLOCK
    row = lambda v: v.reshape(1, -1)
    whole = pl.BlockSpec(memory_space=pltpu.VMEM)
    return pl.pallas_call(
        _rglru_kernel,
        grid=(b, s // LRU_TILE),
        in_specs=[pl.BlockSpec((1, LRU_TILE, d), lambda bi, si: (bi, si, 0))] + [whole] * 10,
        out_specs=pl.BlockSpec((1, LRU_TILE, d), lambda bi, si: (bi, si, 0)),
        out_shape=jax.ShapeDtypeStruct((b, s, d), F32),
        scratch_shapes=[pltpu.VMEM((LRU_TILE + SUBLANES, w), F32), pltpu.VMEM((SUBLANES, w), F32),
                        pltpu.VMEM((LRU_TILE, w), F32), pltpu.VMEM((LRU_TILE, w), F32),
                        pltpu.VMEM((LRU_TILE, w), F32)],
        compiler_params=pltpu.CompilerParams(
            dimension_semantics=("arbitrary", "arbitrary"), vmem_limit_bytes=48 * MIB),
        name="rglru_mixer",
    )(x, row(norm_g), w_in.astype(BF16), conv_w, row(conv_b), w_a.astype(BF16), row(b_a),
      w_x.astype(BF16), row(b_x), row(lam), w_out.astype(BF16))


def _peer_scores_kernel(x_ref, g_ref, wq_ref, sk_ref, xp_ref, st_ref, xw_ref):
    xb = _rms_rows(x_ref[...], g_ref[...]).astype(BF16)
    words = _bf16_pair_words(xb)
    xw_ref[...] = words
    tm = words.shape[0]
    for c in range(SUBLANES):
        piece = words[:, (c % (SUBLANES // 2)) * LANES:(c % (SUBLANES // 2) + 1) * LANES]
        xp_ref[pl.ds(c, tm, stride=SUBLANES), :] = piece
    q = _dot(xb, wq_ref[...]).astype(BF16)
    for hp in range(2 * PEER_HEADS):
        st_ref[hp] = _dot_nt(sk_ref[hp % 2], q[:, hp * D_HALF:(hp + 1) * D_HALF])


def _peer_scores(x2d, norm_g, w_q, sub_keys):
    t, d = x2d.shape
    nq = w_q.shape[1]
    whole = pl.BlockSpec(memory_space=pltpu.VMEM)
    return pl.pallas_call(
        _peer_scores_kernel,
        grid=(t // ROW_TILE,),
        in_specs=[pl.BlockSpec((ROW_TILE, d), lambda i: (i, 0)), whole, whole, whole],
        out_specs=[pl.BlockSpec((ROW_TILE * SUBLANES, LANES), lambda i: (i, 0)),
                   pl.BlockSpec((2 * PEER_HEADS, N_KEYS, ROW_TILE), lambda i: (0, 0, i)),
                   pl.BlockSpec((ROW_TILE, d // 2), lambda i: (i, 0))],
        out_shape=[jax.ShapeDtypeStruct((t * SUBLANES, LANES), U32),
                   jax.ShapeDtypeStruct((2 * PEER_HEADS, N_KEYS, t), F32),
                   jax.ShapeDtypeStruct((t, d // 2), U32)],
        compiler_params=pltpu.CompilerParams(
            dimension_semantics=("parallel",), vmem_limit_bytes=48 * MIB),
        name="peer_scores",
    )(x2d, norm_g.reshape(1, d), w_q.astype(BF16), sub_keys.astype(BF16))


def _candidate_blocks():
    blocks = []
    for i in range(PEER_TOPK // 2):
        nj = PEER_TOPK // (i + 1)
        for j0 in range(0, nj, SUBLANES):
            blocks.append(("row", i, j0, min(SUBLANES, nj - j0)))
    blocks.append(("col", PEER_TOPK // 2, PEER_TOPK // 2))
    return blocks


def _peer_topk_kernel(st_ref, r8_ref, par_ref, g_ref, e_ref, r8t_ref, et_ref):
    tk = st_ref.shape[2]
    key_id = lax.broadcasted_iota(I32, (N_KEYS, tk), 0).astype(F32)
    row16 = lax.broadcasted_iota(I32, (PEER_TOPK, tk), 0)
    row8 = lax.broadcasted_iota(I32, (SUBLANES, tk), 0)
    row8_f = row8.astype(F32)
    neg_inf = jnp.float32(-jnp.inf)
    blocks = _candidate_blocks()
    big = jnp.float32(PEER_TOPK * PEER_TOPK)

    def top16(x):
        vals = jnp.zeros((PEER_TOPK, tk), F32)
        idxs = jnp.zeros((PEER_TOPK, tk), F32)
        for k in range(PEER_TOPK):
            m = jnp.max(x, axis=0, keepdims=True)
            idx = jnp.min(jnp.where(x == m, key_id, float(N_KEYS)), axis=0, keepdims=True)
            x = jnp.where(key_id == idx, neg_inf, x)
            vals = jnp.where(row16 == k, m, vals)
            idxs = jnp.where(row16 == k, idx, idxs)
        return vals, idxs

    for h in range(PEER_HEADS):
        s0, i0 = top16(st_ref[2 * h])
        s1, i1 = top16(st_ref[2 * h + 1])
        cand, flat, expert = [], [], []
        for blk in blocks:
            if blk[0] == "row":
                _, i, j0, n = blk
                c = s0[i:i + 1, :] + s1[j0:j0 + SUBLANES, :]
                f = float(PEER_TOPK * i + j0) + row8_f
                e = i0[i:i + 1, :] * float(N_KEYS) + i1[j0:j0 + SUBLANES, :]
            else:
                _, i_start, n = blk
                c = s0[i_start:i_start + SUBLANES, :] + s1[0:1, :]
                f = float(PEER_TOPK) * (float(i_start) + row8_f)
                e = i0[i_start:i_start + SUBLANES, :] * float(N_KEYS) + i1[0:1, :]
            cand.append(jnp.where(row8 < n, c, neg_inf))
            flat.append(f)
            expert.append(e)
        cand = jnp.concatenate(cand, axis=0)
        flat = jnp.concatenate(flat, axis=0)
        expert = jnp.concatenate(expert, axis=0)
        best_s = jnp.zeros((PEER_TOPK, tk), F32)
        best_e = jnp.zeros((PEER_TOPK, tk), F32)
        for k in range(PEER_TOPK):
            m = jnp.max(cand, axis=0, keepdims=True)
            fidx = jnp.min(jnp.where(cand == m, flat, big), axis=0, keepdims=True)
            pick = flat == fidx
            e_sel = jnp.max(jnp.where(pick, expert, -1.0), axis=0, keepdims=True)
            cand = jnp.where(pick, neg_inf, cand)
            best_s = jnp.where(row16 == k, m, best_s)
            best_e = jnp.where(row16 == k, e_sel, best_e)
        ex = jnp.exp(best_s - best_s[0:1, :])
        gate = ex / jnp.sum(ex, axis=0, keepdims=True)
        rows = slice(h * PEER_TOPK, (h + 1) * PEER_TOPK)
        best_i = best_e.astype(I32)
        r8t_ref[rows, :] = (best_i >> 1) * SUBLANES
        et_ref[rows, :] = best_i
        par_ref[0, rows, :] = best_i & 1
        g_ref[0, rows, :] = gate
    r8_ref[0] = r8t_ref[...].T
    e_ref[0] = et_ref[...].T


def _peer_topk(scores_t, first_block, nblk):
    slot_major = pl.BlockSpec((1, N_SLOTS, PEER_TOKENS), lambda i: (i, 0, 0))
    token_major = pl.BlockSpec((1, PEER_TOKENS, N_SLOTS), lambda i: (i, 0, 0))
    return pl.pallas_call(
        _peer_topk_kernel,
        grid=(nblk,),
        in_specs=[pl.BlockSpec((2 * PEER_HEADS, N_KEYS, PEER_TOKENS), lambda i: (0, 0, i + first_block))],
        out_specs=[token_major, slot_major, slot_major, token_major],
        out_shape=[jax.ShapeDtypeStruct((nblk, PEER_TOKENS, N_SLOTS), I32),
                   jax.ShapeDtypeStruct((nblk, N_SLOTS, PEER_TOKENS), I32),
                   jax.ShapeDtypeStruct((nblk, N_SLOTS, PEER_TOKENS), F32),
                   jax.ShapeDtypeStruct((nblk, PEER_TOKENS, N_SLOTS), I32)],
        scratch_shapes=[pltpu.VMEM((N_SLOTS, PEER_TOKENS), I32), pltpu.VMEM((N_SLOTS, PEER_TOKENS), I32)],
        compiler_params=pltpu.CompilerParams(dimension_semantics=("parallel",)),
        name="peer_topk",
    )(scores_t)


def _bf16_pair_words(x):
    bits = lax.bitcast_convert_type(x.astype(BF16).astype(F32), U32)
    half = bits.shape[1] // 2
    return (bits[:, :half] >> 16) | (bits[:, half:] & jnp.uint32(0xFFFF0000))


def _pack_kernel(t_ref, o_ref, w_ref):
    words = _bf16_pair_words(t_ref[0])
    w_ref[...] = words
    rows = words.shape[0]
    n = words.shape[1] // LANES
    for s in range(n):
        o_ref[pl.ds(s, rows, stride=n), :] = words[:, s * LANES:(s + 1) * LANES]


def _pack_table(tabs, layer):
    _, e, d = tabs.shape
    n = d // (2 * LANES)
    return pl.pallas_call(
        _pack_kernel,
        grid=(e // ROW_TILE,),
        in_specs=[pl.BlockSpec((1, ROW_TILE, d), lambda i: (layer, i, 0))],
        out_specs=[pl.BlockSpec((ROW_TILE * n, LANES), lambda i: (i, 0)),
                   pl.BlockSpec((ROW_TILE, d // 2), lambda i: (i, 0))],
        out_shape=[jax.ShapeDtypeStruct((e * n, LANES), U32), jax.ShapeDtypeStruct((e, d // 2), U32)],
        compiler_params=pltpu.CompilerParams(dimension_semantics=("parallel",)),
        name="peer_pack_table",
    )(tabs)


def _unpack_words(w):
    lo = lax.bitcast_convert_type(w << 16, F32)
    hi = lax.bitcast_convert_type(w & jnp.uint32(0xFFFF0000), F32)
    return lo, hi


def _as_bf16(words):
    return pltpu.bitcast(words, BF16)


def _as_words(packed):
    return pltpu.bitcast(packed, U32)


def _merge_packed(a, b, shift, mask):
    ta = a + _as_bf16(pltpu.roll(_as_words(a), shift, 0))
    tb = b + _as_bf16(pltpu.roll(_as_words(b), SUBLANES - shift, 0))
    return jnp.where(mask, _as_words(ta), _as_words(tb))


def _sublane_iota():
    return lax.broadcasted_iota(I32, (SUBLANES, LANES), 0)


def _merge(a, b, shift, mask):
    ta = a + pltpu.roll(a, shift, 0)
    tb = b + pltpu.roll(b, SUBLANES - shift, 0)
    return jnp.where(mask, ta, tb)


GROUP = SUBLANES // 2
N_GROUPS = N_SLOTS // GROUP
N_ROWS = 2 * N_SLOTS
N_SPREAD = 4
SPREAD_AHEAD = 2


def _dup_halves(v, low):
    r = pltpu.roll(v, GROUP, 0)
    return jnp.where(low, v, r), jnp.where(low, r, v)


def _peer_u_kernel(r8_ref, xp_ref, g_ref, par_ref, tab_ref, a_ref, s_ref, part_ref):
    tb = g_ref.shape[2]
    sub = _sublane_iota()
    m2 = (sub & 2) != 0
    lane_t = lax.broadcasted_iota(I32, (SUBLANES, tb), 1)
    n_part = 2 * N_GROUPS

    def products(t, slot):
        xb = _as_bf16(xp_ref[pl.ds(pl.multiple_of(t * SUBLANES, SUBLANES), SUBLANES), :])
        ids = r8_ref.at[0, t]
        for grp in range(N_GROUPS):
            prods = []
            for i in range(GROUP):
                r8 = pl.multiple_of(ids[grp * GROUP + i], SUBLANES)
                prods.append(_as_bf16(tab_ref[pl.ds(r8, SUBLANES), :]) * xb)
            halves = (_merge_packed(prods[3], prods[1], 2, m2), _merge_packed(prods[2], prods[0], 2, m2))
            for k in range(2):
                lo, hi = _unpack_words(halves[k])
                part_ref[slot, 2 * grp + k] = lo + hi

    def reduce_lanes(t, slot):
        for n in range(n_part):
            col = jnp.sum(part_ref[slot, n], axis=1, keepdims=True)
            rows = slice(n * SUBLANES, (n + 1) * SUBLANES)
            s_ref[rows, :] = jnp.where(lane_t == t, col, s_ref[rows, :])

    def token_pair(i, carry):
        t = 2 * i
        reduce_lanes(t - 2, 0)
        reduce_lanes(t - 1, 1)
        products(t, 0)
        products(t + 1, 1)
        return carry

    s_ref[...] = jnp.zeros_like(s_ref)
    part_ref[...] = jnp.zeros_like(part_ref)
    lax.fori_loop(0, tb // 2, token_pair, 0)
    reduce_lanes(tb - 2, 0)
    reduce_lanes(tb - 1, 1)

    odd_t = (lax.broadcasted_iota(I32, (SUBLANES, tb), 0) & 1) != 0
    _gate_rows(lambda grp: _merge(s_ref[2 * grp * SUBLANES:(2 * grp + 1) * SUBLANES, :],
                                  s_ref[(2 * grp + 1) * SUBLANES:(2 * grp + 2) * SUBLANES, :], 1, odd_t),
               g_ref, par_ref, a_ref)


def _gate_rows(dots_of_group, g_ref, par_ref, a_ref):
    tb = g_ref.shape[2]
    low_t = lax.broadcasted_iota(I32, (SUBLANES, tb), 0) < GROUP
    want = jnp.where(low_t, 0, 1)
    for v in range(N_SLOTS // SUBLANES):
        src = slice(v * SUBLANES, (v + 1) * SUBLANES)
        gates = _dup_halves(g_ref[0, src, :], low_t)
        pars = _dup_halves(par_ref[0, src, :], low_t)
        for half in range(2):
            grp = 2 * v + half
            act = gates[half] * jax.nn.gelu(dots_of_group(grp))
            rows = slice(grp * SUBLANES, (grp + 1) * SUBLANES)
            a_ref[0, rows, :] = jnp.where(pars[half] == want, act, 0.0)


SEM_IDS, SEM_IN, SEM_OUT, N_SEMS = 2, 3, 5, 7


def _sc_token_loop(tab_hbm, e_hbm, idx_v, rows_v, sems, per_worker, token_in, out_buf, out_row, chunk_rows):
    wid = lax.axis_index("s") * SC_CORES + lax.axis_index("c")
    n_chunks = N_SLOTS // SC_GATHER_ROWS
    first = wid * per_worker
    last = first + per_worker - 1

    def gather(slot, chunk):
        ids = idx_v.at[slot, pl.ds(chunk * SC_GATHER_ROWS, SC_GATHER_ROWS)]
        return pltpu.make_async_copy(tab_hbm.at[ids], rows_v.at[chunk % 2], sems.at[chunk % 2])

    def ids_copy(tok, slot):
        return pltpu.make_async_copy(e_hbm.at[tok], idx_v.at[slot], sems.at[SEM_IDS])

    def in_copies(tok, slot):
        return [pltpu.make_async_copy(row_of(tok), buf.at[slot], sems.at[SEM_IN + slot]) for row_of, buf in token_in]

    def out_copy(tok, slot):
        return pltpu.make_async_copy(out_buf.at[slot], out_row(tok), sems.at[SEM_OUT + slot])

    def one_token(k, tok, slot, next_tok):
        ids_copy(next_tok, 1 - slot).start()
        for c in in_copies(next_tok, 1 - slot):
            c.start()
        for c in in_copies(tok, slot):
            c.wait()

        @pl.when(k > 0)
        def _():
            out_copy(tok, slot).wait()

        for chunk in range(n_chunks):
            if chunk + 1 < n_chunks:
                gather(slot, chunk + 1).start()
            else:
                ids_copy(next_tok, 1 - slot).wait()
                gather(1 - slot, 0).start()
            gather(slot, chunk).wait()
            chunk_rows(chunk, rows_v.at[chunk % 2], slot)
        out_copy(tok, slot).start()

    def token_pair(k, carry):
        tok = first + 2 * k
        one_token(k, tok, 0, tok + 1)
        one_token(k, tok + 1, 1, jnp.minimum(tok + 2, last))
        return carry

    pltpu.sync_copy(e_hbm.at[first], idx_v.at[0])
    gather(0, 0).start()
    for c in in_copies(first, 0):
        c.start()
    lax.fori_loop(0, per_worker // 2, token_pair, 0)
    gather(0, 0).wait()
    for c in in_copies(last, 0):
        c.wait()
    out_copy(last, 0).wait()
    out_copy(last, 1).wait()


def _sc_scratch(row_len, dtype):
    return [pltpu.VMEM((2, N_SLOTS), I32), pltpu.VMEM((2, SC_GATHER_ROWS, row_len), dtype),
            pltpu.SemaphoreType.DMA((N_SEMS,))]


def _sc_u_dots(u_words, e_tok, x_words, first_token):
    nw = x_words.shape[1]
    n_tokens = e_tok.shape[0]
    per_worker = n_tokens // (SC_CORES * SC_SUBCORES)
    n_half = nw // SC_LANES // 2
    fold = 4
    mesh = plsc.VectorSubcoreMesh(core_axis_name="c", subcore_axis_name="s")

    def body(u_hbm, e_hbm, x_hbm, o_hbm, idx_v, rows_v, sems, x_v, out_v):
        lane = lax.iota(I32, SC_LANES)

        def chunk_rows(chunk, rows, slot):
            for part in range(SC_GATHER_ROWS // SC_LANES):
                total = jnp.zeros((SC_LANES,), F32)
                for half in range(2):
                    base = half * n_half * SC_LANES
                    xs = [plsc.bitcast(x_v[slot, pl.ds(base + c * SC_LANES, SC_LANES)], BF16) for c in range(n_half)]

                    def row(r, vec):
                        rr = part * SC_LANES + r
                        acc = jnp.zeros((SC_LANES,), F32)
                        for g in range(n_half // fold):
                            p = [plsc.bitcast(rows[rr, pl.ds(base + (fold * g + k) * SC_LANES, SC_LANES)], BF16)
                                 * xs[fold * g + k] for k in range(fold)]
                            w32 = plsc.bitcast((p[0] + p[1]) + (p[2] + p[3]), U32)
                            acc = acc + plsc.bitcast(w32 << 16, F32) + plsc.bitcast(w32 & jnp.uint32(0xFFFF0000), F32)
                        return jnp.where(lane == r, jnp.sum(acc), vec)

                    total = total + lax.fori_loop(0, SC_LANES, row, jnp.zeros((SC_LANES,), F32))
                out_v[slot, pl.ds(chunk * SC_GATHER_ROWS + part * SC_LANES, SC_LANES)] = total

        _sc_token_loop(u_hbm, e_hbm, idx_v, rows_v, sems, per_worker,
                       [(lambda tok: x_hbm.at[first_token + tok], x_v)], out_v, lambda tok: o_hbm.at[tok], chunk_rows)

    return pl.kernel(
        body,
        out_type=jax.ShapeDtypeStruct((n_tokens, N_SLOTS), F32),
        mesh=mesh,
        scratch_types=_sc_scratch(nw, U32) + [pltpu.VMEM((2, nw), U32), pltpu.VMEM((2, N_SLOTS), F32)],
        compiler_params=pltpu.CompilerParams(needs_layout_passes=False),
        name="peer_expert_in_sc",
    )(u_words, e_tok, x_words)


def _peer_gate_kernel(s_ref, g_ref, par_ref, a_ref):
    dots = s_ref[...].T
    low_t = lax.broadcasted_iota(I32, (SUBLANES, dots.shape[1]), 0) < GROUP
    _gate_rows(lambda grp: _dup_halves(dots[(grp // 2) * SUBLANES:(grp // 2 + 1) * SUBLANES, :], low_t)[grp % 2],
               g_ref, par_ref, a_ref)


def _peer_weights_kernel(a_ref, w_ref):
    low_t = lax.broadcasted_iota(I32, (SUBLANES, a_ref.shape[2]), 0) < GROUP
    tiles = []
    for v in range(N_SLOTS // SUBLANES):
        first = a_ref[0, 2 * v * SUBLANES:(2 * v + 1) * SUBLANES, :]
        second = a_ref[0, (2 * v + 1) * SUBLANES:(2 * v + 2) * SUBLANES, :]
        first = first + pltpu.roll(first, GROUP, 0)
        second = second + pltpu.roll(second, GROUP, 0)
        tiles.append(jnp.where(low_t, first, second))
    w_ref[...] = jnp.concatenate(tiles, axis=0).T


def _sc_v_sums(v_words, e_tok, w_tok, x2d, first_token):
    d = x2d.shape[1]
    nw = v_words.shape[1]
    n_tokens = e_tok.shape[0]
    per_worker = n_tokens // (SC_CORES * SC_SUBCORES)
    passes = 4
    fold = 4
    n_acc = nw // SC_LANES // passes
    mesh = plsc.VectorSubcoreMesh(core_axis_name="c", subcore_axis_name="s")

    def body(v_hbm, e_hbm, w_hbm, x_hbm, o_hbm, idx_v, rows_v, sems, w_v, x_v, out_v):
        def chunk_rows(chunk, rows, slot):
            src = x_v if chunk == 0 else out_v
            for ps in range(passes):
                base = ps * n_acc * SC_LANES

                def row_group(q, acc):
                    ws = []
                    for k in range(fold):
                        pos = jnp.full((SC_LANES,), chunk * SC_GATHER_ROWS + fold * q + k, I32)
                        w = plsc.load_gather(w_v.at[slot], [pos])
                        ws.append(plsc.pack(w, w, format=plsc.PackFormat.INTERLEAVED, preferred_element_type=BF16))
                    lo, hi = list(acc[:n_acc]), list(acc[n_acc:])
                    for c in range(n_acc):
                        p = [plsc.bitcast(rows[fold * q + k, pl.ds(base + c * SC_LANES, SC_LANES)], BF16) * ws[k]
                             for k in range(fold)]
                        w32 = plsc.bitcast((p[0] + p[1]) + (p[2] + p[3]), U32)
                        lo[c] = lo[c] + plsc.bitcast(w32 << 16, F32)
                        hi[c] = hi[c] + plsc.bitcast(w32 & jnp.uint32(0xFFFF0000), F32)
                    return tuple(lo + hi)

                acc0 = tuple([src[slot, pl.ds(base + c * SC_LANES, SC_LANES)] for c in range(n_acc)]
                             + [src[slot, pl.ds(nw + base + c * SC_LANES, SC_LANES)] for c in range(n_acc)])
                acc = lax.fori_loop(0, SC_GATHER_ROWS // fold, row_group, acc0)
                for c in range(n_acc):
                    out_v[slot, pl.ds(base + c * SC_LANES, SC_LANES)] = acc[c]
                    out_v[slot, pl.ds(nw + base + c * SC_LANES, SC_LANES)] = acc[n_acc + c]

        _sc_token_loop(v_hbm, e_hbm, idx_v, rows_v, sems, per_worker,
                       [(lambda tok: w_hbm.at[tok], w_v), (lambda tok: x_hbm.at[first_token + tok], x_v)],
                       out_v, lambda tok: o_hbm.at[tok], chunk_rows)

    return pl.kernel(
        body,
        out_type=jax.ShapeDtypeStruct((n_tokens, d), F32),
        mesh=mesh,
        scratch_types=_sc_scratch(nw, U32) + [pltpu.VMEM((2, N_SLOTS), F32), pltpu.VMEM((2, d), F32),
                                             pltpu.VMEM((2, d), F32)],
        compiler_params=pltpu.CompilerParams(needs_layout_passes=False),
        name="peer_expert_out_sc",
    )(v_words, e_tok, w_tok, x2d)


def _token_tiles(ref, xt_ref, to_tiles):
    tb, d = ref.shape
    for q in range(tb // SUBLANES):
        for c in range(d // LANES):
            rows = slice(q * SUBLANES, (q + 1) * SUBLANES)
            cols = slice(c * LANES, (c + 1) * LANES)
            strided = pl.ds(q * SUBLANES * SUBLANES + c, SUBLANES, stride=SUBLANES)
            if to_tiles:
                xt_ref[strided, :] = ref[rows, cols]
            else:
                ref[rows, cols] = xt_ref[strided, :]


def _peer_v_kernel(r8_ref, a_ref, x_ref, tab_ref, o_ref, bc_ref, xt_ref):
    tb = x_ref.shape[0]
    sub = _sublane_iota()
    low = sub < GROUP
    lane_t = lax.broadcasted_iota(I32, (SUBLANES, tb), 1)
    _token_tiles(x_ref, xt_ref, True)

    def spread(t, buf):
        for k in range(N_ROWS // SUBLANES):
            rows = slice(k * SUBLANES, (k + 1) * SUBLANES)
            col = jnp.sum(jnp.where(lane_t == t, a_ref[0, rows, :], 0.0), axis=1, keepdims=True)
            bits = lax.bitcast_convert_type(col.astype(BF16).astype(F32), U32)
            bc_ref[buf, rows, :] = jnp.broadcast_to(bits | (bits >> 16), (SUBLANES, LANES))

    def accumulate(t, buf):
        acc_lo = jnp.zeros((SUBLANES, LANES), F32)
        acc_hi = jnp.zeros((SUBLANES, LANES), F32)
        ids = r8_ref.at[0, t]
        for grp in range(N_GROUPS):
            prods = []
            for i in range(GROUP):
                row = grp * SUBLANES + i
                r8 = pl.multiple_of(ids[grp * GROUP + i], SUBLANES)
                am = jnp.where(low, bc_ref[buf, row:row + 1, :], bc_ref[buf, row + GROUP:row + GROUP + 1, :])
                prods.append(_as_bf16(tab_ref[pl.ds(r8, SUBLANES), :]) * _as_bf16(am))
            lo, hi = _unpack_words(_as_words((prods[0] + prods[1]) + (prods[2] + prods[3])))
            acc_lo = acc_lo + lo
            acc_hi = acc_hi + hi
        acc_lo = acc_lo + pltpu.roll(acc_lo, GROUP, 0)
        acc_hi = acc_hi + pltpu.roll(acc_hi, GROUP, 0)
        tile = pl.ds(pl.multiple_of(t * SUBLANES, SUBLANES), SUBLANES)
        xt_ref[tile, :] = xt_ref[tile, :] + jnp.where(low, acc_lo, acc_hi)

    def token_group(i, carry):
        t = N_SPREAD * i
        for k in range(N_SPREAD):
            spread(jnp.minimum(t + k + SPREAD_AHEAD, tb - 1), (k + SPREAD_AHEAD) % N_SPREAD)
            accumulate(t + k, k)
        return carry

    for k in range(SPREAD_AHEAD):
        spread(k, k)
    lax.fori_loop(0, tb // N_SPREAD, token_group, 0)
    _token_tiles(o_ref, xt_ref, False)


def _peer_experts(x2d, xp, r8, e_tok, par, gates, dots_sc, v_words, u_tab, v_tab):
    t, d = x2d.shape
    nblk = t // PEER_TOKENS
    nblk_tc = nblk - SC_TOKEN_BLOCKS
    smem_block = pl.BlockSpec((1, PEER_TOKENS, N_SLOTS), lambda i: (i, 0, 0), memory_space=pltpu.SMEM)
    slot_block = pl.BlockSpec((1, N_SLOTS, PEER_TOKENS), lambda i: (i, 0, 0))
    row_block = pl.BlockSpec((1, N_ROWS, PEER_TOKENS), lambda i: (i, 0, 0))
    tile_block = pl.BlockSpec((PEER_TOKENS * SUBLANES, LANES), lambda i: (i, 0))
    tok_block = pl.BlockSpec((PEER_TOKENS, d), lambda i: (i, 0))
    whole = pl.BlockSpec(memory_space=pltpu.VMEM)
    params = pltpu.CompilerParams(dimension_semantics=("parallel",), vmem_limit_bytes=56 * MIB)

    act_tc = pl.pallas_call(
        _peer_u_kernel,
        grid=(nblk_tc,),
        in_specs=[smem_block, tile_block, slot_block, slot_block, whole],
        out_specs=row_block,
        out_shape=jax.ShapeDtypeStruct((nblk_tc, N_ROWS, PEER_TOKENS), F32),
        scratch_shapes=[pltpu.VMEM((2 * N_ROWS, PEER_TOKENS), F32),
                        pltpu.VMEM((2, 2 * N_GROUPS, SUBLANES, LANES), F32)],
        compiler_params=params,
        name="peer_expert_in",
    )(r8, xp, gates, par, u_tab)
    shifted = pl.BlockSpec((1, N_SLOTS, PEER_TOKENS), lambda i: (i + nblk_tc, 0, 0))
    act_sc = pl.pallas_call(
        _peer_gate_kernel,
        grid=(SC_TOKEN_BLOCKS,),
        in_specs=[pl.BlockSpec((PEER_TOKENS, N_SLOTS), lambda i: (i, 0)), shifted, shifted],
        out_specs=row_block,
        out_shape=jax.ShapeDtypeStruct((SC_TOKEN_BLOCKS, N_ROWS, PEER_TOKENS), F32),
        compiler_params=pltpu.CompilerParams(dimension_semantics=("parallel",)),
        name="peer_gate_sc",
    )(dots_sc, gates, par)
    act = jnp.concatenate([act_tc, act_sc], axis=0)

    nblk_v = nblk - SC_V_BLOCKS
    w_sc = pl.pallas_call(
        _peer_weights_kernel,
        grid=(SC_V_BLOCKS,),
        in_specs=[pl.BlockSpec((1, N_ROWS, PEER_TOKENS), lambda i: (i + nblk_v, 0, 0))],
        out_specs=pl.BlockSpec((PEER_TOKENS, N_SLOTS), lambda i: (i, 0)),
        out_shape=jax.ShapeDtypeStruct((SC_V_BLOCKS * PEER_TOKENS, N_SLOTS), F32),
        compiler_params=pltpu.CompilerParams(dimension_semantics=("parallel",)),
        name="peer_weights_sc",
    )(act)
    y_sc = _sc_v_sums(v_words, e_tok[nblk_v:].reshape(-1, N_SLOTS), w_sc, x2d, nblk_v * PEER_TOKENS)
    y_tc = pl.pallas_call(
        _peer_v_kernel,
        grid=(nblk_v,),
        in_specs=[smem_block, row_block, tok_block, whole],
        out_specs=tok_block,
        out_shape=jax.ShapeDtypeStruct((nblk_v * PEER_TOKENS, d), F32),
        scratch_shapes=[pltpu.VMEM((N_SPREAD, N_ROWS, LANES), U32),
                        pltpu.VMEM((PEER_TOKENS * SUBLANES, LANES), F32)],
        compiler_params=params,
        name="peer_expert_out",
    )(r8, act, x2d, v_tab)
    return jnp.concatenate([y_tc, y_sc], axis=0)


def _peer_ffn(x2d, norm_g, w_q, sub_keys, u_tabs, v_tabs, layer):
    xp, scores_t, x_words = _peer_scores(x2d, norm_g, w_q, sub_keys)
    u_tiles, u_words = _pack_table(u_tabs, layer)
    v_tiles, v_words = _pack_table(v_tabs, layer)
    nblk = x2d.shape[0] // PEER_TOKENS
    nblk_tc = nblk - SC_TOKEN_BLOCKS
    top_sc = _peer_topk(scores_t, nblk_tc, SC_TOKEN_BLOCKS)
    dots_sc = _sc_u_dots(u_words, top_sc[3].reshape(-1, N_SLOTS), x_words, nblk_tc * PEER_TOKENS)
    top_tc = _peer_topk(scores_t, 0, nblk_tc)
    r8, par, gates, e_tok = [jnp.concatenate([a, b], axis=0) for a, b in zip(top_tc, top_sc)]
    return _peer_experts(x2d, xp, r8, e_tok, par, gates, dots_sc, v_words, u_tiles, v_tiles)


def kernel(x, attn_norm_g, attn_w_qkv, attn_q_g, attn_k_g, attn_rel_bias, attn_w_out, rec_norm_g, rec_w_in, rec_conv_w, rec_conv_b, rec_w_a, rec_b_a, rec_w_x, rec_b_x, rec_lambda, rec_w_out, ffn_norm_g, peer_w_q, peer_sub_keys, peer_u, peer_v):
    b, s, d = x.shape
    depth = ffn_norm_g.shape[0]
    for layer in range(depth):
        j = layer // 2
        if layer % 2 == 0:
            qkv = _qkv_proj(x.reshape(b * s, d), attn_norm_g[j], attn_w_qkv[j], attn_q_g[j], attn_k_g[j])
            x = _attention(x, qkv, attn_rel_bias[j], attn_w_out[j])
        else:
            x = _rglru(x, rec_norm_g[j], rec_w_in[j], rec_conv_w[j], rec_conv_b[j], rec_w_a[j],
                       rec_b_a[j], rec_w_x[j], rec_b_x[j], rec_lambda[j], rec_w_out[j])
        x = _peer_ffn(x.reshape(b * s, d), ffn_norm_g[layer], peer_w_q[layer], peer_sub_keys[layer],
                      peer_u, peer_v, layer).reshape(b, s, d)
    return x
```

```python
import numpy as np
import jax
import jax.numpy as jnp
from jax import lax
from jax.experimental import pallas as pl
from jax.experimental.pallas import tpu as pltpu
from jax.experimental.pallas import tpu_sc as plsc

F32 = jnp.float32
BF16 = jnp.bfloat16
U32 = jnp.uint32
I32 = jnp.int32

EPS = 1e-6
NEG_INF = -1e30

D_MODEL = 1024
CHUNK = 64
ATTN_HEADS = 16
HEAD_DIM = 64
LEFT_CHUNKS = 8
REL_CLIP = 256
LRU_HEADS = 4
LRU_BLOCK = 256
CONV_WIDTH = 4
LRU_C = 8.0
PEER_HEADS = 8
N_KEYS = 128
PEER_TOPK = 16
D_HALF = 128

SUBLANES = 8
LANES = 128

Q_TILE = 4 * CHUNK
N_KEY_BLOCKS = 3
ROW_TILE = 512
PEER_TOKENS = 128
LRU_TILE = 256
N_SLOTS = PEER_HEADS * PEER_TOPK

MIB = 1024 * 1024

SC_CORES = 2
SC_SUBCORES = 16
SC_LANES = 16
SC_GATHER_ROWS = 32
SC_TOKEN_BLOCKS = 140
SC_V_BLOCKS = 114


def _rms_rows(x, g):
    ms = jnp.mean(x * x, axis=-1, keepdims=True)
    return (x * lax.rsqrt(ms + EPS)) * g


def _split_bf16(v):
    hi = v.astype(BF16)
    lo = (v - hi.astype(F32)).astype(BF16)
    return hi, lo


def _dot(a, b):
    return jnp.dot(a, b, preferred_element_type=F32)


def _dot_nt(a, b):
    return lax.dot_general(a, b, (((1,), (1,)), ((), ())), preferred_element_type=F32)


def _qkv_kernel(x_ref, g_ref, w_ref, gain_ref, bsel_ref, bexp_ref, o_ref, xn_ref):
    j = pl.program_id(1)

    @pl.when(j == 0)
    def _():
        xn_ref[...] = _rms_rows(x_ref[...], g_ref[...]).astype(BF16)

    y = _dot(xn_ref[...], w_ref[...])

    @pl.when(j < 2)
    def _():
        hi, lo = _split_bf16(y * y)
        ms = _dot(hi, bsel_ref[...]) + _dot(lo, bsel_ref[...])
        rhi, rlo = _split_bf16(lax.rsqrt(ms + EPS))
        rs = _dot(rhi, bexp_ref[...]) + _dot(rlo, bexp_ref[...])
        o_ref[...] = ((y * rs) * gain_ref[0]).astype(BF16)

    @pl.when(j == 2)
    def _():
        o_ref[...] = y.astype(BF16)


def _qkv_proj(x2d, norm_g, w_qkv, q_g, k_g):
    t, d = x2d.shape
    scale = HEAD_DIM ** -0.5
    gains = jnp.stack([jnp.tile(q_g, ATTN_HEADS) * scale, jnp.tile(k_g, ATTN_HEADS),
                       jnp.ones((d,), F32)]).reshape(3, 1, d)
    head_of_col = np.arange(d) // HEAD_DIM
    bsel = (head_of_col[:, None] == np.arange(LANES)[None, :]).astype(np.float32) / HEAD_DIM
    bexp = (np.arange(LANES)[:, None] == head_of_col[None, :]).astype(np.float32)
    return pl.pallas_call(
        _qkv_kernel,
        grid=(t // ROW_TILE, 3),
        in_specs=[
            pl.BlockSpec((ROW_TILE, d), lambda i, j: (i, 0)),
            pl.BlockSpec((1, d), lambda i, j: (0, 0)),
            pl.BlockSpec((d, d), lambda i, j: (0, j)),
            pl.BlockSpec((1, 1, d), lambda i, j: (j, 0, 0)),
            pl.BlockSpec((d, LANES), lambda i, j: (0, 0)),
            pl.BlockSpec((LANES, d), lambda i, j: (0, 0)),
        ],
        out_specs=pl.BlockSpec((ROW_TILE, d), lambda i, j: (i, j)),
        out_shape=jax.ShapeDtypeStruct((t, 3 * d), BF16),
        scratch_shapes=[pltpu.VMEM((ROW_TILE, d), BF16)],
        compiler_params=pltpu.CompilerParams(
            dimension_semantics=("parallel", "arbitrary"), vmem_limit_bytes=40 * MIB),
        name="qkv_proj",
    )(x2d, norm_g.reshape(1, d), w_qkv.astype(BF16), gains,
      jnp.asarray(bsel, BF16), jnp.asarray(bexp, BF16))


def _attn_kernel(x_ref, q_ref, k0_ref, k1_ref, k2_ref, v0_ref, v1_ref, v2_ref,
                 bias_ref, wout_ref, o_ref, att_ref):
    qt = pl.program_id(1)
    k_refs = (k0_ref, k1_ref, k2_ref)
    v_refs = (v0_ref, v1_ref, v2_ref)
    negs = [jnp.where(qt - (N_KEY_BLOCKS - 1) + kb < 0, NEG_INF, 0.0).astype(F32)
            for kb in range(N_KEY_BLOCKS)]
    lane = lax.broadcasted_iota(I32, (Q_TILE, LANES), 1)
    first_half = lane < HEAD_DIM
    for p in range(ATTN_HEADS // 2):
        cols = slice(p * LANES, (p + 1) * LANES)
        qp = q_ref[0, :, cols]
        outs = []
        for hh in range(2):
            h = 2 * p + hh
            keep = first_half if hh == 0 else jnp.logical_not(first_half)
            qm = jnp.where(keep, qp, jnp.zeros_like(qp))
            s = [_dot_nt(qm, k_refs[kb][0, :, cols]) + bias_ref[kb, h] + negs[kb]
                 for kb in range(N_KEY_BLOCKS)]
            m = jnp.max(s[0], axis=-1, keepdims=True)
            for kb in range(1, N_KEY_BLOCKS):
                m = jnp.maximum(m, jnp.max(s[kb], axis=-1, keepdims=True))
            acc = jnp.zeros((Q_TILE, LANES), F32)
            l = jnp.zeros((Q_TILE, 1), F32)
            for kb in range(N_KEY_BLOCKS):
                e = jnp.exp(s[kb] - m)
                l = l + jnp.sum(e, axis=-1, keepdims=True)
                acc = acc + _dot(e.astype(BF16), v_refs[kb][0, :, cols])
            outs.append(acc / l)
        att_ref[:, cols] = jnp.where(first_half, outs[0], outs[1]).astype(BF16)
    o_ref[0] = x_ref[0] + _dot(att_ref[...], wout_ref[...])


def _band_bias(table):
    qi = np.arange(Q_TILE)[:, None]
    kj = np.arange(Q_TILE)[None, :]
    period = 2 * Q_TILE
    m = np.arange(period)
    delta = np.where(m < Q_TILE, m, m - period)
    blocks = []
    for kb in range(N_KEY_BLOCKS):
        dist = (N_KEY_BLOCKS - 1 - kb) * Q_TILE - delta
        diag = table[:, np.clip(dist, -REL_CLIP, REL_CLIP) + REL_CLIP]
        toep = jnp.tile(diag, (1, Q_TILE))[:, :Q_TILE * (period - 1)]
        toep = toep.reshape(-1, Q_TILE, period - 1)[:, :, :Q_TILE]
        dchunk = (Q_TILE // CHUNK) * (kb - (N_KEY_BLOCKS - 1)) + kj // CHUNK - qi // CHUNK
        valid = (dchunk >= -LEFT_CHUNKS) & (dchunk <= 0)
        blocks.append(jnp.where(valid[None], toep, NEG_INF))
    return jnp.stack(blocks).astype(F32)


def _attention(x, qkv, rel_bias_table, w_out):
    b, s, d = x.shape
    nq = s // Q_TILE
    qkv3 = qkv.reshape(b, s, 3 * d)
    bias = _band_bias(rel_bias_table)

    def kv_spec(kb, col):
        return pl.BlockSpec(
            (1, Q_TILE, d),
            lambda bi, qt: (bi, jnp.maximum(qt - (N_KEY_BLOCKS - 1) + kb, 0), col))

    in_specs = [pl.BlockSpec((1, Q_TILE, d), lambda bi, qt: (bi, qt, 0)),
                pl.BlockSpec((1, Q_TILE, d), lambda bi, qt: (bi, qt, 0))]
    in_specs += [kv_spec(kb, 1) for kb in range(N_KEY_BLOCKS)]
    in_specs += [kv_spec(kb, 2) for kb in range(N_KEY_BLOCKS)]
    in_specs += [pl.BlockSpec(memory_space=pltpu.VMEM), pl.BlockSpec(memory_space=pltpu.VMEM)]
    return pl.pallas_call(
        _attn_kernel,
        grid=(b, nq),
        in_specs=in_specs,
        out_specs=pl.BlockSpec((1, Q_TILE, d), lambda bi, qt: (bi, qt, 0)),
        out_shape=jax.ShapeDtypeStruct((b, s, d), F32),
        scratch_shapes=[pltpu.VMEM((Q_TILE, d), BF16)],
        compiler_params=pltpu.CompilerParams(
            dimension_semantics=("parallel", "parallel"), vmem_limit_bytes=48 * MIB),
        name="band_attention",
    )(x, qkv3, qkv3, qkv3, qkv3, qkv3, qkv3, qkv3, bias, w_out.astype(BF16))


def _rglru_kernel(x_ref, g_ref, win_ref, cw_ref, cb_ref, wa_ref, ba_ref, wx_ref, bx_ref,
                  lam_ref, wout_ref, o_ref, ext_ref, h_ref, a_scr, b_scr, hs_scr):
    ts = x_ref.shape[1]
    w = LRU_HEADS * LRU_BLOCK

    @pl.when(pl.program_id(1) == 0)
    def _():
        ext_ref[...] = jnp.zeros_like(ext_ref)
        h_ref[...] = jnp.zeros_like(h_ref)

    x = x_ref[0]
    xn = _rms_rows(x, g_ref[...]).astype(BF16)
    gu = _dot(xn, win_ref[...])
    gate = gu[:, :w]
    u_raw = gu[:, w:]
    ext_ref[0:SUBLANES, :] = ext_ref[ts:ts + SUBLANES, :]
    ext_ref[SUBLANES:, :] = u_raw
    u = cb_ref[...] + jnp.zeros((ts, w), F32)
    for k in range(CONV_WIDTH):
        off = SUBLANES - (CONV_WIDTH - 1) + k
        u = u + cw_ref[k:k + 1, :] * ext_ref[off:off + ts, :]
    ub = u.astype(BF16)

    def block_diag(wref):
        return jnp.concatenate(
            [_dot(ub[:, hh * LRU_BLOCK:(hh + 1) * LRU_BLOCK], wref[hh]) for hh in range(LRU_HEADS)],
            axis=1)

    r = jax.nn.sigmoid(block_diag(wa_ref) + ba_ref[...])
    i = jax.nn.sigmoid(block_diag(wx_ref) + bx_ref[...])
    z = -lam_ref[...]
    softplus = jnp.maximum(z, 0.0) + jnp.log1p(jnp.exp(-jnp.abs(z)))
    log_a = (-LRU_C) * r * softplus
    a_scr[...] = jnp.exp(log_a)
    th = jnp.tanh(log_a)
    b_scr[...] = jnp.sqrt(2.0 * th / (th - 1.0)) * (i * u)

    def step(t, h):
        h = a_scr[pl.ds(t, 1), :] * h + b_scr[pl.ds(t, 1), :]
        hs_scr[pl.ds(t, 1), :] = h
        return h

    h_ref[0:1, :] = lax.fori_loop(0, ts, step, h_ref[0:1, :], unroll=8)
    y = jax.nn.gelu(gate) * hs_scr[...]
    o_ref[0] = x + _dot(y.astype(BF16), wout_ref[...])


def _rglru(x, norm_g, w_in, conv_w, conv_b, w_a, b_a, w_x, b_x, lam, w_out):
    b, s, d = x.shape
    w = LRU_HEADS * LRU_BLOCK
    row = lambda v: v.reshape(1, -1)
    whole = pl.BlockSpec(memory_space=pltpu.VMEM)
    return pl.pallas_call(
        _rglru_kernel,
        grid=(b, s // LRU_TILE),
        in_specs=[pl.BlockSpec((1, LRU_TILE, d), lambda bi, si: (bi, si, 0))] + [whole] * 10,
        out_specs=pl.BlockSpec((1, LRU_TILE, d), lambda bi, si: (bi, si, 0)),
        out_shape=jax.ShapeDtypeStruct((b, s, d), F32),
        scratch_shapes=[pltpu.VMEM((LRU_TILE + SUBLANES, w), F32), pltpu.VMEM((SUBLANES, w), F32),
                        pltpu.VMEM((LRU_TILE, w), F32), pltpu.VMEM((LRU_TILE, w), F32),
                        pltpu.VMEM((LRU_TILE, w), F32)],
        compiler_params=pltpu.CompilerParams(
            dimension_semantics=("arbitrary", "arbitrary"), vmem_limit_bytes=48 * MIB),
        name="rglru_mixer",
    )(x, row(norm_g), w_in.astype(BF16), conv_w, row(conv_b), w_a.astype(BF16), row(b_a),
      w_x.astype(BF16), row(b_x), row(lam), w_out.astype(BF16))


def _peer_scores_kernel(x_ref, g_ref, wq_ref, sk_ref, xp_ref, st_ref, xw_ref):
    xb = _rms_rows(x_ref[...], g_ref[...]).astype(BF16)
    words = _bf16_pair_words(xb)
    xw_ref[...] = words
    tm = words.shape[0]
    for c in range(SUBLANES):
        piece = words[:, (c % (SUBLANES // 2)) * LANES:(c % (SUBLANES // 2) + 1) * LANES]
        xp_ref[pl.ds(c, tm, stride=SUBLANES), :] = piece
    q = _dot(xb, wq_ref[...]).astype(BF16)
    for hp in range(2 * PEER_HEADS):
        st_ref[hp] = _dot_nt(sk_ref[hp % 2], q[:, hp * D_HALF:(hp + 1) * D_HALF])


def _peer_scores(x2d, norm_g, w_q, sub_keys):
    t, d = x2d.shape
    nq = w_q.shape[1]
    whole = pl.BlockSpec(memory_space=pltpu.VMEM)
    return pl.pallas_call(
        _peer_scores_kernel,
        grid=(t // ROW_TILE,),
        in_specs=[pl.BlockSpec((ROW_TILE, d), lambda i: (i, 0)), whole, whole, whole],
        out_specs=[pl.BlockSpec((ROW_TILE * SUBLANES, LANES), lambda i: (i, 0)),
                   pl.BlockSpec((2 * PEER_HEADS, N_KEYS, ROW_TILE), lambda i: (0, 0, i)),
                   pl.BlockSpec((ROW_TILE, d // 2), lambda i: (i, 0))],
        out_shape=[jax.ShapeDtypeStruct((t * SUBLANES, LANES), U32),
                   jax.ShapeDtypeStruct((2 * PEER_HEADS, N_KEYS, t), F32),
                   jax.ShapeDtypeStruct((t, d // 2), U32)],
        compiler_params=pltpu.CompilerParams(
            dimension_semantics=("parallel",), vmem_limit_bytes=48 * MIB),
        name="peer_scores",
    )(x2d, norm_g.reshape(1, d), w_q.astype(BF16), sub_keys.astype(BF16))


def _candidate_blocks():
    blocks = []
    for i in range(PEER_TOPK // 2):
        nj = PEER_TOPK // (i + 1)
        for j0 in range(0, nj, SUBLANES):
            blocks.append(("row", i, j0, min(SUBLANES, nj - j0)))
    blocks.append(("col", PEER_TOPK // 2, PEER_TOPK // 2))
    return blocks


def _peer_topk_kernel(st_ref, r8_ref, par_ref, g_ref, e_ref, r8t_ref, et_ref):
    tk = st_ref.shape[2]
    key_id = lax.broadcasted_iota(I32, (N_KEYS, tk), 0).astype(F32)
    row16 = lax.broadcasted_iota(I32, (PEER_TOPK, tk), 0)
    row8 = lax.broadcasted_iota(I32, (SUBLANES, tk), 0)
    row8_f = row8.astype(F32)
    neg_inf = jnp.float32(-jnp.inf)
    blocks = _candidate_blocks()
    big = jnp.float32(PEER_TOPK * PEER_TOPK)

    def top16(x):
        vals = jnp.zeros((PEER_TOPK, tk), F32)
        idxs = jnp.zeros((PEER_TOPK, tk), F32)
        for k in range(PEER_TOPK):
            m = jnp.max(x, axis=0, keepdims=True)
            idx = jnp.min(jnp.where(x == m, key_id, float(N_KEYS)), axis=0, keepdims=True)
            x = jnp.where(key_id == idx, neg_inf, x)
            vals = jnp.where(row16 == k, m, vals)
            idxs = jnp.where(row16 == k, idx, idxs)
        return vals, idxs

    for h in range(PEER_HEADS):
        s0, i0 = top16(st_ref[2 * h])
        s1, i1 = top16(st_ref[2 * h + 1])
        cand, flat, expert = [], [], []
        for blk in blocks:
            if blk[0] == "row":
                _, i, j0, n = blk
                c = s0[i:i + 1, :] + s1[j0:j0 + SUBLANES, :]
                f = float(PEER_TOPK * i + j0) + row8_f
                e = i0[i:i + 1, :] * float(N_KEYS) + i1[j0:j0 + SUBLANES, :]
            else:
                _, i_start, n = blk
                c = s0[i_start:i_start + SUBLANES, :] + s1[0:1, :]
                f = float(PEER_TOPK) * (float(i_start) + row8_f)
                e = i0[i_start:i_start + SUBLANES, :] * float(N_KEYS) + i1[0:1, :]
            cand.append(jnp.where(row8 < n, c, neg_inf))
            flat.append(f)
            expert.append(e)
        cand = jnp.concatenate(cand, axis=0)
        flat = jnp.concatenate(flat, axis=0)
        expert = jnp.concatenate(expert, axis=0)
        best_s = jnp.zeros((PEER_TOPK, tk), F32)
        best_e = jnp.zeros((PEER_TOPK, tk), F32)
        for k in range(PEER_TOPK):
            m = jnp.max(cand, axis=0, keepdims=True)
            fidx = jnp.min(jnp.where(cand == m, flat, big), axis=0, keepdims=True)
            pick = flat == fidx
            e_sel = jnp.max(jnp.where(pick, expert, -1.0), axis=0, keepdims=True)
            cand = jnp.where(pick, neg_inf, cand)
            best_s = jnp.where(row16 == k, m, best_s)
            best_e = jnp.where(row16 == k, e_sel, best_e)
        ex = jnp.exp(best_s - best_s[0:1, :])
        gate = ex / jnp.sum(ex, axis=0, keepdims=True)
        rows = slice(h * PEER_TOPK, (h + 1) * PEER_TOPK)
        best_i = best_e.astype(I32)
        r8t_ref[rows, :] = (best_i >> 1) * SUBLANES
        et_ref[rows, :] = best_i
        par_ref[0, rows, :] = best_i & 1
        g_ref[0, rows, :] = gate
    r8_ref[0] = r8t_ref[...].T
    e_ref[0] = et_ref[...].T


def _peer_topk(scores_t, first_block, nblk):
    slot_major = pl.BlockSpec((1, N_SLOTS, PEER_TOKENS), lambda i: (i, 0, 0))
    token_major = pl.BlockSpec((1, PEER_TOKENS, N_SLOTS), lambda i: (i, 0, 0))
    return pl.pallas_call(
        _peer_topk_kernel,
        grid=(nblk,),
        in_specs=[pl.BlockSpec((2 * PEER_HEADS, N_KEYS, PEER_TOKENS), lambda i: (0, 0, i + first_block))],
        out_specs=[token_major, slot_major, slot_major, token_major],
        out_shape=[jax.ShapeDtypeStruct((nblk, PEER_TOKENS, N_SLOTS), I32),
                   jax.ShapeDtypeStruct((nblk, N_SLOTS, PEER_TOKENS), I32),
                   jax.ShapeDtypeStruct((nblk, N_SLOTS, PEER_TOKENS), F32),
                   jax.ShapeDtypeStruct((nblk, PEER_TOKENS, N_SLOTS), I32)],
        scratch_shapes=[pltpu.VMEM((N_SLOTS, PEER_TOKENS), I32), pltpu.VMEM((N_SLOTS, PEER_TOKENS), I32)],
        compiler_params=pltpu.CompilerParams(dimension_semantics=("parallel",)),
        name="peer_topk",
    )(scores_t)


def _bf16_pair_words(x):
    bits = lax.bitcast_convert_type(x.astype(BF16).astype(F32), U32)
    half = bits.shape[1] // 2
    return (bits[:, :half] >> 16) | (bits[:, half:] & jnp.uint32(0xFFFF0000))


def _pack_kernel(t_ref, o_ref, w_ref):
    words = _bf16_pair_words(t_ref[0])
    w_ref[...] = words
    rows = words.shape[0]
    n = words.shape[1] // LANES
    for s in range(n):
        o_ref[pl.ds(s, rows, stride=n), :] = words[:, s * LANES:(s + 1) * LANES]


def _pack_table(tabs, layer):
    _, e, d = tabs.shape
    n = d // (2 * LANES)
    return pl.pallas_call(
        _pack_kernel,
        grid=(e // ROW_TILE,),
        in_specs=[pl.BlockSpec((1, ROW_TILE, d), lambda i: (layer, i, 0))],
        out_specs=[pl.BlockSpec((ROW_TILE * n, LANES), lambda i: (i, 0)),
                   pl.BlockSpec((ROW_TILE, d // 2), lambda i: (i, 0))],
        out_shape=[jax.ShapeDtypeStruct((e * n, LANES), U32), jax.ShapeDtypeStruct((e, d // 2), U32)],
        compiler_params=pltpu.CompilerParams(dimension_semantics=("parallel",)),
        name="peer_pack_table",
    )(tabs)


def _unpack_words(w):
    lo = lax.bitcast_convert_type(w << 16, F32)
    hi = lax.bitcast_convert_type(w & jnp.uint32(0xFFFF0000), F32)
    return lo, hi


def _as_bf16(words):
    return pltpu.bitcast(words, BF16)


def _as_words(packed):
    return pltpu.bitcast(packed, U32)


def _merge_packed(a, b, shift, mask):
    ta = a + _as_bf16(pltpu.roll(_as_words(a), shift, 0))
    tb = b + _as_bf16(pltpu.roll(_as_words(b), SUBLANES - shift, 0))
    return jnp.where(mask, _as_words(ta), _as_words(tb))


def _sublane_iota():
    return lax.broadcasted_iota(I32, (SUBLANES, LANES), 0)


def _merge(a, b, shift, mask):
    ta = a + pltpu.roll(a, shift, 0)
    tb = b + pltpu.roll(b, SUBLANES - shift, 0)
    return jnp.where(mask, ta, tb)


GROUP = SUBLANES // 2
N_GROUPS = N_SLOTS // GROUP
N_ROWS = 2 * N_SLOTS
N_SPREAD = 4
SPREAD_AHEAD = 2


def _dup_halves(v, low):
    r = pltpu.roll(v, GROUP, 0)
    return jnp.where(low, v, r), jnp.where(low, r, v)


def _peer_u_kernel(r8_ref, xp_ref, g_ref, par_ref, tab_ref, a_ref, s_ref, part_ref):
    tb = g_ref.shape[2]
    sub = _sublane_iota()
    m2 = (sub & 2) != 0
    lane_t = lax.broadcasted_iota(I32, (SUBLANES, tb), 1)
    n_part = 2 * N_GROUPS

    def products(t, slot):
        xb = _as_bf16(xp_ref[pl.ds(pl.multiple_of(t * SUBLANES, SUBLANES), SUBLANES), :])
        ids = r8_ref.at[0, t]
        for grp in range(N_GROUPS):
            prods = []
            for i in range(GROUP):
                r8 = pl.multiple_of(ids[grp * GROUP + i], SUBLANES)
                prods.append(_as_bf16(tab_ref[pl.ds(r8, SUBLANES), :]) * xb)
            halves = (_merge_packed(prods[3], prods[1], 2, m2), _merge_packed(prods[2], prods[0], 2, m2))
            for k in range(2):
                lo, hi = _unpack_words(halves[k])
                part_ref[slot, 2 * grp + k] = lo + hi

    def reduce_lanes(t, slot):
        for n in range(n_part):
            col = jnp.sum(part_ref[slot, n], axis=1, keepdims=True)
            rows = slice(n * SUBLANES, (n + 1) * SUBLANES)
            s_ref[rows, :] = jnp.where(lane_t == t, col, s_ref[rows, :])

    def token_pair(i, carry):
        t = 2 * i
        reduce_lanes(t - 2, 0)
        reduce_lanes(t - 1, 1)
        products(t, 0)
        products(t + 1, 1)
        return carry

    s_ref[...] = jnp.zeros_like(s_ref)
    part_ref[...] = jnp.zeros_like(part_ref)
    lax.fori_loop(0, tb // 2, token_pair, 0)
    reduce_lanes(tb - 2, 0)
    reduce_lanes(tb - 1, 1)

    odd_t = (lax.broadcasted_iota(I32, (SUBLANES, tb), 0) & 1) != 0
    _gate_rows(lambda grp: _merge(s_ref[2 * grp * SUBLANES:(2 * grp + 1) * SUBLANES, :],
                                  s_ref[(2 * grp + 1) * SUBLANES:(2 * grp + 2) * SUBLANES, :], 1, odd_t),
               g_ref, par_ref, a_ref)


def _gate_rows(dots_of_group, g_ref, par_ref, a_ref):
    tb = g_ref.shape[2]
    low_t = lax.broadcasted_iota(I32, (SUBLANES, tb), 0) < GROUP
    want = jnp.where(low_t, 0, 1)
    for v in range(N_SLOTS // SUBLANES):
        src = slice(v * SUBLANES, (v + 1) * SUBLANES)
        gates = _dup_halves(g_ref[0, src, :], low_t)
        pars = _dup_halves(par_ref[0, src, :], low_t)
        for half in range(2):
            grp = 2 * v + half
            act = gates[half] * jax.nn.gelu(dots_of_group(grp))
            rows = slice(grp * SUBLANES, (grp + 1) * SUBLANES)
            a_ref[0, rows, :] = jnp.where(pars[half] == want, act, 0.0)


SEM_IDS, SEM_IN, SEM_OUT, N_SEMS = 2, 3, 5, 7


def _sc_token_loop(tab_hbm, e_hbm, idx_v, rows_v, sems, per_worker, token_in, out_buf, out_row, chunk_rows):
    wid = lax.axis_index("s") * SC_CORES + lax.axis_index("c")
    n_chunks = N_SLOTS // SC_GATHER_ROWS
    first = wid * per_worker
    last = first + per_worker - 1

    def gather(slot, chunk):
        ids = idx_v.at[slot, pl.ds(chunk * SC_GATHER_ROWS, SC_GATHER_ROWS)]
        return pltpu.make_async_copy(tab_hbm.at[ids], rows_v.at[chunk % 2], sems.at[chunk % 2])

    def ids_copy(tok, slot):
        return pltpu.make_async_copy(e_hbm.at[tok], idx_v.at[slot], sems.at[SEM_IDS])

    def in_copies(tok, slot):
        return [pltpu.make_async_copy(row_of(tok), buf.at[slot], sems.at[SEM_IN + slot]) for row_of, buf in token_in]

    def out_copy(tok, slot):
        return pltpu.make_async_copy(out_buf.at[slot], out_row(tok), sems.at[SEM_OUT + slot])

    def one_token(k, tok, slot, next_tok):
        ids_copy(next_tok, 1 - slot).start()
        for c in in_copies(next_tok, 1 - slot):
            c.start()
        for c in in_copies(tok, slot):
            c.wait()

        @pl.when(k > 0)
        def _():
            out_copy(tok, slot).wait()

        for chunk in range(n_chunks):
            if chunk + 1 < n_chunks:
                gather(slot, chunk + 1).start()
            else:
                ids_copy(next_tok, 1 - slot).wait()
                gather(1 - slot, 0).start()
            gather(slot, chunk).wait()
            chunk_rows(chunk, rows_v.at[chunk % 2], slot)
        out_copy(tok, slot).start()

    def token_pair(k, carry):
        tok = first + 2 * k
        one_token(k, tok, 0, tok + 1)
        one_token(k, tok + 1, 1, jnp.minimum(tok + 2, last))
        return carry

    pltpu.sync_copy(e_hbm.at[first], idx_v.at[0])
    gather(0, 0).start()
    for c in in_copies(first, 0):
        c.start()
    lax.fori_loop(0, per_worker // 2, token_pair, 0)
    gather(0, 0).wait()
    for c in in_copies(last, 0):
        c.wait()
    out_copy(last, 0).wait()
    out_copy(last, 1).wait()


def _sc_scratch(row_len, dtype):
    return [pltpu.VMEM((2, N_SLOTS), I32), pltpu.VMEM((2, SC_GATHER_ROWS, row_len), dtype),
            pltpu.SemaphoreType.DMA((N_SEMS,))]


def _sc_u_dots(u_words, e_tok, x_words, first_token):
    nw = x_words.shape[1]
    n_tokens = e_tok.shape[0]
    per_worker = n_tokens // (SC_CORES * SC_SUBCORES)
    n_half = nw // SC_LANES // 2
    fold = 4
    mesh = plsc.VectorSubcoreMesh(core_axis_name="c", subcore_axis_name="s")

    def body(u_hbm, e_hbm, x_hbm, o_hbm, idx_v, rows_v, sems, x_v, out_v):
        lane = lax.iota(I32, SC_LANES)

        def chunk_rows(chunk, rows, slot):
            for part in range(SC_GATHER_ROWS // SC_LANES):
                total = jnp.zeros((SC_LANES,), F32)
                for half in range(2):
                    base = half * n_half * SC_LANES
                    xs = [plsc.bitcast(x_v[slot, pl.ds(base + c * SC_LANES, SC_LANES)], BF16) for c in range(n_half)]

                    def row(r, vec):
                        rr = part * SC_LANES + r
                        acc = jnp.zeros((SC_LANES,), F32)
                        for g in range(n_half // fold):
                            p = [plsc.bitcast(rows[rr, pl.ds(base + (fold * g + k) * SC_LANES, SC_LANES)], BF16)
                                 * xs[fold * g + k] for k in range(fold)]
                            w32 = plsc.bitcast((p[0] + p[1]) + (p[2] + p[3]), U32)
                            acc = acc + plsc.bitcast(w32 << 16, F32) + plsc.bitcast(w32 & jnp.uint32(0xFFFF0000), F32)
                        return jnp.where(lane == r, jnp.sum(acc), vec)

                    total = total + lax.fori_loop(0, SC_LANES, row, jnp.zeros((SC_LANES,), F32))
                out_v[slot, pl.ds(chunk * SC_GATHER_ROWS + part * SC_LANES, SC_LANES)] = total

        _sc_token_loop(u_hbm, e_hbm, idx_v, rows_v, sems, per_worker,
                       [(lambda tok: x_hbm.at[first_token + tok], x_v)], out_v, lambda tok: o_hbm.at[tok], chunk_rows)

    return pl.kernel(
        body,
        out_type=jax.ShapeDtypeStruct((n_tokens, N_SLOTS), F32),
        mesh=mesh,
        scratch_types=_sc_scratch(nw, U32) + [pltpu.VMEM((2, nw), U32), pltpu.VMEM((2, N_SLOTS), F32)],
        compiler_params=pltpu.CompilerParams(needs_layout_passes=False),
        name="peer_expert_in_sc",
    )(u_words, e_tok, x_words)


def _peer_gate_kernel(s_ref, g_ref, par_ref, a_ref, w_ref):
    w_ref[...] = g_ref[0].T * jax.nn.gelu(s_ref[...])
    dots = s_ref[...].T
    low_t = lax.broadcasted_iota(I32, (SUBLANES, dots.shape[1]), 0) < GROUP
    _gate_rows(lambda grp: _dup_halves(dots[(grp // 2) * SUBLANES:(grp // 2 + 1) * SUBLANES, :], low_t)[grp % 2],
               g_ref, par_ref, a_ref)


def _sc_v_sums(v_words, e_tok, w_tok, x2d, first_token):
    d = x2d.shape[1]
    nw = v_words.shape[1]
    n_tokens = e_tok.shape[0]
    per_worker = n_tokens // (SC_CORES * SC_SUBCORES)
    passes = 4
    fold = 4
    n_acc = nw // SC_LANES // passes
    mesh = plsc.VectorSubcoreMesh(core_axis_name="c", subcore_axis_name="s")

    def body(v_hbm, e_hbm, w_hbm, x_hbm, o_hbm, idx_v, rows_v, sems, w_v, x_v, out_v):
        def chunk_rows(chunk, rows, slot):
            src = x_v if chunk == 0 else out_v
            for ps in range(passes):
                base = ps * n_acc * SC_LANES

                def row_group(q, acc):
                    ws = []
                    for k in range(fold):
                        pos = jnp.full((SC_LANES,), chunk * SC_GATHER_ROWS + fold * q + k, I32)
                        w = plsc.load_gather(w_v.at[slot], [pos])
                        ws.append(plsc.pack(w, w, format=plsc.PackFormat.INTERLEAVED, preferred_element_type=BF16))
                    lo, hi = list(acc[:n_acc]), list(acc[n_acc:])
                    for c in range(n_acc):
                        p = [plsc.bitcast(rows[fold * q + k, pl.ds(base + c * SC_LANES, SC_LANES)], BF16) * ws[k]
                             for k in range(fold)]
                        w32 = plsc.bitcast((p[0] + p[1]) + (p[2] + p[3]), U32)
                        lo[c] = lo[c] + plsc.bitcast(w32 << 16, F32)
                        hi[c] = hi[c] + plsc.bitcast(w32 & jnp.uint32(0xFFFF0000), F32)
                    return tuple(lo + hi)

                acc0 = tuple([src[slot, pl.ds(base + c * SC_LANES, SC_LANES)] for c in range(n_acc)]
                             + [src[slot, pl.ds(nw + base + c * SC_LANES, SC_LANES)] for c in range(n_acc)])
                acc = lax.fori_loop(0, SC_GATHER_ROWS // fold, row_group, acc0)
                for c in range(n_acc):
                    out_v[slot, pl.ds(base + c * SC_LANES, SC_LANES)] = acc[c]
                    out_v[slot, pl.ds(nw + base + c * SC_LANES, SC_LANES)] = acc[n_acc + c]

        _sc_token_loop(v_hbm, e_hbm, idx_v, rows_v, sems, per_worker,
                       [(lambda tok: w_hbm.at[tok], w_v), (lambda tok: x_hbm.at[first_token + tok], x_v)],
                       out_v, lambda tok: o_hbm.at[tok], chunk_rows)

    return pl.kernel(
        body,
        out_type=jax.ShapeDtypeStruct((n_tokens, d), F32),
        mesh=mesh,
        scratch_types=_sc_scratch(nw, U32) + [pltpu.VMEM((2, N_SLOTS), F32), pltpu.VMEM((2, d), F32),
                                             pltpu.VMEM((2, d), F32)],
        compiler_params=pltpu.CompilerParams(needs_layout_passes=False),
        name="peer_expert_out_sc",
    )(v_words, e_tok, w_tok, x2d)


def _token_tiles(ref, xt_ref, to_tiles):
    tb, d = ref.shape
    for q in range(tb // SUBLANES):
        for c in range(d // LANES):
            rows = slice(q * SUBLANES, (q + 1) * SUBLANES)
            cols = slice(c * LANES, (c + 1) * LANES)
            strided = pl.ds(q * SUBLANES * SUBLANES + c, SUBLANES, stride=SUBLANES)
            if to_tiles:
                xt_ref[strided, :] = ref[rows, cols]
            else:
                ref[rows, cols] = xt_ref[strided, :]


def _peer_v_kernel(r8_ref, a_ref, x_ref, tab_ref, o_ref, bc_ref, xt_ref):
    tb = x_ref.shape[0]
    sub = _sublane_iota()
    low = sub < GROUP
    lane_t = lax.broadcasted_iota(I32, (SUBLANES, tb), 1)
    _token_tiles(x_ref, xt_ref, True)

    def spread(t, buf):
        for k in range(N_ROWS // SUBLANES):
            rows = slice(k * SUBLANES, (k + 1) * SUBLANES)
            col = jnp.sum(jnp.where(lane_t == t, a_ref[0, rows, :], 0.0), axis=1, keepdims=True)
            bits = lax.bitcast_convert_type(col.astype(BF16).astype(F32), U32)
            bc_ref[buf, rows, :] = jnp.broadcast_to(bits | (bits >> 16), (SUBLANES, LANES))

    def accumulate(t, buf):
        acc_lo = jnp.zeros((SUBLANES, LANES), F32)
        acc_hi = jnp.zeros((SUBLANES, LANES), F32)
        ids = r8_ref.at[0, t]
        for grp in range(N_GROUPS):
            prods = []
            for i in range(GROUP):
                row = grp * SUBLANES + i
                r8 = pl.multiple_of(ids[grp * GROUP + i], SUBLANES)
                am = jnp.where(low, bc_ref[buf, row:row + 1, :], bc_ref[buf, row + GROUP:row + GROUP + 1, :])
                prods.append(_as_bf16(tab_ref[pl.ds(r8, SUBLANES), :]) * _as_bf16(am))
            lo, hi = _unpack_words(_as_words((prods[0] + prods[1]) + (prods[2] + prods[3])))
            acc_lo = acc_lo + lo
            acc_hi = acc_hi + hi
        acc_lo = acc_lo + pltpu.roll(acc_lo, GROUP, 0)
        acc_hi = acc_hi + pltpu.roll(acc_hi, GROUP, 0)
        tile = pl.ds(pl.multiple_of(t * SUBLANES, SUBLANES), SUBLANES)
        xt_ref[tile, :] = xt_ref[tile, :] + jnp.where(low, acc_lo, acc_hi)

    def token_group(i, carry):
        t = N_SPREAD * i
        for k in range(N_SPREAD):
            spread(jnp.minimum(t + k + SPREAD_AHEAD, tb - 1), (k + SPREAD_AHEAD) % N_SPREAD)
            accumulate(t + k, k)
        return carry

    for k in range(SPREAD_AHEAD):
        spread(k, k)
    lax.fori_loop(0, tb // N_SPREAD, token_group, 0)
    _token_tiles(o_ref, xt_ref, False)


def _peer_experts(x2d, xp, r8, e_tok, par, gates, dots_sc, v_words, u_tab, v_tab):
    t, d = x2d.shape
    nblk = t // PEER_TOKENS
    nblk_tc = nblk - SC_TOKEN_BLOCKS
    smem_block = pl.BlockSpec((1, PEER_TOKENS, N_SLOTS), lambda i: (i, 0, 0), memory_space=pltpu.SMEM)
    slot_block = pl.BlockSpec((1, N_SLOTS, PEER_TOKENS), lambda i: (i, 0, 0))
    row_block = pl.BlockSpec((1, N_ROWS, PEER_TOKENS), lambda i: (i, 0, 0))
    tile_block = pl.BlockSpec((PEER_TOKENS * SUBLANES, LANES), lambda i: (i, 0))
    tok_block = pl.BlockSpec((PEER_TOKENS, d), lambda i: (i, 0))
    whole = pl.BlockSpec(memory_space=pltpu.VMEM)
    params = pltpu.CompilerParams(dimension_semantics=("parallel",), vmem_limit_bytes=56 * MIB)

    act_tc = pl.pallas_call(
        _peer_u_kernel,
        grid=(nblk_tc,),
        in_specs=[smem_block, tile_block, slot_block, slot_block, whole],
        out_specs=row_block,
        out_shape=jax.ShapeDtypeStruct((nblk_tc, N_ROWS, PEER_TOKENS), F32),
        scratch_shapes=[pltpu.VMEM((2 * N_ROWS, PEER_TOKENS), F32),
                        pltpu.VMEM((2, 2 * N_GROUPS, SUBLANES, LANES), F32)],
        compiler_params=params,
        name="peer_expert_in",
    )(r8, xp, gates, par, u_tab)
    shifted = pl.BlockSpec((1, N_SLOTS, PEER_TOKENS), lambda i: (i + nblk_tc, 0, 0))
    token_rows = pl.BlockSpec((PEER_TOKENS, N_SLOTS), lambda i: (i, 0))
    act_sc, w_sc = pl.pallas_call(
        _peer_gate_kernel,
        grid=(SC_TOKEN_BLOCKS,),
        in_specs=[token_rows, shifted, shifted],
        out_specs=[row_block, token_rows],
        out_shape=[jax.ShapeDtypeStruct((SC_TOKEN_BLOCKS, N_ROWS, PEER_TOKENS), F32),
                   jax.ShapeDtypeStruct((SC_TOKEN_BLOCKS * PEER_TOKENS, N_SLOTS), F32)],
        compiler_params=pltpu.CompilerParams(dimension_semantics=("parallel",)),
        name="peer_gate_sc",
    )(dots_sc, gates, par)
    act = jnp.concatenate([act_tc, act_sc], axis=0)

    assert SC_V_BLOCKS <= SC_TOKEN_BLOCKS
    nblk_v = nblk - SC_V_BLOCKS
    w_sc = w_sc[(SC_TOKEN_BLOCKS - SC_V_BLOCKS) * PEER_TOKENS:]
    y_sc = _sc_v_sums(v_words, e_tok[nblk_v:].reshape(-1, N_SLOTS), w_sc, x2d, nblk_v * PEER_TOKENS)
    y_tc = pl.pallas_call(
        _peer_v_kernel,
        grid=(nblk_v,),
        in_specs=[smem_block, row_block, tok_block, whole],
        out_specs=tok_block,
        out_shape=jax.ShapeDtypeStruct((nblk_v * PEER_TOKENS, d), F32),
        scratch_shapes=[pltpu.VMEM((N_SPREAD, N_ROWS, LANES), U32),
                        pltpu.VMEM((PEER_TOKENS * SUBLANES, LANES), F32)],
        compiler_params=params,
        name="peer_expert_out",
    )(r8, act, x2d, v_tab)
    return jnp.concatenate([y_tc, y_sc], axis=0)


def _peer_ffn(x2d, norm_g, w_q, sub_keys, u_tabs, v_tabs, layer):
    xp, scores_t, x_words = _peer_scores(x2d, norm_g, w_q, sub_keys)
    u_tiles, u_words = _pack_table(u_tabs, layer)
    v_tiles, v_words = _pack_table(v_tabs, layer)
    nblk = x2d.shape[0] // PEER_TOKENS
    nblk_tc = nblk - SC_TOKEN_BLOCKS
    top_sc = _peer_topk(scores_t, nblk_tc, SC_TOKEN_BLOCKS)
    dots_sc = _sc_u_dots(u_words, top_sc[3].reshape(-1, N_SLOTS), x_words, nblk_tc * PEER_TOKENS)
    top_tc = _peer_topk(scores_t, 0, nblk_tc)
    r8, par, gates, e_tok = [jnp.concatenate([a, b], axis=0) for a, b in zip(top_tc, top_sc)]
    return _peer_experts(x2d, xp, r8, e_tok, par, gates, dots_sc, v_words, u_tiles, v_tiles)


def kernel(x, attn_norm_g, attn_w_qkv, attn_q_g, attn_k_g, attn_rel_bias, attn_w_out, rec_norm_g, rec_w_in, rec_conv_w, rec_conv_b, rec_w_a, rec_b_a, rec_w_x, rec_b_x, rec_lambda, rec_w_out, ffn_norm_g, peer_w_q, peer_sub_keys, peer_u, peer_v):
    b, s, d = x.shape
    depth = ffn_norm_g.shape[0]
    for layer in range(depth):
        j = layer // 2
        if layer % 2 == 0:
            qkv = _qkv_proj(x.reshape(b * s, d), attn_norm_g[j], attn_w_qkv[j], attn_q_g[j], attn_k_g[j])
            x = _attention(x, qkv, attn_rel_bias[j], attn_w_out[j])
        else:
            x = _rglru(x, rec_norm_g[j], rec_w_in[j], rec_conv_w[j], rec_conv_b[j], rec_w_a[j],
                       rec_b_a[j], rec_w_x[j], rec_b_x[j], rec_lambda[j], rec_w_out[j])
        x = _peer_ffn(x.reshape(b * s, d), ffn_norm_g[layer], peer_w_q[layer], peer_sub_keys[layer],
                      peer_u, peer_v, layer).reshape(b, s, d)
    return x
```

```python
import numpy as np
import jax
import jax.numpy as jnp
from jax import lax
from jax.experimental import pallas as pl
from jax.experimental.pallas import tpu as pltpu
from jax.experimental.pallas import tpu_sc as plsc

F32 = jnp.float32
BF16 = jnp.bfloat16
U32 = jnp.uint32
I32 = jnp.int32

EPS = 1e-6
NEG_INF = -1e30

D_MODEL = 1024
CHUNK = 64
ATTN_HEADS = 16
HEAD_DIM = 64
LEFT_CHUNKS = 8
REL_CLIP = 256
LRU_HEADS = 4
LRU_BLOCK = 256
CONV_WIDTH = 4
LRU_C = 8.0
PEER_HEADS = 8
N_KEYS = 128
PEER_TOPK = 16
D_HALF = 128

SUBLANES = 8
LANES = 128

Q_TILE = 4 * CHUNK
N_KEY_BLOCKS = 3
ROW_TILE = 512
PEER_TOKENS = 128
LRU_TILE = 256
N_SLOTS = PEER_HEADS * PEER_TOPK

MIB = 1024 * 1024

SC_CORES = 2
SC_SUBCORES = 16
SC_LANES = 16
SC_GATHER_ROWS = 32
SC_TOKEN_BLOCKS = 140
SC_V_BLOCKS = 114


def _rms_rows(x, g):
    ms = jnp.mean(x * x, axis=-1, keepdims=True)
    return (x * lax.rsqrt(ms + EPS)) * g


def _split_bf16(v):
    hi = v.astype(BF16)
    lo = (v - hi.astype(F32)).astype(BF16)
    return hi, lo


def _dot(a, b):
    return jnp.dot(a, b, preferred_element_type=F32)


def _dot_nt(a, b):
    return lax.dot_general(a, b, (((1,), (1,)), ((), ())), preferred_element_type=F32)


def _qkv_kernel(x_ref, g_ref, w_ref, gain_ref, bsel_ref, bexp_ref, o_ref, xn_ref):
    j = pl.program_id(1)

    @pl.when(j == 0)
    def _():
        xn_ref[...] = _rms_rows(x_ref[...], g_ref[...]).astype(BF16)

    y = _dot(xn_ref[...], w_ref[...])

    @pl.when(j < 2)
    def _():
        hi, lo = _split_bf16(y * y)
        ms = _dot(hi, bsel_ref[...]) + _dot(lo, bsel_ref[...])
        rhi, rlo = _split_bf16(lax.rsqrt(ms + EPS))
        rs = _dot(rhi, bexp_ref[...]) + _dot(rlo, bexp_ref[...])
        o_ref[...] = ((y * rs) * gain_ref[0]).astype(BF16)

    @pl.when(j == 2)
    def _():
        o_ref[...] = y.astype(BF16)


def _qkv_proj(x2d, norm_g, w_qkv, q_g, k_g):
    t, d = x2d.shape
    scale = HEAD_DIM ** -0.5
    gains = jnp.stack([jnp.tile(q_g, ATTN_HEADS) * scale, jnp.tile(k_g, ATTN_HEADS),
                       jnp.ones((d,), F32)]).reshape(3, 1, d)
    head_of_col = np.arange(d) // HEAD_DIM
    bsel = (head_of_col[:, None] == np.arange(LANES)[None, :]).astype(np.float32) / HEAD_DIM
    bexp = (np.arange(LANES)[:, None] == head_of_col[None, :]).astype(np.float32)
    return pl.pallas_call(
        _qkv_kernel,
        grid=(t // ROW_TILE, 3),
        in_specs=[
            pl.BlockSpec((ROW_TILE, d), lambda i, j: (i, 0)),
            pl.BlockSpec((1, d), lambda i, j: (0, 0)),
            pl.BlockSpec((d, d), lambda i, j: (0, j)),
            pl.BlockSpec((1, 1, d), lambda i, j: (j, 0, 0)),
            pl.BlockSpec((d, LANES), lambda i, j: (0, 0)),
            pl.BlockSpec((LANES, d), lambda i, j: (0, 0)),
        ],
        out_specs=pl.BlockSpec((ROW_TILE, d), lambda i, j: (i, j)),
        out_shape=jax.ShapeDtypeStruct((t, 3 * d), BF16),
        scratch_shapes=[pltpu.VMEM((ROW_TILE, d), BF16)],
        compiler_params=pltpu.CompilerParams(
            dimension_semantics=("parallel", "arbitrary"), vmem_limit_bytes=40 * MIB),
        name="qkv_proj",
    )(x2d, norm_g.reshape(1, d), w_qkv.astype(BF16), gains,
      jnp.asarray(bsel, BF16), jnp.asarray(bexp, BF16))


def _attn_kernel(x_ref, q_ref, k0_ref, k1_ref, k2_ref, v0_ref, v1_ref, v2_ref,
                 bias_ref, wout_ref, o_ref, att_ref):
    qt = pl.program_id(1)
    k_refs = (k0_ref, k1_ref, k2_ref)
    v_refs = (v0_ref, v1_ref, v2_ref)
    negs = [jnp.where(qt - (N_KEY_BLOCKS - 1) + kb < 0, NEG_INF, 0.0).astype(F32)
            for kb in range(N_KEY_BLOCKS)]
    lane = lax.broadcasted_iota(I32, (Q_TILE, LANES), 1)
    first_half = lane < HEAD_DIM
    for p in range(ATTN_HEADS // 2):
        cols = slice(p * LANES, (p + 1) * LANES)
        qp = q_ref[0, :, cols]
        outs = []
        for hh in range(2):
            h = 2 * p + hh
            keep = first_half if hh == 0 else jnp.logical_not(first_half)
            qm = jnp.where(keep, qp, jnp.zeros_like(qp))
            s = [_dot_nt(qm, k_refs[kb][0, :, cols]) + bias_ref[kb, h] + negs[kb]
                 for kb in range(N_KEY_BLOCKS)]
            m = jnp.max(s[0], axis=-1, keepdims=True)
            for kb in range(1, N_KEY_BLOCKS):
                m = jnp.maximum(m, jnp.max(s[kb], axis=-1, keepdims=True))
            acc = jnp.zeros((Q_TILE, LANES), F32)
            l = jnp.zeros((Q_TILE, 1), F32)
            for kb in range(N_KEY_BLOCKS):
                e = jnp.exp(s[kb] - m)
                l = l + jnp.sum(e, axis=-1, keepdims=True)
                acc = acc + _dot(e.astype(BF16), v_refs[kb][0, :, cols])
            outs.append(acc / l)
        att_ref[:, cols] = jnp.where(first_half, outs[0], outs[1]).astype(BF16)
    o_ref[0] = x_ref[0] + _dot(att_ref[...], wout_ref[...])


def _band_bias(table):
    qi = np.arange(Q_TILE)[:, None]
    kj = np.arange(Q_TILE)[None, :]
    period = 2 * Q_TILE
    m = np.arange(period)
    delta = np.where(m < Q_TILE, m, m - period)
    blocks = []
    for kb in range(N_KEY_BLOCKS):
        dist = (N_KEY_BLOCKS - 1 - kb) * Q_TILE - delta
        diag = table[:, np.clip(dist, -REL_CLIP, REL_CLIP) + REL_CLIP]
        toep = jnp.tile(diag, (1, Q_TILE))[:, :Q_TILE * (period - 1)]
        toep = toep.reshape(-1, Q_TILE, period - 1)[:, :, :Q_TILE]
        dchunk = (Q_TILE // CHUNK) * (kb - (N_KEY_BLOCKS - 1)) + kj // CHUNK - qi // CHUNK
        valid = (dchunk >= -LEFT_CHUNKS) & (dchunk <= 0)
        blocks.append(jnp.where(valid[None], toep, NEG_INF))
    return jnp.stack(blocks).astype(F32)


def _attention(x, qkv, rel_bias_table, w_out):
    b, s, d = x.shape
    nq = s // Q_TILE
    qkv3 = qkv.reshape(b, s, 3 * d)
    bias = _band_bias(rel_bias_table)

    def kv_spec(kb, col):
        return pl.BlockSpec(
            (1, Q_TILE, d),
            lambda bi, qt: (bi, jnp.maximum(qt - (N_KEY_BLOCKS - 1) + kb, 0), col))

    in_specs = [pl.BlockSpec((1, Q_TILE, d), lambda bi, qt: (bi, qt, 0)),
                pl.BlockSpec((1, Q_TILE, d), lambda bi, qt: (bi, qt, 0))]
    in_specs += [kv_spec(kb, 1) for kb in range(N_KEY_BLOCKS)]
    in_specs += [kv_spec(kb, 2) for kb in range(N_KEY_BLOCKS)]
    in_specs += [pl.BlockSpec(memory_space=pltpu.VMEM), pl.BlockSpec(memory_space=pltpu.VMEM)]
    return pl.pallas_call(
        _attn_kernel,
        grid=(b, nq),
        in_specs=in_specs,
        out_specs=pl.BlockSpec((1, Q_TILE, d), lambda bi, qt: (bi, qt, 0)),
        out_shape=jax.ShapeDtypeStruct((b, s, d), F32),
        scratch_shapes=[pltpu.VMEM((Q_TILE, d), BF16)],
        compiler_params=pltpu.CompilerParams(
            dimension_semantics=("parallel", "parallel"), vmem_limit_bytes=48 * MIB),
        name="band_attention",
    )(x, qkv3, qkv3, qkv3, qkv3, qkv3, qkv3, qkv3, bias, w_out.astype(BF16))


def _rglru_kernel(x_ref, g_ref, win_ref, cw_ref, cb_ref, wa_ref, ba_ref, wx_ref, bx_ref,
                  lam_ref, wout_ref, o_ref, ext_ref, h_ref, a_scr, b_scr, hs_scr):
    ts = x_ref.shape[1]
    w = LRU_HEADS * LRU_BLOCK

    @pl.when(pl.program_id(1) == 0)
    def _():
        ext_ref[...] = jnp.zeros_like(ext_ref)
        h_ref[...] = jnp.zeros_like(h_ref)

    x = x_ref[0]
    xn = _rms_rows(x, g_ref[...]).astype(BF16)
    gu = _dot(xn, win_ref[...])
    gate = gu[:, :w]
    u_raw = gu[:, w:]
    ext_ref[0:SUBLANES, :] = ext_ref[ts:ts + SUBLANES, :]
    ext_ref[SUBLANES:, :] = u_raw
    u = cb_ref[...] + jnp.zeros((ts, w), F32)
    for k in range(CONV_WIDTH):
        off = SUBLANES - (CONV_WIDTH - 1) + k
        u = u + cw_ref[k:k + 1, :] * ext_ref[off:off + ts, :]
    ub = u.astype(BF16)

    def block_diag(wref):
        return jnp.concatenate(
            [_dot(ub[:, hh * LRU_BLOCK:(hh + 1) * LRU_BLOCK], wref[hh]) for hh in range(LRU_HEADS)],
            axis=1)

    r = jax.nn.sigmoid(block_diag(wa_ref) + ba_ref[...])
    i = jax.nn.sigmoid(block_diag(wx_ref) + bx_ref[...])
    z = -lam_ref[...]
    softplus = jnp.maximum(z, 0.0) + jnp.log1p(jnp.exp(-jnp.abs(z)))
    log_a = (-LRU_C) * r * softplus
    a_scr[...] = jnp.exp(log_a)
    th = jnp.tanh(log_a)
    b_scr[...] = jnp.sqrt(2.0 * th / (th - 1.0)) * (i * u)

    def step(t, h):
        h = a_scr[pl.ds(t, 1), :] * h + b_scr[pl.ds(t, 1), :]
        hs_scr[pl.ds(t, 1), :] = h
        return h

    h_ref[0:1, :] = lax.fori_loop(0, ts, step, h_ref[0:1, :], unroll=8)
    y = jax.nn.gelu(gate) * hs_scr[...]
    o_ref[0] = x + _dot(y.astype(BF16), wout_ref[...])


def _rglru(x, norm_g, w_in, conv_w, conv_b, w_a, b_a, w_x, b_x, lam, w_out):
    b, s, d = x.shape
    w = LRU_HEADS * LRU_BLOCK
    row = lambda v: v.reshape(1, -1)
    whole = pl.BlockSpec(memory_space=pltpu.VMEM)
    return pl.pallas_call(
        _rglru_kernel,
        grid=(b, s // LRU_TILE),
        in_specs=[pl.BlockSpec((1, LRU_TILE, d), lambda bi, si: (bi, si, 0))] + [whole] * 10,
        out_specs=pl.BlockSpec((1, LRU_TILE, d), lambda bi, si: (bi, si, 0)),
        out_shape=jax.ShapeDtypeStruct((b, s, d), F32),
        scratch_shapes=[pltpu.VMEM((LRU_TILE + SUBLANES, w), F32), pltpu.VMEM((SUBLANES, w), F32),
                        pltpu.VMEM((LRU_TILE, w), F32), pltpu.VMEM((LRU_TILE, w), F32),
                        pltpu.VMEM((LRU_TILE, w), F32)],
        compiler_params=pltpu.CompilerParams(
            dimension_semantics=("arbitrary", "arbitrary"), vmem_limit_bytes=48 * MIB),
        name="rglru_mixer",
    )(x, row(norm_g), w_in.astype(BF16), conv_w, row(conv_b), w_a.astype(BF16), row(b_a),
      w_x.astype(BF16), row(b_x), row(lam), w_out.astype(BF16))


def _peer_scores_kernel(x_ref, g_ref, wq_ref, sk_ref, xp_ref, st_ref, xw_ref):
    xb = _rms_rows(x_ref[...], g_ref[...]).astype(BF16)
    words = _bf16_pair_words(xb)
    xw_ref[...] = words
    tm = words.shape[0]
    for c in range(SUBLANES):
        piece = words[:, (c % (SUBLANES // 2)) * LANES:(c % (SUBLANES // 2) + 1) * LANES]
        xp_ref[pl.ds(c, tm, stride=SUBLANES), :] = piece
    q = _dot(xb, wq_ref[...]).astype(BF16)
    for hp in range(2 * PEER_HEADS):
        st_ref[hp] = _dot_nt(sk_ref[hp % 2], q[:, hp * D_HALF:(hp + 1) * D_HALF])


def _peer_scores(x2d, norm_g, w_q, sub_keys):
    t, d = x2d.shape
    nq = w_q.shape[1]
    whole = pl.BlockSpec(memory_space=pltpu.VMEM)
    return pl.pallas_call(
        _peer_scores_kernel,
        grid=(t // ROW_TILE,),
        in_specs=[pl.BlockSpec((ROW_TILE, d), lambda i: (i, 0)), whole, whole, whole],
        out_specs=[pl.BlockSpec((ROW_TILE * SUBLANES, LANES), lambda i: (i, 0)),
                   pl.BlockSpec((2 * PEER_HEADS, N_KEYS, ROW_TILE), lambda i: (0, 0, i)),
                   pl.BlockSpec((ROW_TILE, d // 2), lambda i: (i, 0))],
        out_shape=[jax.ShapeDtypeStruct((t * SUBLANES, LANES), U32),
                   jax.ShapeDtypeStruct((2 * PEER_HEADS, N_KEYS, t), F32),
                   jax.ShapeDtypeStruct((t, d // 2), U32)],
        compiler_params=pltpu.CompilerParams(
            dimension_semantics=("parallel",), vmem_limit_bytes=48 * MIB),
        name="peer_scores",
    )(x2d, norm_g.reshape(1, d), w_q.astype(BF16), sub_keys.astype(BF16))


def _candidate_blocks():
    blocks = []
    for i in range(PEER_TOPK // 2):
        nj = PEER_TOPK // (i + 1)
        for j0 in range(0, nj, SUBLANES):
            blocks.append(("row", i, j0, min(SUBLANES, nj - j0)))
    blocks.append(("col", PEER_TOPK // 2, PEER_TOPK // 2))
    return blocks


def _peer_topk_kernel(st_ref, r8_ref, par_ref, g_ref, e_ref, r8t_ref, et_ref):
    tk = st_ref.shape[2]
    key_id = lax.broadcasted_iota(I32, (N_KEYS, tk), 0).astype(F32)
    row16 = lax.broadcasted_iota(I32, (PEER_TOPK, tk), 0)
    row8 = lax.broadcasted_iota(I32, (SUBLANES, tk), 0)
    row8_f = row8.astype(F32)
    neg_inf = jnp.float32(-jnp.inf)
    blocks = _candidate_blocks()
    big = jnp.float32(PEER_TOPK * PEER_TOPK)

    def top16(x):
        vals = jnp.zeros((PEER_TOPK, tk), F32)
        idxs = jnp.zeros((PEER_TOPK, tk), F32)
        for k in range(PEER_TOPK):
            m = jnp.max(x, axis=0, keepdims=True)
            idx = jnp.min(jnp.where(x == m, key_id, float(N_KEYS)), axis=0, keepdims=True)
            x = jnp.where(key_id == idx, neg_inf, x)
            vals = jnp.where(row16 == k, m, vals)
            idxs = jnp.where(row16 == k, idx, idxs)
        return vals, idxs

    for h in range(PEER_HEADS):
        s0, i0 = top16(st_ref[2 * h])
        s1, i1 = top16(st_ref[2 * h + 1])
        cand, flat, expert = [], [], []
        for blk in blocks:
            if blk[0] == "row":
                _, i, j0, n = blk
                c = s0[i:i + 1, :] + s1[j0:j0 + SUBLANES, :]
                f = float(PEER_TOPK * i + j0) + row8_f
                e = i0[i:i + 1, :] * float(N_KEYS) + i1[j0:j0 + SUBLANES, :]
            else:
                _, i_start, n = blk
                c = s0[i_start:i_start + SUBLANES, :] + s1[0:1, :]
                f = float(PEER_TOPK) * (float(i_start) + row8_f)
                e = i0[i_start:i_start + SUBLANES, :] * float(N_KEYS) + i1[0:1, :]
            cand.append(jnp.where(row8 < n, c, neg_inf))
            flat.append(f)
            expert.append(e)
        cand = jnp.concatenate(cand, axis=0)
        flat = jnp.concatenate(flat, axis=0)
        expert = jnp.concatenate(expert, axis=0)
        best_s = jnp.zeros((PEER_TOPK, tk), F32)
        best_e = jnp.zeros((PEER_TOPK, tk), F32)
        for k in range(PEER_TOPK):
            m = jnp.max(cand, axis=0, keepdims=True)
            fidx = jnp.min(jnp.where(cand == m, flat, big), axis=0, keepdims=True)
            pick = flat == fidx
            e_sel = jnp.max(jnp.where(pick, expert, -1.0), axis=0, keepdims=True)
            cand = jnp.where(pick, neg_inf, cand)
            best_s = jnp.where(row16 == k, m, best_s)
            best_e = jnp.where(row16 == k, e_sel, best_e)
        ex = jnp.exp(best_s - best_s[0:1, :])
        gate = ex / jnp.sum(ex, axis=0, keepdims=True)
        rows = slice(h * PEER_TOPK, (h + 1) * PEER_TOPK)
        best_i = best_e.astype(I32)
        r8t_ref[rows, :] = (best_i >> 1) * SUBLANES
        et_ref[rows, :] = best_i
        par_ref[0, rows, :] = best_i & 1
        g_ref[0, rows, :] = gate
    r8_ref[0] = r8t_ref[...].T
    e_ref[0] = et_ref[...].T


def _peer_topk(scores_t, first_block, nblk):
    slot_major = pl.BlockSpec((1, N_SLOTS, PEER_TOKENS), lambda i: (i, 0, 0))
    token_major = pl.BlockSpec((1, PEER_TOKENS, N_SLOTS), lambda i: (i, 0, 0))
    return pl.pallas_call(
        _peer_topk_kernel,
        grid=(nblk,),
        in_specs=[pl.BlockSpec((2 * PEER_HEADS, N_KEYS, PEER_TOKENS), lambda i: (0, 0, i + first_block))],
        out_specs=[token_major, slot_major, slot_major, token_major],
        out_shape=[jax.ShapeDtypeStruct((nblk, PEER_TOKENS, N_SLOTS), I32),
                   jax.ShapeDtypeStruct((nblk, N_SLOTS, PEER_TOKENS), I32),
                   jax.ShapeDtypeStruct((nblk, N_SLOTS, PEER_TOKENS), F32),
                   jax.ShapeDtypeStruct((nblk, PEER_TOKENS, N_SLOTS), I32)],
        scratch_shapes=[pltpu.VMEM((N_SLOTS, PEER_TOKENS), I32), pltpu.VMEM((N_SLOTS, PEER_TOKENS), I32)],
        compiler_params=pltpu.CompilerParams(dimension_semantics=("parallel",)),
        name="peer_topk",
    )(scores_t)


def _bf16_pair_words(x):
    bits = lax.bitcast_convert_type(x.astype(BF16).astype(F32), U32)
    half = bits.shape[1] // 2
    return (bits[:, :half] >> 16) | (bits[:, half:] & jnp.uint32(0xFFFF0000))


def _pack_kernel(t_ref, o_ref, w_ref):
    words = _bf16_pair_words(t_ref[0])
    w_ref[...] = words
    rows = words.shape[0]
    n = words.shape[1] // LANES
    for s in range(n):
        o_ref[pl.ds(s, rows, stride=n), :] = words[:, s * LANES:(s + 1) * LANES]


def _pack_table(tabs, layer):
    _, e, d = tabs.shape
    n = d // (2 * LANES)
    return pl.pallas_call(
        _pack_kernel,
        grid=(e // ROW_TILE,),
        in_specs=[pl.BlockSpec((1, ROW_TILE, d), lambda i: (layer, i, 0))],
        out_specs=[pl.BlockSpec((ROW_TILE * n, LANES), lambda i: (i, 0)),
                   pl.BlockSpec((ROW_TILE, d // 2), lambda i: (i, 0))],
        out_shape=[jax.ShapeDtypeStruct((e * n, LANES), U32), jax.ShapeDtypeStruct((e, d // 2), U32)],
        compiler_params=pltpu.CompilerParams(dimension_semantics=("parallel",)),
        name="peer_pack_table",
    )(tabs)


def _unpack_words(w):
    lo = lax.bitcast_convert_type(w << 16, F32)
    hi = lax.bitcast_convert_type(w & jnp.uint32(0xFFFF0000), F32)
    return lo, hi


def _as_bf16(words):
    return pltpu.bitcast(words, BF16)


def _as_words(packed):
    return pltpu.bitcast(packed, U32)


def _merge_packed(a, b, shift, mask):
    ta = a + _as_bf16(pltpu.roll(_as_words(a), shift, 0))
    tb = b + _as_bf16(pltpu.roll(_as_words(b), SUBLANES - shift, 0))
    return jnp.where(mask, _as_words(ta), _as_words(tb))


def _sublane_iota():
    return lax.broadcasted_iota(I32, (SUBLANES, LANES), 0)


def _merge(a, b, shift, mask):
    ta = a + pltpu.roll(a, shift, 0)
    tb = b + pltpu.roll(b, SUBLANES - shift, 0)
    return jnp.where(mask, ta, tb)


GROUP = SUBLANES // 2
N_GROUPS = N_SLOTS // GROUP
N_ROWS = 2 * N_SLOTS
N_SPREAD = 4
SPREAD_AHEAD = 2


def _dup_halves(v, low):
    r = pltpu.roll(v, GROUP, 0)
    return jnp.where(low, v, r), jnp.where(low, r, v)


def _peer_u_kernel(r8_ref, xp_ref, g_ref, par_ref, tab_ref, a_ref, s_ref, part_ref):
    tb = g_ref.shape[2]
    sub = _sublane_iota()
    m2 = (sub & 2) != 0
    lane_t = lax.broadcasted_iota(I32, (SUBLANES, tb), 1)
    n_part = 2 * N_GROUPS

    def products(t, slot):
        xb = _as_bf16(xp_ref[pl.ds(pl.multiple_of(t * SUBLANES, SUBLANES), SUBLANES), :])
        ids = r8_ref.at[0, t]
        for grp in range(N_GROUPS):
            prods = []
            for i in range(GROUP):
                r8 = pl.multiple_of(ids[grp * GROUP + i], SUBLANES)
                prods.append(_as_bf16(tab_ref[pl.ds(r8, SUBLANES), :]) * xb)
            halves = (_merge_packed(prods[3], prods[1], 2, m2), _merge_packed(prods[2], prods[0], 2, m2))
            for k in range(2):
                lo, hi = _unpack_words(halves[k])
                part_ref[slot, 2 * grp + k] = lo + hi

    def reduce_lanes(t, slot):
        for n in range(n_part):
            col = jnp.sum(part_ref[slot, n], axis=1, keepdims=True)
            rows = slice(n * SUBLANES, (n + 1) * SUBLANES)
            s_ref[rows, :] = jnp.where(lane_t == t, col, s_ref[rows, :])

    def token_pair(i, carry):
        t = 2 * i
        reduce_lanes(t - 2, 0)
        reduce_lanes(t - 1, 1)
        products(t, 0)
        products(t + 1, 1)
        return carry

    s_ref[...] = jnp.zeros_like(s_ref)
    part_ref[...] = jnp.zeros_like(part_ref)
    lax.fori_loop(0, tb // 2, token_pair, 0)
    reduce_lanes(tb - 2, 0)
    reduce_lanes(tb - 1, 1)

    odd_t = (lax.broadcasted_iota(I32, (SUBLANES, tb), 0) & 1) != 0
    _gate_rows(lambda grp: _merge(s_ref[2 * grp * SUBLANES:(2 * grp + 1) * SUBLANES, :],
                                  s_ref[(2 * grp + 1) * SUBLANES:(2 * grp + 2) * SUBLANES, :], 1, odd_t),
               g_ref, par_ref, a_ref)


def _gate_rows(dots_of_group, g_ref, par_ref, a_ref):
    tb = g_ref.shape[2]
    low_t = lax.broadcasted_iota(I32, (SUBLANES, tb), 0) < GROUP
    want = jnp.where(low_t, 0, 1)
    for v in range(N_SLOTS // SUBLANES):
        src = slice(v * SUBLANES, (v + 1) * SUBLANES)
        gates = _dup_halves(g_ref[0, src, :], low_t)
        pars = _dup_halves(par_ref[0, src, :], low_t)
        for half in range(2):
            grp = 2 * v + half
            act = gates[half] * jax.nn.gelu(dots_of_group(grp))
            rows = slice(grp * SUBLANES, (grp + 1) * SUBLANES)
            a_ref[0, rows, :] = jnp.where(pars[half] == want, act, 0.0)


SEM_IDS, SEM_IN, SEM_OUT, N_SEMS = 2, 3, 5, 7


def _sc_token_loop(tab_hbm, e_hbm, idx_v, rows_v, sems, per_worker, token_in, out_buf, out_row, chunk_rows):
    wid = lax.axis_index("s") * SC_CORES + lax.axis_index("c")
    n_chunks = N_SLOTS // SC_GATHER_ROWS
    first = wid * per_worker
    last = first + per_worker - 1

    def gather(slot, chunk):
        ids = idx_v.at[slot, pl.ds(chunk * SC_GATHER_ROWS, SC_GATHER_ROWS)]
        return pltpu.make_async_copy(tab_hbm.at[ids], rows_v.at[chunk % 2], sems.at[chunk % 2])

    def ids_copy(tok, slot):
        return pltpu.make_async_copy(e_hbm.at[tok], idx_v.at[slot], sems.at[SEM_IDS])

    def in_copies(tok, slot):
        return [pltpu.make_async_copy(row_of(tok), buf.at[slot], sems.at[SEM_IN + slot]) for row_of, buf in token_in]

    def out_copy(tok, slot):
        return pltpu.make_async_copy(out_buf.at[slot], out_row(tok), sems.at[SEM_OUT + slot])

    def one_token(k, tok, slot, next_tok):
        ids_copy(next_tok, 1 - slot).start()
        for c in in_copies(next_tok, 1 - slot):
            c.start()
        for c in in_copies(tok, slot):
            c.wait()

        @pl.when(k > 0)
        def _():
            out_copy(tok, slot).wait()

        for chunk in range(n_chunks):
            if chunk + 1 < n_chunks:
                gather(slot, chunk + 1).start()
            else:
                ids_copy(next_tok, 1 - slot).wait()
                gather(1 - slot, 0).start()
            gather(slot, chunk).wait()
            chunk_rows(chunk, rows_v.at[chunk % 2], slot)
        out_copy(tok, slot).start()

    def token_pair(k, carry):
        tok = first + 2 * k
        one_token(k, tok, 0, tok + 1)
        one_token(k, tok + 1, 1, jnp.minimum(tok + 2, last))
        return carry

    pltpu.sync_copy(e_hbm.at[first], idx_v.at[0])
    gather(0, 0).start()
    for c in in_copies(first, 0):
        c.start()
    lax.fori_loop(0, per_worker // 2, token_pair, 0)
    gather(0, 0).wait()
    for c in in_copies(last, 0):
        c.wait()
    out_copy(last, 0).wait()
    out_copy(last, 1).wait()


def _sc_scratch(row_len, dtype):
    return [pltpu.VMEM((2, N_SLOTS), I32), pltpu.VMEM((2, SC_GATHER_ROWS, row_len), dtype),
            pltpu.SemaphoreType.DMA((N_SEMS,))]


def _sc_u_dots(u_words, e_tok, x_words, first_token):
    nw = x_words.shape[1]
    n_tokens = e_tok.shape[0]
    per_worker = n_tokens // (SC_CORES * SC_SUBCORES)
    n_half = nw // SC_LANES // 2
    fold = 4
    mesh = plsc.VectorSubcoreMesh(core_axis_name="c", subcore_axis_name="s")

    def body(u_hbm, e_hbm, x_hbm, o_hbm, idx_v, rows_v, sems, x_v, out_v):
        lane = lax.iota(I32, SC_LANES)

        def chunk_rows(chunk, rows, slot):
            for part in range(SC_GATHER_ROWS // SC_LANES):
                total = jnp.zeros((SC_LANES,), F32)
                for half in range(2):
                    base = half * n_half * SC_LANES
                    xs = [plsc.bitcast(x_v[slot, pl.ds(base + c * SC_LANES, SC_LANES)], BF16) for c in range(n_half)]

                    def row(r, vec):
                        rr = part * SC_LANES + r
                        acc = jnp.zeros((SC_LANES,), F32)
                        for g in range(n_half // fold):
                            p = [plsc.bitcast(rows[rr, pl.ds(base + (fold * g + k) * SC_LANES, SC_LANES)], BF16)
                                 * xs[fold * g + k] for k in range(fold)]
                            w32 = plsc.bitcast((p[0] + p[1]) + (p[2] + p[3]), U32)
                            acc = acc + plsc.bitcast(w32 << 16, F32) + plsc.bitcast(w32 & jnp.uint32(0xFFFF0000), F32)
                        return jnp.where(lane == r, jnp.sum(acc), vec)

                    total = total + lax.fori_loop(0, SC_LANES, row, jnp.zeros((SC_LANES,), F32))
                out_v[slot, pl.ds(chunk * SC_GATHER_ROWS + part * SC_LANES, SC_LANES)] = total

        _sc_token_loop(u_hbm, e_hbm, idx_v, rows_v, sems, per_worker,
                       [(lambda tok: x_hbm.at[first_token + tok], x_v)], out_v, lambda tok: o_hbm.at[tok], chunk_rows)

    return pl.kernel(
        body,
        out_type=jax.ShapeDtypeStruct((n_tokens, N_SLOTS), F32),
        mesh=mesh,
        scratch_types=_sc_scratch(nw, U32) + [pltpu.VMEM((2, nw), U32), pltpu.VMEM((2, N_SLOTS), F32)],
        compiler_params=pltpu.CompilerParams(needs_layout_passes=False),
        name="peer_expert_in_sc",
    )(u_words, e_tok, x_words)


def _peer_gate_kernel(s_ref, g_ref, par_ref, a_ref):
    dots = s_ref[...].T
    low_t = lax.broadcasted_iota(I32, (SUBLANES, dots.shape[1]), 0) < GROUP
    _gate_rows(lambda grp: _dup_halves(dots[(grp // 2) * SUBLANES:(grp // 2 + 1) * SUBLANES, :], low_t)[grp % 2],
               g_ref, par_ref, a_ref)


def _peer_weights_kernel(a_ref, w_ref):
    low_t = lax.broadcasted_iota(I32, (SUBLANES, a_ref.shape[2]), 0) < GROUP
    tiles = []
    for v in range(N_SLOTS // SUBLANES):
        first = a_ref[0, 2 * v * SUBLANES:(2 * v + 1) * SUBLANES, :]
        second = a_ref[0, (2 * v + 1) * SUBLANES:(2 * v + 2) * SUBLANES, :]
        first = first + pltpu.roll(first, GROUP, 0)
        second = second + pltpu.roll(second, GROUP, 0)
        tiles.append(jnp.where(low_t, first, second))
    w_ref[...] = jnp.concatenate(tiles, axis=0).T


def _sc_v_sums(v_words, e_tok, w_tok, x2d, first_token):
    d = x2d.shape[1]
    nw = v_words.shape[1]
    n_tokens = e_tok.shape[0]
    per_worker = n_tokens // (SC_CORES * SC_SUBCORES)
    passes = 4
    fold = 4
    n_acc = nw // SC_LANES // passes
    mesh = plsc.VectorSubcoreMesh(core_axis_name="c", subcore_axis_name="s")

    def body(v_hbm, e_hbm, w_hbm, x_hbm, o_hbm, idx_v, rows_v, sems, w_v, x_v, out_v):
        def chunk_rows(chunk, rows, slot):
            src = x_v if chunk == 0 else out_v
            for ps in range(passes):
                base = ps * n_acc * SC_LANES

                def row_group(q, acc):
                    ws = []
                    for k in range(fold):
                        pos = jnp.full((SC_LANES,), chunk * SC_GATHER_ROWS + fold * q + k, I32)
                        w = plsc.load_gather(w_v.at[slot], [pos])
                        ws.append(plsc.pack(w, w, format=plsc.PackFormat.INTERLEAVED, preferred_element_type=BF16))
                    lo, hi = list(acc[:n_acc]), list(acc[n_acc:])
                    for c in range(n_acc):
                        p = [plsc.bitcast(rows[fold * q + k, pl.ds(base + c * SC_LANES, SC_LANES)], BF16) * ws[k]
                             for k in range(fold)]
                        w32 = plsc.bitcast((p[0] + p[1]) + (p[2] + p[3]), U32)
                        lo[c] = lo[c] + plsc.bitcast(w32 << 16, F32)
                        hi[c] = hi[c] + plsc.bitcast(w32 & jnp.uint32(0xFFFF0000), F32)
                    return tuple(lo + hi)

                acc0 = tuple([src[slot, pl.ds(base + c * SC_LANES, SC_LANES)] for c in range(n_acc)]
                             + [src[slot, pl.ds(nw + base + c * SC_LANES, SC_LANES)] for c in range(n_acc)])
                acc = lax.fori_loop(0, SC_GATHER_ROWS // fold, row_group, acc0)
                for c in range(n_acc):
                    out_v[slot, pl.ds(base + c * SC_LANES, SC_LANES)] = acc[c]
                    out_v[slot, pl.ds(nw + base + c * SC_LANES, SC_LANES)] = acc[n_acc + c]

        _sc_token_loop(v_hbm, e_hbm, idx_v, rows_v, sems, per_worker,
                       [(lambda tok: w_hbm.at[tok], w_v), (lambda tok: x_hbm.at[first_token + tok], x_v)],
                       out_v, lambda tok: o_hbm.at[tok], chunk_rows)

    return pl.kernel(
        body,
        out_type=jax.ShapeDtypeStruct((n_tokens, d), F32),
        mesh=mesh,
        scratch_types=_sc_scratch(nw, U32) + [pltpu.VMEM((2, N_SLOTS), F32), pltpu.VMEM((2, d), F32),
                                             pltpu.VMEM((2, d), F32)],
        compiler_params=pltpu.CompilerParams(needs_layout_passes=False),
        name="peer_expert_out_sc",
    )(v_words, e_tok, w_tok, x2d)


def _token_tiles(ref, xt_ref, to_tiles):
    tb, d = ref.shape
    for q in range(tb // SUBLANES):
        for c in range(d // LANES):
            rows = slice(q * SUBLANES, (q + 1) * SUBLANES)
            cols = slice(c * LANES, (c + 1) * LANES)
            strided = pl.ds(q * SUBLANES * SUBLANES + c, SUBLANES, stride=SUBLANES)
            if to_tiles:
                xt_ref[strided, :] = ref[rows, cols]
            else:
                ref[rows, cols] = xt_ref[strided, :]


def _peer_v_kernel(r8_ref, a_ref, x_ref, tab_ref, o_ref, bc_ref, xt_ref):
    tb = x_ref.shape[0]
    sub = _sublane_iota()
    low = sub < GROUP
    lane_t = lax.broadcasted_iota(I32, (SUBLANES, tb), 1)
    _token_tiles(x_ref, xt_ref, True)

    def spread(t, buf):
        for k in range(N_ROWS // SUBLANES):
            rows = slice(k * SUBLANES, (k + 1) * SUBLANES)
            col = jnp.sum(jnp.where(lane_t == t, a_ref[0, rows, :], 0.0), axis=1, keepdims=True)
            bits = lax.bitcast_convert_type(col.astype(BF16).astype(F32), U32)
            bc_ref[buf, rows, :] = jnp.broadcast_to(bits | (bits >> 16), (SUBLANES, LANES))

    def accumulate(t, buf):
        acc_lo = jnp.zeros((SUBLANES, LANES), F32)
        acc_hi = jnp.zeros((SUBLANES, LANES), F32)
        ids = r8_ref.at[0, t]
        for grp in range(N_GROUPS):
            prods = []
            for i in range(GROUP):
                row = grp * SUBLANES + i
                r8 = pl.multiple_of(ids[grp * GROUP + i], SUBLANES)
                am = jnp.where(low, bc_ref[buf, row:row + 1, :], bc_ref[buf, row + GROUP:row + GROUP + 1, :])
                prods.append(_as_bf16(tab_ref[pl.ds(r8, SUBLANES), :]) * _as_bf16(am))
            lo, hi = _unpack_words(_as_words((prods[0] + prods[1]) + (prods[2] + prods[3])))
            acc_lo = acc_lo + lo
            acc_hi = acc_hi + hi
        acc_lo = acc_lo + pltpu.roll(acc_lo, GROUP, 0)
        acc_hi = acc_hi + pltpu.roll(acc_hi, GROUP, 0)
        tile = pl.ds(pl.multiple_of(t * SUBLANES, SUBLANES), SUBLANES)
        xt_ref[tile, :] = xt_ref[tile, :] + jnp.where(low, acc_lo, acc_hi)

    def token_group(i, carry):
        t = N_SPREAD * i
        for k in range(N_SPREAD):
            spread(jnp.minimum(t + k + SPREAD_AHEAD, tb - 1), (k + SPREAD_AHEAD) % N_SPREAD)
            accumulate(t + k, k)
        return carry

    for k in range(SPREAD_AHEAD):
        spread(k, k)
    lax.fori_loop(0, tb // N_SPREAD, token_group, 0)
    _token_tiles(o_ref, xt_ref, False)


def _peer_experts(x2d, xp, r8, e_tok, par, gates, dots_sc, v_words, u_tab, v_tab):
    t, d = x2d.shape
    nblk = t // PEER_TOKENS
    nblk_tc = nblk - SC_TOKEN_BLOCKS
    smem_block = pl.BlockSpec((1, PEER_TOKENS, N_SLOTS), lambda i: (i, 0, 0), memory_space=pltpu.SMEM)
    slot_block = pl.BlockSpec((1, N_SLOTS, PEER_TOKENS), lambda i: (i, 0, 0))
    row_block = pl.BlockSpec((1, N_ROWS, PEER_TOKENS), lambda i: (i, 0, 0))
    tile_block = pl.BlockSpec((PEER_TOKENS * SUBLANES, LANES), lambda i: (i, 0))
    tok_block = pl.BlockSpec((PEER_TOKENS, d), lambda i: (i, 0))
    whole = pl.BlockSpec(memory_space=pltpu.VMEM)
    params = pltpu.CompilerParams(dimension_semantics=("parallel",), vmem_limit_bytes=56 * MIB)

    act_tc = pl.pallas_call(
        _peer_u_kernel,
        grid=(nblk_tc,),
        in_specs=[smem_block, tile_block, slot_block, slot_block, whole],
        out_specs=row_block,
        out_shape=jax.ShapeDtypeStruct((nblk_tc, N_ROWS, PEER_TOKENS), F32),
        scratch_shapes=[pltpu.VMEM((2 * N_ROWS, PEER_TOKENS), F32),
                        pltpu.VMEM((2, 2 * N_GROUPS, SUBLANES, LANES), F32)],
        compiler_params=params,
        name="peer_expert_in",
    )(r8, xp, gates, par, u_tab)
    shifted = pl.BlockSpec((1, N_SLOTS, PEER_TOKENS), lambda i: (i + nblk_tc, 0, 0))
    act_sc = pl.pallas_call(
        _peer_gate_kernel,
        grid=(SC_TOKEN_BLOCKS,),
        in_specs=[pl.BlockSpec((PEER_TOKENS, N_SLOTS), lambda i: (i, 0)), shifted, shifted],
        out_specs=row_block,
        out_shape=jax.ShapeDtypeStruct((SC_TOKEN_BLOCKS, N_ROWS, PEER_TOKENS), F32),
        compiler_params=pltpu.CompilerParams(dimension_semantics=("parallel",)),
        name="peer_gate_sc",
    )(dots_sc, gates, par)
    act = jnp.concatenate([act_tc, act_sc], axis=0)

    nblk_v = nblk - SC_V_BLOCKS
    w_sc = pl.pallas_call(
        _peer_weights_kernel,
        grid=(SC_V_BLOCKS,),
        in_specs=[pl.BlockSpec((1, N_ROWS, PEER_TOKENS), lambda i: (i + nblk_v, 0, 0))],
        out_specs=pl.BlockSpec((PEER_TOKENS, N_SLOTS), lambda i: (i, 0)),
        out_shape=jax.ShapeDtypeStruct((SC_V_BLOCKS * PEER_TOKENS, N_SLOTS), F32),
        compiler_params=pltpu.CompilerParams(dimension_semantics=("parallel",)),
        name="peer_weights_sc",
    )(act)
    y_sc = _sc_v_sums(v_words, e_tok[nblk_v:].reshape(-1, N_SLOTS), w_sc, x2d, nblk_v * PEER_TOKENS)
    y_tc = pl.pallas_call(
        _peer_v_kernel,
        grid=(nblk_v,),
        in_specs=[smem_block, row_block, tok_block, whole],
        out_specs=tok_block,
        out_shape=jax.ShapeDtypeStruct((nblk_v * PEER_TOKENS, d), F32),
        scratch_shapes=[pltpu.VMEM((N_SPREAD, N_ROWS, LANES), U32),
                        pltpu.VMEM((PEER_TOKENS * SUBLANES, LANES), F32)],
        compiler_params=params,
        name="peer_expert_out",
    )(r8, act, x2d, v_tab)
    return jnp.concatenate([y_tc, y_sc], axis=0)


def _peer_ffn(x2d, norm_g, w_q, sub_keys, u_tabs, v_tabs, layer):
    xp, scores_t, x_words = _peer_scores(x2d, norm_g, w_q, sub_keys)
    u_tiles, u_words = _pack_table(u_tabs, layer)
    v_tiles, v_words = _pack_table(v_tabs, layer)
    nblk = x2d.shape[0] // PEER_TOKENS
    nblk_tc = nblk - SC_TOKEN_BLOCKS
    top_sc = _peer_topk(scores_t, nblk_tc, SC_TOKEN_BLOCKS)
    dots_sc = _sc_u_dots(u_words, top_sc[3].reshape(-1, N_SLOTS), x_words, nblk_tc * PEER_TOKENS)
    top_tc = _peer_topk(scores_t, 0, nblk_tc)
    r8, par, gates, e_tok = [jnp.concatenate([a, b], axis=0) for a, b in zip(top_tc, top_sc)]
    return _peer_experts(x2d, xp, r8, e_tok, par, gates, dots_sc, v_words, u_tiles, v_tiles)


def kernel(x, attn_norm_g, attn_w_qkv, attn_q_g, attn_k_g, attn_rel_bias, attn_w_out, rec_norm_g, rec_w_in, rec_conv_w, rec_conv_b, rec_w_a, rec_b_a, rec_w_x, rec_b_x, rec_lambda, rec_w_out, ffn_norm_g, peer_w_q, peer_sub_keys, peer_u, peer_v):
    b, s, d = x.shape
    depth = ffn_norm_g.shape[0]
    for layer in range(depth):
        j = layer // 2
        if layer % 2 == 0:
            qkv = _qkv_proj(x.reshape(b * s, d), attn_norm_g[j], attn_w_qkv[j], attn_q_g[j], attn_k_g[j])
            x = _attention(x, qkv, attn_rel_bias[j], attn_w_out[j])
        else:
            x = _rglru(x, rec_norm_g[j], rec_w_in[j], rec_conv_w[j], rec_conv_b[j], rec_w_a[j],
                       rec_b_a[j], rec_w_x[j], rec_b_x[j], rec_lambda[j], rec_w_out[j])
        x = _peer_ffn(x.reshape(b * s, d), ffn_norm_g[layer], peer_w_q[layer], peer_sub_keys[layer],
                      peer_u, peer_v, layer).reshape(b, s, d)
    return x
```

```python
import numpy as np
import jax
import jax.numpy as jnp
from jax import lax
from jax.experimental import pallas as pl
from jax.experimental.pallas import tpu as pltpu
from jax.experimental.pallas import tpu_sc as plsc

F32 = jnp.float32
BF16 = jnp.bfloat16
U32 = jnp.uint32
I32 = jnp.int32

EPS = 1e-6
NEG_INF = -1e30

D_MODEL = 1024
CHUNK = 64
ATTN_HEADS = 16
HEAD_DIM = 64
LEFT_CHUNKS = 8
REL_CLIP = 256
LRU_HEADS = 4
LRU_BLOCK = 256
CONV_WIDTH = 4
LRU_C = 8.0
PEER_HEADS = 8
N_KEYS = 128
PEER_TOPK = 16
D_HALF = 128

SUBLANES = 8
LANES = 128

Q_TILE = 4 * CHUNK
N_KEY_BLOCKS = 3
ROW_TILE = 512
PEER_TOKENS = 128
LRU_TILE = 256
N_SLOTS = PEER_HEADS * PEER_TOPK

MIB = 1024 * 1024

SC_CORES = 2
SC_SUBCORES = 16
SC_LANES = 16
SC_U_GATHER_ROWS = 32
SC_V_GATHER_ROWS = 64
SC_TOKEN_BLOCKS = 140
SC_V_BLOCKS = 132


def _rms_rows(x, g):
    ms = jnp.mean(x * x, axis=-1, keepdims=True)
    return (x * lax.rsqrt(ms + EPS)) * g


def _split_bf16(v):
    hi = v.astype(BF16)
    lo = (v - hi.astype(F32)).astype(BF16)
    return hi, lo


def _dot(a, b):
    return jnp.dot(a, b, preferred_element_type=F32)


def _dot_nt(a, b):
    return lax.dot_general(a, b, (((1,), (1,)), ((), ())), preferred_element_type=F32)


def _qkv_kernel(x_ref, g_ref, w_ref, gain_ref, bsel_ref, bexp_ref, o_ref, xn_ref):
    j = pl.program_id(1)

    @pl.when(j == 0)
    def _():
        xn_ref[...] = _rms_rows(x_ref[...], g_ref[...]).astype(BF16)

    y = _dot(xn_ref[...], w_ref[...])

    @pl.when(j < 2)
    def _():
        hi, lo = _split_bf16(y * y)
        ms = _dot(hi, bsel_ref[...]) + _dot(lo, bsel_ref[...])
        rhi, rlo = _split_bf16(lax.rsqrt(ms + EPS))
        rs = _dot(rhi, bexp_ref[...]) + _dot(rlo, bexp_ref[...])
        o_ref[...] = ((y * rs) * gain_ref[0]).astype(BF16)

    @pl.when(j == 2)
    def _():
        o_ref[...] = y.astype(BF16)


def _qkv_proj(x2d, norm_g, w_qkv, q_g, k_g):
    t, d = x2d.shape
    scale = HEAD_DIM ** -0.5
    gains = jnp.stack([jnp.tile(q_g, ATTN_HEADS) * scale, jnp.tile(k_g, ATTN_HEADS),
                       jnp.ones((d,), F32)]).reshape(3, 1, d)
    head_of_col = np.arange(d) // HEAD_DIM
    bsel = (head_of_col[:, None] == np.arange(LANES)[None, :]).astype(np.float32) / HEAD_DIM
    bexp = (np.arange(LANES)[:, None] == head_of_col[None, :]).astype(np.float32)
    return pl.pallas_call(
        _qkv_kernel,
        grid=(t // ROW_TILE, 3),
        in_specs=[
            pl.BlockSpec((ROW_TILE, d), lambda i, j: (i, 0)),
            pl.BlockSpec((1, d), lambda i, j: (0, 0)),
            pl.BlockSpec((d, d), lambda i, j: (0, j)),
            pl.BlockSpec((1, 1, d), lambda i, j: (j, 0, 0)),
            pl.BlockSpec((d, LANES), lambda i, j: (0, 0)),
            pl.BlockSpec((LANES, d), lambda i, j: (0, 0)),
        ],
        out_specs=pl.BlockSpec((ROW_TILE, d), lambda i, j: (i, j)),
        out_shape=jax.ShapeDtypeStruct((t, 3 * d), BF16),
        scratch_shapes=[pltpu.VMEM((ROW_TILE, d), BF16)],
        compiler_params=pltpu.CompilerParams(
            dimension_semantics=("parallel", "arbitrary"), vmem_limit_bytes=40 * MIB),
        name="qkv_proj",
    )(x2d, norm_g.reshape(1, d), w_qkv.astype(BF16), gains,
      jnp.asarray(bsel, BF16), jnp.asarray(bexp, BF16))


def _attn_kernel(x_ref, q_ref, k0_ref, k1_ref, k2_ref, v0_ref, v1_ref, v2_ref,
                 bias_ref, wout_ref, o_ref, att_ref):
    qt = pl.program_id(1)
    k_refs = (k0_ref, k1_ref, k2_ref)
    v_refs = (v0_ref, v1_ref, v2_ref)
    negs = [jnp.where(qt - (N_KEY_BLOCKS - 1) + kb < 0, NEG_INF, 0.0).astype(F32)
            for kb in range(N_KEY_BLOCKS)]
    lane = lax.broadcasted_iota(I32, (Q_TILE, LANES), 1)
    first_half = lane < HEAD_DIM
    for p in range(ATTN_HEADS // 2):
        cols = slice(p * LANES, (p + 1) * LANES)
        qp = q_ref[0, :, cols]
        outs = []
        for hh in range(2):
            h = 2 * p + hh
            keep = first_half if hh == 0 else jnp.logical_not(first_half)
            qm = jnp.where(keep, qp, jnp.zeros_like(qp))
            s = [_dot_nt(qm, k_refs[kb][0, :, cols]) + bias_ref[kb, h] + negs[kb]
                 for kb in range(N_KEY_BLOCKS)]
            m = jnp.max(s[0], axis=-1, keepdims=True)
            for kb in range(1, N_KEY_BLOCKS):
                m = jnp.maximum(m, jnp.max(s[kb], axis=-1, keepdims=True))
            acc = jnp.zeros((Q_TILE, LANES), F32)
            l = jnp.zeros((Q_TILE, 1), F32)
            for kb in range(N_KEY_BLOCKS):
                e = jnp.exp(s[kb] - m)
                l = l + jnp.sum(e, axis=-1, keepdims=True)
                acc = acc + _dot(e.astype(BF16), v_refs[kb][0, :, cols])
            outs.append(acc / l)
        att_ref[:, cols] = jnp.where(first_half, outs[0], outs[1]).astype(BF16)
    o_ref[0] = x_ref[0] + _dot(att_ref[...], wout_ref[...])


def _band_bias(table):
    qi = np.arange(Q_TILE)[:, None]
    kj = np.arange(Q_TILE)[None, :]
    period = 2 * Q_TILE
    m = np.arange(period)
    delta = np.where(m < Q_TILE, m, m - period)
    blocks = []
    for kb in range(N_KEY_BLOCKS):
        dist = (N_KEY_BLOCKS - 1 - kb) * Q_TILE - delta
        diag = table[:, np.clip(dist, -REL_CLIP, REL_CLIP) + REL_CLIP]
        toep = jnp.tile(diag, (1, Q_TILE))[:, :Q_TILE * (period - 1)]
        toep = toep.reshape(-1, Q_TILE, period - 1)[:, :, :Q_TILE]
        dchunk = (Q_TILE // CHUNK) * (kb - (N_KEY_BLOCKS - 1)) + kj // CHUNK - qi // CHUNK
        valid = (dchunk >= -LEFT_CHUNKS) & (dchunk <= 0)
        blocks.append(jnp.where(valid[None], toep, NEG_INF))
    return jnp.stack(blocks).astype(F32)


def _attention(x, qkv, rel_bias_table, w_out):
    b, s, d = x.shape
    nq = s // Q_TILE
    qkv3 = qkv.reshape(b, s, 3 * d)
    bias = _band_bias(rel_bias_table)

    def kv_spec(kb, col):
        return pl.BlockSpec(
            (1, Q_TILE, d),
            lambda bi, qt: (bi, jnp.maximum(qt - (N_KEY_BLOCKS - 1) + kb, 0), col))

    in_specs = [pl.BlockSpec((1, Q_TILE, d), lambda bi, qt: (bi, qt, 0)),
                pl.BlockSpec((1, Q_TILE, d), lambda bi, qt: (bi, qt, 0))]
    in_specs += [kv_spec(kb, 1) for kb in range(N_KEY_BLOCKS)]
    in_specs += [kv_spec(kb, 2) for kb in range(N_KEY_BLOCKS)]
    in_specs += [pl.BlockSpec(memory_space=pltpu.VMEM), pl.BlockSpec(memory_space=pltpu.VMEM)]
    return pl.pallas_call(
        _attn_kernel,
        grid=(b, nq),
        in_specs=in_specs,
        out_specs=pl.BlockSpec((1, Q_TILE, d), lambda bi, qt: (bi, qt, 0)),
        out_shape=jax.ShapeDtypeStruct((b, s, d), F32),
        scratch_shapes=[pltpu.VMEM((Q_TILE, d), BF16)],
        compiler_params=pltpu.CompilerParams(
            dimension_semantics=("parallel", "parallel"), vmem_limit_bytes=48 * MIB),
        name="band_attention",
    )(x, qkv3, qkv3, qkv3, qkv3, qkv3, qkv3, qkv3, bias, w_out.astype(BF16))


def _rglru_kernel(x_ref, g_ref, win_ref, cw_ref, cb_ref, wa_ref, ba_ref, wx_ref, bx_ref,
                  lam_ref, wout_ref, o_ref, ext_ref, h_ref, a_scr, b_scr, hs_scr):
    ts = x_ref.shape[1]
    w = LRU_HEADS * LRU_BLOCK

    @pl.when(pl.program_id(1) == 0)
    def _():
        ext_ref[...] = jnp.zeros_like(ext_ref)
        h_ref[...] = jnp.zeros_like(h_ref)

    x = x_ref[0]
    xn = _rms_rows(x, g_ref[...]).astype(BF16)
    gu = _dot(xn, win_ref[...])
    gate = gu[:, :w]
    u_raw = gu[:, w:]
    ext_ref[0:SUBLANES, :] = ext_ref[ts:ts + SUBLANES, :]
    ext_ref[SUBLANES:, :] = u_raw
    u = cb_ref[...] + jnp.zeros((ts, w), F32)
    for k in range(CONV_WIDTH):
        off = SUBLANES - (CONV_WIDTH - 1) + k
        u = u + cw_ref[k:k + 1, :] * ext_ref[off:off + ts, :]
    ub = u.astype(BF16)

    def block_diag(wref):
        return jnp.concatenate(
            [_dot(ub[:, hh * LRU_BLOCK:(hh + 1) * LRU_BLOCK], wref[hh]) for hh in range(LRU_HEADS)],
            axis=1)

    r = jax.nn.sigmoid(block_diag(wa_ref) + ba_ref[...])
    i = jax.nn.sigmoid(block_diag(wx_ref) + bx_ref[...])
    z = -lam_ref[...]
    softplus = jnp.maximum(z, 0.0) + jnp.log1p(jnp.exp(-jnp.abs(z)))
    log_a = (-LRU_C) * r * softplus
    a_scr[...] = jnp.exp(log_a)
    th = jnp.tanh(log_a)
    b_scr[...] = jnp.sqrt(2.0 * th / (th - 1.0)) * (i * u)

    def step(t, h):
        h = a_scr[pl.ds(t, 1), :] * h + b_scr[pl.ds(t, 1), :]
        hs_scr[pl.ds(t, 1), :] = h
        return h

    h_ref[0:1, :] = lax.fori_loop(0, ts, step, h_ref[0:1, :], unroll=8)
    y = jax.nn.gelu(gate) * hs_scr[...]
    o_ref[0] = x + _dot(y.astype(BF16), wout_ref[...])


def _rglru(x, norm_g, w_in, conv_w, conv_b, w_a, b_a, w_x, b_x, lam, w_out):
    b, s, d = x.shape
    w = LRU_HEADS * LRU_BLOCK
    row = lambda v: v.reshape(1, -1)
    whole = pl.BlockSpec(memory_space=pltpu.VMEM)
    return pl.pallas_call(
        _rglru_kernel,
        grid=(b, s // LRU_TILE),
        in_specs=[pl.BlockSpec((1, LRU_TILE, d), lambda bi, si: (bi, si, 0))] + [whole] * 10,
        out_specs=pl.BlockSpec((1, LRU_TILE, d), lambda bi, si: (bi, si, 0)),
        out_shape=jax.ShapeDtypeStruct((b, s, d), F32),
        scratch_shapes=[pltpu.VMEM((LRU_TILE + SUBLANES, w), F32), pltpu.VMEM((SUBLANES, w), F32),
                        pltpu.VMEM((LRU_TILE, w), F32), pltpu.VMEM((LRU_TILE, w), F32),
                        pltpu.VMEM((LRU_TILE, w), F32)],
        compiler_params=pltpu.CompilerParams(
            dimension_semantics=("arbitrary", "arbitrary"), vmem_limit_bytes=48 * MIB),
        name="rglru_mixer",
    )(x, row(norm_g), w_in.astype(BF16), conv_w, row(conv_b), w_a.astype(BF16), row(b_a),
      w_x.astype(BF16), row(b_x), row(lam), w_out.astype(BF16))


def _peer_scores_kernel(x_ref, g_ref, wq_ref, sk_ref, xp_ref, st_ref, xw_ref):
    xb = _rms_rows(x_ref[...], g_ref[...]).astype(BF16)
    words = _bf16_pair_words(xb)
    xw_ref[...] = words
    tm = words.shape[0]
    for c in range(SUBLANES):
        piece = words[:, (c % (SUBLANES // 2)) * LANES:(c % (SUBLANES // 2) + 1) * LANES]
        xp_ref[pl.ds(c, tm, stride=SUBLANES), :] = piece
    q = _dot(xb, wq_ref[...]).astype(BF16)
    for hp in range(2 * PEER_HEADS):
        st_ref[hp] = _dot_nt(sk_ref[hp % 2], q[:, hp * D_HALF:(hp + 1) * D_HALF])


def _peer_scores(x2d, norm_g, w_q, sub_keys):
    t, d = x2d.shape
    nq = w_q.shape[1]
    whole = pl.BlockSpec(memory_space=pltpu.VMEM)
    return pl.pallas_call(
        _peer_scores_kernel,
        grid=(t // ROW_TILE,),
        in_specs=[pl.BlockSpec((ROW_TILE, d), lambda i: (i, 0)), whole, whole, whole],
        out_specs=[pl.BlockSpec((ROW_TILE * SUBLANES, LANES), lambda i: (i, 0)),
                   pl.BlockSpec((2 * PEER_HEADS, N_KEYS, ROW_TILE), lambda i: (0, 0, i)),
                   pl.BlockSpec((ROW_TILE, d // 2), lambda i: (i, 0))],
        out_shape=[jax.ShapeDtypeStruct((t * SUBLANES, LANES), U32),
                   jax.ShapeDtypeStruct((2 * PEER_HEADS, N_KEYS, t), F32),
                   jax.ShapeDtypeStruct((t, d // 2), U32)],
        compiler_params=pltpu.CompilerParams(
            dimension_semantics=("parallel",), vmem_limit_bytes=48 * MIB),
        name="peer_scores",
    )(x2d, norm_g.reshape(1, d), w_q.astype(BF16), sub_keys.astype(BF16))


def _candidate_blocks():
    blocks = []
    for i in range(PEER_TOPK // 2):
        nj = PEER_TOPK // (i + 1)
        for j0 in range(0, nj, SUBLANES):
            blocks.append(("row", i, j0, min(SUBLANES, nj - j0)))
    blocks.append(("col", PEER_TOPK // 2, PEER_TOPK // 2))
    return blocks


def _peer_topk_kernel(st_ref, r8_ref, par_ref, g_ref, e_ref, r8t_ref, et_ref):
    tk = st_ref.shape[2]
    key_id = lax.broadcasted_iota(I32, (N_KEYS, tk), 0).astype(F32)
    row16 = lax.broadcasted_iota(I32, (PEER_TOPK, tk), 0)
    row8 = lax.broadcasted_iota(I32, (SUBLANES, tk), 0)
    row8_f = row8.astype(F32)
    neg_inf = jnp.float32(-jnp.inf)
    blocks = _candidate_blocks()
    big = jnp.float32(PEER_TOPK * PEER_TOPK)

    def top16(x):
        vals = jnp.zeros((PEER_TOPK, tk), F32)
        idxs = jnp.zeros((PEER_TOPK, tk), F32)
        for k in range(PEER_TOPK):
            m = jnp.max(x, axis=0, keepdims=True)
            idx = jnp.min(jnp.where(x == m, key_id, float(N_KEYS)), axis=0, keepdims=True)
            x = jnp.where(key_id == idx, neg_inf, x)
            vals = jnp.where(row16 == k, m, vals)
            idxs = jnp.where(row16 == k, idx, idxs)
        return vals, idxs

    for h in range(PEER_HEADS):
        s0, i0 = top16(st_ref[2 * h])
        s1, i1 = top16(st_ref[2 * h + 1])
        cand, flat, expert = [], [], []
        for blk in blocks:
            if blk[0] == "row":
                _, i, j0, n = blk
                c = s0[i:i + 1, :] + s1[j0:j0 + SUBLANES, :]
                f = float(PEER_TOPK * i + j0) + row8_f
                e = i0[i:i + 1, :] * float(N_KEYS) + i1[j0:j0 + SUBLANES, :]
            else:
                _, i_start, n = blk
                c = s0[i_start:i_start + SUBLANES, :] + s1[0:1, :]
                f = float(PEER_TOPK) * (float(i_start) + row8_f)
                e = i0[i_start:i_start + SUBLANES, :] * float(N_KEYS) + i1[0:1, :]
            cand.append(jnp.where(row8 < n, c, neg_inf))
            flat.append(f)
            expert.append(e)
        cand = jnp.concatenate(cand, axis=0)
        flat = jnp.concatenate(flat, axis=0)
        expert = jnp.concatenate(expert, axis=0)
        best_s = jnp.zeros((PEER_TOPK, tk), F32)
        best_e = jnp.zeros((PEER_TOPK, tk), F32)
        for k in range(PEER_TOPK):
            m = jnp.max(cand, axis=0, keepdims=True)
            fidx = jnp.min(jnp.where(cand == m, flat, big), axis=0, keepdims=True)
            pick = flat == fidx
            e_sel = jnp.max(jnp.where(pick, expert, -1.0), axis=0, keepdims=True)
            cand = jnp.where(pick, neg_inf, cand)
            best_s = jnp.where(row16 == k, m, best_s)
            best_e = jnp.where(row16 == k, e_sel, best_e)
        ex = jnp.exp(best_s - best_s[0:1, :])
        gate = ex / jnp.sum(ex, axis=0, keepdims=True)
        rows = slice(h * PEER_TOPK, (h + 1) * PEER_TOPK)
        best_i = best_e.astype(I32)
        r8t_ref[rows, :] = (best_i >> 1) * SUBLANES
        et_ref[rows, :] = best_i
        par_ref[0, rows, :] = best_i & 1
        g_ref[0, rows, :] = gate
    r8_ref[0] = r8t_ref[...].T
    e_ref[0] = et_ref[...].T


def _peer_topk(scores_t, first_block, nblk):
    slot_major = pl.BlockSpec((1, N_SLOTS, PEER_TOKENS), lambda i: (i, 0, 0))
    token_major = pl.BlockSpec((1, PEER_TOKENS, N_SLOTS), lambda i: (i, 0, 0))
    return pl.pallas_call(
        _peer_topk_kernel,
        grid=(nblk,),
        in_specs=[pl.BlockSpec((2 * PEER_HEADS, N_KEYS, PEER_TOKENS), lambda i: (0, 0, i + first_block))],
        out_specs=[token_major, slot_major, slot_major, token_major],
        out_shape=[jax.ShapeDtypeStruct((nblk, PEER_TOKENS, N_SLOTS), I32),
                   jax.ShapeDtypeStruct((nblk, N_SLOTS, PEER_TOKENS), I32),
                   jax.ShapeDtypeStruct((nblk, N_SLOTS, PEER_TOKENS), F32),
                   jax.ShapeDtypeStruct((nblk, PEER_TOKENS, N_SLOTS), I32)],
        scratch_shapes=[pltpu.VMEM((N_SLOTS, PEER_TOKENS), I32), pltpu.VMEM((N_SLOTS, PEER_TOKENS), I32)],
        compiler_params=pltpu.CompilerParams(dimension_semantics=("parallel",)),
        name="peer_topk",
    )(scores_t)


def _bf16_pair_words(x):
    bits = lax.bitcast_convert_type(x.astype(BF16).astype(F32), U32)
    half = bits.shape[1] // 2
    return (bits[:, :half] >> 16) | (bits[:, half:] & jnp.uint32(0xFFFF0000))


def _pack_kernel(t_ref, o_ref, w_ref):
    words = _bf16_pair_words(t_ref[0])
    w_ref[...] = words
    rows = words.shape[0]
    n = words.shape[1] // LANES
    for s in range(n):
        o_ref[pl.ds(s, rows, stride=n), :] = words[:, s * LANES:(s + 1) * LANES]


def _pack_table(tabs, layer):
    _, e, d = tabs.shape
    n = d // (2 * LANES)
    return pl.pallas_call(
        _pack_kernel,
        grid=(e // ROW_TILE,),
        in_specs=[pl.BlockSpec((1, ROW_TILE, d), lambda i: (layer, i, 0))],
        out_specs=[pl.BlockSpec((ROW_TILE * n, LANES), lambda i: (i, 0)),
                   pl.BlockSpec((ROW_TILE, d // 2), lambda i: (i, 0))],
        out_shape=[jax.ShapeDtypeStruct((e * n, LANES), U32), jax.ShapeDtypeStruct((e, d // 2), U32)],
        compiler_params=pltpu.CompilerParams(dimension_semantics=("parallel",)),
        name="peer_pack_table",
    )(tabs)


def _unpack_words(w):
    lo = lax.bitcast_convert_type(w << 16, F32)
    hi = lax.bitcast_convert_type(w & jnp.uint32(0xFFFF0000), F32)
    return lo, hi


def _as_bf16(words):
    return pltpu.bitcast(words, BF16)


def _as_words(packed):
    return pltpu.bitcast(packed, U32)


def _merge_packed(a, b, shift, mask):
    ta = a + _as_bf16(pltpu.roll(_as_words(a), shift, 0))
    tb = b + _as_bf16(pltpu.roll(_as_words(b), SUBLANES - shift, 0))
    return jnp.where(mask, _as_words(ta), _as_words(tb))


def _sublane_iota():
    return lax.broadcasted_iota(I32, (SUBLANES, LANES), 0)


def _merge(a, b, shift, mask):
    ta = a + pltpu.roll(a, shift, 0)
    tb = b + pltpu.roll(b, SUBLANES - shift, 0)
    return jnp.where(mask, ta, tb)


GROUP = SUBLANES // 2
N_GROUPS = N_SLOTS // GROUP
N_ROWS = 2 * N_SLOTS
N_SPREAD = 4
SPREAD_AHEAD = 2


def _dup_halves(v, low):
    r = pltpu.roll(v, GROUP, 0)
    return jnp.where(low, v, r), jnp.where(low, r, v)


def _peer_u_kernel(r8_ref, xp_ref, g_ref, par_ref, tab_ref, a_ref, s_ref, part_ref):
    tb = g_ref.shape[2]
    sub = _sublane_iota()
    m2 = (sub & 2) != 0
    lane_t = lax.broadcasted_iota(I32, (SUBLANES, tb), 1)
    n_part = 2 * N_GROUPS

    def products(t, slot):
        xb = _as_bf16(xp_ref[pl.ds(pl.multiple_of(t * SUBLANES, SUBLANES), SUBLANES), :])
        ids = r8_ref.at[0, t]
        for grp in range(N_GROUPS):
            prods = []
            for i in range(GROUP):
                r8 = pl.multiple_of(ids[grp * GROUP + i], SUBLANES)
                prods.append(_as_bf16(tab_ref[pl.ds(r8, SUBLANES), :]) * xb)
            halves = (_merge_packed(prods[3], prods[1], 2, m2), _merge_packed(prods[2], prods[0], 2, m2))
            for k in range(2):
                lo, hi = _unpack_words(halves[k])
                part_ref[slot, 2 * grp + k] = lo + hi

    def reduce_lanes(t, slot):
        for n in range(n_part):
            col = jnp.sum(part_ref[slot, n], axis=1, keepdims=True)
            rows = slice(n * SUBLANES, (n + 1) * SUBLANES)
            s_ref[rows, :] = jnp.where(lane_t == t, col, s_ref[rows, :])

    def token_pair(i, carry):
        t = 2 * i
        reduce_lanes(t - 2, 0)
        reduce_lanes(t - 1, 1)
        products(t, 0)
        products(t + 1, 1)
        return carry

    s_ref[...] = jnp.zeros_like(s_ref)
    part_ref[...] = jnp.zeros_like(part_ref)
    lax.fori_loop(0, tb // 2, token_pair, 0)
    reduce_lanes(tb - 2, 0)
    reduce_lanes(tb - 1, 1)

    odd_t = (lax.broadcasted_iota(I32, (SUBLANES, tb), 0) & 1) != 0
    _gate_rows(lambda grp: _merge(s_ref[2 * grp * SUBLANES:(2 * grp + 1) * SUBLANES, :],
                                  s_ref[(2 * grp + 1) * SUBLANES:(2 * grp + 2) * SUBLANES, :], 1, odd_t),
               g_ref, par_ref, a_ref)


def _gate_rows(dots_of_group, g_ref, par_ref, a_ref):
    tb = g_ref.shape[2]
    low_t = lax.broadcasted_iota(I32, (SUBLANES, tb), 0) < GROUP
    want = jnp.where(low_t, 0, 1)
    for v in range(N_SLOTS // SUBLANES):
        src = slice(v * SUBLANES, (v + 1) * SUBLANES)
        gates = _dup_halves(g_ref[0, src, :], low_t)
        pars = _dup_halves(par_ref[0, src, :], low_t)
        for half in range(2):
            grp = 2 * v + half
            act = gates[half] * jax.nn.gelu(dots_of_group(grp))
            rows = slice(grp * SUBLANES, (grp + 1) * SUBLANES)
            a_ref[0, rows, :] = jnp.where(pars[half] == want, act, 0.0)


SEM_IDS, SEM_IN, SEM_OUT, N_SEMS = 2, 3, 5, 7


def _sc_token_loop(tab_hbm, e_hbm, idx_v, rows_v, sems, per_worker, token_in, out_buf, out_row, chunk_rows):
    wid = lax.axis_index("s") * SC_CORES + lax.axis_index("c")
    gather_rows = rows_v.shape[1]
    n_chunks = N_SLOTS // gather_rows
    first = wid * per_worker
    last = first + per_worker - 1

    def gather(slot, chunk):
        ids = idx_v.at[slot, pl.ds(chunk * gather_rows, gather_rows)]
        return pltpu.make_async_copy(tab_hbm.at[ids], rows_v.at[chunk % 2], sems.at[chunk % 2])

    def ids_copy(tok, slot):
        return pltpu.make_async_copy(e_hbm.at[tok], idx_v.at[slot], sems.at[SEM_IDS])

    def in_copies(tok, slot):
        return [pltpu.make_async_copy(row_of(tok), buf.at[slot], sems.at[SEM_IN + slot]) for row_of, buf in token_in]

    def out_copy(tok, slot):
        return pltpu.make_async_copy(out_buf.at[slot], out_row(tok), sems.at[SEM_OUT + slot])

    def one_token(k, tok, slot, next_tok):
        ids_copy(next_tok, 1 - slot).start()
        for c in in_copies(next_tok, 1 - slot):
            c.start()
        for c in in_copies(tok, slot):
            c.wait()

        @pl.when(k > 0)
        def _():
            out_copy(tok, slot).wait()

        for chunk in range(n_chunks):
            if chunk + 1 < n_chunks:
                gather(slot, chunk + 1).start()
            else:
                ids_copy(next_tok, 1 - slot).wait()
                gather(1 - slot, 0).start()
            gather(slot, chunk).wait()
            chunk_rows(chunk, rows_v.at[chunk % 2], slot)
        out_copy(tok, slot).start()

    def token_pair(k, carry):
        tok = first + 2 * k
        one_token(k, tok, 0, tok + 1)
        one_token(k, tok + 1, 1, jnp.minimum(tok + 2, last))
        return carry

    pltpu.sync_copy(e_hbm.at[first], idx_v.at[0])
    gather(0, 0).start()
    for c in in_copies(first, 0):
        c.start()
    lax.fori_loop(0, per_worker // 2, token_pair, 0)
    gather(0, 0).wait()
    for c in in_copies(last, 0):
        c.wait()
    out_copy(last, 0).wait()
    out_copy(last, 1).wait()


def _sc_scratch(row_len, dtype, gather_rows):
    return [pltpu.VMEM((2, N_SLOTS), I32), pltpu.VMEM((2, gather_rows, row_len), dtype),
            pltpu.SemaphoreType.DMA((N_SEMS,))]


def _sc_u_dots(u_words, e_tok, x_words, first_token):
    nw = x_words.shape[1]
    n_tokens = e_tok.shape[0]
    per_worker = n_tokens // (SC_CORES * SC_SUBCORES)
    n_half = nw // SC_LANES // 2
    fold = 4
    mesh = plsc.VectorSubcoreMesh(core_axis_name="c", subcore_axis_name="s")

    def body(u_hbm, e_hbm, x_hbm, o_hbm, idx_v, rows_v, sems, x_v, out_v):
        lane = lax.iota(I32, SC_LANES)

        def chunk_rows(chunk, rows, slot):
            for part in range(rows.shape[0] // SC_LANES):
                total = jnp.zeros((SC_LANES,), F32)
                for half in range(2):
                    base = half * n_half * SC_LANES
                    xs = [plsc.bitcast(x_v[slot, pl.ds(base + c * SC_LANES, SC_LANES)], BF16) for c in range(n_half)]

                    def row(r, vec):
                        rr = part * SC_LANES + r
                        acc = jnp.zeros((SC_LANES,), F32)
                        for g in range(n_half // fold):
                            p = [plsc.bitcast(rows[rr, pl.ds(base + (fold * g + k) * SC_LANES, SC_LANES)], BF16)
                                 * xs[fold * g + k] for k in range(fold)]
                            w32 = plsc.bitcast((p[0] + p[1]) + (p[2] + p[3]), U32)
                            acc = acc + plsc.bitcast(w32 << 16, F32) + plsc.bitcast(w32 & jnp.uint32(0xFFFF0000), F32)
                        return jnp.where(lane == r, jnp.sum(acc), vec)

                    total = total + lax.fori_loop(0, SC_LANES, row, jnp.zeros((SC_LANES,), F32))
                out_v[slot, pl.ds(chunk * rows.shape[0] + part * SC_LANES, SC_LANES)] = total

        _sc_token_loop(u_hbm, e_hbm, idx_v, rows_v, sems, per_worker,
                       [(lambda tok: x_hbm.at[first_token + tok], x_v)], out_v, lambda tok: o_hbm.at[tok], chunk_rows)

    return pl.kernel(
        body,
        out_type=jax.ShapeDtypeStruct((n_tokens, N_SLOTS), F32),
        mesh=mesh,
        scratch_types=(_sc_scratch(nw, U32, SC_U_GATHER_ROWS)
                       + [pltpu.VMEM((2, nw), U32), pltpu.VMEM((2, N_SLOTS), F32)]),
        compiler_params=pltpu.CompilerParams(needs_layout_passes=False),
        name="peer_expert_in_sc",
    )(u_words, e_tok, x_words)


def _peer_gate_kernel(s_ref, g_ref, par_ref, a_ref):
    dots = s_ref[...].T
    low_t = lax.broadcasted_iota(I32, (SUBLANES, dots.shape[1]), 0) < GROUP
    _gate_rows(lambda grp: _dup_halves(dots[(grp // 2) * SUBLANES:(grp // 2 + 1) * SUBLANES, :], low_t)[grp % 2],
               g_ref, par_ref, a_ref)


def _peer_weights_kernel(a_ref, w_ref):
    low_t = lax.broadcasted_iota(I32, (SUBLANES, a_ref.shape[2]), 0) < GROUP
    tiles = []
    for v in range(N_SLOTS // SUBLANES):
        first = a_ref[0, 2 * v * SUBLANES:(2 * v + 1) * SUBLANES, :]
        second = a_ref[0, (2 * v + 1) * SUBLANES:(2 * v + 2) * SUBLANES, :]
        first = first + pltpu.roll(first, GROUP, 0)
        second = second + pltpu.roll(second, GROUP, 0)
        tiles.append(jnp.where(low_t, first, second))
    w_ref[...] = jnp.concatenate(tiles, axis=0).T


def _sc_v_sums(v_words, e_tok, w_tok, x2d, first_token):
    d = x2d.shape[1]
    nw = v_words.shape[1]
    n_tokens = e_tok.shape[0]
    per_worker = n_tokens // (SC_CORES * SC_SUBCORES)
    passes = 4
    fold = 4
    n_acc = nw // SC_LANES // passes
    mesh = plsc.VectorSubcoreMesh(core_axis_name="c", subcore_axis_name="s")

    def body(v_hbm, e_hbm, w_hbm, x_hbm, o_hbm, idx_v, rows_v, sems, w_v, x_v, out_v):
        def chunk_rows(chunk, rows, slot):
            src = x_v if chunk == 0 else out_v
            for ps in range(passes):
                base = ps * n_acc * SC_LANES

                def row_group(q, acc):
                    ws = []
                    for k in range(fold):
                        pos = jnp.full((SC_LANES,), chunk * rows.shape[0] + fold * q + k, I32)
                        w = plsc.load_gather(w_v.at[slot], [pos])
                        ws.append(plsc.pack(w, w, format=plsc.PackFormat.INTERLEAVED, preferred_element_type=BF16))
                    lo, hi = list(acc[:n_acc]), list(acc[n_acc:])
                    for c in range(n_acc):
                        p = [plsc.bitcast(rows[fold * q + k, pl.ds(base + c * SC_LANES, SC_LANES)], BF16) * ws[k]
                             for k in range(fold)]
                        w32 = plsc.bitcast((p[0] + p[1]) + (p[2] + p[3]), U32)
                        lo[c] = lo[c] + plsc.bitcast(w32 << 16, F32)
                        hi[c] = hi[c] + plsc.bitcast(w32 & jnp.uint32(0xFFFF0000), F32)
                    return tuple(lo + hi)

                acc0 = tuple([src[slot, pl.ds(base + c * SC_LANES, SC_LANES)] for c in range(n_acc)]
                             + [src[slot, pl.ds(nw + base + c * SC_LANES, SC_LANES)] for c in range(n_acc)])
                acc = lax.fori_loop(0, rows.shape[0] // fold, row_group, acc0)
                for c in range(n_acc):
                    out_v[slot, pl.ds(base + c * SC_LANES, SC_LANES)] = acc[c]
                    out_v[slot, pl.ds(nw + base + c * SC_LANES, SC_LANES)] = acc[n_acc + c]

        _sc_token_loop(v_hbm, e_hbm, idx_v, rows_v, sems, per_worker,
                       [(lambda tok: w_hbm.at[tok], w_v), (lambda tok: x_hbm.at[first_token + tok], x_v)],
                       out_v, lambda tok: o_hbm.at[tok], chunk_rows)

    return pl.kernel(
        body,
        out_type=jax.ShapeDtypeStruct((n_tokens, d), F32),
        mesh=mesh,
        scratch_types=(_sc_scratch(nw, U32, SC_V_GATHER_ROWS)
                       + [pltpu.VMEM((2, N_SLOTS), F32), pltpu.VMEM((2, d), F32), pltpu.VMEM((2, d), F32)]),
        compiler_params=pltpu.CompilerParams(needs_layout_passes=False),
        name="peer_expert_out_sc",
    )(v_words, e_tok, w_tok, x2d)


def _token_tiles(ref, xt_ref, to_tiles):
    tb, d = ref.shape
    for q in range(tb // SUBLANES):
        for c in range(d // LANES):
            rows = slice(q * SUBLANES, (q + 1) * SUBLANES)
            cols = slice(c * LANES, (c + 1) * LANES)
            strided = pl.ds(q * SUBLANES * SUBLANES + c, SUBLANES, stride=SUBLANES)
            if to_tiles:
                xt_ref[strided, :] = ref[rows, cols]
            else:
                ref[rows, cols] = xt_ref[strided, :]


def _peer_v_kernel(r8_ref, a_ref, x_ref, tab_ref, o_ref, bc_ref, xt_ref):
    tb = x_ref.shape[0]
    sub = _sublane_iota()
    low = sub < GROUP
    lane_t = lax.broadcasted_iota(I32, (SUBLANES, tb), 1)
    _token_tiles(x_ref, xt_ref, True)

    def spread(t, buf):
        for k in range(N_ROWS // SUBLANES):
            rows = slice(k * SUBLANES, (k + 1) * SUBLANES)
            col = jnp.sum(jnp.where(lane_t == t, a_ref[0, rows, :], 0.0), axis=1, keepdims=True)
            bits = lax.bitcast_convert_type(col.astype(BF16).astype(F32), U32)
            bc_ref[buf, rows, :] = jnp.broadcast_to(bits | (bits >> 16), (SUBLANES, LANES))

    def accumulate(t, buf):
        acc_lo = jnp.zeros((SUBLANES, LANES), F32)
        acc_hi = jnp.zeros((SUBLANES, LANES), F32)
        ids = r8_ref.at[0, t]
        for grp in range(N_GROUPS):
            prods = []
            for i in range(GROUP):
                row = grp * SUBLANES + i
                r8 = pl.multiple_of(ids[grp * GROUP + i], SUBLANES)
                am = jnp.where(low, bc_ref[buf, row:row + 1, :], bc_ref[buf, row + GROUP:row + GROUP + 1, :])
                prods.append(_as_bf16(tab_ref[pl.ds(r8, SUBLANES), :]) * _as_bf16(am))
            lo, hi = _unpack_words(_as_words((prods[0] + prods[1]) + (prods[2] + prods[3])))
            acc_lo = acc_lo + lo
            acc_hi = acc_hi + hi
        acc_lo = acc_lo + pltpu.roll(acc_lo, GROUP, 0)
        acc_hi = acc_hi + pltpu.roll(acc_hi, GROUP, 0)
        tile = pl.ds(pl.multiple_of(t * SUBLANES, SUBLANES), SUBLANES)
        xt_ref[tile, :] = xt_ref[tile, :] + jnp.where(low, acc_lo, acc_hi)

    def token_group(i, carry):
        t = N_SPREAD * i
        for k in range(N_SPREAD):
            spread(jnp.minimum(t + k + SPREAD_AHEAD, tb - 1), (k + SPREAD_AHEAD) % N_SPREAD)
            accumulate(t + k, k)
        return carry

    for k in range(SPREAD_AHEAD):
        spread(k, k)
    lax.fori_loop(0, tb // N_SPREAD, token_group, 0)
    _token_tiles(o_ref, xt_ref, False)


def _peer_experts(x2d, xp, r8, e_tok, par, gates, dots_sc, v_words, u_tab, v_tab):
    t, d = x2d.shape
    nblk = t // PEER_TOKENS
    nblk_tc = nblk - SC_TOKEN_BLOCKS
    smem_block = pl.BlockSpec((1, PEER_TOKENS, N_SLOTS), lambda i: (i, 0, 0), memory_space=pltpu.SMEM)
    slot_block = pl.BlockSpec((1, N_SLOTS, PEER_TOKENS), lambda i: (i, 0, 0))
    row_block = pl.BlockSpec((1, N_ROWS, PEER_TOKENS), lambda i: (i, 0, 0))
    tile_block = pl.BlockSpec((PEER_TOKENS * SUBLANES, LANES), lambda i: (i, 0))
    tok_block = pl.BlockSpec((PEER_TOKENS, d), lambda i: (i, 0))
    whole = pl.BlockSpec(memory_space=pltpu.VMEM)
    params = pltpu.CompilerParams(dimension_semantics=("parallel",), vmem_limit_bytes=56 * MIB)

    act_tc = pl.pallas_call(
        _peer_u_kernel,
        grid=(nblk_tc,),
        in_specs=[smem_block, tile_block, slot_block, slot_block, whole],
        out_specs=row_block,
        out_shape=jax.ShapeDtypeStruct((nblk_tc, N_ROWS, PEER_TOKENS), F32),
        scratch_shapes=[pltpu.VMEM((2 * N_ROWS, PEER_TOKENS), F32),
                        pltpu.VMEM((2, 2 * N_GROUPS, SUBLANES, LANES), F32)],
        compiler_params=params,
        name="peer_expert_in",
    )(r8, xp, gates, par, u_tab)
    shifted = pl.BlockSpec((1, N_SLOTS, PEER_TOKENS), lambda i: (i + nblk_tc, 0, 0))
    act_sc = pl.pallas_call(
        _peer_gate_kernel,
        grid=(SC_TOKEN_BLOCKS,),
        in_specs=[pl.BlockSpec((PEER_TOKENS, N_SLOTS), lambda i: (i, 0)), shifted, shifted],
        out_specs=row_block,
        out_shape=jax.ShapeDtypeStruct((SC_TOKEN_BLOCKS, N_ROWS, PEER_TOKENS), F32),
        compiler_params=pltpu.CompilerParams(dimension_semantics=("parallel",)),
        name="peer_gate_sc",
    )(dots_sc, gates, par)
    act = jnp.concatenate([act_tc, act_sc], axis=0)

    nblk_v = nblk - SC_V_BLOCKS
    w_sc = pl.pallas_call(
        _peer_weights_kernel,
        grid=(SC_V_BLOCKS,),
        in_specs=[pl.BlockSpec((1, N_ROWS, PEER_TOKENS), lambda i: (i + nblk_v, 0, 0))],
        out_specs=pl.BlockSpec((PEER_TOKENS, N_SLOTS), lambda i: (i, 0)),
        out_shape=jax.ShapeDtypeStruct((SC_V_BLOCKS * PEER_TOKENS, N_SLOTS), F32),
        compiler_params=pltpu.CompilerParams(dimension_semantics=("parallel",)),
        name="peer_weights_sc",
    )(act)
    y_sc = _sc_v_sums(v_words, e_tok[nblk_v:].reshape(-1, N_SLOTS), w_sc, x2d, nblk_v * PEER_TOKENS)
    y_tc = pl.pallas_call(
        _peer_v_kernel,
        grid=(nblk_v,),
        in_specs=[smem_block, row_block, tok_block, whole],
        out_specs=tok_block,
        out_shape=jax.ShapeDtypeStruct((nblk_v * PEER_TOKENS, d), F32),
        scratch_shapes=[pltpu.VMEM((N_SPREAD, N_ROWS, LANES), U32),
                        pltpu.VMEM((PEER_TOKENS * SUBLANES, LANES), F32)],
        compiler_params=params,
        name="peer_expert_out",
    )(r8, act, x2d, v_tab)
    return jnp.concatenate([y_tc, y_sc], axis=0)


def _peer_ffn(x2d, norm_g, w_q, sub_keys, u_tabs, v_tabs, layer):
    xp, scores_t, x_words = _peer_scores(x2d, norm_g, w_q, sub_keys)
    u_tiles, u_words = _pack_table(u_tabs, layer)
    v_tiles, v_words = _pack_table(v_tabs, layer)
    nblk = x2d.shape[0] // PEER_TOKENS
    nblk_tc = nblk - SC_TOKEN_BLOCKS
    top_sc = _peer_topk(scores_t, nblk_tc, SC_TOKEN_BLOCKS)
    dots_sc = _sc_u_dots(u_words, top_sc[3].reshape(-1, N_SLOTS), x_words, nblk_tc * PEER_TOKENS)
    top_tc = _peer_topk(scores_t, 0, nblk_tc)
    r8, par, gates, e_tok = [jnp.concatenate([a, b], axis=0) for a, b in zip(top_tc, top_sc)]
    return _peer_experts(x2d, xp, r8, e_tok, par, gates, dots_sc, v_words, u_tiles, v_tiles)


def kernel(x, attn_norm_g, attn_w_qkv, attn_q_g, attn_k_g, attn_rel_bias, attn_w_out, rec_norm_g, rec_w_in, rec_conv_w, rec_conv_b, rec_w_a, rec_b_a, rec_w_x, rec_b_x, rec_lambda, rec_w_out, ffn_norm_g, peer_w_q, peer_sub_keys, peer_u, peer_v):
    b, s, d = x.shape
    depth = ffn_norm_g.shape[0]
    for layer in range(depth):
        j = layer // 2
        if layer % 2 == 0:
            qkv = _qkv_proj(x.reshape(b * s, d), attn_norm_g[j], attn_w_qkv[j], attn_q_g[j], attn_k_g[j])
            x = _attention(x, qkv, attn_rel_bias[j], attn_w_out[j])
        else:
            x = _rglru(x, rec_norm_g[j], rec_w_in[j], rec_conv_w[j], rec_conv_b[j], rec_w_a[j],
                       rec_b_a[j], rec_w_x[j], rec_b_x[j], rec_lambda[j], rec_w_out[j])
        x = _peer_ffn(x.reshape(b * s, d), ffn_norm_g[layer], peer_w_q[layer], peer_sub_keys[layer],
                      peer_u, peer_v, layer).reshape(b, s, d)
    return x
```

```python
import numpy as np
import jax
import jax.numpy as jnp
from jax import lax
from jax.experimental import pallas as pl
from jax.experimental.pallas import tpu as pltpu
from jax.experimental.pallas import tpu_sc as plsc

F32 = jnp.float32
BF16 = jnp.bfloat16
U32 = jnp.uint32
I32 = jnp.int32

EPS = 1e-6
NEG_INF = -1e30

D_MODEL = 1024
CHUNK = 64
ATTN_HEADS = 16
HEAD_DIM = 64
LEFT_CHUNKS = 8
REL_CLIP = 256
LRU_HEADS = 4
LRU_BLOCK = 256
CONV_WIDTH = 4
LRU_C = 8.0
PEER_HEADS = 8
N_KEYS = 128
PEER_TOPK = 16
D_HALF = 128

SUBLANES = 8
LANES = 128

Q_TILE = 4 * CHUNK
N_KEY_BLOCKS = 3
ROW_TILE = 512
PEER_TOKENS = 128
LRU_TILE = 256
N_SLOTS = PEER_HEADS * PEER_TOPK

MIB = 1024 * 1024

SC_CORES = 2
SC_SUBCORES = 16
SC_LANES = 16
SC_U_GATHER_ROWS = 32
SC_V_GATHER_ROWS = 64
SC_TOKEN_BLOCKS = 140
SC_V_BLOCKS = 138


def _rms_rows(x, g):
    ms = jnp.mean(x * x, axis=-1, keepdims=True)
    return (x * lax.rsqrt(ms + EPS)) * g


def _split_bf16(v):
    hi = v.astype(BF16)
    lo = (v - hi.astype(F32)).astype(BF16)
    return hi, lo


def _dot(a, b):
    return jnp.dot(a, b, preferred_element_type=F32)


def _dot_nt(a, b):
    return lax.dot_general(a, b, (((1,), (1,)), ((), ())), preferred_element_type=F32)


def _qkv_kernel(x_ref, g_ref, w_ref, gain_ref, bsel_ref, bexp_ref, o_ref, xn_ref):
    j = pl.program_id(1)

    @pl.when(j == 0)
    def _():
        xn_ref[...] = _rms_rows(x_ref[...], g_ref[...]).astype(BF16)

    y = _dot(xn_ref[...], w_ref[...])

    @pl.when(j < 2)
    def _():
        hi, lo = _split_bf16(y * y)
        ms = _dot(hi, bsel_ref[...]) + _dot(lo, bsel_ref[...])
        rhi, rlo = _split_bf16(lax.rsqrt(ms + EPS))
        rs = _dot(rhi, bexp_ref[...]) + _dot(rlo, bexp_ref[...])
        o_ref[...] = ((y * rs) * gain_ref[0]).astype(BF16)

    @pl.when(j == 2)
    def _():
        o_ref[...] = y.astype(BF16)


def _qkv_proj(x2d, norm_g, w_qkv, q_g, k_g):
    t, d = x2d.shape
    scale = HEAD_DIM ** -0.5
    gains = jnp.stack([jnp.tile(q_g, ATTN_HEADS) * scale, jnp.tile(k_g, ATTN_HEADS),
                       jnp.ones((d,), F32)]).reshape(3, 1, d)
    head_of_col = np.arange(d) // HEAD_DIM
    bsel = (head_of_col[:, None] == np.arange(LANES)[None, :]).astype(np.float32) / HEAD_DIM
    bexp = (np.arange(LANES)[:, None] == head_of_col[None, :]).astype(np.float32)
    return pl.pallas_call(
        _qkv_kernel,
        grid=(t // ROW_TILE, 3),
        in_specs=[
            pl.BlockSpec((ROW_TILE, d), lambda i, j: (i, 0)),
            pl.BlockSpec((1, d), lambda i, j: (0, 0)),
            pl.BlockSpec((d, d), lambda i, j: (0, j)),
            pl.BlockSpec((1, 1, d), lambda i, j: (j, 0, 0)),
            pl.BlockSpec((d, LANES), lambda i, j: (0, 0)),
            pl.BlockSpec((LANES, d), lambda i, j: (0, 0)),
        ],
        out_specs=pl.BlockSpec((ROW_TILE, d), lambda i, j: (i, j)),
        out_shape=jax.ShapeDtypeStruct((t, 3 * d), BF16),
        scratch_shapes=[pltpu.VMEM((ROW_TILE, d), BF16)],
        compiler_params=pltpu.CompilerParams(
            dimension_semantics=("parallel", "arbitrary"), vmem_limit_bytes=40 * MIB),
        name="qkv_proj",
    )(x2d, norm_g.reshape(1, d), w_qkv.astype(BF16), gains,
      jnp.asarray(bsel, BF16), jnp.asarray(bexp, BF16))


def _attn_kernel(x_ref, q_ref, k0_ref, k1_ref, k2_ref, v0_ref, v1_ref, v2_ref,
                 bias_ref, wout_ref, o_ref, att_ref):
    qt = pl.program_id(1)
    k_refs = (k0_ref, k1_ref, k2_ref)
    v_refs = (v0_ref, v1_ref, v2_ref)
    negs = [jnp.where(qt - (N_KEY_BLOCKS - 1) + kb < 0, NEG_INF, 0.0).astype(F32)
            for kb in range(N_KEY_BLOCKS)]
    lane = lax.broadcasted_iota(I32, (Q_TILE, LANES), 1)
    first_half = lane < HEAD_DIM
    for p in range(ATTN_HEADS // 2):
        cols = slice(p * LANES, (p + 1) * LANES)
        qp = q_ref[0, :, cols]
        outs = []
        for hh in range(2):
            h = 2 * p + hh
            keep = first_half if hh == 0 else jnp.logical_not(first_half)
            qm = jnp.where(keep, qp, jnp.zeros_like(qp))
            s = [_dot_nt(qm, k_refs[kb][0, :, cols]) + bias_ref[kb, h] + negs[kb]
                 for kb in range(N_KEY_BLOCKS)]
            m = jnp.max(s[0], axis=-1, keepdims=True)
            for kb in range(1, N_KEY_BLOCKS):
                m = jnp.maximum(m, jnp.max(s[kb], axis=-1, keepdims=True))
            acc = jnp.zeros((Q_TILE, LANES), F32)
            l = jnp.zeros((Q_TILE, 1), F32)
            for kb in range(N_KEY_BLOCKS):
                e = jnp.exp(s[kb] - m)
                l = l + jnp.sum(e, axis=-1, keepdims=True)
                acc = acc + _dot(e.astype(BF16), v_refs[kb][0, :, cols])
            outs.append(acc / l)
        att_ref[:, cols] = jnp.where(first_half, outs[0], outs[1]).astype(BF16)
    o_ref[0] = x_ref[0] + _dot(att_ref[...], wout_ref[...])


def _band_bias(table):
    qi = np.arange(Q_TILE)[:, None]
    kj = np.arange(Q_TILE)[None, :]
    period = 2 * Q_TILE
    m = np.arange(period)
    delta = np.where(m < Q_TILE, m, m - period)
    blocks = []
    for kb in range(N_KEY_BLOCKS):
        dist = (N_KEY_BLOCKS - 1 - kb) * Q_TILE - delta
        diag = table[:, np.clip(dist, -REL_CLIP, REL_CLIP) + REL_CLIP]
        toep = jnp.tile(diag, (1, Q_TILE))[:, :Q_TILE * (period - 1)]
        toep = toep.reshape(-1, Q_TILE, period - 1)[:, :, :Q_TILE]
        dchunk = (Q_TILE // CHUNK) * (kb - (N_KEY_BLOCKS - 1)) + kj // CHUNK - qi // CHUNK
        valid = (dchunk >= -LEFT_CHUNKS) & (dchunk <= 0)
        blocks.append(jnp.where(valid[None], toep, NEG_INF))
    return jnp.stack(blocks).astype(F32)


def _attention(x, qkv, rel_bias_table, w_out):
    b, s, d = x.shape
    nq = s // Q_TILE
    qkv3 = qkv.reshape(b, s, 3 * d)
    bias = _band_bias(rel_bias_table)

    def kv_spec(kb, col):
        return pl.BlockSpec(
            (1, Q_TILE, d),
            lambda bi, qt: (bi, jnp.maximum(qt - (N_KEY_BLOCKS - 1) + kb, 0), col))

    in_specs = [pl.BlockSpec((1, Q_TILE, d), lambda bi, qt: (bi, qt, 0)),
                pl.BlockSpec((1, Q_TILE, d), lambda bi, qt: (bi, qt, 0))]
    in_specs += [kv_spec(kb, 1) for kb in range(N_KEY_BLOCKS)]
    in_specs += [kv_spec(kb, 2) for kb in range(N_KEY_BLOCKS)]
    in_specs += [pl.BlockSpec(memory_space=pltpu.VMEM), pl.BlockSpec(memory_space=pltpu.VMEM)]
    return pl.pallas_call(
        _attn_kernel,
        grid=(b, nq),
        in_specs=in_specs,
        out_specs=pl.BlockSpec((1, Q_TILE, d), lambda bi, qt: (bi, qt, 0)),
        out_shape=jax.ShapeDtypeStruct((b, s, d), F32),
        scratch_shapes=[pltpu.VMEM((Q_TILE, d), BF16)],
        compiler_params=pltpu.CompilerParams(
            dimension_semantics=("parallel", "parallel"), vmem_limit_bytes=48 * MIB),
        name="band_attention",
    )(x, qkv3, qkv3, qkv3, qkv3, qkv3, qkv3, qkv3, bias, w_out.astype(BF16))


def _rglru_kernel(x_ref, g_ref, win_ref, cw_ref, cb_ref, wa_ref, ba_ref, wx_ref, bx_ref,
                  lam_ref, wout_ref, o_ref, ext_ref, h_ref, a_scr, b_scr, hs_scr):
    ts = x_ref.shape[1]
    w = LRU_HEADS * LRU_BLOCK

    @pl.when(pl.program_id(1) == 0)
    def _():
        ext_ref[...] = jnp.zeros_like(ext_ref)
        h_ref[...] = jnp.zeros_like(h_ref)

    x = x_ref[0]
    xn = _rms_rows(x, g_ref[...]).astype(BF16)
    gu = _dot(xn, win_ref[...])
    gate = gu[:, :w]
    u_raw = gu[:, w:]
    ext_ref[0:SUBLANES, :] = ext_ref[ts:ts + SUBLANES, :]
    ext_ref[SUBLANES:, :] = u_raw
    u = cb_ref[...] + jnp.zeros((ts, w), F32)
    for k in range(CONV_WIDTH):
        off = SUBLANES - (CONV_WIDTH - 1) + k
        u = u + cw_ref[k:k + 1, :] * ext_ref[off:off + ts, :]
    ub = u.astype(BF16)

    def block_diag(wref):
        return jnp.concatenate(
            [_dot(ub[:, hh * LRU_BLOCK:(hh + 1) * LRU_BLOCK], wref[hh]) for hh in range(LRU_HEADS)],
            axis=1)

    r = jax.nn.sigmoid(block_diag(wa_ref) + ba_ref[...])
    i = jax.nn.sigmoid(block_diag(wx_ref) + bx_ref[...])
    z = -lam_ref[...]
    softplus = jnp.maximum(z, 0.0) + jnp.log1p(jnp.exp(-jnp.abs(z)))
    log_a = (-LRU_C) * r * softplus
    a_scr[...] = jnp.exp(log_a)
    th = jnp.tanh(log_a)
    b_scr[...] = jnp.sqrt(2.0 * th / (th - 1.0)) * (i * u)

    def step(t, h):
        h = a_scr[pl.ds(t, 1), :] * h + b_scr[pl.ds(t, 1), :]
        hs_scr[pl.ds(t, 1), :] = h
        return h

    h_ref[0:1, :] = lax.fori_loop(0, ts, step, h_ref[0:1, :], unroll=8)
    y = jax.nn.gelu(gate) * hs_scr[...]
    o_ref[0] = x + _dot(y.astype(BF16), wout_ref[...])


def _rglru(x, norm_g, w_in, conv_w, conv_b, w_a, b_a, w_x, b_x, lam, w_out):
    b, s, d = x.shape
    w = LRU_HEADS * LRU_BLOCK
    row = lambda v: v.reshape(1, -1)
    whole = pl.BlockSpec(memory_space=pltpu.VMEM)
    return pl.pallas_call(
        _rglru_kernel,
        grid=(b, s // LRU_TILE),
        in_specs=[pl.BlockSpec((1, LRU_TILE, d), lambda bi, si: (bi, si, 0))] + [whole] * 10,
        out_specs=pl.BlockSpec((1, LRU_TILE, d), lambda bi, si: (bi, si, 0)),
        out_shape=jax.ShapeDtypeStruct((b, s, d), F32),
        scratch_shapes=[pltpu.VMEM((LRU_TILE + SUBLANES, w), F32), pltpu.VMEM((SUBLANES, w), F32),
                        pltpu.VMEM((LRU_TILE, w), F32), pltpu.VMEM((LRU_TILE, w), F32),
                        pltpu.VMEM((LRU_TILE, w), F32)],
        compiler_params=pltpu.CompilerParams(
            dimension_semantics=("arbitrary", "arbitrary"), vmem_limit_bytes=48 * MIB),
        name="rglru_mixer",
    )(x, row(norm_g), w_in.astype(BF16), conv_w, row(conv_b), w_a.astype(BF16), row(b_a),
      w_x.astype(BF16), row(b_x), row(lam), w_out.astype(BF16))


def _peer_scores_kernel(x_ref, g_ref, wq_ref, sk_ref, xp_ref, st_ref, xw_ref):
    xb = _rms_rows(x_ref[...], g_ref[...]).astype(BF16)
    words = _bf16_pair_words(xb)
    xw_ref[...] = words
    tm = words.shape[0]
    for c in range(SUBLANES):
        piece = words[:, (c % (SUBLANES // 2)) * LANES:(c % (SUBLANES // 2) + 1) * LANES]
        xp_ref[pl.ds(c, tm, stride=SUBLANES), :] = piece
    q = _dot(xb, wq_ref[...]).astype(BF16)
    for hp in range(2 * PEER_HEADS):
        st_ref[hp] = _dot_nt(sk_ref[hp % 2], q[:, hp * D_HALF:(hp + 1) * D_HALF])


def _peer_scores(x2d, norm_g, w_q, sub_keys):
    t, d = x2d.shape
    nq = w_q.shape[1]
    whole = pl.BlockSpec(memory_space=pltpu.VMEM)
    return pl.pallas_call(
        _peer_scores_kernel,
        grid=(t // ROW_TILE,),
        in_specs=[pl.BlockSpec((ROW_TILE, d), lambda i: (i, 0)), whole, whole, whole],
        out_specs=[pl.BlockSpec((ROW_TILE * SUBLANES, LANES), lambda i: (i, 0)),
                   pl.BlockSpec((2 * PEER_HEADS, N_KEYS, ROW_TILE), lambda i: (0, 0, i)),
                   pl.BlockSpec((ROW_TILE, d // 2), lambda i: (i, 0))],
        out_shape=[jax.ShapeDtypeStruct((t * SUBLANES, LANES), U32),
                   jax.ShapeDtypeStruct((2 * PEER_HEADS, N_KEYS, t), F32),
                   jax.ShapeDtypeStruct((t, d // 2), U32)],
        compiler_params=pltpu.CompilerParams(
            dimension_semantics=("parallel",), vmem_limit_bytes=48 * MIB),
        name="peer_scores",
    )(x2d, norm_g.reshape(1, d), w_q.astype(BF16), sub_keys.astype(BF16))


def _candidate_blocks():
    blocks = []
    for i in range(PEER_TOPK // 2):
        nj = PEER_TOPK // (i + 1)
        for j0 in range(0, nj, SUBLANES):
            blocks.append(("row", i, j0, min(SUBLANES, nj - j0)))
    blocks.append(("col", PEER_TOPK // 2, PEER_TOPK // 2))
    return blocks


def _peer_topk_kernel(st_ref, r8_ref, par_ref, g_ref, e_ref, r8t_ref, et_ref):
    tk = st_ref.shape[2]
    key_id = lax.broadcasted_iota(I32, (N_KEYS, tk), 0).astype(F32)
    row16 = lax.broadcasted_iota(I32, (PEER_TOPK, tk), 0)
    row8 = lax.broadcasted_iota(I32, (SUBLANES, tk), 0)
    row8_f = row8.astype(F32)
    neg_inf = jnp.float32(-jnp.inf)
    blocks = _candidate_blocks()
    big = jnp.float32(PEER_TOPK * PEER_TOPK)

    def top16(x):
        vals = jnp.zeros((PEER_TOPK, tk), F32)
        idxs = jnp.zeros((PEER_TOPK, tk), F32)
        for k in range(PEER_TOPK):
            m = jnp.max(x, axis=0, keepdims=True)
            idx = jnp.min(jnp.where(x == m, key_id, float(N_KEYS)), axis=0, keepdims=True)
            x = jnp.where(key_id == idx, neg_inf, x)
            vals = jnp.where(row16 == k, m, vals)
            idxs = jnp.where(row16 == k, idx, idxs)
        return vals, idxs

    for h in range(PEER_HEADS):
        s0, i0 = top16(st_ref[2 * h])
        s1, i1 = top16(st_ref[2 * h + 1])
        cand, flat, expert = [], [], []
        for blk in blocks:
            if blk[0] == "row":
                _, i, j0, n = blk
                c = s0[i:i + 1, :] + s1[j0:j0 + SUBLANES, :]
                f = float(PEER_TOPK * i + j0) + row8_f
                e = i0[i:i + 1, :] * float(N_KEYS) + i1[j0:j0 + SUBLANES, :]
            else:
                _, i_start, n = blk
                c = s0[i_start:i_start + SUBLANES, :] + s1[0:1, :]
                f = float(PEER_TOPK) * (float(i_start) + row8_f)
                e = i0[i_start:i_start + SUBLANES, :] * float(N_KEYS) + i1[0:1, :]
            cand.append(jnp.where(row8 < n, c, neg_inf))
            flat.append(f)
            expert.append(e)
        cand = jnp.concatenate(cand, axis=0)
        flat = jnp.concatenate(flat, axis=0)
        expert = jnp.concatenate(expert, axis=0)
        best_s = jnp.zeros((PEER_TOPK, tk), F32)
        best_e = jnp.zeros((PEER_TOPK, tk), F32)
        for k in range(PEER_TOPK):
            m = jnp.max(cand, axis=0, keepdims=True)
            fidx = jnp.min(jnp.where(cand == m, flat, big), axis=0, keepdims=True)
            pick = flat == fidx
            e_sel = jnp.max(jnp.where(pick, expert, -1.0), axis=0, keepdims=True)
            cand = jnp.where(pick, neg_inf, cand)
            best_s = jnp.where(row16 == k, m, best_s)
            best_e = jnp.where(row16 == k, e_sel, best_e)
        ex = jnp.exp(best_s - best_s[0:1, :])
        gate = ex / jnp.sum(ex, axis=0, keepdims=True)
        rows = slice(h * PEER_TOPK, (h + 1) * PEER_TOPK)
        best_i = best_e.astype(I32)
        r8t_ref[rows, :] = (best_i >> 1) * SUBLANES
        et_ref[rows, :] = best_i
        par_ref[0, rows, :] = best_i & 1
        g_ref[0, rows, :] = gate
    r8_ref[0] = r8t_ref[...].T
    e_ref[0] = et_ref[...].T


def _peer_topk(scores_t, first_block, nblk):
    slot_major = pl.BlockSpec((1, N_SLOTS, PEER_TOKENS), lambda i: (i, 0, 0))
    token_major = pl.BlockSpec((1, PEER_TOKENS, N_SLOTS), lambda i: (i, 0, 0))
    return pl.pallas_call(
        _peer_topk_kernel,
        grid=(nblk,),
        in_specs=[pl.BlockSpec((2 * PEER_HEADS, N_KEYS, PEER_TOKENS), lambda i: (0, 0, i + first_block))],
        out_specs=[token_major, slot_major, slot_major, token_major],
        out_shape=[jax.ShapeDtypeStruct((nblk, PEER_TOKENS, N_SLOTS), I32),
                   jax.ShapeDtypeStruct((nblk, N_SLOTS, PEER_TOKENS), I32),
                   jax.ShapeDtypeStruct((nblk, N_SLOTS, PEER_TOKENS), F32),
                   jax.ShapeDtypeStruct((nblk, PEER_TOKENS, N_SLOTS), I32)],
        scratch_shapes=[pltpu.VMEM((N_SLOTS, PEER_TOKENS), I32), pltpu.VMEM((N_SLOTS, PEER_TOKENS), I32)],
        compiler_params=pltpu.CompilerParams(dimension_semantics=("parallel",)),
        name="peer_topk",
    )(scores_t)


def _bf16_pair_words(x):
    bits = lax.bitcast_convert_type(x.astype(BF16).astype(F32), U32)
    half = bits.shape[1] // 2
    return (bits[:, :half] >> 16) | (bits[:, half:] & jnp.uint32(0xFFFF0000))


def _pack_kernel(t_ref, o_ref, w_ref):
    words = _bf16_pair_words(t_ref[0])
    w_ref[...] = words
    rows = words.shape[0]
    n = words.shape[1] // LANES
    for s in range(n):
        o_ref[pl.ds(s, rows, stride=n), :] = words[:, s * LANES:(s + 1) * LANES]


def _pack_table(tabs, layer):
    _, e, d = tabs.shape
    n = d // (2 * LANES)
    return pl.pallas_call(
        _pack_kernel,
        grid=(e // ROW_TILE,),
        in_specs=[pl.BlockSpec((1, ROW_TILE, d), lambda i: (layer, i, 0))],
        out_specs=[pl.BlockSpec((ROW_TILE * n, LANES), lambda i: (i, 0)),
                   pl.BlockSpec((ROW_TILE, d // 2), lambda i: (i, 0))],
        out_shape=[jax.ShapeDtypeStruct((e * n, LANES), U32), jax.ShapeDtypeStruct((e, d // 2), U32)],
        compiler_params=pltpu.CompilerParams(dimension_semantics=("parallel",)),
        name="peer_pack_table",
    )(tabs)


def _unpack_words(w):
    lo = lax.bitcast_convert_type(w << 16, F32)
    hi = lax.bitcast_convert_type(w & jnp.uint32(0xFFFF0000), F32)
    return lo, hi


def _as_bf16(words):
    return pltpu.bitcast(words, BF16)


def _as_words(packed):
    return pltpu.bitcast(packed, U32)


def _merge_packed(a, b, shift, mask):
    ta = a + _as_bf16(pltpu.roll(_as_words(a), shift, 0))
    tb = b + _as_bf16(pltpu.roll(_as_words(b), SUBLANES - shift, 0))
    return jnp.where(mask, _as_words(ta), _as_words(tb))


def _sublane_iota():
    return lax.broadcasted_iota(I32, (SUBLANES, LANES), 0)


def _merge(a, b, shift, mask):
    ta = a + pltpu.roll(a, shift, 0)
    tb = b + pltpu.roll(b, SUBLANES - shift, 0)
    return jnp.where(mask, ta, tb)


GROUP = SUBLANES // 2
N_GROUPS = N_SLOTS // GROUP
N_ROWS = 2 * N_SLOTS
N_SPREAD = 4
SPREAD_AHEAD = 2


def _dup_halves(v, low):
    r = pltpu.roll(v, GROUP, 0)
    return jnp.where(low, v, r), jnp.where(low, r, v)


def _peer_u_kernel(r8_ref, xp_ref, g_ref, par_ref, tab_ref, a_ref, s_ref, part_ref):
    tb = g_ref.shape[2]
    sub = _sublane_iota()
    m2 = (sub & 2) != 0
    lane_t = lax.broadcasted_iota(I32, (SUBLANES, tb), 1)
    n_part = 2 * N_GROUPS

    def products(t, slot):
        xb = _as_bf16(xp_ref[pl.ds(pl.multiple_of(t * SUBLANES, SUBLANES), SUBLANES), :])
        ids = r8_ref.at[0, t]
        for grp in range(N_GROUPS):
            prods = []
            for i in range(GROUP):
                r8 = pl.multiple_of(ids[grp * GROUP + i], SUBLANES)
                prods.append(_as_bf16(tab_ref[pl.ds(r8, SUBLANES), :]) * xb)
            halves = (_merge_packed(prods[3], prods[1], 2, m2), _merge_packed(prods[2], prods[0], 2, m2))
            for k in range(2):
                lo, hi = _unpack_words(halves[k])
                part_ref[slot, 2 * grp + k] = lo + hi

    def reduce_lanes(t, slot):
        for n in range(n_part):
            col = jnp.sum(part_ref[slot, n], axis=1, keepdims=True)
            rows = slice(n * SUBLANES, (n + 1) * SUBLANES)
            s_ref[rows, :] = jnp.where(lane_t == t, col, s_ref[rows, :])

    def token_pair(i, carry):
        t = 2 * i
        reduce_lanes(t - 2, 0)
        reduce_lanes(t - 1, 1)
        products(t, 0)
        products(t + 1, 1)
        return carry

    s_ref[...] = jnp.zeros_like(s_ref)
    part_ref[...] = jnp.zeros_like(part_ref)
    lax.fori_loop(0, tb // 2, token_pair, 0)
    reduce_lanes(tb - 2, 0)
    reduce_lanes(tb - 1, 1)

    odd_t = (lax.broadcasted_iota(I32, (SUBLANES, tb), 0) & 1) != 0
    _gate_rows(lambda grp: _merge(s_ref[2 * grp * SUBLANES:(2 * grp + 1) * SUBLANES, :],
                                  s_ref[(2 * grp + 1) * SUBLANES:(2 * grp + 2) * SUBLANES, :], 1, odd_t),
               g_ref, par_ref, a_ref)


def _gate_rows(dots_of_group, g_ref, par_ref, a_ref):
    tb = g_ref.shape[2]
    low_t = lax.broadcasted_iota(I32, (SUBLANES, tb), 0) < GROUP
    want = jnp.where(low_t, 0, 1)
    for v in range(N_SLOTS // SUBLANES):
        src = slice(v * SUBLANES, (v + 1) * SUBLANES)
        gates = _dup_halves(g_ref[0, src, :], low_t)
        pars = _dup_halves(par_ref[0, src, :], low_t)
        for half in range(2):
            grp = 2 * v + half
            act = gates[half] * jax.nn.gelu(dots_of_group(grp))
            rows = slice(grp * SUBLANES, (grp + 1) * SUBLANES)
            a_ref[0, rows, :] = jnp.where(pars[half] == want, act, 0.0)


SEM_IDS, SEM_IN, SEM_OUT, N_SEMS = 2, 3, 5, 7


def _sc_token_loop(tab_hbm, e_hbm, idx_v, rows_v, sems, per_worker, token_in, out_buf, out_row, chunk_rows):
    wid = lax.axis_index("s") * SC_CORES + lax.axis_index("c")
    gather_rows = rows_v.shape[1]
    n_chunks = N_SLOTS // gather_rows
    first = wid * per_worker
    last = first + per_worker - 1

    def gather(slot, chunk):
        ids = idx_v.at[slot, pl.ds(chunk * gather_rows, gather_rows)]
        return pltpu.make_async_copy(tab_hbm.at[ids], rows_v.at[chunk % 2], sems.at[chunk % 2])

    def ids_copy(tok, slot):
        return pltpu.make_async_copy(e_hbm.at[tok], idx_v.at[slot], sems.at[SEM_IDS])

    def in_copies(tok, slot):
        return [pltpu.make_async_copy(row_of(tok), buf.at[slot], sems.at[SEM_IN + slot]) for row_of, buf in token_in]

    def out_copy(tok, slot):
        return pltpu.make_async_copy(out_buf.at[slot], out_row(tok), sems.at[SEM_OUT + slot])

    def one_token(k, tok, slot, next_tok):
        ids_copy(next_tok, 1 - slot).start()
        for c in in_copies(next_tok, 1 - slot):
            c.start()
        for c in in_copies(tok, slot):
            c.wait()

        @pl.when(k > 0)
        def _():
            out_copy(tok, slot).wait()

        for chunk in range(n_chunks):
            if chunk + 1 < n_chunks:
                gather(slot, chunk + 1).start()
            else:
                ids_copy(next_tok, 1 - slot).wait()
                gather(1 - slot, 0).start()
            gather(slot, chunk).wait()
            chunk_rows(chunk, rows_v.at[chunk % 2], slot)
        out_copy(tok, slot).start()

    def token_pair(k, carry):
        tok = first + 2 * k
        one_token(k, tok, 0, tok + 1)
        one_token(k, tok + 1, 1, jnp.minimum(tok + 2, last))
        return carry

    pltpu.sync_copy(e_hbm.at[first], idx_v.at[0])
    gather(0, 0).start()
    for c in in_copies(first, 0):
        c.start()
    lax.fori_loop(0, per_worker // 2, token_pair, 0)
    gather(0, 0).wait()
    for c in in_copies(last, 0):
        c.wait()
    out_copy(last, 0).wait()
    out_copy(last, 1).wait()


def _sc_scratch(row_len, dtype, gather_rows):
    return [pltpu.VMEM((2, N_SLOTS), I32), pltpu.VMEM((2, gather_rows, row_len), dtype),
            pltpu.SemaphoreType.DMA((N_SEMS,))]


def _sc_u_dots(u_words, e_tok, x_words, first_token):
    nw = x_words.shape[1]
    n_tokens = e_tok.shape[0]
    per_worker = n_tokens // (SC_CORES * SC_SUBCORES)
    n_half = nw // SC_LANES // 2
    fold = 4
    mesh = plsc.VectorSubcoreMesh(core_axis_name="c", subcore_axis_name="s")

    def body(u_hbm, e_hbm, x_hbm, o_hbm, idx_v, rows_v, sems, x_v, out_v):
        lane = lax.iota(I32, SC_LANES)

        def chunk_rows(chunk, rows, slot):
            for part in range(rows.shape[0] // SC_LANES):
                total = jnp.zeros((SC_LANES,), F32)
                for half in range(2):
                    base = half * n_half * SC_LANES
                    xs = [plsc.bitcast(x_v[slot, pl.ds(base + c * SC_LANES, SC_LANES)], BF16) for c in range(n_half)]

                    def row(r, vec):
                        rr = part * SC_LANES + r
                        acc = jnp.zeros((SC_LANES,), F32)
                        for g in range(n_half // fold):
                            p = [plsc.bitcast(rows[rr, pl.ds(base + (fold * g + k) * SC_LANES, SC_LANES)], BF16)
                                 * xs[fold * g + k] for k in range(fold)]
                            w32 = plsc.bitcast((p[0] + p[1]) + (p[2] + p[3]), U32)
                            acc = acc + plsc.bitcast(w32 << 16, F32) + plsc.bitcast(w32 & jnp.uint32(0xFFFF0000), F32)
                        return jnp.where(lane == r, jnp.sum(acc), vec)

                    total = total + lax.fori_loop(0, SC_LANES, row, jnp.zeros((SC_LANES,), F32))
                out_v[slot, pl.ds(chunk * rows.shape[0] + part * SC_LANES, SC_LANES)] = total

        _sc_token_loop(u_hbm, e_hbm, idx_v, rows_v, sems, per_worker,
                       [(lambda tok: x_hbm.at[first_token + tok], x_v)], out_v, lambda tok: o_hbm.at[tok], chunk_rows)

    return pl.kernel(
        body,
        out_type=jax.ShapeDtypeStruct((n_tokens, N_SLOTS), F32),
        mesh=mesh,
        scratch_types=(_sc_scratch(nw, U32, SC_U_GATHER_ROWS)
                       + [pltpu.VMEM((2, nw), U32), pltpu.VMEM((2, N_SLOTS), F32)]),
        compiler_params=pltpu.CompilerParams(needs_layout_passes=False),
        name="peer_expert_in_sc",
    )(u_words, e_tok, x_words)


def _peer_gate_kernel(s_ref, g_ref, par_ref, a_ref):
    dots = s_ref[...].T
    low_t = lax.broadcasted_iota(I32, (SUBLANES, dots.shape[1]), 0) < GROUP
    _gate_rows(lambda grp: _dup_halves(dots[(grp // 2) * SUBLANES:(grp // 2 + 1) * SUBLANES, :], low_t)[grp % 2],
               g_ref, par_ref, a_ref)


def _peer_weights_kernel(a_ref, w_ref):
    low_t = lax.broadcasted_iota(I32, (SUBLANES, a_ref.shape[2]), 0) < GROUP
    tiles = []
    for v in range(N_SLOTS // SUBLANES):
        first = a_ref[0, 2 * v * SUBLANES:(2 * v + 1) * SUBLANES, :]
        second = a_ref[0, (2 * v + 1) * SUBLANES:(2 * v + 2) * SUBLANES, :]
        first = first + pltpu.roll(first, GROUP, 0)
        second = second + pltpu.roll(second, GROUP, 0)
        tiles.append(jnp.where(low_t, first, second))
    w_ref[...] = jnp.concatenate(tiles, axis=0).T


def _sc_v_sums(v_words, e_tok, w_tok, x2d, first_token):
    d = x2d.shape[1]
    nw = v_words.shape[1]
    n_tokens = e_tok.shape[0]
    per_worker = n_tokens // (SC_CORES * SC_SUBCORES)
    passes = 4
    fold = 4
    n_acc = nw // SC_LANES // passes
    mesh = plsc.VectorSubcoreMesh(core_axis_name="c", subcore_axis_name="s")

    def body(v_hbm, e_hbm, w_hbm, x_hbm, o_hbm, idx_v, rows_v, sems, w_v, x_v, out_v):
        def chunk_rows(chunk, rows, slot):
            src = x_v if chunk == 0 else out_v
            for ps in range(passes):
                base = ps * n_acc * SC_LANES

                def row_group(q, acc):
                    ws = []
                    for k in range(fold):
                        pos = jnp.full((SC_LANES,), chunk * rows.shape[0] + fold * q + k, I32)
                        w = plsc.load_gather(w_v.at[slot], [pos])
                        ws.append(plsc.pack(w, w, format=plsc.PackFormat.INTERLEAVED, preferred_element_type=BF16))
                    lo, hi = list(acc[:n_acc]), list(acc[n_acc:])
                    for c in range(n_acc):
                        p = [plsc.bitcast(rows[fold * q + k, pl.ds(base + c * SC_LANES, SC_LANES)], BF16) * ws[k]
                             for k in range(fold)]
                        w32 = plsc.bitcast((p[0] + p[1]) + (p[2] + p[3]), U32)
                        lo[c] = lo[c] + plsc.bitcast(w32 << 16, F32)
                        hi[c] = hi[c] + plsc.bitcast(w32 & jnp.uint32(0xFFFF0000), F32)
                    return tuple(lo + hi)

                acc0 = tuple([src[slot, pl.ds(base + c * SC_LANES, SC_LANES)] for c in range(n_acc)]
                             + [src[slot, pl.ds(nw + base + c * SC_LANES, SC_LANES)] for c in range(n_acc)])
                acc = lax.fori_loop(0, rows.shape[0] // fold, row_group, acc0)
                for c in range(n_acc):
                    out_v[slot, pl.ds(base + c * SC_LANES, SC_LANES)] = acc[c]
                    out_v[slot, pl.ds(nw + base + c * SC_LANES, SC_LANES)] = acc[n_acc + c]

        _sc_token_loop(v_hbm, e_hbm, idx_v, rows_v, sems, per_worker,
                       [(lambda tok: w_hbm.at[tok], w_v), (lambda tok: x_hbm.at[first_token + tok], x_v)],
                       out_v, lambda tok: o_hbm.at[tok], chunk_rows)

    return pl.kernel(
        body,
        out_type=jax.ShapeDtypeStruct((n_tokens, d), F32),
        mesh=mesh,
        scratch_types=(_sc_scratch(nw, U32, SC_V_GATHER_ROWS)
                       + [pltpu.VMEM((2, N_SLOTS), F32), pltpu.VMEM((2, d), F32), pltpu.VMEM((2, d), F32)]),
        compiler_params=pltpu.CompilerParams(needs_layout_passes=False),
        name="peer_expert_out_sc",
    )(v_words, e_tok, w_tok, x2d)


def _token_tiles(ref, xt_ref, to_tiles):
    tb, d = ref.shape
    for q in range(tb // SUBLANES):
        for c in range(d // LANES):
            rows = slice(q * SUBLANES, (q + 1) * SUBLANES)
            cols = slice(c * LANES, (c + 1) * LANES)
            strided = pl.ds(q * SUBLANES * SUBLANES + c, SUBLANES, stride=SUBLANES)
            if to_tiles:
                xt_ref[strided, :] = ref[rows, cols]
            else:
                ref[rows, cols] = xt_ref[strided, :]


def _peer_v_kernel(r8_ref, a_ref, x_ref, tab_ref, o_ref, bc_ref, xt_ref):
    tb = x_ref.shape[0]
    sub = _sublane_iota()
    low = sub < GROUP
    lane_t = lax.broadcasted_iota(I32, (SUBLANES, tb), 1)
    _token_tiles(x_ref, xt_ref, True)

    def spread(t, buf):
        for k in range(N_ROWS // SUBLANES):
            rows = slice(k * SUBLANES, (k + 1) * SUBLANES)
            col = jnp.sum(jnp.where(lane_t == t, a_ref[0, rows, :], 0.0), axis=1, keepdims=True)
            bits = lax.bitcast_convert_type(col.astype(BF16).astype(F32), U32)
            bc_ref[buf, rows, :] = jnp.broadcast_to(bits | (bits >> 16), (SUBLANES, LANES))

    def accumulate(t, buf):
        acc_lo = jnp.zeros((SUBLANES, LANES), F32)
        acc_hi = jnp.zeros((SUBLANES, LANES), F32)
        ids = r8_ref.at[0, t]
        for grp in range(N_GROUPS):
            prods = []
            for i in range(GROUP):
                row = grp * SUBLANES + i
                r8 = pl.multiple_of(ids[grp * GROUP + i], SUBLANES)
                am = jnp.where(low, bc_ref[buf, row:row + 1, :], bc_ref[buf, row + GROUP:row + GROUP + 1, :])
                prods.append(_as_bf16(tab_ref[pl.ds(r8, SUBLANES), :]) * _as_bf16(am))
            lo, hi = _unpack_words(_as_words((prods[0] + prods[1]) + (prods[2] + prods[3])))
            acc_lo = acc_lo + lo
            acc_hi = acc_hi + hi
        acc_lo = acc_lo + pltpu.roll(acc_lo, GROUP, 0)
        acc_hi = acc_hi + pltpu.roll(acc_hi, GROUP, 0)
        tile = pl.ds(pl.multiple_of(t * SUBLANES, SUBLANES), SUBLANES)
        xt_ref[tile, :] = xt_ref[tile, :] + jnp.where(low, acc_lo, acc_hi)

    def token_group(i, carry):
        t = N_SPREAD * i
        for k in range(N_SPREAD):
            spread(jnp.minimum(t + k + SPREAD_AHEAD, tb - 1), (k + SPREAD_AHEAD) % N_SPREAD)
            accumulate(t + k, k)
        return carry

    for k in range(SPREAD_AHEAD):
        spread(k, k)
    lax.fori_loop(0, tb // N_SPREAD, token_group, 0)
    _token_tiles(o_ref, xt_ref, False)


def _peer_experts(x2d, xp, r8, e_tok, par, gates, dots_sc, v_words, u_tab, v_tab):
    t, d = x2d.shape
    nblk = t // PEER_TOKENS
    nblk_tc = nblk - SC_TOKEN_BLOCKS
    smem_block = pl.BlockSpec((1, PEER_TOKENS, N_SLOTS), lambda i: (i, 0, 0), memory_space=pltpu.SMEM)
    slot_block = pl.BlockSpec((1, N_SLOTS, PEER_TOKENS), lambda i: (i, 0, 0))
    row_block = pl.BlockSpec((1, N_ROWS, PEER_TOKENS), lambda i: (i, 0, 0))
    tile_block = pl.BlockSpec((PEER_TOKENS * SUBLANES, LANES), lambda i: (i, 0))
    tok_block = pl.BlockSpec((PEER_TOKENS, d), lambda i: (i, 0))
    whole = pl.BlockSpec(memory_space=pltpu.VMEM)
    params = pltpu.CompilerParams(dimension_semantics=("parallel",), vmem_limit_bytes=56 * MIB)

    act_tc = pl.pallas_call(
        _peer_u_kernel,
        grid=(nblk_tc,),
        in_specs=[smem_block, tile_block, slot_block, slot_block, whole],
        out_specs=row_block,
        out_shape=jax.ShapeDtypeStruct((nblk_tc, N_ROWS, PEER_TOKENS), F32),
        scratch_shapes=[pltpu.VMEM((2 * N_ROWS, PEER_TOKENS), F32),
                        pltpu.VMEM((2, 2 * N_GROUPS, SUBLANES, LANES), F32)],
        compiler_params=params,
        name="peer_expert_in",
    )(r8, xp, gates, par, u_tab)
    shifted = pl.BlockSpec((1, N_SLOTS, PEER_TOKENS), lambda i: (i + nblk_tc, 0, 0))
    act_sc = pl.pallas_call(
        _peer_gate_kernel,
        grid=(SC_TOKEN_BLOCKS,),
        in_specs=[pl.BlockSpec((PEER_TOKENS, N_SLOTS), lambda i: (i, 0)), shifted, shifted],
        out_specs=row_block,
        out_shape=jax.ShapeDtypeStruct((SC_TOKEN_BLOCKS, N_ROWS, PEER_TOKENS), F32),
        compiler_params=pltpu.CompilerParams(dimension_semantics=("parallel",)),
        name="peer_gate_sc",
    )(dots_sc, gates, par)
    act = jnp.concatenate([act_tc, act_sc], axis=0)

    nblk_v = nblk - SC_V_BLOCKS
    w_sc = pl.pallas_call(
        _peer_weights_kernel,
        grid=(SC_V_BLOCKS,),
        in_specs=[pl.BlockSpec((1, N_ROWS, PEER_TOKENS), lambda i: (i + nblk_v, 0, 0))],
        out_specs=pl.BlockSpec((PEER_TOKENS, N_SLOTS), lambda i: (i, 0)),
        out_shape=jax.ShapeDtypeStruct((SC_V_BLOCKS * PEER_TOKENS, N_SLOTS), F32),
        compiler_params=pltpu.CompilerParams(dimension_semantics=("parallel",)),
        name="peer_weights_sc",
    )(act)
    y_sc = _sc_v_sums(v_words, e_tok[nblk_v:].reshape(-1, N_SLOTS), w_sc, x2d, nblk_v * PEER_TOKENS)
    y_tc = pl.pallas_call(
        _peer_v_kernel,
        grid=(nblk_v,),
        in_specs=[smem_block, row_block, tok_block, whole],
        out_specs=tok_block,
        out_shape=jax.ShapeDtypeStruct((nblk_v * PEER_TOKENS, d), F32),
        scratch_shapes=[pltpu.VMEM((N_SPREAD, N_ROWS, LANES), U32),
                        pltpu.VMEM((PEER_TOKENS * SUBLANES, LANES), F32)],
        compiler_params=params,
        name="peer_expert_out",
    )(r8, act, x2d, v_tab)
    return jnp.concatenate([y_tc, y_sc], axis=0)


def _peer_ffn(x2d, norm_g, w_q, sub_keys, u_tabs, v_tabs, layer):
    xp, scores_t, x_words = _peer_scores(x2d, norm_g, w_q, sub_keys)
    u_tiles, u_words = _pack_table(u_tabs, layer)
    v_tiles, v_words = _pack_table(v_tabs, layer)
    nblk = x2d.shape[0] // PEER_TOKENS
    nblk_tc = nblk - SC_TOKEN_BLOCKS
    top_sc = _peer_topk(scores_t, nblk_tc, SC_TOKEN_BLOCKS)
    dots_sc = _sc_u_dots(u_words, top_sc[3].reshape(-1, N_SLOTS), x_words, nblk_tc * PEER_TOKENS)
    top_tc = _peer_topk(scores_t, 0, nblk_tc)
    r8, par, gates, e_tok = [jnp.concatenate([a, b], axis=0) for a, b in zip(top_tc, top_sc)]
    return _peer_experts(x2d, xp, r8, e_tok, par, gates, dots_sc, v_words, u_tiles, v_tiles)


def kernel(x, attn_norm_g, attn_w_qkv, attn_q_g, attn_k_g, attn_rel_bias, attn_w_out, rec_norm_g, rec_w_in, rec_conv_w, rec_conv_b, rec_w_a, rec_b_a, rec_w_x, rec_b_x, rec_lambda, rec_w_out, ffn_norm_g, peer_w_q, peer_sub_keys, peer_u, peer_v):
    b, s, d = x.shape
    depth = ffn_norm_g.shape[0]
    for layer in range(depth):
        j = layer // 2
        if layer % 2 == 0:
            qkv = _qkv_proj(x.reshape(b * s, d), attn_norm_g[j], attn_w_qkv[j], attn_q_g[j], attn_k_g[j])
            x = _attention(x, qkv, attn_rel_bias[j], attn_w_out[j])
        else:
            x = _rglru(x, rec_norm_g[j], rec_w_in[j], rec_conv_w[j], rec_conv_b[j], rec_w_a[j],
                       rec_b_a[j], rec_w_x[j], rec_b_x[j], rec_lambda[j], rec_w_out[j])
        x = _peer_ffn(x.reshape(b * s, d), ffn_norm_g[layer], peer_w_q[layer], peer_sub_keys[layer],
                      peer_u, peer_v, layer).reshape(b, s, d)
    return x
```

```python
import numpy as np
import jax
import jax.numpy as jnp
from jax import lax
from jax.experimental import pallas as pl
from jax.experimental.pallas import tpu as pltpu
from jax.experimental.pallas import tpu_sc as plsc

F32 = jnp.float32
BF16 = jnp.bfloat16
U32 = jnp.uint32
I32 = jnp.int32

EPS = 1e-6
NEG_INF = -1e30

D_MODEL = 1024
CHUNK = 64
ATTN_HEADS = 16
HEAD_DIM = 64
LEFT_CHUNKS = 8
REL_CLIP = 256
LRU_HEADS = 4
LRU_BLOCK = 256
CONV_WIDTH = 4
LRU_C = 8.0
PEER_HEADS = 8
N_KEYS = 128
PEER_TOPK = 16
D_HALF = 128

SUBLANES = 8
LANES = 128

Q_TILE = 4 * CHUNK
N_KEY_BLOCKS = 3
ROW_TILE = 512
PEER_TOKENS = 128
LRU_TILE = 256
N_SLOTS = PEER_HEADS * PEER_TOPK

MIB = 1024 * 1024

SC_CORES = 2
SC_SUBCORES = 16
SC_LANES = 16
SC_U_GATHER_ROWS = 32
SC_V_GATHER_ROWS = 64
SC_TOKEN_BLOCKS = 142
SC_V_BLOCKS = 138


def _rms_rows(x, g):
    ms = jnp.mean(x * x, axis=-1, keepdims=True)
    return (x * lax.rsqrt(ms + EPS)) * g


def _split_bf16(v):
    hi = v.astype(BF16)
    lo = (v - hi.astype(F32)).astype(BF16)
    return hi, lo


def _dot(a, b):
    return jnp.dot(a, b, preferred_element_type=F32)


def _dot_nt(a, b):
    return lax.dot_general(a, b, (((1,), (1,)), ((), ())), preferred_element_type=F32)


def _qkv_kernel(x_ref, g_ref, w_ref, gain_ref, bsel_ref, bexp_ref, o_ref, xn_ref):
    j = pl.program_id(1)

    @pl.when(j == 0)
    def _():
        xn_ref[...] = _rms_rows(x_ref[...], g_ref[...]).astype(BF16)

    y = _dot(xn_ref[...], w_ref[...])

    @pl.when(j < 2)
    def _():
        hi, lo = _split_bf16(y * y)
        ms = _dot(hi, bsel_ref[...]) + _dot(lo, bsel_ref[...])
        rhi, rlo = _split_bf16(lax.rsqrt(ms + EPS))
        rs = _dot(rhi, bexp_ref[...]) + _dot(rlo, bexp_ref[...])
        o_ref[...] = ((y * rs) * gain_ref[0]).astype(BF16)

    @pl.when(j == 2)
    def _():
        o_ref[...] = y.astype(BF16)


def _qkv_proj(x2d, norm_g, w_qkv, q_g, k_g):
    t, d = x2d.shape
    scale = HEAD_DIM ** -0.5
    gains = jnp.stack([jnp.tile(q_g, ATTN_HEADS) * scale, jnp.tile(k_g, ATTN_HEADS),
                       jnp.ones((d,), F32)]).reshape(3, 1, d)
    head_of_col = np.arange(d) // HEAD_DIM
    bsel = (head_of_col[:, None] == np.arange(LANES)[None, :]).astype(np.float32) / HEAD_DIM
    bexp = (np.arange(LANES)[:, None] == head_of_col[None, :]).astype(np.float32)
    return pl.pallas_call(
        _qkv_kernel,
        grid=(t // ROW_TILE, 3),
        in_specs=[
            pl.BlockSpec((ROW_TILE, d), lambda i, j: (i, 0)),
            pl.BlockSpec((1, d), lambda i, j: (0, 0)),
            pl.BlockSpec((d, d), lambda i, j: (0, j)),
            pl.BlockSpec((1, 1, d), lambda i, j: (j, 0, 0)),
            pl.BlockSpec((d, LANES), lambda i, j: (0, 0)),
            pl.BlockSpec((LANES, d), lambda i, j: (0, 0)),
        ],
        out_specs=pl.BlockSpec((ROW_TILE, d), lambda i, j: (i, j)),
        out_shape=jax.ShapeDtypeStruct((t, 3 * d), BF16),
        scratch_shapes=[pltpu.VMEM((ROW_TILE, d), BF16)],
        compiler_params=pltpu.CompilerParams(
            dimension_semantics=("parallel", "arbitrary"), vmem_limit_bytes=40 * MIB),
        name="qkv_proj",
    )(x2d, norm_g.reshape(1, d), w_qkv.astype(BF16), gains,
      jnp.asarray(bsel, BF16), jnp.asarray(bexp, BF16))


def _attn_kernel(x_ref, q_ref, k0_ref, k1_ref, k2_ref, v0_ref, v1_ref, v2_ref,
                 bias_ref, wout_ref, o_ref, att_ref):
    qt = pl.program_id(1)
    k_refs = (k0_ref, k1_ref, k2_ref)
    v_refs = (v0_ref, v1_ref, v2_ref)
    negs = [jnp.where(qt - (N_KEY_BLOCKS - 1) + kb < 0, NEG_INF, 0.0).astype(F32)
            for kb in range(N_KEY_BLOCKS)]
    lane = lax.broadcasted_iota(I32, (Q_TILE, LANES), 1)
    first_half = lane < HEAD_DIM
    for p in range(ATTN_HEADS // 2):
        cols = slice(p * LANES, (p + 1) * LANES)
        qp = q_ref[0, :, cols]
        outs = []
        for hh in range(2):
            h = 2 * p + hh
            keep = first_half if hh == 0 else jnp.logical_not(first_half)
            qm = jnp.where(keep, qp, jnp.zeros_like(qp))
            s = [_dot_nt(qm, k_refs[kb][0, :, cols]) + bias_ref[kb, h] + negs[kb]
                 for kb in range(N_KEY_BLOCKS)]
            m = jnp.max(s[0], axis=-1, keepdims=True)
            for kb in range(1, N_KEY_BLOCKS):
                m = jnp.maximum(m, jnp.max(s[kb], axis=-1, keepdims=True))
            acc = jnp.zeros((Q_TILE, LANES), F32)
            l = jnp.zeros((Q_TILE, 1), F32)
            for kb in range(N_KEY_BLOCKS):
                e = jnp.exp(s[kb] - m)
                l = l + jnp.sum(e, axis=-1, keepdims=True)
                acc = acc + _dot(e.astype(BF16), v_refs[kb][0, :, cols])
            outs.append(acc / l)
        att_ref[:, cols] = jnp.where(first_half, outs[0], outs[1]).astype(BF16)
    o_ref[0] = x_ref[0] + _dot(att_ref[...], wout_ref[...])


def _band_bias(table):
    qi = np.arange(Q_TILE)[:, None]
    kj = np.arange(Q_TILE)[None, :]
    period = 2 * Q_TILE
    m = np.arange(period)
    delta = np.where(m < Q_TILE, m, m - period)
    blocks = []
    for kb in range(N_KEY_BLOCKS):
        dist = (N_KEY_BLOCKS - 1 - kb) * Q_TILE - delta
        diag = table[:, np.clip(dist, -REL_CLIP, REL_CLIP) + REL_CLIP]
        toep = jnp.tile(diag, (1, Q_TILE))[:, :Q_TILE * (period - 1)]
        toep = toep.reshape(-1, Q_TILE, period - 1)[:, :, :Q_TILE]
        dchunk = (Q_TILE // CHUNK) * (kb - (N_KEY_BLOCKS - 1)) + kj // CHUNK - qi // CHUNK
        valid = (dchunk >= -LEFT_CHUNKS) & (dchunk <= 0)
        blocks.append(jnp.where(valid[None], toep, NEG_INF))
    return jnp.stack(blocks).astype(F32)


def _attention(x, qkv, rel_bias_table, w_out):
    b, s, d = x.shape
    nq = s // Q_TILE
    qkv3 = qkv.reshape(b, s, 3 * d)
    bias = _band_bias(rel_bias_table)

    def kv_spec(kb, col):
        return pl.BlockSpec(
            (1, Q_TILE, d),
            lambda bi, qt: (bi, jnp.maximum(qt - (N_KEY_BLOCKS - 1) + kb, 0), col))

    in_specs = [pl.BlockSpec((1, Q_TILE, d), lambda bi, qt: (bi, qt, 0)),
                pl.BlockSpec((1, Q_TILE, d), lambda bi, qt: (bi, qt, 0))]
    in_specs += [kv_spec(kb, 1) for kb in range(N_KEY_BLOCKS)]
    in_specs += [kv_spec(kb, 2) for kb in range(N_KEY_BLOCKS)]
    in_specs += [pl.BlockSpec(memory_space=pltpu.VMEM), pl.BlockSpec(memory_space=pltpu.VMEM)]
    return pl.pallas_call(
        _attn_kernel,
        grid=(b, nq),
        in_specs=in_specs,
        out_specs=pl.BlockSpec((1, Q_TILE, d), lambda bi, qt: (bi, qt, 0)),
        out_shape=jax.ShapeDtypeStruct((b, s, d), F32),
        scratch_shapes=[pltpu.VMEM((Q_TILE, d), BF16)],
        compiler_params=pltpu.CompilerParams(
            dimension_semantics=("parallel", "parallel"), vmem_limit_bytes=48 * MIB),
        name="band_attention",
    )(x, qkv3, qkv3, qkv3, qkv3, qkv3, qkv3, qkv3, bias, w_out.astype(BF16))


def _rglru_kernel(x_ref, g_ref, win_ref, cw_ref, cb_ref, wa_ref, ba_ref, wx_ref, bx_ref,
                  lam_ref, wout_ref, o_ref, ext_ref, h_ref, a_scr, b_scr, hs_scr):
    ts = x_ref.shape[1]
    w = LRU_HEADS * LRU_BLOCK

    @pl.when(pl.program_id(1) == 0)
    def _():
        ext_ref[...] = jnp.zeros_like(ext_ref)
        h_ref[...] = jnp.zeros_like(h_ref)

    x = x_ref[0]
    xn = _rms_rows(x, g_ref[...]).astype(BF16)
    gu = _dot(xn, win_ref[...])
    gate = gu[:, :w]
    u_raw = gu[:, w:]
    ext_ref[0:SUBLANES, :] = ext_ref[ts:ts + SUBLANES, :]
    ext_ref[SUBLANES:, :] = u_raw
    u = cb_ref[...] + jnp.zeros((ts, w), F32)
    for k in range(CONV_WIDTH):
        off = SUBLANES - (CONV_WIDTH - 1) + k
        u = u + cw_ref[k:k + 1, :] * ext_ref[off:off + ts, :]
    ub = u.astype(BF16)

    def block_diag(wref):
        return jnp.concatenate(
            [_dot(ub[:, hh * LRU_BLOCK:(hh + 1) * LRU_BLOCK], wref[hh]) for hh in range(LRU_HEADS)],
            axis=1)

    r = jax.nn.sigmoid(block_diag(wa_ref) + ba_ref[...])
    i = jax.nn.sigmoid(block_diag(wx_ref) + bx_ref[...])
    z = -lam_ref[...]
    softplus = jnp.maximum(z, 0.0) + jnp.log1p(jnp.exp(-jnp.abs(z)))
    log_a = (-LRU_C) * r * softplus
    a_scr[...] = jnp.exp(log_a)
    th = jnp.tanh(log_a)
    b_scr[...] = jnp.sqrt(2.0 * th / (th - 1.0)) * (i * u)

    def step(t, h):
        h = a_scr[pl.ds(t, 1), :] * h + b_scr[pl.ds(t, 1), :]
        hs_scr[pl.ds(t, 1), :] = h
        return h

    h_ref[0:1, :] = lax.fori_loop(0, ts, step, h_ref[0:1, :], unroll=8)
    y = jax.nn.gelu(gate) * hs_scr[...]
    o_ref[0] = x + _dot(y.astype(BF16), wout_ref[...])


def _rglru(x, norm_g, w_in, conv_w, conv_b, w_a, b_a, w_x, b_x, lam, w_out):
    b, s, d = x.shape
    w = LRU_HEADS * LRU_BLOCK
    row = lambda v: v.reshape(1, -1)
    whole = pl.BlockSpec(memory_space=pltpu.VMEM)
    return pl.pallas_call(
        _rglru_kernel,
        grid=(b, s // LRU_TILE),
        in_specs=[pl.BlockSpec((1, LRU_TILE, d), lambda bi, si: (bi, si, 0))] + [whole] * 10,
        out_specs=pl.BlockSpec((1, LRU_TILE, d), lambda bi, si: (bi, si, 0)),
        out_shape=jax.ShapeDtypeStruct((b, s, d), F32),
        scratch_shapes=[pltpu.VMEM((LRU_TILE + SUBLANES, w), F32), pltpu.VMEM((SUBLANES, w), F32),
                        pltpu.VMEM((LRU_TILE, w), F32), pltpu.VMEM((LRU_TILE, w), F32),
                        pltpu.VMEM((LRU_TILE, w), F32)],
        compiler_params=pltpu.CompilerParams(
            dimension_semantics=("arbitrary", "arbitrary"), vmem_limit_bytes=48 * MIB),
        name="rglru_mixer",
    )(x, row(norm_g), w_in.astype(BF16), conv_w, row(conv_b), w_a.astype(BF16), row(b_a),
      w_x.astype(BF16), row(b_x), row(lam), w_out.astype(BF16))


def _peer_scores_kernel(x_ref, g_ref, wq_ref, sk_ref, xp_ref, st_ref, xw_ref):
    xb = _rms_rows(x_ref[...], g_ref[...]).astype(BF16)
    words = _bf16_pair_words(xb)
    xw_ref[...] = words
    tm = words.shape[0]
    for c in range(SUBLANES):
        piece = words[:, (c % (SUBLANES // 2)) * LANES:(c % (SUBLANES // 2) + 1) * LANES]
        xp_ref[pl.ds(c, tm, stride=SUBLANES), :] = piece
    q = _dot(xb, wq_ref[...]).astype(BF16)
    for hp in range(2 * PEER_HEADS):
        st_ref[hp] = _dot_nt(sk_ref[hp % 2], q[:, hp * D_HALF:(hp + 1) * D_HALF])


def _peer_scores(x2d, norm_g, w_q, sub_keys):
    t, d = x2d.shape
    nq = w_q.shape[1]
    whole = pl.BlockSpec(memory_space=pltpu.VMEM)
    return pl.pallas_call(
        _peer_scores_kernel,
        grid=(t // ROW_TILE,),
        in_specs=[pl.BlockSpec((ROW_TILE, d), lambda i: (i, 0)), whole, whole, whole],
        out_specs=[pl.BlockSpec((ROW_TILE * SUBLANES, LANES), lambda i: (i, 0)),
                   pl.BlockSpec((2 * PEER_HEADS, N_KEYS, ROW_TILE), lambda i: (0, 0, i)),
                   pl.BlockSpec((ROW_TILE, d // 2), lambda i: (i, 0))],
        out_shape=[jax.ShapeDtypeStruct((t * SUBLANES, LANES), U32),
                   jax.ShapeDtypeStruct((2 * PEER_HEADS, N_KEYS, t), F32),
                   jax.ShapeDtypeStruct((t, d // 2), U32)],
        compiler_params=pltpu.CompilerParams(
            dimension_semantics=("parallel",), vmem_limit_bytes=48 * MIB),
        name="peer_scores",
    )(x2d, norm_g.reshape(1, d), w_q.astype(BF16), sub_keys.astype(BF16))


def _candidate_blocks():
    blocks = []
    for i in range(PEER_TOPK // 2):
        nj = PEER_TOPK // (i + 1)
        for j0 in range(0, nj, SUBLANES):
            blocks.append(("row", i, j0, min(SUBLANES, nj - j0)))
    blocks.append(("col", PEER_TOPK // 2, PEER_TOPK // 2))
    return blocks


def _peer_topk_kernel(st_ref, r8_ref, par_ref, g_ref, e_ref, r8t_ref, et_ref):
    tk = st_ref.shape[2]
    key_id = lax.broadcasted_iota(I32, (N_KEYS, tk), 0).astype(F32)
    row16 = lax.broadcasted_iota(I32, (PEER_TOPK, tk), 0)
    row8 = lax.broadcasted_iota(I32, (SUBLANES, tk), 0)
    row8_f = row8.astype(F32)
    neg_inf = jnp.float32(-jnp.inf)
    blocks = _candidate_blocks()
    big = jnp.float32(PEER_TOPK * PEER_TOPK)

    def top16(x):
        vals = jnp.zeros((PEER_TOPK, tk), F32)
        idxs = jnp.zeros((PEER_TOPK, tk), F32)
        for k in range(PEER_TOPK):
            m = jnp.max(x, axis=0, keepdims=True)
            idx = jnp.min(jnp.where(x == m, key_id, float(N_KEYS)), axis=0, keepdims=True)
            x = jnp.where(key_id == idx, neg_inf, x)
            vals = jnp.where(row16 == k, m, vals)
            idxs = jnp.where(row16 == k, idx, idxs)
        return vals, idxs

    for h in range(PEER_HEADS):
        s0, i0 = top16(st_ref[2 * h])
        s1, i1 = top16(st_ref[2 * h + 1])
        cand, flat, expert = [], [], []
        for blk in blocks:
            if blk[0] == "row":
                _, i, j0, n = blk
                c = s0[i:i + 1, :] + s1[j0:j0 + SUBLANES, :]
                f = float(PEER_TOPK * i + j0) + row8_f
                e = i0[i:i + 1, :] * float(N_KEYS) + i1[j0:j0 + SUBLANES, :]
            else:
                _, i_start, n = blk
                c = s0[i_start:i_start + SUBLANES, :] + s1[0:1, :]
                f = float(PEER_TOPK) * (float(i_start) + row8_f)
                e = i0[i_start:i_start + SUBLANES, :] * float(N_KEYS) + i1[0:1, :]
            cand.append(jnp.where(row8 < n, c, neg_inf))
            flat.append(f)
            expert.append(e)
        cand = jnp.concatenate(cand, axis=0)
        flat = jnp.concatenate(flat, axis=0)
        expert = jnp.concatenate(expert, axis=0)
        best_s = jnp.zeros((PEER_TOPK, tk), F32)
        best_e = jnp.zeros((PEER_TOPK, tk), F32)
        for k in range(PEER_TOPK):
            m = jnp.max(cand, axis=0, keepdims=True)
            fidx = jnp.min(jnp.where(cand == m, flat, big), axis=0, keepdims=True)
            pick = flat == fidx
            e_sel = jnp.max(jnp.where(pick, expert, -1.0), axis=0, keepdims=True)
            cand = jnp.where(pick, neg_inf, cand)
            best_s = jnp.where(row16 == k, m, best_s)
            best_e = jnp.where(row16 == k, e_sel, best_e)
        ex = jnp.exp(best_s - best_s[0:1, :])
        gate = ex / jnp.sum(ex, axis=0, keepdims=True)
        rows = slice(h * PEER_TOPK, (h + 1) * PEER_TOPK)
        best_i = best_e.astype(I32)
        r8t_ref[rows, :] = (best_i >> 1) * SUBLANES
        et_ref[rows, :] = best_i
        par_ref[0, rows, :] = best_i & 1
        g_ref[0, rows, :] = gate
    r8_ref[0] = r8t_ref[...].T
    e_ref[0] = et_ref[...].T


def _peer_topk(scores_t, first_block, nblk):
    slot_major = pl.BlockSpec((1, N_SLOTS, PEER_TOKENS), lambda i: (i, 0, 0))
    token_major = pl.BlockSpec((1, PEER_TOKENS, N_SLOTS), lambda i: (i, 0, 0))
    return pl.pallas_call(
        _peer_topk_kernel,
        grid=(nblk,),
        in_specs=[pl.BlockSpec((2 * PEER_HEADS, N_KEYS, PEER_TOKENS), lambda i: (0, 0, i + first_block))],
        out_specs=[token_major, slot_major, slot_major, token_major],
        out_shape=[jax.ShapeDtypeStruct((nblk, PEER_TOKENS, N_SLOTS), I32),
                   jax.ShapeDtypeStruct((nblk, N_SLOTS, PEER_TOKENS), I32),
                   jax.ShapeDtypeStruct((nblk, N_SLOTS, PEER_TOKENS), F32),
                   jax.ShapeDtypeStruct((nblk, PEER_TOKENS, N_SLOTS), I32)],
        scratch_shapes=[pltpu.VMEM((N_SLOTS, PEER_TOKENS), I32), pltpu.VMEM((N_SLOTS, PEER_TOKENS), I32)],
        compiler_params=pltpu.CompilerParams(dimension_semantics=("parallel",)),
        name="peer_topk",
    )(scores_t)


def _bf16_pair_words(x):
    bits = lax.bitcast_convert_type(x.astype(BF16).astype(F32), U32)
    half = bits.shape[1] // 2
    return (bits[:, :half] >> 16) | (bits[:, half:] & jnp.uint32(0xFFFF0000))


def _pack_kernel(t_ref, o_ref, w_ref):
    words = _bf16_pair_words(t_ref[0])
    w_ref[...] = words
    rows = words.shape[0]
    n = words.shape[1] // LANES
    for s in range(n):
        o_ref[pl.ds(s, rows, stride=n), :] = words[:, s * LANES:(s + 1) * LANES]


def _pack_table(tabs, layer):
    _, e, d = tabs.shape
    n = d // (2 * LANES)
    return pl.pallas_call(
        _pack_kernel,
        grid=(e // ROW_TILE,),
        in_specs=[pl.BlockSpec((1, ROW_TILE, d), lambda i: (layer, i, 0))],
        out_specs=[pl.BlockSpec((ROW_TILE * n, LANES), lambda i: (i, 0)),
                   pl.BlockSpec((ROW_TILE, d // 2), lambda i: (i, 0))],
        out_shape=[jax.ShapeDtypeStruct((e * n, LANES), U32), jax.ShapeDtypeStruct((e, d // 2), U32)],
        compiler_params=pltpu.CompilerParams(dimension_semantics=("parallel",)),
        name="peer_pack_table",
    )(tabs)


def _unpack_words(w):
    lo = lax.bitcast_convert_type(w << 16, F32)
    hi = lax.bitcast_convert_type(w & jnp.uint32(0xFFFF0000), F32)
    return lo, hi


def _as_bf16(words):
    return pltpu.bitcast(words, BF16)


def _as_words(packed):
    return pltpu.bitcast(packed, U32)


def _merge_packed(a, b, shift, mask):
    ta = a + _as_bf16(pltpu.roll(_as_words(a), shift, 0))
    tb = b + _as_bf16(pltpu.roll(_as_words(b), SUBLANES - shift, 0))
    return jnp.where(mask, _as_words(ta), _as_words(tb))


def _sublane_iota():
    return lax.broadcasted_iota(I32, (SUBLANES, LANES), 0)


def _merge(a, b, shift, mask):
    ta = a + pltpu.roll(a, shift, 0)
    tb = b + pltpu.roll(b, SUBLANES - shift, 0)
    return jnp.where(mask, ta, tb)


GROUP = SUBLANES // 2
N_GROUPS = N_SLOTS // GROUP
N_ROWS = 2 * N_SLOTS
N_SPREAD = 4
SPREAD_AHEAD = 2


def _dup_halves(v, low):
    r = pltpu.roll(v, GROUP, 0)
    return jnp.where(low, v, r), jnp.where(low, r, v)


def _peer_u_kernel(r8_ref, xp_ref, g_ref, par_ref, tab_ref, a_ref, s_ref, part_ref):
    tb = g_ref.shape[2]
    sub = _sublane_iota()
    m2 = (sub & 2) != 0
    lane_t = lax.broadcasted_iota(I32, (SUBLANES, tb), 1)
    n_part = 2 * N_GROUPS

    def products(t, slot):
        xb = _as_bf16(xp_ref[pl.ds(pl.multiple_of(t * SUBLANES, SUBLANES), SUBLANES), :])
        ids = r8_ref.at[0, t]
        for grp in range(N_GROUPS):
            prods = []
            for i in range(GROUP):
                r8 = pl.multiple_of(ids[grp * GROUP + i], SUBLANES)
                prods.append(_as_bf16(tab_ref[pl.ds(r8, SUBLANES), :]) * xb)
            halves = (_merge_packed(prods[3], prods[1], 2, m2), _merge_packed(prods[2], prods[0], 2, m2))
            for k in range(2):
                lo, hi = _unpack_words(halves[k])
                part_ref[slot, 2 * grp + k] = lo + hi

    def reduce_lanes(t, slot):
        for n in range(n_part):
            col = jnp.sum(part_ref[slot, n], axis=1, keepdims=True)
            rows = slice(n * SUBLANES, (n + 1) * SUBLANES)
            s_ref[rows, :] = jnp.where(lane_t == t, col, s_ref[rows, :])

    def token_pair(i, carry):
        t = 2 * i
        reduce_lanes(t - 2, 0)
        reduce_lanes(t - 1, 1)
        products(t, 0)
        products(t + 1, 1)
        return carry

    s_ref[...] = jnp.zeros_like(s_ref)
    part_ref[...] = jnp.zeros_like(part_ref)
    lax.fori_loop(0, tb // 2, token_pair, 0)
    reduce_lanes(tb - 2, 0)
    reduce_lanes(tb - 1, 1)

    odd_t = (lax.broadcasted_iota(I32, (SUBLANES, tb), 0) & 1) != 0
    _gate_rows(lambda grp: _merge(s_ref[2 * grp * SUBLANES:(2 * grp + 1) * SUBLANES, :],
                                  s_ref[(2 * grp + 1) * SUBLANES:(2 * grp + 2) * SUBLANES, :], 1, odd_t),
               g_ref, par_ref, a_ref)


def _gate_rows(dots_of_group, g_ref, par_ref, a_ref):
    tb = g_ref.shape[2]
    low_t = lax.broadcasted_iota(I32, (SUBLANES, tb), 0) < GROUP
    want = jnp.where(low_t, 0, 1)
    for v in range(N_SLOTS // SUBLANES):
        src = slice(v * SUBLANES, (v + 1) * SUBLANES)
        gates = _dup_halves(g_ref[0, src, :], low_t)
        pars = _dup_halves(par_ref[0, src, :], low_t)
        for half in range(2):
            grp = 2 * v + half
            act = gates[half] * jax.nn.gelu(dots_of_group(grp))
            rows = slice(grp * SUBLANES, (grp + 1) * SUBLANES)
            a_ref[0, rows, :] = jnp.where(pars[half] == want, act, 0.0)


SEM_IDS, SEM_IN, SEM_OUT, N_SEMS = 2, 3, 5, 7


def _sc_token_loop(tab_hbm, e_hbm, idx_v, rows_v, sems, per_worker, token_in, out_buf, out_row, chunk_rows):
    wid = lax.axis_index("s") * SC_CORES + lax.axis_index("c")
    gather_rows = rows_v.shape[1]
    n_chunks = N_SLOTS // gather_rows
    first = wid * per_worker
    last = first + per_worker - 1

    def gather(slot, chunk):
        ids = idx_v.at[slot, pl.ds(chunk * gather_rows, gather_rows)]
        return pltpu.make_async_copy(tab_hbm.at[ids], rows_v.at[chunk % 2], sems.at[chunk % 2])

    def ids_copy(tok, slot):
        return pltpu.make_async_copy(e_hbm.at[tok], idx_v.at[slot], sems.at[SEM_IDS])

    def in_copies(tok, slot):
        return [pltpu.make_async_copy(row_of(tok), buf.at[slot], sems.at[SEM_IN + slot]) for row_of, buf in token_in]

    def out_copy(tok, slot):
        return pltpu.make_async_copy(out_buf.at[slot], out_row(tok), sems.at[SEM_OUT + slot])

    def one_token(k, tok, slot, next_tok):
        ids_copy(next_tok, 1 - slot).start()
        for c in in_copies(next_tok, 1 - slot):
            c.start()
        for c in in_copies(tok, slot):
            c.wait()

        @pl.when(k > 0)
        def _():
            out_copy(tok, slot).wait()

        for chunk in range(n_chunks):
            if chunk + 1 < n_chunks:
                gather(slot, chunk + 1).start()
            else:
                ids_copy(next_tok, 1 - slot).wait()
                gather(1 - slot, 0).start()
            gather(slot, chunk).wait()
            chunk_rows(chunk, rows_v.at[chunk % 2], slot)
        out_copy(tok, slot).start()

    def token_pair(k, carry):
        tok = first + 2 * k
        one_token(k, tok, 0, tok + 1)
        one_token(k, tok + 1, 1, jnp.minimum(tok + 2, last))
        return carry

    pltpu.sync_copy(e_hbm.at[first], idx_v.at[0])
    gather(0, 0).start()
    for c in in_copies(first, 0):
        c.start()
    lax.fori_loop(0, per_worker // 2, token_pair, 0)
    gather(0, 0).wait()
    for c in in_copies(last, 0):
        c.wait()
    out_copy(last, 0).wait()
    out_copy(last, 1).wait()


def _sc_scratch(row_len, dtype, gather_rows):
    return [pltpu.VMEM((2, N_SLOTS), I32), pltpu.VMEM((2, gather_rows, row_len), dtype),
            pltpu.SemaphoreType.DMA((N_SEMS,))]


def _sc_u_dots(u_words, e_tok, x_words, first_token):
    nw = x_words.shape[1]
    n_tokens = e_tok.shape[0]
    per_worker = n_tokens // (SC_CORES * SC_SUBCORES)
    n_chunks = nw // SC_LANES
    fold = 4
    mesh = plsc.VectorSubcoreMesh(core_axis_name="c", subcore_axis_name="s")

    def body(u_hbm, e_hbm, x_hbm, o_hbm, idx_v, rows_v, sems, x_v, out_v):
        lane = lax.iota(I32, SC_LANES)

        def chunk_rows(chunk, rows, slot):
            xs = [plsc.bitcast(x_v[slot, pl.ds(c * SC_LANES, SC_LANES)], BF16) for c in range(n_chunks)]
            for part in range(rows.shape[0] // SC_LANES):
                def row(r, vec):
                    rr = part * SC_LANES + r
                    acc = jnp.zeros((SC_LANES,), F32)
                    for g in range(n_chunks // fold):
                        p = [plsc.bitcast(rows[rr, pl.ds((fold * g + k) * SC_LANES, SC_LANES)], BF16)
                             * xs[fold * g + k] for k in range(fold)]
                        w32 = plsc.bitcast((p[0] + p[1]) + (p[2] + p[3]), U32)
                        acc = acc + plsc.bitcast(w32 << 16, F32) + plsc.bitcast(w32 & jnp.uint32(0xFFFF0000), F32)
                    return jnp.where(lane == r, jnp.sum(acc), vec)

                total = lax.fori_loop(0, SC_LANES, row, jnp.zeros((SC_LANES,), F32))
                out_v[slot, pl.ds(chunk * rows.shape[0] + part * SC_LANES, SC_LANES)] = total

        _sc_token_loop(u_hbm, e_hbm, idx_v, rows_v, sems, per_worker,
                       [(lambda tok: x_hbm.at[first_token + tok], x_v)], out_v, lambda tok: o_hbm.at[tok], chunk_rows)

    return pl.kernel(
        body,
        out_type=jax.ShapeDtypeStruct((n_tokens, N_SLOTS), F32),
        mesh=mesh,
        scratch_types=(_sc_scratch(nw, U32, SC_U_GATHER_ROWS)
                       + [pltpu.VMEM((2, nw), U32), pltpu.VMEM((2, N_SLOTS), F32)]),
        compiler_params=pltpu.CompilerParams(needs_layout_passes=False),
        name="peer_expert_in_sc",
    )(u_words, e_tok, x_words)


def _peer_gate_kernel(s_ref, g_ref, par_ref, a_ref):
    dots = s_ref[...].T
    low_t = lax.broadcasted_iota(I32, (SUBLANES, dots.shape[1]), 0) < GROUP
    _gate_rows(lambda grp: _dup_halves(dots[(grp // 2) * SUBLANES:(grp // 2 + 1) * SUBLANES, :], low_t)[grp % 2],
               g_ref, par_ref, a_ref)


def _peer_weights_kernel(a_ref, w_ref):
    low_t = lax.broadcasted_iota(I32, (SUBLANES, a_ref.shape[2]), 0) < GROUP
    tiles = []
    for v in range(N_SLOTS // SUBLANES):
        first = a_ref[0, 2 * v * SUBLANES:(2 * v + 1) * SUBLANES, :]
        second = a_ref[0, (2 * v + 1) * SUBLANES:(2 * v + 2) * SUBLANES, :]
        first = first + pltpu.roll(first, GROUP, 0)
        second = second + pltpu.roll(second, GROUP, 0)
        tiles.append(jnp.where(low_t, first, second))
    w_ref[...] = jnp.concatenate(tiles, axis=0).T


def _sc_v_sums(v_words, e_tok, w_tok, x2d, first_token):
    d = x2d.shape[1]
    nw = v_words.shape[1]
    n_tokens = e_tok.shape[0]
    per_worker = n_tokens // (SC_CORES * SC_SUBCORES)
    passes = 4
    fold = 4
    n_acc = nw // SC_LANES // passes
    mesh = plsc.VectorSubcoreMesh(core_axis_name="c", subcore_axis_name="s")

    def body(v_hbm, e_hbm, w_hbm, x_hbm, o_hbm, idx_v, rows_v, sems, w_v, x_v, out_v):
        def chunk_rows(chunk, rows, slot):
            src = x_v if chunk == 0 else out_v
            for ps in range(passes):
                base = ps * n_acc * SC_LANES

                def row_group(q, acc):
                    ws = []
                    for k in range(fold):
                        pos = jnp.full((SC_LANES,), chunk * rows.shape[0] + fold * q + k, I32)
                        w = plsc.load_gather(w_v.at[slot], [pos])
                        ws.append(plsc.pack(w, w, format=plsc.PackFormat.INTERLEAVED, preferred_element_type=BF16))
                    lo, hi = list(acc[:n_acc]), list(acc[n_acc:])
                    for c in range(n_acc):
                        p = [plsc.bitcast(rows[fold * q + k, pl.ds(base + c * SC_LANES, SC_LANES)], BF16) * ws[k]
                             for k in range(fold)]
                        w32 = plsc.bitcast((p[0] + p[1]) + (p[2] + p[3]), U32)
                        lo[c] = lo[c] + plsc.bitcast(w32 << 16, F32)
                        hi[c] = hi[c] + plsc.bitcast(w32 & jnp.uint32(0xFFFF0000), F32)
                    return tuple(lo + hi)

                acc0 = tuple([src[slot, pl.ds(base + c * SC_LANES, SC_LANES)] for c in range(n_acc)]
                             + [src[slot, pl.ds(nw + base + c * SC_LANES, SC_LANES)] for c in range(n_acc)])
                acc = lax.fori_loop(0, rows.shape[0] // fold, row_group, acc0)
                for c in range(n_acc):
                    out_v[slot, pl.ds(base + c * SC_LANES, SC_LANES)] = acc[c]
                    out_v[slot, pl.ds(nw + base + c * SC_LANES, SC_LANES)] = acc[n_acc + c]

        _sc_token_loop(v_hbm, e_hbm, idx_v, rows_v, sems, per_worker,
                       [(lambda tok: w_hbm.at[tok], w_v), (lambda tok: x_hbm.at[first_token + tok], x_v)],
                       out_v, lambda tok: o_hbm.at[tok], chunk_rows)

    return pl.kernel(
        body,
        out_type=jax.ShapeDtypeStruct((n_tokens, d), F32),
        mesh=mesh,
        scratch_types=(_sc_scratch(nw, U32, SC_V_GATHER_ROWS)
                       + [pltpu.VMEM((2, N_SLOTS), F32), pltpu.VMEM((2, d), F32), pltpu.VMEM((2, d), F32)]),
        compiler_params=pltpu.CompilerParams(needs_layout_passes=False),
        name="peer_expert_out_sc",
    )(v_words, e_tok, w_tok, x2d)


def _token_tiles(ref, xt_ref, to_tiles):
    tb, d = ref.shape
    for q in range(tb // SUBLANES):
        for c in range(d // LANES):
            rows = slice(q * SUBLANES, (q + 1) * SUBLANES)
            cols = slice(c * LANES, (c + 1) * LANES)
            strided = pl.ds(q * SUBLANES * SUBLANES + c, SUBLANES, stride=SUBLANES)
            if to_tiles:
                xt_ref[strided, :] = ref[rows, cols]
            else:
                ref[rows, cols] = xt_ref[strided, :]


def _peer_v_kernel(r8_ref, a_ref, x_ref, tab_ref, o_ref, bc_ref, xt_ref):
    tb = x_ref.shape[0]
    sub = _sublane_iota()
    low = sub < GROUP
    lane_t = lax.broadcasted_iota(I32, (SUBLANES, tb), 1)
    _token_tiles(x_ref, xt_ref, True)

    def spread(t, buf):
        for k in range(N_ROWS // SUBLANES):
            rows = slice(k * SUBLANES, (k + 1) * SUBLANES)
            col = jnp.sum(jnp.where(lane_t == t, a_ref[0, rows, :], 0.0), axis=1, keepdims=True)
            bits = lax.bitcast_convert_type(col.astype(BF16).astype(F32), U32)
            bc_ref[buf, rows, :] = jnp.broadcast_to(bits | (bits >> 16), (SUBLANES, LANES))

    def accumulate(t, buf):
        acc_lo = jnp.zeros((SUBLANES, LANES), F32)
        acc_hi = jnp.zeros((SUBLANES, LANES), F32)
        ids = r8_ref.at[0, t]
        for grp in range(N_GROUPS):
            prods = []
            for i in range(GROUP):
                row = grp * SUBLANES + i
                r8 = pl.multiple_of(ids[grp * GROUP + i], SUBLANES)
                am = jnp.where(low, bc_ref[buf, row:row + 1, :], bc_ref[buf, row + GROUP:row + GROUP + 1, :])
                prods.append(_as_bf16(tab_ref[pl.ds(r8, SUBLANES), :]) * _as_bf16(am))
            lo, hi = _unpack_words(_as_words((prods[0] + prods[1]) + (prods[2] + prods[3])))
            acc_lo = acc_lo + lo
            acc_hi = acc_hi + hi
        acc_lo = acc_lo + pltpu.roll(acc_lo, GROUP, 0)
        acc_hi = acc_hi + pltpu.roll(acc_hi, GROUP, 0)
        tile = pl.ds(pl.multiple_of(t * SUBLANES, SUBLANES), SUBLANES)
        xt_ref[tile, :] = xt_ref[tile, :] + jnp.where(low, acc_lo, acc_hi)

    def token_group(i, carry):
        t = N_SPREAD * i
        for k in range(N_SPREAD):
            spread(jnp.minimum(t + k + SPREAD_AHEAD, tb - 1), (k + SPREAD_AHEAD) % N_SPREAD)
            accumulate(t + k, k)
        return carry

    for k in range(SPREAD_AHEAD):
        spread(k, k)
    lax.fori_loop(0, tb // N_SPREAD, token_group, 0)
    _token_tiles(o_ref, xt_ref, False)


def _peer_experts(x2d, xp, r8, e_tok, par, gates, dots_sc, v_words, u_tab, v_tab):
    t, d = x2d.shape
    nblk = t // PEER_TOKENS
    nblk_tc = nblk - SC_TOKEN_BLOCKS
    smem_block = pl.BlockSpec((1, PEER_TOKENS, N_SLOTS), lambda i: (i, 0, 0), memory_space=pltpu.SMEM)
    slot_block = pl.BlockSpec((1, N_SLOTS, PEER_TOKENS), lambda i: (i, 0, 0))
    row_block = pl.BlockSpec((1, N_ROWS, PEER_TOKENS), lambda i: (i, 0, 0))
    tile_block = pl.BlockSpec((PEER_TOKENS * SUBLANES, LANES), lambda i: (i, 0))
    tok_block = pl.BlockSpec((PEER_TOKENS, d), lambda i: (i, 0))
    whole = pl.BlockSpec(memory_space=pltpu.VMEM)
    params = pltpu.CompilerParams(dimension_semantics=("parallel",), vmem_limit_bytes=56 * MIB)

    act_tc = pl.pallas_call(
        _peer_u_kernel,
        grid=(nblk_tc,),
        in_specs=[smem_block, tile_block, slot_block, slot_block, whole],
        out_specs=row_block,
        out_shape=jax.ShapeDtypeStruct((nblk_tc, N_ROWS, PEER_TOKENS), F32),
        scratch_shapes=[pltpu.VMEM((2 * N_ROWS, PEER_TOKENS), F32),
                        pltpu.VMEM((2, 2 * N_GROUPS, SUBLANES, LANES), F32)],
        compiler_params=params,
        name="peer_expert_in",
    )(r8, xp, gates, par, u_tab)
    shifted = pl.BlockSpec((1, N_SLOTS, PEER_TOKENS), lambda i: (i + nblk_tc, 0, 0))
    act_sc = pl.pallas_call(
        _peer_gate_kernel,
        grid=(SC_TOKEN_BLOCKS,),
        in_specs=[pl.BlockSpec((PEER_TOKENS, N_SLOTS), lambda i: (i, 0)), shifted, shifted],
        out_specs=row_block,
        out_shape=jax.ShapeDtypeStruct((SC_TOKEN_BLOCKS, N_ROWS, PEER_TOKENS), F32),
        compiler_params=pltpu.CompilerParams(dimension_semantics=("parallel",)),
        name="peer_gate_sc",
    )(dots_sc, gates, par)
    act = jnp.concatenate([act_tc, act_sc], axis=0)

    nblk_v = nblk - SC_V_BLOCKS
    w_sc = pl.pallas_call(
        _peer_weights_kernel,
        grid=(SC_V_BLOCKS,),
        in_specs=[pl.BlockSpec((1, N_ROWS, PEER_TOKENS), lambda i: (i + nblk_v, 0, 0))],
        out_specs=pl.BlockSpec((PEER_TOKENS, N_SLOTS), lambda i: (i, 0)),
        out_shape=jax.ShapeDtypeStruct((SC_V_BLOCKS * PEER_TOKENS, N_SLOTS), F32),
        compiler_params=pltpu.CompilerParams(dimension_semantics=("parallel",)),
        name="peer_weights_sc",
    )(act)
    y_sc = _sc_v_sums(v_words, e_tok[nblk_v:].reshape(-1, N_SLOTS), w_sc, x2d, nblk_v * PEER_TOKENS)
    y_tc = pl.pallas_call(
        _peer_v_kernel,
        grid=(nblk_v,),
        in_specs=[smem_block, row_block, tok_block, whole],
        out_specs=tok_block,
        out_shape=jax.ShapeDtypeStruct((nblk_v * PEER_TOKENS, d), F32),
        scratch_shapes=[pltpu.VMEM((N_SPREAD, N_ROWS, LANES), U32),
                        pltpu.VMEM((PEER_TOKENS * SUBLANES, LANES), F32)],
        compiler_params=params,
        name="peer_expert_out",
    )(r8, act, x2d, v_tab)
    return jnp.concatenate([y_tc, y_sc], axis=0)


def _peer_ffn(x2d, norm_g, w_q, sub_keys, u_tabs, v_tabs, layer):
    xp, scores_t, x_words = _peer_scores(x2d, norm_g, w_q, sub_keys)
    u_tiles, u_words = _pack_table(u_tabs, layer)
    v_tiles, v_words = _pack_table(v_tabs, layer)
    nblk = x2d.shape[0] // PEER_TOKENS
    nblk_tc = nblk - SC_TOKEN_BLOCKS
    top_sc = _peer_topk(scores_t, nblk_tc, SC_TOKEN_BLOCKS)
    dots_sc = _sc_u_dots(u_words, top_sc[3].reshape(-1, N_SLOTS), x_words, nblk_tc * PEER_TOKENS)
    top_tc = _peer_topk(scores_t, 0, nblk_tc)
    r8, par, gates, e_tok = [jnp.concatenate([a, b], axis=0) for a, b in zip(top_tc, top_sc)]
    return _peer_experts(x2d, xp, r8, e_tok, par, gates, dots_sc, v_words, u_tiles, v_tiles)


def kernel(x, attn_norm_g, attn_w_qkv, attn_q_g, attn_k_g, attn_rel_bias, attn_w_out, rec_norm_g, rec_w_in, rec_conv_w, rec_conv_b, rec_w_a, rec_b_a, rec_w_x, rec_b_x, rec_lambda, rec_w_out, ffn_norm_g, peer_w_q, peer_sub_keys, peer_u, peer_v):
    b, s, d = x.shape
    depth = ffn_norm_g.shape[0]
    for layer in range(depth):
        j = layer // 2
        if layer % 2 == 0:
            qkv = _qkv_proj(x.reshape(b * s, d), attn_norm_g[j], attn_w_qkv[j], attn_q_g[j], attn_k_g[j])
            x = _attention(x, qkv, attn_rel_bias[j], attn_w_out[j])
        else:
            x = _rglru(x, rec_norm_g[j], rec_w_in[j], rec_conv_w[j], rec_conv_b[j], rec_w_a[j],
                       rec_b_a[j], rec_w_x[j], rec_b_x[j], rec_lambda[j], rec_w_out[j])
        x = _peer_ffn(x.reshape(b * s, d), ffn_norm_g[layer], peer_w_q[layer], peer_sub_keys[layer],
                      peer_u, peer_v, layer).reshape(b, s, d)
    return x
```

```python
import numpy as np
import jax
import jax.numpy as jnp
from jax import lax
from jax.experimental import pallas as pl
from jax.experimental.pallas import tpu as pltpu
from jax.experimental.pallas import tpu_sc as plsc

F32 = jnp.float32
BF16 = jnp.bfloat16
U32 = jnp.uint32
I32 = jnp.int32

EPS = 1e-6
NEG_INF = -1e30

D_MODEL = 1024
CHUNK = 64
ATTN_HEADS = 16
HEAD_DIM = 64
LEFT_CHUNKS = 8
REL_CLIP = 256
LRU_HEADS = 4
LRU_BLOCK = 256
CONV_WIDTH = 4
LRU_C = 8.0
PEER_HEADS = 8
N_KEYS = 128
PEER_TOPK = 16
D_HALF = 128

SUBLANES = 8
LANES = 128

Q_TILE = 4 * CHUNK
N_KEY_BLOCKS = 3
ROW_TILE = 512
PEER_TOKENS = 128
LRU_TILE = 256
N_SLOTS = PEER_HEADS * PEER_TOPK

MIB = 1024 * 1024

SC_CORES = 2
SC_SUBCORES = 16
SC_LANES = 16
SC_U_GATHER_ROWS = 32
SC_V_GATHER_ROWS = 64
SC_TOKEN_BLOCKS = 146
SC_V_BLOCKS = 138


def _rms_rows(x, g):
    ms = jnp.mean(x * x, axis=-1, keepdims=True)
    return (x * lax.rsqrt(ms + EPS)) * g


def _split_bf16(v):
    hi = v.astype(BF16)
    lo = (v - hi.astype(F32)).astype(BF16)
    return hi, lo


def _dot(a, b):
    return jnp.dot(a, b, preferred_element_type=F32)


def _dot_nt(a, b):
    return lax.dot_general(a, b, (((1,), (1,)), ((), ())), preferred_element_type=F32)


def _qkv_kernel(x_ref, g_ref, w_ref, gain_ref, bsel_ref, bexp_ref, o_ref, xn_ref):
    j = pl.program_id(1)

    @pl.when(j == 0)
    def _():
        xn_ref[...] = _rms_rows(x_ref[...], g_ref[...]).astype(BF16)

    y = _dot(xn_ref[...], w_ref[...])

    @pl.when(j < 2)
    def _():
        hi, lo = _split_bf16(y * y)
        ms = _dot(hi, bsel_ref[...]) + _dot(lo, bsel_ref[...])
        rhi, rlo = _split_bf16(lax.rsqrt(ms + EPS))
        rs = _dot(rhi, bexp_ref[...]) + _dot(rlo, bexp_ref[...])
        o_ref[...] = ((y * rs) * gain_ref[0]).astype(BF16)

    @pl.when(j == 2)
    def _():
        o_ref[...] = y.astype(BF16)


def _qkv_proj(x2d, norm_g, w_qkv, q_g, k_g):
    t, d = x2d.shape
    scale = HEAD_DIM ** -0.5
    gains = jnp.stack([jnp.tile(q_g, ATTN_HEADS) * scale, jnp.tile(k_g, ATTN_HEADS),
                       jnp.ones((d,), F32)]).reshape(3, 1, d)
    head_of_col = np.arange(d) // HEAD_DIM
    bsel = (head_of_col[:, None] == np.arange(LANES)[None, :]).astype(np.float32) / HEAD_DIM
    bexp = (np.arange(LANES)[:, None] == head_of_col[None, :]).astype(np.float32)
    return pl.pallas_call(
        _qkv_kernel,
        grid=(t // ROW_TILE, 3),
        in_specs=[
            pl.BlockSpec((ROW_TILE, d), lambda i, j: (i, 0)),
            pl.BlockSpec((1, d), lambda i, j: (0, 0)),
            pl.BlockSpec((d, d), lambda i, j: (0, j)),
            pl.BlockSpec((1, 1, d), lambda i, j: (j, 0, 0)),
            pl.BlockSpec((d, LANES), lambda i, j: (0, 0)),
            pl.BlockSpec((LANES, d), lambda i, j: (0, 0)),
        ],
        out_specs=pl.BlockSpec((ROW_TILE, d), lambda i, j: (i, j)),
        out_shape=jax.ShapeDtypeStruct((t, 3 * d), BF16),
        scratch_shapes=[pltpu.VMEM((ROW_TILE, d), BF16)],
        compiler_params=pltpu.CompilerParams(
            dimension_semantics=("parallel", "arbitrary"), vmem_limit_bytes=40 * MIB),
        name="qkv_proj",
    )(x2d, norm_g.reshape(1, d), w_qkv.astype(BF16), gains,
      jnp.asarray(bsel, BF16), jnp.asarray(bexp, BF16))


def _attn_kernel(x_ref, q_ref, k0_ref, k1_ref, k2_ref, v0_ref, v1_ref, v2_ref,
                 bias_ref, wout_ref, o_ref, att_ref):
    qt = pl.program_id(1)
    k_refs = (k0_ref, k1_ref, k2_ref)
    v_refs = (v0_ref, v1_ref, v2_ref)
    negs = [jnp.where(qt - (N_KEY_BLOCKS - 1) + kb < 0, NEG_INF, 0.0).astype(F32)
            for kb in range(N_KEY_BLOCKS)]
    lane = lax.broadcasted_iota(I32, (Q_TILE, LANES), 1)
    first_half = lane < HEAD_DIM
    for p in range(ATTN_HEADS // 2):
        cols = slice(p * LANES, (p + 1) * LANES)
        qp = q_ref[0, :, cols]
        outs = []
        for hh in range(2):
            h = 2 * p + hh
            keep = first_half if hh == 0 else jnp.logical_not(first_half)
            qm = jnp.where(keep, qp, jnp.zeros_like(qp))
            s = [_dot_nt(qm, k_refs[kb][0, :, cols]) + bias_ref[kb, h] + negs[kb]
                 for kb in range(N_KEY_BLOCKS)]
            m = jnp.max(s[0], axis=-1, keepdims=True)
            for kb in range(1, N_KEY_BLOCKS):
                m = jnp.maximum(m, jnp.max(s[kb], axis=-1, keepdims=True))
            acc = jnp.zeros((Q_TILE, LANES), F32)
            l = jnp.zeros((Q_TILE, 1), F32)
            for kb in range(N_KEY_BLOCKS):
                e = jnp.exp(s[kb] - m)
                l = l + jnp.sum(e, axis=-1, keepdims=True)
                acc = acc + _dot(e.astype(BF16), v_refs[kb][0, :, cols])
            outs.append(acc / l)
        att_ref[:, cols] = jnp.where(first_half, outs[0], outs[1]).astype(BF16)
    o_ref[0] = x_ref[0] + _dot(att_ref[...], wout_ref[...])


def _band_bias(table):
    qi = np.arange(Q_TILE)[:, None]
    kj = np.arange(Q_TILE)[None, :]
    period = 2 * Q_TILE
    m = np.arange(period)
    delta = np.where(m < Q_TILE, m, m - period)
    blocks = []
    for kb in range(N_KEY_BLOCKS):
        dist = (N_KEY_BLOCKS - 1 - kb) * Q_TILE - delta
        diag = table[:, np.clip(dist, -REL_CLIP, REL_CLIP) + REL_CLIP]
        toep = jnp.tile(diag, (1, Q_TILE))[:, :Q_TILE * (period - 1)]
        toep = toep.reshape(-1, Q_TILE, period - 1)[:, :, :Q_TILE]
        dchunk = (Q_TILE // CHUNK) * (kb - (N_KEY_BLOCKS - 1)) + kj // CHUNK - qi // CHUNK
        valid = (dchunk >= -LEFT_CHUNKS) & (dchunk <= 0)
        blocks.append(jnp.where(valid[None], toep, NEG_INF))
    return jnp.stack(blocks).astype(F32)


def _attention(x, qkv, rel_bias_table, w_out):
    b, s, d = x.shape
    nq = s // Q_TILE
    qkv3 = qkv.reshape(b, s, 3 * d)
    bias = _band_bias(rel_bias_table)

    def kv_spec(kb, col):
        return pl.BlockSpec(
            (1, Q_TILE, d),
            lambda bi, qt: (bi, jnp.maximum(qt - (N_KEY_BLOCKS - 1) + kb, 0), col))

    in_specs = [pl.BlockSpec((1, Q_TILE, d), lambda bi, qt: (bi, qt, 0)),
                pl.BlockSpec((1, Q_TILE, d), lambda bi, qt: (bi, qt, 0))]
    in_specs += [kv_spec(kb, 1) for kb in range(N_KEY_BLOCKS)]
    in_specs += [kv_spec(kb, 2) for kb in range(N_KEY_BLOCKS)]
    in_specs += [pl.BlockSpec(memory_space=pltpu.VMEM), pl.BlockSpec(memory_space=pltpu.VMEM)]
    return pl.pallas_call(
        _attn_kernel,
        grid=(b, nq),
        in_specs=in_specs,
        out_specs=pl.BlockSpec((1, Q_TILE, d), lambda bi, qt: (bi, qt, 0)),
        out_shape=jax.ShapeDtypeStruct((b, s, d), F32),
        scratch_shapes=[pltpu.VMEM((Q_TILE, d), BF16)],
        compiler_params=pltpu.CompilerParams(
            dimension_semantics=("parallel", "parallel"), vmem_limit_bytes=48 * MIB),
        name="band_attention",
    )(x, qkv3, qkv3, qkv3, qkv3, qkv3, qkv3, qkv3, bias, w_out.astype(BF16))


def _rglru_kernel(x_ref, g_ref, win_ref, cw_ref, cb_ref, wa_ref, ba_ref, wx_ref, bx_ref,
                  lam_ref, wout_ref, o_ref, ext_ref, h_ref, a_scr, b_scr, hs_scr):
    ts = x_ref.shape[1]
    w = LRU_HEADS * LRU_BLOCK

    @pl.when(pl.program_id(1) == 0)
    def _():
        ext_ref[...] = jnp.zeros_like(ext_ref)
        h_ref[...] = jnp.zeros_like(h_ref)

    x = x_ref[0]
    xn = _rms_rows(x, g_ref[...]).astype(BF16)
    gu = _dot(xn, win_ref[...])
    gate = gu[:, :w]
    u_raw = gu[:, w:]
    ext_ref[0:SUBLANES, :] = ext_ref[ts:ts + SUBLANES, :]
    ext_ref[SUBLANES:, :] = u_raw
    u = cb_ref[...] + jnp.zeros((ts, w), F32)
    for k in range(CONV_WIDTH):
        off = SUBLANES - (CONV_WIDTH - 1) + k
        u = u + cw_ref[k:k + 1, :] * ext_ref[off:off + ts, :]
    ub = u.astype(BF16)

    def block_diag(wref):
        return jnp.concatenate(
            [_dot(ub[:, hh * LRU_BLOCK:(hh + 1) * LRU_BLOCK], wref[hh]) for hh in range(LRU_HEADS)],
            axis=1)

    r = jax.nn.sigmoid(block_diag(wa_ref) + ba_ref[...])
    i = jax.nn.sigmoid(block_diag(wx_ref) + bx_ref[...])
    z = -lam_ref[...]
    softplus = jnp.maximum(z, 0.0) + jnp.log1p(jnp.exp(-jnp.abs(z)))
    log_a = (-LRU_C) * r * softplus
    a_scr[...] = jnp.exp(log_a)
    th = jnp.tanh(log_a)
    b_scr[...] = jnp.sqrt(2.0 * th / (th - 1.0)) * (i * u)

    def step(t, h):
        h = a_scr[pl.ds(t, 1), :] * h + b_scr[pl.ds(t, 1), :]
        hs_scr[pl.ds(t, 1), :] = h
        return h

    h_ref[0:1, :] = lax.fori_loop(0, ts, step, h_ref[0:1, :], unroll=8)
    y = jax.nn.gelu(gate) * hs_scr[...]
    o_ref[0] = x + _dot(y.astype(BF16), wout_ref[...])


def _rglru(x, norm_g, w_in, conv_w, conv_b, w_a, b_a, w_x, b_x, lam, w_out):
    b, s, d = x.shape
    w = LRU_HEADS * LRU_BLOCK
    row = lambda v: v.reshape(1, -1)
    whole = pl.BlockSpec(memory_space=pltpu.VMEM)
    return pl.pallas_call(
        _rglru_kernel,
        grid=(b, s // LRU_TILE),
        in_specs=[pl.BlockSpec((1, LRU_TILE, d), lambda bi, si: (bi, si, 0))] + [whole] * 10,
        out_specs=pl.BlockSpec((1, LRU_TILE, d), lambda bi, si: (bi, si, 0)),
        out_shape=jax.ShapeDtypeStruct((b, s, d), F32),
        scratch_shapes=[pltpu.VMEM((LRU_TILE + SUBLANES, w), F32), pltpu.VMEM((SUBLANES, w), F32),
                        pltpu.VMEM((LRU_TILE, w), F32), pltpu.VMEM((LRU_TILE, w), F32),
                        pltpu.VMEM((LRU_TILE, w), F32)],
        compiler_params=pltpu.CompilerParams(
            dimension_semantics=("arbitrary", "arbitrary"), vmem_limit_bytes=48 * MIB),
        name="rglru_mixer",
    )(x, row(norm_g), w_in.astype(BF16), conv_w, row(conv_b), w_a.astype(BF16), row(b_a),
      w_x.astype(BF16), row(b_x), row(lam), w_out.astype(BF16))


def _peer_scores_kernel(x_ref, g_ref, wq_ref, sk_ref, xp_ref, st_ref, xw_ref):
    xb = _rms_rows(x_ref[...], g_ref[...]).astype(BF16)
    words = _bf16_pair_words(xb)
    xw_ref[...] = words
    tm = words.shape[0]
    for c in range(SUBLANES):
        piece = words[:, (c % (SUBLANES // 2)) * LANES:(c % (SUBLANES // 2) + 1) * LANES]
        xp_ref[pl.ds(c, tm, stride=SUBLANES), :] = piece
    q = _dot(xb, wq_ref[...]).astype(BF16)
    for hp in range(2 * PEER_HEADS):
        st_ref[hp] = _dot_nt(sk_ref[hp % 2], q[:, hp * D_HALF:(hp + 1) * D_HALF])


def _peer_scores(x2d, norm_g, w_q, sub_keys):
    t, d = x2d.shape
    nq = w_q.shape[1]
    whole = pl.BlockSpec(memory_space=pltpu.VMEM)
    return pl.pallas_call(
        _peer_scores_kernel,
        grid=(t // ROW_TILE,),
        in_specs=[pl.BlockSpec((ROW_TILE, d), lambda i: (i, 0)), whole, whole, whole],
        out_specs=[pl.BlockSpec((ROW_TILE * SUBLANES, LANES), lambda i: (i, 0)),
                   pl.BlockSpec((2 * PEER_HEADS, N_KEYS, ROW_TILE), lambda i: (0, 0, i)),
                   pl.BlockSpec((ROW_TILE, d // 2), lambda i: (i, 0))],
        out_shape=[jax.ShapeDtypeStruct((t * SUBLANES, LANES), U32),
                   jax.ShapeDtypeStruct((2 * PEER_HEADS, N_KEYS, t), F32),
                   jax.ShapeDtypeStruct((t, d // 2), U32)],
        compiler_params=pltpu.CompilerParams(
            dimension_semantics=("parallel",), vmem_limit_bytes=48 * MIB),
        name="peer_scores",
    )(x2d, norm_g.reshape(1, d), w_q.astype(BF16), sub_keys.astype(BF16))


def _candidate_blocks():
    blocks = []
    for i in range(PEER_TOPK // 2):
        nj = PEER_TOPK // (i + 1)
        for j0 in range(0, nj, SUBLANES):
            blocks.append(("row", i, j0, min(SUBLANES, nj - j0)))
    blocks.append(("col", PEER_TOPK // 2, PEER_TOPK // 2))
    return blocks


def _peer_topk_kernel(st_ref, r8_ref, par_ref, g_ref, e_ref, r8t_ref, et_ref):
    tk = st_ref.shape[2]
    key_id = lax.broadcasted_iota(I32, (N_KEYS, tk), 0).astype(F32)
    row16 = lax.broadcasted_iota(I32, (PEER_TOPK, tk), 0)
    row8 = lax.broadcasted_iota(I32, (SUBLANES, tk), 0)
    row8_f = row8.astype(F32)
    neg_inf = jnp.float32(-jnp.inf)
    blocks = _candidate_blocks()
    big = jnp.float32(PEER_TOPK * PEER_TOPK)

    def top16(x):
        vals = jnp.zeros((PEER_TOPK, tk), F32)
        idxs = jnp.zeros((PEER_TOPK, tk), F32)
        for k in range(PEER_TOPK):
            m = jnp.max(x, axis=0, keepdims=True)
            idx = jnp.min(jnp.where(x == m, key_id, float(N_KEYS)), axis=0, keepdims=True)
            x = jnp.where(key_id == idx, neg_inf, x)
            vals = jnp.where(row16 == k, m, vals)
            idxs = jnp.where(row16 == k, idx, idxs)
        return vals, idxs

    for h in range(PEER_HEADS):
        s0, i0 = top16(st_ref[2 * h])
        s1, i1 = top16(st_ref[2 * h + 1])
        cand, flat, expert = [], [], []
        for blk in blocks:
            if blk[0] == "row":
                _, i, j0, n = blk
                c = s0[i:i + 1, :] + s1[j0:j0 + SUBLANES, :]
                f = float(PEER_TOPK * i + j0) + row8_f
                e = i0[i:i + 1, :] * float(N_KEYS) + i1[j0:j0 + SUBLANES, :]
            else:
                _, i_start, n = blk
                c = s0[i_start:i_start + SUBLANES, :] + s1[0:1, :]
                f = float(PEER_TOPK) * (float(i_start) + row8_f)
                e = i0[i_start:i_start + SUBLANES, :] * float(N_KEYS) + i1[0:1, :]
            cand.append(jnp.where(row8 < n, c, neg_inf))
            flat.append(f)
            expert.append(e)
        cand = jnp.concatenate(cand, axis=0)
        flat = jnp.concatenate(flat, axis=0)
        expert = jnp.concatenate(expert, axis=0)
        best_s = jnp.zeros((PEER_TOPK, tk), F32)
        best_e = jnp.zeros((PEER_TOPK, tk), F32)
        for k in range(PEER_TOPK):
            m = jnp.max(cand, axis=0, keepdims=True)
            fidx = jnp.min(jnp.where(cand == m, flat, big), axis=0, keepdims=True)
            pick = flat == fidx
            e_sel = jnp.max(jnp.where(pick, expert, -1.0), axis=0, keepdims=True)
            cand = jnp.where(pick, neg_inf, cand)
            best_s = jnp.where(row16 == k, m, best_s)
            best_e = jnp.where(row16 == k, e_sel, best_e)
        ex = jnp.exp(best_s - best_s[0:1, :])
        gate = ex / jnp.sum(ex, axis=0, keepdims=True)
        rows = slice(h * PEER_TOPK, (h + 1) * PEER_TOPK)
        best_i = best_e.astype(I32)
        r8t_ref[rows, :] = (best_i >> 1) * SUBLANES
        et_ref[rows, :] = best_i
        par_ref[0, rows, :] = best_i & 1
        g_ref[0, rows, :] = gate
    r8_ref[0] = r8t_ref[...].T
    e_ref[0] = et_ref[...].T


def _peer_topk(scores_t, first_block, nblk):
    slot_major = pl.BlockSpec((1, N_SLOTS, PEER_TOKENS), lambda i: (i, 0, 0))
    token_major = pl.BlockSpec((1, PEER_TOKENS, N_SLOTS), lambda i: (i, 0, 0))
    return pl.pallas_call(
        _peer_topk_kernel,
        grid=(nblk,),
        in_specs=[pl.BlockSpec((2 * PEER_HEADS, N_KEYS, PEER_TOKENS), lambda i: (0, 0, i + first_block))],
        out_specs=[token_major, slot_major, slot_major, token_major],
        out_shape=[jax.ShapeDtypeStruct((nblk, PEER_TOKENS, N_SLOTS), I32),
                   jax.ShapeDtypeStruct((nblk, N_SLOTS, PEER_TOKENS), I32),
                   jax.ShapeDtypeStruct((nblk, N_SLOTS, PEER_TOKENS), F32),
                   jax.ShapeDtypeStruct((nblk, PEER_TOKENS, N_SLOTS), I32)],
        scratch_shapes=[pltpu.VMEM((N_SLOTS, PEER_TOKENS), I32), pltpu.VMEM((N_SLOTS, PEER_TOKENS), I32)],
        compiler_params=pltpu.CompilerParams(dimension_semantics=("parallel",)),
        name="peer_topk",
    )(scores_t)


def _bf16_pair_words(x):
    bits = lax.bitcast_convert_type(x.astype(BF16).astype(F32), U32)
    half = bits.shape[1] // 2
    return (bits[:, :half] >> 16) | (bits[:, half:] & jnp.uint32(0xFFFF0000))


def _pack_kernel(t_ref, o_ref, w_ref):
    words = _bf16_pair_words(t_ref[0])
    w_ref[...] = words
    rows = words.shape[0]
    n = words.shape[1] // LANES
    for s in range(n):
        o_ref[pl.ds(s, rows, stride=n), :] = words[:, s * LANES:(s + 1) * LANES]


def _pack_table(tabs, layer):
    _, e, d = tabs.shape
    n = d // (2 * LANES)
    return pl.pallas_call(
        _pack_kernel,
        grid=(e // ROW_TILE,),
        in_specs=[pl.BlockSpec((1, ROW_TILE, d), lambda i: (layer, i, 0))],
        out_specs=[pl.BlockSpec((ROW_TILE * n, LANES), lambda i: (i, 0)),
                   pl.BlockSpec((ROW_TILE, d // 2), lambda i: (i, 0))],
        out_shape=[jax.ShapeDtypeStruct((e * n, LANES), U32), jax.ShapeDtypeStruct((e, d // 2), U32)],
        compiler_params=pltpu.CompilerParams(dimension_semantics=("parallel",)),
        name="peer_pack_table",
    )(tabs)


def _unpack_words(w):
    lo = lax.bitcast_convert_type(w << 16, F32)
    hi = lax.bitcast_convert_type(w & jnp.uint32(0xFFFF0000), F32)
    return lo, hi


def _as_bf16(words):
    return pltpu.bitcast(words, BF16)


def _as_words(packed):
    return pltpu.bitcast(packed, U32)


def _merge_packed(a, b, shift, mask):
    ta = a + _as_bf16(pltpu.roll(_as_words(a), shift, 0))
    tb = b + _as_bf16(pltpu.roll(_as_words(b), SUBLANES - shift, 0))
    return jnp.where(mask, _as_words(ta), _as_words(tb))


def _sublane_iota():
    return lax.broadcasted_iota(I32, (SUBLANES, LANES), 0)


def _merge(a, b, shift, mask):
    ta = a + pltpu.roll(a, shift, 0)
    tb = b + pltpu.roll(b, SUBLANES - shift, 0)
    return jnp.where(mask, ta, tb)


GROUP = SUBLANES // 2
N_GROUPS = N_SLOTS // GROUP
N_ROWS = 2 * N_SLOTS
N_SPREAD = 4
SPREAD_AHEAD = 2


def _dup_halves(v, low):
    r = pltpu.roll(v, GROUP, 0)
    return jnp.where(low, v, r), jnp.where(low, r, v)


def _peer_u_kernel(r8_ref, xp_ref, g_ref, par_ref, tab_ref, a_ref, s_ref, part_ref):
    tb = g_ref.shape[2]
    sub = _sublane_iota()
    m2 = (sub & 2) != 0
    lane_t = lax.broadcasted_iota(I32, (SUBLANES, tb), 1)
    n_part = 2 * N_GROUPS

    def products(t, slot):
        xb = _as_bf16(xp_ref[pl.ds(pl.multiple_of(t * SUBLANES, SUBLANES), SUBLANES), :])
        ids = r8_ref.at[0, t]
        for grp in range(N_GROUPS):
            prods = []
            for i in range(GROUP):
                r8 = pl.multiple_of(ids[grp * GROUP + i], SUBLANES)
                prods.append(_as_bf16(tab_ref[pl.ds(r8, SUBLANES), :]) * xb)
            halves = (_merge_packed(prods[3], prods[1], 2, m2), _merge_packed(prods[2], prods[0], 2, m2))
            for k in range(2):
                lo, hi = _unpack_words(halves[k])
                part_ref[slot, 2 * grp + k] = lo + hi

    def reduce_lanes(t, slot):
        for n in range(n_part):
            col = jnp.sum(part_ref[slot, n], axis=1, keepdims=True)
            rows = slice(n * SUBLANES, (n + 1) * SUBLANES)
            s_ref[rows, :] = jnp.where(lane_t == t, col, s_ref[rows, :])

    def token_pair(i, carry):
        t = 2 * i
        reduce_lanes(t - 2, 0)
        reduce_lanes(t - 1, 1)
        products(t, 0)
        products(t + 1, 1)
        return carry

    s_ref[...] = jnp.zeros_like(s_ref)
    part_ref[...] = jnp.zeros_like(part_ref)
    lax.fori_loop(0, tb // 2, token_pair, 0)
    reduce_lanes(tb - 2, 0)
    reduce_lanes(tb - 1, 1)

    odd_t = (lax.broadcasted_iota(I32, (SUBLANES, tb), 0) & 1) != 0
    _gate_rows(lambda grp: _merge(s_ref[2 * grp * SUBLANES:(2 * grp + 1) * SUBLANES, :],
                                  s_ref[(2 * grp + 1) * SUBLANES:(2 * grp + 2) * SUBLANES, :], 1, odd_t),
               g_ref, par_ref, a_ref)


def _gate_rows(dots_of_group, g_ref, par_ref, a_ref):
    tb = g_ref.shape[2]
    low_t = lax.broadcasted_iota(I32, (SUBLANES, tb), 0) < GROUP
    want = jnp.where(low_t, 0, 1)
    for v in range(N_SLOTS // SUBLANES):
        src = slice(v * SUBLANES, (v + 1) * SUBLANES)
        gates = _dup_halves(g_ref[0, src, :], low_t)
        pars = _dup_halves(par_ref[0, src, :], low_t)
        for half in range(2):
            grp = 2 * v + half
            act = gates[half] * jax.nn.gelu(dots_of_group(grp))
            rows = slice(grp * SUBLANES, (grp + 1) * SUBLANES)
            a_ref[0, rows, :] = jnp.where(pars[half] == want, act, 0.0)


SEM_IDS, SEM_IN, SEM_OUT, N_SEMS = 2, 3, 5, 7


def _sc_token_loop(tab_hbm, e_hbm, idx_v, rows_v, sems, per_worker, token_in, out_buf, out_row, chunk_rows):
    wid = lax.axis_index("s") * SC_CORES + lax.axis_index("c")
    gather_rows = rows_v.shape[1]
    n_chunks = N_SLOTS // gather_rows
    first = wid * per_worker
    last = first + per_worker - 1

    def gather(slot, chunk):
        ids = idx_v.at[slot, pl.ds(chunk * gather_rows, gather_rows)]
        return pltpu.make_async_copy(tab_hbm.at[ids], rows_v.at[chunk % 2], sems.at[chunk % 2])

    def ids_copy(tok, slot):
        return pltpu.make_async_copy(e_hbm.at[tok], idx_v.at[slot], sems.at[SEM_IDS])

    def in_copies(tok, slot):
        return [pltpu.make_async_copy(row_of(tok), buf.at[slot], sems.at[SEM_IN + slot]) for row_of, buf in token_in]

    def out_copy(tok, slot):
        return pltpu.make_async_copy(out_buf.at[slot], out_row(tok), sems.at[SEM_OUT + slot])

    def one_token(k, tok, slot, next_tok):
        ids_copy(next_tok, 1 - slot).start()
        for c in in_copies(next_tok, 1 - slot):
            c.start()
        for c in in_copies(tok, slot):
            c.wait()

        @pl.when(k > 0)
        def _():
            out_copy(tok, slot).wait()

        for chunk in range(n_chunks):
            if chunk + 1 < n_chunks:
                gather(slot, chunk + 1).start()
            else:
                ids_copy(next_tok, 1 - slot).wait()
                gather(1 - slot, 0).start()
            gather(slot, chunk).wait()
            chunk_rows(chunk, rows_v.at[chunk % 2], slot)
        out_copy(tok, slot).start()

    def token_pair(k, carry):
        tok = first + 2 * k
        one_token(k, tok, 0, tok + 1)
        one_token(k, tok + 1, 1, jnp.minimum(tok + 2, last))
        return carry

    pltpu.sync_copy(e_hbm.at[first], idx_v.at[0])
    gather(0, 0).start()
    for c in in_copies(first, 0):
        c.start()
    lax.fori_loop(0, per_worker // 2, token_pair, 0)
    gather(0, 0).wait()
    for c in in_copies(last, 0):
        c.wait()
    out_copy(last, 0).wait()
    out_copy(last, 1).wait()


def _sc_scratch(row_len, dtype, gather_rows):
    return [pltpu.VMEM((2, N_SLOTS), I32), pltpu.VMEM((2, gather_rows, row_len), dtype),
            pltpu.SemaphoreType.DMA((N_SEMS,))]


def _sc_u_dots(u_words, e_tok, x_words, first_token):
    nw = x_words.shape[1]
    n_tokens = e_tok.shape[0]
    per_worker = n_tokens // (SC_CORES * SC_SUBCORES)
    n_chunks = nw // SC_LANES
    fold = 4
    mesh = plsc.VectorSubcoreMesh(core_axis_name="c", subcore_axis_name="s")

    def body(u_hbm, e_hbm, x_hbm, o_hbm, idx_v, rows_v, sems, x_v, out_v):
        lane = lax.iota(I32, SC_LANES)

        def chunk_rows(chunk, rows, slot):
            xs = [plsc.bitcast(x_v[slot, pl.ds(c * SC_LANES, SC_LANES)], BF16) for c in range(n_chunks)]
            for part in range(rows.shape[0] // SC_LANES):
                def row(r, vec):
                    rr = part * SC_LANES + r
                    acc = jnp.zeros((SC_LANES,), F32)
                    for g in range(n_chunks // fold):
                        p = [plsc.bitcast(rows[rr, pl.ds((fold * g + k) * SC_LANES, SC_LANES)], BF16)
                             * xs[fold * g + k] for k in range(fold)]
                        w32 = plsc.bitcast((p[0] + p[1]) + (p[2] + p[3]), U32)
                        acc = acc + plsc.bitcast(w32 << 16, F32) + plsc.bitcast(w32 & jnp.uint32(0xFFFF0000), F32)
                    return jnp.where(lane == r, jnp.sum(acc), vec)

                total = lax.fori_loop(0, SC_LANES, row, jnp.zeros((SC_LANES,), F32))
                out_v[slot, pl.ds(chunk * rows.shape[0] + part * SC_LANES, SC_LANES)] = total

        _sc_token_loop(u_hbm, e_hbm, idx_v, rows_v, sems, per_worker,
                       [(lambda tok: x_hbm.at[first_token + tok], x_v)], out_v, lambda tok: o_hbm.at[tok], chunk_rows)

    return pl.kernel(
        body,
        out_type=jax.ShapeDtypeStruct((n_tokens, N_SLOTS), F32),
        mesh=mesh,
        scratch_types=(_sc_scratch(nw, U32, SC_U_GATHER_ROWS)
                       + [pltpu.VMEM((2, nw), U32), pltpu.VMEM((2, N_SLOTS), F32)]),
        compiler_params=pltpu.CompilerParams(needs_layout_passes=False),
        name="peer_expert_in_sc",
    )(u_words, e_tok, x_words)


def _peer_gate_kernel(s_ref, g_ref, par_ref, a_ref):
    dots = s_ref[...].T
    low_t = lax.broadcasted_iota(I32, (SUBLANES, dots.shape[1]), 0) < GROUP
    _gate_rows(lambda grp: _dup_halves(dots[(grp // 2) * SUBLANES:(grp // 2 + 1) * SUBLANES, :], low_t)[grp % 2],
               g_ref, par_ref, a_ref)


def _peer_weights_kernel(a_ref, w_ref):
    low_t = lax.broadcasted_iota(I32, (SUBLANES, a_ref.shape[2]), 0) < GROUP
    tiles = []
    for v in range(N_SLOTS // SUBLANES):
        first = a_ref[0, 2 * v * SUBLANES:(2 * v + 1) * SUBLANES, :]
        second = a_ref[0, (2 * v + 1) * SUBLANES:(2 * v + 2) * SUBLANES, :]
        first = first + pltpu.roll(first, GROUP, 0)
        second = second + pltpu.roll(second, GROUP, 0)
        tiles.append(jnp.where(low_t, first, second))
    w_ref[...] = jnp.concatenate(tiles, axis=0).T


def _sc_v_sums(v_words, e_tok, w_tok, x2d, first_token):
    d = x2d.shape[1]
    nw = v_words.shape[1]
    n_tokens = e_tok.shape[0]
    per_worker = n_tokens // (SC_CORES * SC_SUBCORES)
    passes = 4
    fold = 4
    n_acc = nw // SC_LANES // passes
    mesh = plsc.VectorSubcoreMesh(core_axis_name="c", subcore_axis_name="s")

    def body(v_hbm, e_hbm, w_hbm, x_hbm, o_hbm, idx_v, rows_v, sems, w_v, x_v, out_v):
        def chunk_rows(chunk, rows, slot):
            src = x_v if chunk == 0 else out_v
            for ps in range(passes):
                base = ps * n_acc * SC_LANES

                def row_group(q, acc):
                    ws = []
                    for k in range(fold):
                        pos = jnp.full((SC_LANES,), chunk * rows.shape[0] + fold * q + k, I32)
                        w = plsc.load_gather(w_v.at[slot], [pos])
                        ws.append(plsc.pack(w, w, format=plsc.PackFormat.INTERLEAVED, preferred_element_type=BF16))
                    lo, hi = list(acc[:n_acc]), list(acc[n_acc:])
                    for c in range(n_acc):
                        p = [plsc.bitcast(rows[fold * q + k, pl.ds(base + c * SC_LANES, SC_LANES)], BF16) * ws[k]
                             for k in range(fold)]
                        w32 = plsc.bitcast((p[0] + p[1]) + (p[2] + p[3]), U32)
                        lo[c] = lo[c] + plsc.bitcast(w32 << 16, F32)
                        hi[c] = hi[c] + plsc.bitcast(w32 & jnp.uint32(0xFFFF0000), F32)
                    return tuple(lo + hi)

                acc0 = tuple([src[slot, pl.ds(base + c * SC_LANES, SC_LANES)] for c in range(n_acc)]
                             + [src[slot, pl.ds(nw + base + c * SC_LANES, SC_LANES)] for c in range(n_acc)])
                acc = lax.fori_loop(0, rows.shape[0] // fold, row_group, acc0)
                for c in range(n_acc):
                    out_v[slot, pl.ds(base + c * SC_LANES, SC_LANES)] = acc[c]
                    out_v[slot, pl.ds(nw + base + c * SC_LANES, SC_LANES)] = acc[n_acc + c]

        _sc_token_loop(v_hbm, e_hbm, idx_v, rows_v, sems, per_worker,
                       [(lambda tok: w_hbm.at[tok], w_v), (lambda tok: x_hbm.at[first_token + tok], x_v)],
                       out_v, lambda tok: o_hbm.at[tok], chunk_rows)

    return pl.kernel(
        body,
        out_type=jax.ShapeDtypeStruct((n_tokens, d), F32),
        mesh=mesh,
        scratch_types=(_sc_scratch(nw, U32, SC_V_GATHER_ROWS)
                       + [pltpu.VMEM((2, N_SLOTS), F32), pltpu.VMEM((2, d), F32), pltpu.VMEM((2, d), F32)]),
        compiler_params=pltpu.CompilerParams(needs_layout_passes=False),
        name="peer_expert_out_sc",
    )(v_words, e_tok, w_tok, x2d)


def _token_tiles(ref, xt_ref, to_tiles):
    tb, d = ref.shape
    for q in range(tb // SUBLANES):
        for c in range(d // LANES):
            rows = slice(q * SUBLANES, (q + 1) * SUBLANES)
            cols = slice(c * LANES, (c + 1) * LANES)
            strided = pl.ds(q * SUBLANES * SUBLANES + c, SUBLANES, stride=SUBLANES)
            if to_tiles:
                xt_ref[strided, :] = ref[rows, cols]
            else:
                ref[rows, cols] = xt_ref[strided, :]


def _peer_v_kernel(r8_ref, a_ref, x_ref, tab_ref, o_ref, bc_ref, xt_ref):
    tb = x_ref.shape[0]
    sub = _sublane_iota()
    low = sub < GROUP
    lane_t = lax.broadcasted_iota(I32, (SUBLANES, tb), 1)
    _token_tiles(x_ref, xt_ref, True)

    def spread(t, buf):
        for k in range(N_ROWS // SUBLANES):
            rows = slice(k * SUBLANES, (k + 1) * SUBLANES)
            col = jnp.sum(jnp.where(lane_t == t, a_ref[0, rows, :], 0.0), axis=1, keepdims=True)
            bits = lax.bitcast_convert_type(col.astype(BF16).astype(F32), U32)
            bc_ref[buf, rows, :] = jnp.broadcast_to(bits | (bits >> 16), (SUBLANES, LANES))

    def accumulate(t, buf):
        acc_lo = jnp.zeros((SUBLANES, LANES), F32)
        acc_hi = jnp.zeros((SUBLANES, LANES), F32)
        ids = r8_ref.at[0, t]
        for grp in range(N_GROUPS):
            prods = []
            for i in range(GROUP):
                row = grp * SUBLANES + i
                r8 = pl.multiple_of(ids[grp * GROUP + i], SUBLANES)
                am = jnp.where(low, bc_ref[buf, row:row + 1, :], bc_ref[buf, row + GROUP:row + GROUP + 1, :])
                prods.append(_as_bf16(tab_ref[pl.ds(r8, SUBLANES), :]) * _as_bf16(am))
            lo, hi = _unpack_words(_as_words((prods[0] + prods[1]) + (prods[2] + prods[3])))
            acc_lo = acc_lo + lo
            acc_hi = acc_hi + hi
        acc_lo = acc_lo + pltpu.roll(acc_lo, GROUP, 0)
        acc_hi = acc_hi + pltpu.roll(acc_hi, GROUP, 0)
        tile = pl.ds(pl.multiple_of(t * SUBLANES, SUBLANES), SUBLANES)
        xt_ref[tile, :] = xt_ref[tile, :] + jnp.where(low, acc_lo, acc_hi)

    def token_group(i, carry):
        t = N_SPREAD * i
        for k in range(N_SPREAD):
            spread(jnp.minimum(t + k + SPREAD_AHEAD, tb - 1), (k + SPREAD_AHEAD) % N_SPREAD)
            accumulate(t + k, k)
        return carry

    for k in range(SPREAD_AHEAD):
        spread(k, k)
    lax.fori_loop(0, tb // N_SPREAD, token_group, 0)
    _token_tiles(o_ref, xt_ref, False)


def _peer_experts(x2d, xp, r8, e_tok, par, gates, dots_sc, v_words, u_tab, v_tab):
    t, d = x2d.shape
    nblk = t // PEER_TOKENS
    nblk_tc = nblk - SC_TOKEN_BLOCKS
    smem_block = pl.BlockSpec((1, PEER_TOKENS, N_SLOTS), lambda i: (i, 0, 0), memory_space=pltpu.SMEM)
    slot_block = pl.BlockSpec((1, N_SLOTS, PEER_TOKENS), lambda i: (i, 0, 0))
    row_block = pl.BlockSpec((1, N_ROWS, PEER_TOKENS), lambda i: (i, 0, 0))
    tile_block = pl.BlockSpec((PEER_TOKENS * SUBLANES, LANES), lambda i: (i, 0))
    tok_block = pl.BlockSpec((PEER_TOKENS, d), lambda i: (i, 0))
    whole = pl.BlockSpec(memory_space=pltpu.VMEM)
    params = pltpu.CompilerParams(dimension_semantics=("parallel",), vmem_limit_bytes=56 * MIB)

    act_tc = pl.pallas_call(
        _peer_u_kernel,
        grid=(nblk_tc,),
        in_specs=[smem_block, tile_block, slot_block, slot_block, whole],
        out_specs=row_block,
        out_shape=jax.ShapeDtypeStruct((nblk_tc, N_ROWS, PEER_TOKENS), F32),
        scratch_shapes=[pltpu.VMEM((2 * N_ROWS, PEER_TOKENS), F32),
                        pltpu.VMEM((2, 2 * N_GROUPS, SUBLANES, LANES), F32)],
        compiler_params=params,
        name="peer_expert_in",
    )(r8, xp, gates, par, u_tab)
    shifted = pl.BlockSpec((1, N_SLOTS, PEER_TOKENS), lambda i: (i + nblk_tc, 0, 0))
    act_sc = pl.pallas_call(
        _peer_gate_kernel,
        grid=(SC_TOKEN_BLOCKS,),
        in_specs=[pl.BlockSpec((PEER_TOKENS, N_SLOTS), lambda i: (i, 0)), shifted, shifted],
        out_specs=row_block,
        out_shape=jax.ShapeDtypeStruct((SC_TOKEN_BLOCKS, N_ROWS, PEER_TOKENS), F32),
        compiler_params=pltpu.CompilerParams(dimension_semantics=("parallel",)),
        name="peer_gate_sc",
    )(dots_sc, gates, par)
    act = jnp.concatenate([act_tc, act_sc], axis=0)

    nblk_v = nblk - SC_V_BLOCKS
    w_sc = pl.pallas_call(
        _peer_weights_kernel,
        grid=(SC_V_BLOCKS,),
        in_specs=[pl.BlockSpec((1, N_ROWS, PEER_TOKENS), lambda i: (i + nblk_v, 0, 0))],
        out_specs=pl.BlockSpec((PEER_TOKENS, N_SLOTS), lambda i: (i, 0)),
        out_shape=jax.ShapeDtypeStruct((SC_V_BLOCKS * PEER_TOKENS, N_SLOTS), F32),
        compiler_params=pltpu.CompilerParams(dimension_semantics=("parallel",)),
        name="peer_weights_sc",
    )(act)
    y_sc = _sc_v_sums(v_words, e_tok[nblk_v:].reshape(-1, N_SLOTS), w_sc, x2d, nblk_v * PEER_TOKENS)
    y_tc = pl.pallas_call(
        _peer_v_kernel,
        grid=(nblk_v,),
        in_specs=[smem_block, row_block, tok_block, whole],
        out_specs=tok_block,
        out_shape=jax.ShapeDtypeStruct((nblk_v * PEER_TOKENS, d), F32),
        scratch_shapes=[pltpu.VMEM((N_SPREAD, N_ROWS, LANES), U32),
                        pltpu.VMEM((PEER_TOKENS * SUBLANES, LANES), F32)],
        compiler_params=params,
        name="peer_expert_out",
    )(r8, act, x2d, v_tab)
    return jnp.concatenate([y_tc, y_sc], axis=0)


def _peer_ffn(x2d, norm_g, w_q, sub_keys, u_tabs, v_tabs, layer):
    xp, scores_t, x_words = _peer_scores(x2d, norm_g, w_q, sub_keys)
    u_tiles, u_words = _pack_table(u_tabs, layer)
    v_tiles, v_words = _pack_table(v_tabs, layer)
    nblk = x2d.shape[0] // PEER_TOKENS
    nblk_tc = nblk - SC_TOKEN_BLOCKS
    top_sc = _peer_topk(scores_t, nblk_tc, SC_TOKEN_BLOCKS)
    dots_sc = _sc_u_dots(u_words, top_sc[3].reshape(-1, N_SLOTS), x_words, nblk_tc * PEER_TOKENS)
    top_tc = _peer_topk(scores_t, 0, nblk_tc)
    r8, par, gates, e_tok = [jnp.concatenate([a, b], axis=0) for a, b in zip(top_tc, top_sc)]
    return _peer_experts(x2d, xp, r8, e_tok, par, gates, dots_sc, v_words, u_tiles, v_tiles)


def kernel(x, attn_norm_g, attn_w_qkv, attn_q_g, attn_k_g, attn_rel_bias, attn_w_out, rec_norm_g, rec_w_in, rec_conv_w, rec_conv_b, rec_w_a, rec_b_a, rec_w_x, rec_b_x, rec_lambda, rec_w_out, ffn_norm_g, peer_w_q, peer_sub_keys, peer_u, peer_v):
    b, s, d = x.shape
    depth = ffn_norm_g.shape[0]
    for layer in range(depth):
        j = layer // 2
        if layer % 2 == 0:
            qkv = _qkv_proj(x.reshape(b * s, d), attn_norm_g[j], attn_w_qkv[j], attn_q_g[j], attn_k_g[j])
            x = _attention(x, qkv, attn_rel_bias[j], attn_w_out[j])
        else:
            x = _rglru(x, rec_norm_g[j], rec_w_in[j], rec_conv_w[j], rec_conv_b[j], rec_w_a[j],
                       rec_b_a[j], rec_w_x[j], rec_b_x[j], rec_lambda[j], rec_w_out[j])
        x = _peer_ffn(x.reshape(b * s, d), ffn_norm_g[layer], peer_w_q[layer], peer_sub_keys[layer],
                      peer_u, peer_v, layer).reshape(b, s, d)
    return x
```
